```python
import jax, jax.numpy as jnp
from jax import lax
import numpy as np

D_MODEL = 1024
BATCH = 8
SEQ = 8192
DEPTH = 1

PLE_DIM = 256
MIX_WIDTH = D_MODEL
ATTN_WIDTH = MIX_WIDTH // 2
CONV_WIDTH = MIX_WIDTH - ATTN_WIDTH
HEAD_DIM = 64
N_ATTN_HEADS = ATTN_WIDTH // HEAD_DIM
CONV_KERNEL = 31
FFN_CONV_KERNEL = 3
D_FF = 2816
Q_BLOCK = 128
EPS = 1e-6
IN_COLS = 3 * ATTN_WIDTH + 2 * CONV_WIDTH

kernel_name = "hymba_stickbreak_conformer_convglu_ple"


def rms_norm(x, g):
    xf = x.astype(jnp.float32)
    y = xf * lax.rsqrt(jnp.mean(xf * xf, axis=-1, keepdims=True) + EPS)
    return (y * g.astype(jnp.float32)).astype(x.dtype)


def layer_norm(x, g, b):
    xf = x.astype(jnp.float32)
    mu = jnp.mean(xf, axis=-1, keepdims=True)
    xc = xf - mu
    y = xc * lax.rsqrt(jnp.mean(xc * xc, axis=-1, keepdims=True) + EPS)
    return (y * g.astype(jnp.float32) + b.astype(jnp.float32)).astype(x.dtype)


def causal_dwconv(x, w, b):
    k_width, chans = w.shape
    y = lax.conv_general_dilated(
        x, w[:, None, :].astype(x.dtype), window_strides=(1,), padding=[(k_width - 1, 0)],
        dimension_numbers=("NWC", "WIO", "NWC"), feature_group_count=chans)
    return y + b.astype(x.dtype)


def stick_breaking_attention(q, k, v):
    seq = q.shape[2]
    scale = HEAD_DIM ** -0.5
    outs = []
    for blk in range(seq // Q_BLOCK):
        q0 = blk * Q_BLOCK
        q1 = q0 + Q_BLOCK
        qb = q[:, :, q0:q1]
        kb = k[:, :, :q1]
        vb = v[:, :, :q1]
        z = jnp.einsum("bhqd,bhkd->bhqk", qb, kb).astype(jnp.float32) * scale
        causal = jnp.arange(q1)[None, :] < jnp.arange(q0, q1)[:, None]
        log_beta = jax.nn.log_sigmoid(z)
        log_one_minus = jnp.where(causal, log_beta - z, 0.0)
        between = lax.cumsum(log_one_minus, axis=3, reverse=True) - log_one_minus
        weights = jnp.where(causal, jnp.exp(log_beta + between), 0.0)
        outs.append(jnp.einsum("bhqk,bhkd->bhqd", weights.astype(v.dtype), vb))
    return jnp.concatenate(outs, axis=2)


def _fwd_setup_inputs(seed: int = 0) -> dict:
    key = jax.random.key(seed)
    ks = jax.random.split(key, 20)
    f32 = jnp.float32
    nrm = lambda k, shape, s: jax.random.normal(k, shape, f32) * s
    gain = lambda k, shape: 1.0 + 0.05 * jax.random.normal(k, shape, f32)
    return {
        "x": jax.random.normal(ks[0], (BATCH, SEQ, D_MODEL), f32),
        "p": jax.random.normal(ks[1], (DEPTH, BATCH, SEQ, PLE_DIM), f32),
        "g_mix": gain(ks[2], (DEPTH, D_MODEL)),
        "w_in": nrm(ks[3], (DEPTH, D_MODEL, IN_COLS), D_MODEL ** -0.5),
        "q_gain": gain(ks[4], (DEPTH, HEAD_DIM)),
        "k_gain": gain(ks[5], (DEPTH, HEAD_DIM)),
        "dw_w": nrm(ks[6], (DEPTH, CONV_KERNEL, CONV_WIDTH), CONV_KERNEL ** -0.5),
        "dw_b": nrm(ks[7], (DEPTH, CONV_WIDTH), 0.02),
        "conv_ln_g": gain(ks[8], (DEPTH, CONV_WIDTH)),
        "conv_ln_b": nrm(ks[9], (DEPTH, CONV_WIDTH), 0.02),
        "w_out": nrm(ks[10], (DEPTH, MIX_WIDTH, D_MODEL), MIX_WIDTH ** -0.5),
        "g_ffn": gain(ks[11], (DEPTH, D_MODEL)),
        "w_up": nrm(ks[12], (DEPTH, D_MODEL, 2 * D_FF), D_MODEL ** -0.5),
        "ffn_conv_w": nrm(ks[13], (DEPTH, FFN_CONV_KERNEL, D_FF), FFN_CONV_KERNEL ** -0.5),
        "ffn_conv_b": nrm(ks[14], (DEPTH, D_FF), 0.02),
        "w_down": nrm(ks[15], (DEPTH, D_FF, D_MODEL), D_FF ** -0.5),
        "g_ple": gain(ks[16], (DEPTH, D_MODEL)),
        "w_ple_gate": nrm(ks[17], (DEPTH, D_MODEL, D_MODEL), D_MODEL ** -0.5),
        "w_ple_proj": nrm(ks[18], (DEPTH, PLE_DIM, D_MODEL), PLE_DIM ** -0.5),
    }


def _fwd_reference(x, p, g_mix, w_in, q_gain, k_gain, dw_w, dw_b, conv_ln_g, conv_ln_b, w_out,
              g_ffn, w_up, ffn_conv_w, ffn_conv_b, w_down, g_ple, w_ple_gate, w_ple_proj):
    bsz, seq, _ = x.shape
    h = x
    for i in range(DEPTH):
        u = rms_norm(h, g_mix[i])
        proj = u @ w_in[i]
        q, k, v, ca, cg = jnp.split(
            proj, np.cumsum([ATTN_WIDTH] * 3 + [CONV_WIDTH]).tolist(), axis=-1)
        heads = lambda t: t.reshape(bsz, seq, N_ATTN_HEADS, HEAD_DIM).transpose(0, 2, 1, 3)
        qh = rms_norm(heads(q), q_gain[i])
        kh = rms_norm(heads(k), k_gain[i])
        attn = stick_breaking_attention(qh, kh, heads(v))
        attn = attn.transpose(0, 2, 1, 3).reshape(bsz, seq, ATTN_WIDTH)
        c = ca * jax.nn.sigmoid(cg)
        c = causal_dwconv(c, dw_w[i], dw_b[i])
        c = jax.nn.silu(layer_norm(c, conv_ln_g[i], conv_ln_b[i]))
        h = h + jnp.concatenate([attn, c], axis=-1) @ w_out[i]
        u = rms_norm(h, g_ffn[i])
        gate, val = jnp.split(u @ w_up[i], 2, axis=-1)
        gate = causal_dwconv(gate, ffn_conv_w[i], ffn_conv_b[i])
        h = h + (jax.nn.gelu(gate, approximate=False) * val) @ w_down[i]
        ple_gate = jax.nn.sigmoid(rms_norm(h, g_ple[i]) @ w_ple_gate[i])
        h = h + ple_gate * (p[i] @ w_ple_proj[i])
    return h


import jax as _jax
import jax.numpy as _jnp

TWIN_FORMAT = 'train_step'
FWD_PARAMS = ['x', 'p', 'g_mix', 'w_in', 'q_gain', 'k_gain', 'dw_w', 'dw_b', 'conv_ln_g', 'conv_ln_b', 'w_out', 'g_ffn', 'w_up', 'ffn_conv_w', 'ffn_conv_b', 'w_down', 'g_ple', 'w_ple_gate', 'w_ple_proj']
TWIN_WEIGHTS = ['g_mix', 'w_in', 'q_gain', 'k_gain', 'dw_w', 'dw_b', 'conv_ln_g', 'conv_ln_b', 'w_out', 'g_ffn', 'w_up', 'ffn_conv_w', 'ffn_conv_b', 'w_down', 'g_ple', 'w_ple_gate', 'w_ple_proj']
TWIN_DIFF_INPUT = 'x'
TWIN_INPUTS = ['x', 'p', 'g_mix', 'w_in', 'q_gain', 'k_gain', 'dw_w', 'dw_b', 'conv_ln_g', 'conv_ln_b', 'w_out', 'g_ffn', 'w_up', 'ffn_conv_w', 'ffn_conv_b', 'w_down', 'g_ple', 'w_ple_gate', 'w_ple_proj', 'loss_target', 'm_g_mix', 'm_w_in', 'm_q_gain', 'm_k_gain', 'm_dw_w', 'm_dw_b', 'm_conv_ln_g', 'm_conv_ln_b', 'm_w_out', 'm_g_ffn', 'm_w_up', 'm_ffn_conv_w', 'm_ffn_conv_b', 'm_w_down', 'm_g_ple', 'm_w_ple_gate', 'm_w_ple_proj', 'v_g_mix', 'v_w_in', 'v_q_gain', 'v_k_gain', 'v_dw_w', 'v_dw_b', 'v_conv_ln_g', 'v_conv_ln_b', 'v_w_out', 'v_g_ffn', 'v_w_up', 'v_ffn_conv_w', 'v_ffn_conv_b', 'v_w_down', 'v_g_ple', 'v_w_ple_gate', 'v_w_ple_proj']
TWIN_OUTPUTS = ['loss', 'grad_x', 'grad_g_mix', 'grad_w_in', 'grad_q_gain', 'grad_k_gain', 'grad_dw_w', 'grad_dw_b', 'grad_conv_ln_g', 'grad_conv_ln_b', 'grad_w_out', 'grad_g_ffn', 'grad_w_up', 'grad_ffn_conv_w', 'grad_ffn_conv_b', 'grad_w_down', 'grad_g_ple', 'grad_w_ple_gate', 'grad_w_ple_proj', 'delta_g_mix', 'delta_w_in', 'delta_q_gain', 'delta_k_gain', 'delta_dw_w', 'delta_dw_b', 'delta_conv_ln_g', 'delta_conv_ln_b', 'delta_w_out', 'delta_g_ffn', 'delta_w_up', 'delta_ffn_conv_w', 'delta_ffn_conv_b', 'delta_w_down', 'delta_g_ple', 'delta_w_ple_gate', 'delta_w_ple_proj', 'new_m_g_mix', 'new_m_w_in', 'new_m_q_gain', 'new_m_k_gain', 'new_m_dw_w', 'new_m_dw_b', 'new_m_conv_ln_g', 'new_m_conv_ln_b', 'new_m_w_out', 'new_m_g_ffn', 'new_m_w_up', 'new_m_ffn_conv_w', 'new_m_ffn_conv_b', 'new_m_w_down', 'new_m_g_ple', 'new_m_w_ple_gate', 'new_m_w_ple_proj', 'new_v_g_mix', 'new_v_w_in', 'new_v_q_gain', 'new_v_k_gain', 'new_v_dw_w', 'new_v_dw_b', 'new_v_conv_ln_g', 'new_v_conv_ln_b', 'new_v_w_out', 'new_v_g_ffn', 'new_v_w_up', 'new_v_ffn_conv_w', 'new_v_ffn_conv_b', 'new_v_w_down', 'new_v_g_ple', 'new_v_w_ple_gate', 'new_v_w_ple_proj']
TWIN_LEAF_KINDS = {'loss': 'loss', 'grad_x': 'grad_x', 'grad_g_mix': 'grad_w', 'grad_w_in': 'grad_w', 'grad_q_gain': 'grad_w', 'grad_k_gain': 'grad_w', 'grad_dw_w': 'grad_w', 'grad_dw_b': 'grad_w', 'grad_conv_ln_g': 'grad_w', 'grad_conv_ln_b': 'grad_w', 'grad_w_out': 'grad_w', 'grad_g_ffn': 'grad_w', 'grad_w_up': 'grad_w', 'grad_ffn_conv_w': 'grad_w', 'grad_ffn_conv_b': 'grad_w', 'grad_w_down': 'grad_w', 'grad_g_ple': 'grad_w', 'grad_w_ple_gate': 'grad_w', 'grad_w_ple_proj': 'grad_w', 'delta_g_mix': 'delta_w', 'delta_w_in': 'delta_w', 'delta_q_gain': 'delta_w', 'delta_k_gain': 'delta_w', 'delta_dw_w': 'delta_w', 'delta_dw_b': 'delta_w', 'delta_conv_ln_g': 'delta_w', 'delta_conv_ln_b': 'delta_w', 'delta_w_out': 'delta_w', 'delta_g_ffn': 'delta_w', 'delta_w_up': 'delta_w', 'delta_ffn_conv_w': 'delta_w', 'delta_ffn_conv_b': 'delta_w', 'delta_w_down': 'delta_w', 'delta_g_ple': 'delta_w', 'delta_w_ple_gate': 'delta_w', 'delta_w_ple_proj': 'delta_w', 'new_m_g_mix': 'new_m', 'new_m_w_in': 'new_m', 'new_m_q_gain': 'new_m', 'new_m_k_gain': 'new_m', 'new_m_dw_w': 'new_m', 'new_m_dw_b': 'new_m', 'new_m_conv_ln_g': 'new_m', 'new_m_conv_ln_b': 'new_m', 'new_m_w_out': 'new_m', 'new_m_g_ffn': 'new_m', 'new_m_w_up': 'new_m', 'new_m_ffn_conv_w': 'new_m', 'new_m_ffn_conv_b': 'new_m', 'new_m_w_down': 'new_m', 'new_m_g_ple': 'new_m', 'new_m_w_ple_gate': 'new_m', 'new_m_w_ple_proj': 'new_m', 'new_v_g_mix': 'new_v', 'new_v_w_in': 'new_v', 'new_v_q_gain': 'new_v', 'new_v_k_gain': 'new_v', 'new_v_dw_w': 'new_v', 'new_v_dw_b': 'new_v', 'new_v_conv_ln_g': 'new_v', 'new_v_conv_ln_b': 'new_v', 'new_v_w_out': 'new_v', 'new_v_g_ffn': 'new_v', 'new_v_w_up': 'new_v', 'new_v_ffn_conv_w': 'new_v', 'new_v_ffn_conv_b': 'new_v', 'new_v_w_down': 'new_v', 'new_v_g_ple': 'new_v', 'new_v_w_ple_gate': 'new_v', 'new_v_w_ple_proj': 'new_v'}


def _forward(args):
    return _fwd_reference(*[args[k] for k in FWD_PARAMS])


def _output_shape():
    out = _jax.eval_shape(lambda: _forward(_fwd_setup_inputs(0)))
    return out.shape, out.dtype

N_MICROBATCH = 1
ADAM_LR = 0.001
ADAM_B1 = 0.9
ADAM_B2 = 0.999
ADAM_EPS = 1e-08
ADAM_WD = 0.01
ADAM_STEP = 10
PER_EXAMPLE_BATCH_AXIS = {'x': 0, 'p': 1, 'loss_target': 0}
SHARED_INPUTS = []
_WEIGHT_DTYPES = {'g_mix': _jnp.float32, 'w_in': _jnp.float32, 'q_gain': _jnp.float32, 'k_gain': _jnp.float32, 'dw_w': _jnp.float32, 'dw_b': _jnp.float32, 'conv_ln_g': _jnp.float32, 'conv_ln_b': _jnp.float32, 'w_out': _jnp.float32, 'g_ffn': _jnp.float32, 'w_up': _jnp.float32, 'ffn_conv_w': _jnp.float32, 'ffn_conv_b': _jnp.float32, 'w_down': _jnp.float32, 'g_ple': _jnp.float32, 'w_ple_gate': _jnp.float32, 'w_ple_proj': _jnp.float32}
MOMENT_SCALE = {'g_mix': 1.273274e+01, 'w_in': 3.913936e-01, 'q_gain': 3.295435e+01, 'k_gain': 3.230720e+01, 'dw_w': 8.025714e-01, 'dw_b': 1.646967e+01, 'conv_ln_g': 2.833371e+01, 'conv_ln_b': 1.994335e+01, 'w_out': 2.117643e+00, 'g_ffn': 6.063731e+01, 'w_up': 7.690711e-01, 'ffn_conv_w': 6.708356e+00, 'ffn_conv_b': 8.607987e+00, 'w_down': 6.697669e-01, 'g_ple': 1.984048e+00, 'w_ple_gate': 4.975078e-01, 'w_ple_proj': 6.862257e-01}


def _to_microbatches(a, axis):
    t = _jnp.moveaxis(a, axis, 0)
    t = t.reshape((N_MICROBATCH, t.shape[0] // N_MICROBATCH) + t.shape[1:])
    return _jnp.moveaxis(t, 1, axis + 1)


def setup_inputs(seed: int = 0) -> dict:
    inp = _fwd_setup_inputs(seed)
    key = _jax.random.fold_in(_jax.random.key(seed), 7919)
    shape, _ = _output_shape()
    out = dict(inp)
    out["loss_target"] = _jax.random.normal(_jax.random.fold_in(key, 0), shape, _jnp.float32)
    for i, name in enumerate(TWIN_WEIGHTS):
        w = inp[name].astype(_jnp.float32)
        if MOMENT_SCALE is None:
            s = _jnp.sqrt(_jnp.mean(_jnp.square(w)) + 1e-30)
        else:
            s = MOMENT_SCALE[name]
        km, kv = _jax.random.split(_jax.random.fold_in(key, i + 1))
        out[name] = w
        out["m_" + name] = s * _jax.random.normal(km, w.shape, _jnp.float32)
        out["v_" + name] = (s * s) * _jax.random.uniform(kv, w.shape, _jnp.float32, 0.5, 1.5)
    if N_MICROBATCH > 1:
        for name, axis in PER_EXAMPLE_BATCH_AXIS.items():
            out[name] = _to_microbatches(out[name], axis)
    return {'x': out['x'], 'p': out['p'], 'g_mix': out['g_mix'], 'w_in': out['w_in'], 'q_gain': out['q_gain'], 'k_gain': out['k_gain'], 'dw_w': out['dw_w'], 'dw_b': out['dw_b'], 'conv_ln_g': out['conv_ln_g'], 'conv_ln_b': out['conv_ln_b'], 'w_out': out['w_out'], 'g_ffn': out['g_ffn'], 'w_up': out['w_up'], 'ffn_conv_w': out['ffn_conv_w'], 'ffn_conv_b': out['ffn_conv_b'], 'w_down': out['w_down'], 'g_ple': out['g_ple'], 'w_ple_gate': out['w_ple_gate'], 'w_ple_proj': out['w_ple_proj'], 'loss_target': out['loss_target'], 'm_g_mix': out['m_g_mix'], 'm_w_in': out['m_w_in'], 'm_q_gain': out['m_q_gain'], 'm_k_gain': out['m_k_gain'], 'm_dw_w': out['m_dw_w'], 'm_dw_b': out['m_dw_b'], 'm_conv_ln_g': out['m_conv_ln_g'], 'm_conv_ln_b': out['m_conv_ln_b'], 'm_w_out': out['m_w_out'], 'm_g_ffn': out['m_g_ffn'], 'm_w_up': out['m_w_up'], 'm_ffn_conv_w': out['m_ffn_conv_w'], 'm_ffn_conv_b': out['m_ffn_conv_b'], 'm_w_down': out['m_w_down'], 'm_g_ple': out['m_g_ple'], 'm_w_ple_gate': out['m_w_ple_gate'], 'm_w_ple_proj': out['m_w_ple_proj'], 'v_g_mix': out['v_g_mix'], 'v_w_in': out['v_w_in'], 'v_q_gain': out['v_q_gain'], 'v_k_gain': out['v_k_gain'], 'v_dw_w': out['v_dw_w'], 'v_dw_b': out['v_dw_b'], 'v_conv_ln_g': out['v_conv_ln_g'], 'v_conv_ln_b': out['v_conv_ln_b'], 'v_w_out': out['v_w_out'], 'v_g_ffn': out['v_g_ffn'], 'v_w_up': out['v_w_up'], 'v_ffn_conv_w': out['v_ffn_conv_w'], 'v_ffn_conv_b': out['v_ffn_conv_b'], 'v_w_down': out['v_w_down'], 'v_g_ple': out['v_g_ple'], 'v_w_ple_gate': out['v_w_ple_gate'], 'v_w_ple_proj': out['v_w_ple_proj']}


def _loss(weights, diff, rest, loss_target):
    with _jax.named_scope("forward"):
        args = {**rest, TWIN_DIFF_INPUT: diff, **{k: w.astype(_WEIGHT_DTYPES[k]) for k, w in weights.items()}}
        y = _forward(args)
    with _jax.named_scope("loss_head"):
        err = _jnp.square(y.astype(_jnp.float32) - loss_target)
        return 0.5 * _jnp.sum(_jnp.mean(err, axis=-1)) if err.ndim else 0.5 * err


def _adamw(w, g, m, v):
    m = ADAM_B1 * m + (1.0 - ADAM_B1) * g
    v = ADAM_B2 * v + (1.0 - ADAM_B2) * _jnp.square(g)
    m_hat = m / (1.0 - ADAM_B1 ** ADAM_STEP)
    v_hat = v / (1.0 - ADAM_B2 ** ADAM_STEP)
    delta = -ADAM_LR * (m_hat / (_jnp.sqrt(v_hat) + ADAM_EPS) + ADAM_WD * w)
    return delta, m, v


def reference(x, p, g_mix, w_in, q_gain, k_gain, dw_w, dw_b, conv_ln_g, conv_ln_b, w_out, g_ffn, w_up, ffn_conv_w, ffn_conv_b, w_down, g_ple, w_ple_gate, w_ple_proj, loss_target, m_g_mix, m_w_in, m_q_gain, m_k_gain, m_dw_w, m_dw_b, m_conv_ln_g, m_conv_ln_b, m_w_out, m_g_ffn, m_w_up, m_ffn_conv_w, m_ffn_conv_b, m_w_down, m_g_ple, m_w_ple_gate, m_w_ple_proj, v_g_mix, v_w_in, v_q_gain, v_k_gain, v_dw_w, v_dw_b, v_conv_ln_g, v_conv_ln_b, v_w_out, v_g_ffn, v_w_up, v_ffn_conv_w, v_ffn_conv_b, v_w_down, v_g_ple, v_w_ple_gate, v_w_ple_proj):
    given = dict(x=x, p=p, g_mix=g_mix, w_in=w_in, q_gain=q_gain, k_gain=k_gain, dw_w=dw_w, dw_b=dw_b, conv_ln_g=conv_ln_g, conv_ln_b=conv_ln_b, w_out=w_out, g_ffn=g_ffn, w_up=w_up, ffn_conv_w=ffn_conv_w, ffn_conv_b=ffn_conv_b, w_down=w_down, g_ple=g_ple, w_ple_gate=w_ple_gate, w_ple_proj=w_ple_proj, loss_target=loss_target, m_g_mix=m_g_mix, m_w_in=m_w_in, m_q_gain=m_q_gain, m_k_gain=m_k_gain, m_dw_w=m_dw_w, m_dw_b=m_dw_b, m_conv_ln_g=m_conv_ln_g, m_conv_ln_b=m_conv_ln_b, m_w_out=m_w_out, m_g_ffn=m_g_ffn, m_w_up=m_w_up, m_ffn_conv_w=m_ffn_conv_w, m_ffn_conv_b=m_ffn_conv_b, m_w_down=m_w_down, m_g_ple=m_g_ple, m_w_ple_gate=m_w_ple_gate, m_w_ple_proj=m_w_ple_proj, v_g_mix=v_g_mix, v_w_in=v_w_in, v_q_gain=v_q_gain, v_k_gain=v_k_gain, v_dw_w=v_dw_w, v_dw_b=v_dw_b, v_conv_ln_g=v_conv_ln_g, v_conv_ln_b=v_conv_ln_b, v_w_out=v_w_out, v_g_ffn=v_g_ffn, v_w_up=v_w_up, v_ffn_conv_w=v_ffn_conv_w, v_ffn_conv_b=v_ffn_conv_b, v_w_down=v_w_down, v_g_ple=v_g_ple, v_w_ple_gate=v_w_ple_gate, v_w_ple_proj=v_w_ple_proj)
    weights = {n: given[n] for n in TWIN_WEIGHTS}
    shared = {n: given[n] for n in SHARED_INPUTS}
    per_example = {n: given[n] for n in ['x', 'p']}
    grad_fn = _jax.value_and_grad(_loss, argnums=(0, 1))

    def one_microbatch(ex, loss_target):
        ex = dict(ex)
        diff = ex.pop(TWIN_DIFF_INPUT)
        return grad_fn(weights, diff, {**shared, **ex}, loss_target)

    if N_MICROBATCH == 1:
        loss, (grad_w, grad_x) = one_microbatch(per_example, given["loss_target"])
    else:
        def body(carry, xs):
            loss_sum, grad_sum = carry
            l_k, (gw_k, gx_k) = one_microbatch(xs[0], xs[1])
            with _jax.named_scope("update"):
                return (loss_sum + l_k, _jax.tree.map(_jnp.add, grad_sum, gw_k)), gx_k

        init = (_jnp.zeros((), _jnp.float32), _jax.tree.map(_jnp.zeros_like, weights))
        (loss, grad_w), grad_x = _jax.lax.scan(body, init, (per_example, given["loss_target"]))
    with _jax.named_scope("update"):
        delta_w, new_m, new_v = {}, {}, {}
        for n in TWIN_WEIGHTS:
            delta_w[n], new_m[n], new_v[n] = _adamw(weights[n], grad_w[n], given["m_" + n], given["v_" + n])
    return (loss, grad_x, *[grad_w[n] for n in TWIN_WEIGHTS], *[delta_w[n] for n in TWIN_WEIGHTS],
            *[new_m[n] for n in TWIN_WEIGHTS], *[new_v[n] for n in TWIN_WEIGHTS])
```

```python
import functools

import jax
import jax.numpy as jnp
from jax import lax
from jax.experimental import pallas as pl
from jax.experimental.pallas import tpu as pltpu

F32 = jnp.float32
BF16 = jnp.bfloat16
HIGHEST = lax.Precision.HIGHEST
MESH = pl.DeviceIdType.MESH
ANY = pl.BlockSpec(memory_space=pl.ANY)

EPS = 1e-6
HEAD_DIM = 64
N_HEADS = 8
ATTN_W = 512
CONV_W = 512
CONV_K = 31
FFN_K = 3
ATTN_SCALE = 0.125
LANES = 128
CONV_HALO = 32
FFN_HALO = 8
VMEM_LIMIT = 56 * 1024 * 1024

ADAM_LR = 0.001
ADAM_B1 = 0.9
ADAM_B2 = 0.999
ADAM_EPS = 1e-08
ADAM_WD = 0.01
ADAM_STEP = 10

N_CHIPS = 4
N_DEV = 8


def _params(sem):
    return pltpu.CompilerParams(dimension_semantics=sem, vmem_limit_bytes=VMEM_LIMIT)


def _row_tile(s):
    return min(512, s)


def _rstd(x):
    return lax.rsqrt(jnp.mean(x * x, axis=-1, keepdims=True) + EPS)


def _sigmoid(x):
    return 1.0 / (1.0 + jnp.exp(-x))


def _mm(a, b, *, name, dims, grid, a_spec, b_spec, o_spec, o_tile, out_shape, res=None, res_spec=None):
    nk = grid[2]

    def body(*refs):
        if res is None:
            a_ref, b_ref, o_ref, acc_ref = refs
            r_ref = None
        else:
            a_ref, b_ref, r_ref, o_ref, acc_ref = refs
        part = lax.dot_general(a_ref[...], b_ref[...], (dims, ((), ())), preferred_element_type=F32)

        def finish(val):
            if r_ref is not None:
                val = val + r_ref[...]
            o_ref[...] = val.astype(o_ref.dtype)

        if nk == 1:
            finish(part)
        else:
            k = pl.program_id(2)

            @pl.when(k == 0)
            def _():
                acc_ref[...] = part

            @pl.when(k > 0)
            def _():
                acc_ref[...] += part

            @pl.when(k == nk - 1)
            def _():
                finish(acc_ref[...])

    in_specs = [a_spec, b_spec]
    args = [a, b]
    if res is not None:
        in_specs.append(res_spec)
        args.append(res)
    acc_tile = o_tile if nk > 1 else (8, LANES)
    return pl.pallas_call(
        body, name=name, grid=grid, in_specs=in_specs, out_specs=o_spec, out_shape=out_shape,
        scratch_shapes=[pltpu.VMEM(acc_tile, F32)],
        compiler_params=_params(("parallel", "parallel", "arbitrary")))(*args)


NN = ((1,), (0,))
NT = ((1,), (1,))
TN = ((0,), (0,))


def _mm_nn_sharded(a, bg, name, out_dtype=F32):
    s, k = a.shape
    g, _, ns = bg.shape
    tm = _row_tile(s)
    return _mm(a, bg, name=name, dims=NN, grid=(s // tm, g, 1),
               a_spec=pl.BlockSpec((tm, k), lambda i, j, kk: (i, 0)),
               b_spec=pl.BlockSpec((None, k, ns), lambda i, j, kk: (j, 0, 0)),
               o_spec=pl.BlockSpec((tm, ns), lambda i, j, kk: (i, j)), o_tile=(tm, ns),
               out_shape=jax.ShapeDtypeStruct((s, g * ns), out_dtype))


def _mm_nn_full(a, b, name, res=None):
    s, k = a.shape
    n = b.shape[1]
    tm = _row_tile(s)
    rs = pl.BlockSpec((tm, n), lambda i, j, kk: (i, 0))
    return _mm(a, b, name=name, dims=NN, grid=(s // tm, 1, 1),
               a_spec=pl.BlockSpec((tm, k), lambda i, j, kk: (i, 0)),
               b_spec=pl.BlockSpec((k, n), lambda i, j, kk: (0, 0)),
               o_spec=rs, o_tile=(tm, n), out_shape=jax.ShapeDtypeStruct((s, n), F32),
               res=res, res_spec=rs if res is not None else None)


def _mm_nt_full(a, b, name):
    s, n = a.shape
    k = b.shape[0]
    tm = _row_tile(s)
    return _mm(a, b, name=name, dims=NT, grid=(s // tm, 1, 1),
               a_spec=pl.BlockSpec((tm, n), lambda i, j, kk: (i, 0)),
               b_spec=pl.BlockSpec((k, n), lambda i, j, kk: (0, 0)),
               o_spec=pl.BlockSpec((tm, k), lambda i, j, kk: (i, 0)), o_tile=(tm, k),
               out_shape=jax.ShapeDtypeStruct((s, k), F32))


def _mm_nt_sharded(a, bg, name):
    s = a.shape[0]
    g, k, ns = bg.shape
    tm = _row_tile(s)
    return _mm(a, bg, name=name, dims=NT, grid=(s // tm, 1, g),
               a_spec=pl.BlockSpec((tm, ns), lambda i, j, kk: (i, kk)),
               b_spec=pl.BlockSpec((None, k, ns), lambda i, j, kk: (kk, 0, 0)),
               o_spec=pl.BlockSpec((tm, k), lambda i, j, kk: (i, 0)), o_tile=(tm, k),
               out_shape=jax.ShapeDtypeStruct((s, k), F32))


def _mm_tn_sharded(a, b, g, name):
    s, k = a.shape
    ns = b.shape[1] // g
    tk = _row_tile(s)
    return _mm(a, b, name=name, dims=TN, grid=(1, g, s // tk),
               a_spec=pl.BlockSpec((tk, k), lambda i, j, kk: (kk, 0)),
               b_spec=pl.BlockSpec((tk, ns), lambda i, j, kk: (kk, j)),
               o_spec=pl.BlockSpec((None, k, ns), lambda i, j, kk: (j, 0, 0)), o_tile=(k, ns),
               out_shape=jax.ShapeDtypeStruct((g, k, ns), F32))


def _mm_tn_full(a, b, name, tm):
    s, m = a.shape
    n = b.shape[1]
    tk = _row_tile(s)
    return _mm(a, b, name=name, dims=TN, grid=(m // tm, 1, s // tk),
               a_spec=pl.BlockSpec((tk, tm), lambda i, j, kk: (kk, i)),
               b_spec=pl.BlockSpec((tk, n), lambda i, j, kk: (kk, 0)),
               o_spec=pl.BlockSpec((tm, n), lambda i, j, kk: (i, 0)), o_tile=(tm, n),
               out_shape=jax.ShapeDtypeStruct((m, n), F32))


def _rms_fwd(x, g, name):
    s, d = x.shape
    ts = _row_tile(s)

    def body(x_ref, g_ref, u_ref):
        xv = x_ref[...]
        u_ref[...] = (xv * _rstd(xv) * g_ref[...]).astype(BF16)

    row = pl.BlockSpec((ts, d), lambda i: (i, 0))
    return pl.pallas_call(
        body, name=name, grid=(s // ts,), in_specs=[row, pl.BlockSpec((1, d), lambda i: (0, 0))],
        out_specs=row, out_shape=jax.ShapeDtypeStruct((s, d), BF16),
        compiler_params=_params(("parallel",)))(x, g)


def _rms_bwd(h, du, g, dh_in, name):
    s, d = h.shape
    ts = _row_tile(s)

    def body(h_ref, du_ref, g_ref, dhin_ref, dh_ref, dhb_ref, gg_ref):
        hv = h_ref[...]
        r = _rstd(hv)
        xhat = hv * r
        duv = du_ref[...]
        dxhat = duv * g_ref[...]
        m = jnp.mean(dxhat * xhat, axis=-1, keepdims=True)
        dh = dhin_ref[...] + r * (dxhat - xhat * m)
        dh_ref[...] = dh
        dhb_ref[...] = dh.astype(BF16)
        part = jnp.sum(duv * xhat, axis=0, keepdims=True)

        @pl.when(pl.program_id(0) == 0)
        def _():
            gg_ref[...] = part

        @pl.when(pl.program_id(0) > 0)
        def _():
            gg_ref[...] += part

    row = pl.BlockSpec((ts, d), lambda i: (i, 0))
    vec = pl.BlockSpec((1, d), lambda i: (0, 0))
    return pl.pallas_call(
        body, name=name, grid=(s // ts,), in_specs=[row, row, vec, row], out_specs=[row, row, vec],
        out_shape=[jax.ShapeDtypeStruct((s, d), F32), jax.ShapeDtypeStruct((s, d), BF16),
                   jax.ShapeDtypeStruct((1, d), F32)],
        compiler_params=_params(("arbitrary",)))(h, du, g, dh_in)


def _head_sum(x, bd):
    return jnp.dot(x, bd, precision=HIGHEST, preferred_element_type=F32)


def _qkv_prep(proj, qg, kg, bd):
    s = proj.shape[0]
    ts = _row_tile(s)

    def body(q_ref, k_ref, v_ref, qg_ref, kg_ref, bd_ref, qs_ref, kh_ref, vb_ref):
        def norm(x, gain):
            ms = _head_sum(x * x, bd_ref[...]) * (1.0 / HEAD_DIM)
            return x * lax.rsqrt(ms + EPS) * gain

        qs_ref[...] = (norm(q_ref[...], qg_ref[...]) * ATTN_SCALE).astype(BF16)
        kh_ref[...] = norm(k_ref[...], kg_ref[...]).astype(BF16)
        vb_ref[...] = v_ref[...].astype(BF16)

    col = lambda c: pl.BlockSpec((ts, ATTN_W), lambda i: (i, c))
    vec = pl.BlockSpec((1, ATTN_W), lambda i: (0, 0))
    out = pl.BlockSpec((ts, ATTN_W), lambda i: (i, 0))
    sds = jax.ShapeDtypeStruct((s, ATTN_W), BF16)
    return pl.pallas_call(
        body, name="qkv_prep", grid=(s // ts,),
        in_specs=[col(0), col(1), col(2), vec, vec, pl.BlockSpec((ATTN_W, ATTN_W), lambda i: (0, 0))],
        out_specs=[out, out, out], out_shape=[sds, sds, sds],
        compiler_params=_params(("parallel",)))(proj, proj, proj, qg, kg, bd)


def _qk_bwd(proj, dqh, dkh, dv, qg, kg, bd):
    s = proj.shape[0]
    ts = _row_tile(s)

    def body(q_ref, k_ref, dqh_ref, dkh_ref, dv_ref, qg_ref, kg_ref, bd_ref, out_ref, gq_ref, gk_ref):
        first = pl.program_id(0) == 0

        def bwd(x, dy, gain, gg_ref):
            ms = _head_sum(x * x, bd_ref[...]) * (1.0 / HEAD_DIM)
            r = lax.rsqrt(ms + EPS)
            xhat = x * r
            dxhat = dy * gain
            m = _head_sum(dxhat * xhat, bd_ref[...]) * (1.0 / HEAD_DIM)
            part = jnp.sum(dy * xhat, axis=0, keepdims=True)

            @pl.when(first)
            def _():
                gg_ref[...] = part

            @pl.when(jnp.logical_not(first))
            def _():
                gg_ref[...] += part

            return r * (dxhat - xhat * m)

        out_ref[:, 0:ATTN_W] = bwd(q_ref[...], dqh_ref[...], qg_ref[...], gq_ref).astype(BF16)
        out_ref[:, ATTN_W:2 * ATTN_W] = bwd(k_ref[...], dkh_ref[...], kg_ref[...], gk_ref).astype(BF16)
        out_ref[:, 2 * ATTN_W:3 * ATTN_W] = dv_ref[...].astype(BF16)

    col = lambda c: pl.BlockSpec((ts, ATTN_W), lambda i: (i, c))
    row = pl.BlockSpec((ts, ATTN_W), lambda i: (i, 0))
    vec = pl.BlockSpec((1, ATTN_W), lambda i: (0, 0))
    return pl.pallas_call(
        body, name="qk_bwd", grid=(s // ts,),
        in_specs=[col(0), col(1), row, row, row, vec, vec, pl.BlockSpec((ATTN_W, ATTN_W), lambda i: (0, 0))],
        out_specs=[pl.BlockSpec((ts, 3 * ATTN_W), lambda i: (i, 0)), vec, vec],
        out_shape=[jax.ShapeDtypeStruct((s, 3 * ATTN_W), BF16), jax.ShapeDtypeStruct((1, ATTN_W), F32),
                   jax.ShapeDtypeStruct((1, ATTN_W), F32)],
        compiler_params=_params(("arbitrary",)))(proj, proj, dqh, dkh, dv, qg, kg, bd)


def _split_dot(x, tri):
    hi = x.astype(BF16)
    lo = (x - hi.astype(F32)).astype(BF16)
    return (jnp.dot(hi, tri, preferred_element_type=F32) + jnp.dot(lo, tri, preferred_element_type=F32))


def _log_sigmoids(z):
    sp = jnp.log(1.0 + jnp.exp(-jnp.abs(z)))
    lb = jnp.minimum(z, 0.0) - sp
    return lb, lb - z


def _head_masks():
    lane = lax.broadcasted_iota(jnp.int32, (1, LANES), 1)
    return [lane < HEAD_DIM, lane >= HEAD_DIM]


def _attn_fwd(qs, kh, vb, tri_excl):
    s = qs.shape[0]
    t = tri_excl.shape[0]
    nq = s // t

    def body(q_ref, k_ref, v_ref, tri_ref, o_ref, ob_ref, acc_ref, c_ref):
        i = pl.program_id(1)
        hmask = _head_masks()
        q = q_ref[...]
        qm = [jnp.where(hm, q, jnp.zeros_like(q)) for hm in hmask]
        acc_ref[...] = jnp.zeros_like(acc_ref)
        c_ref[...] = jnp.zeros_like(c_ref)
        causal = (lax.broadcasted_iota(jnp.int32, (t, t), 1) < lax.broadcasted_iota(jnp.int32, (t, t), 0))

        def tile(kb, masked):
            k0 = pl.multiple_of(kb * t, t)
            kblk = k_ref[pl.ds(k0, t), :]
            vblk = v_ref[pl.ds(k0, t), :]
            for h in range(2):
                z = lax.dot_general(qm[h], kblk, (NT, ((), ())), preferred_element_type=F32)
                lb, lom = _log_sigmoids(z)
                if masked:
                    lom = jnp.where(causal, lom, 0.0)
                c = c_ref[h]
                w = jnp.exp(lb + _split_dot(lom, tri_ref[...]) + c)
                if masked:
                    w = jnp.where(causal, w, 0.0)
                c_ref[h] = c + jnp.sum(lom, axis=-1, keepdims=True)
                acc_ref[h] += jnp.dot(w.astype(BF16), vblk, preferred_element_type=F32)

        tile(i, True)

        def loop(step, carry):
            tile(i - 1 - step, False)
            return carry

        lax.fori_loop(0, i, loop, 0)
        o = jnp.where(hmask[0], acc_ref[0], acc_ref[1])
        o_ref[...] = o
        ob_ref[...] = o.astype(BF16)

    qspec = pl.BlockSpec((t, LANES), lambda hp, i: (i, hp))
    kspec = pl.BlockSpec((s, LANES), lambda hp, i: (0, hp))
    return pl.pallas_call(
        body, name="attn_fwd", grid=(ATTN_W // LANES, nq),
        in_specs=[qspec, kspec, kspec, pl.BlockSpec((t, t), lambda hp, i: (0, 0))],
        out_specs=[qspec, qspec],
        out_shape=[jax.ShapeDtypeStruct((s, ATTN_W), F32), jax.ShapeDtypeStruct((s, ATTN_W), BF16)],
        scratch_shapes=[pltpu.VMEM((2, t, LANES), F32), pltpu.VMEM((2, t, 1), F32)],
        compiler_params=_params(("parallel", "arbitrary")))(qs, kh, vb, tri_excl)


def _attn_bwd(qs, kh, vb, o, dmix, tri_excl, tri_incl):
    s = qs.shape[0]
    t = tri_excl.shape[0]
    nq = s // t

    def body(q_ref, k_ref, v_ref, o_ref, do_ref, te_ref, ti_ref, dq_ref, dk_ref, dv_ref, dqacc_ref, c_ref):
        i = pl.program_id(1)

        @pl.when(i == 0)
        def _():
            dk_ref[...] = jnp.zeros_like(dk_ref)
            dv_ref[...] = jnp.zeros_like(dv_ref)

        hmask = _head_masks()
        q = q_ref[...]
        do = do_ref[...]
        dob = do.astype(BF16)
        prod = dob.astype(F32) * o_ref[...]
        qm =[jnp.where(hm, q, jnp.zeros_like(q)) for hm in hmask]
        dom = [jnp.where(hm, dob, jnp.zeros_like(dob)) for hm in hmask]
        total = [jnp.sum(jnp.where(hm, prod, 0.0), axis=-1, keepdims=True) for hm in hmask]
        dqacc_ref[...] = jnp.zeros_like(dqacc_ref)
        c_ref[...] = jnp.zeros_like(c_ref)
        causal = (lax.broadcasted_iota(jnp.int32, (t, t), 1) < lax.broadcasted_iota(jnp.int32, (t, t), 0))

        def tile(kb, masked):
            k0 = pl.multiple_of(kb * t, t)
            kblk = k_ref[pl.ds(k0, t), :]
            vblk = v_ref[pl.ds(k0, t), :]
            dk_t = jnp.zeros((t, LANES), F32)
            dv_t = jnp.zeros((t, LANES), F32)
            for h in range(2):
                z = lax.dot_general(qm[h], kblk, (NT, ((), ())), preferred_element_type=F32)
                lb, lom = _log_sigmoids(z)
                if masked:
                    lom = jnp.where(causal, lom, 0.0)
                c_lom = c_ref[2 * h]
                c_g = c_ref[2 * h + 1]
                w = jnp.exp(lb + _split_dot(lom, te_ref[...]) + c_lom)
                if masked:
                    w = jnp.where(causal, w, 0.0)
                dw = lax.dot_general(dom[h], vblk, (NT, ((), ())), preferred_element_type=F32)
                wb = w.astype(BF16)
                g = dw * wb.astype(F32)
                g_before = total[h] - (_split_dot(g, ti_ref[...]) + c_g)
                beta = jnp.exp(lb)
                dz = g * (1.0 - beta) - beta * g_before
                if masked:
                    dz = jnp.where(causal, dz, 0.0)
                c_ref[2 * h] = c_lom + jnp.sum(lom, axis=-1, keepdims=True)
                c_ref[2 * h + 1] = c_g + jnp.sum(g, axis=-1, keepdims=True)
                dzb = dz.astype(BF16)
                dqacc_ref[h] += jnp.dot(dzb, kblk, preferred_element_type=F32)
                dk_t = dk_t + lax.dot_general(dzb, qm[h], (TN, ((), ())), preferred_element_type=F32)
                dv_t = dv_t + lax.dot_general(wb, dom[h], (TN, ((), ())), preferred_element_type=F32)
            dk_ref[pl.ds(k0, t), :] += dk_t
            dv_ref[pl.ds(k0, t), :] += dv_t

        tile(i, True)

        def loop(step, carry):
            tile(i - 1 - step, False)
            return carry

        lax.fori_loop(0, i, loop, 0)
        dq_ref[...] = jnp.where(hmask[0], dqacc_ref[0], dqacc_ref[1]) * ATTN_SCALE

    qspec = pl.BlockSpec((t, LANES), lambda hp, i: (i, hp))
    kspec = pl.BlockSpec((s, LANES), lambda hp, i: (0, hp))
    tspec = pl.BlockSpec((t, t), lambda hp, i: (0, 0))
    sds = jax.ShapeDtypeStruct((s, ATTN_W), F32)
    return pl.pallas_call(
        body, name="attn_bwd", grid=(ATTN_W // LANES, nq),
        in_specs=[qspec, kspec, kspec, qspec, qspec, tspec, tspec],
        out_specs=[qspec, kspec, kspec], out_shape=[sds, sds, sds],
        scratch_shapes=[pltpu.VMEM((2, t, LANES), F32), pltpu.VMEM((4, t, 1), F32)],
        compiler_params=_params(("parallel", "arbitrary")))(qs, kh, vb, o, dmix, tri_excl, tri_incl)


CONV_ROWS = 64
CONV_COLS = 256


def _taps(src_ref, w_ref, n_taps, first_row, rows, reverse=False):
    width = src_ref.shape[1]
    cols = min(CONV_COLS, width)
    out = []
    for r0 in range(0, rows, CONV_ROWS):
        for c0 in range(0, width, cols):
            acc = jnp.zeros((CONV_ROWS, cols), F32)
            for k in range(n_taps):
                off = (n_taps - 1 - k) if reverse else k
                acc = acc + w_ref[k:k + 1, c0:c0 + cols] * src_ref[first_row + r0 + off:first_row + r0 + off + CONV_ROWS,
                                                                   c0:c0 + cols]
            out.append(((r0, c0), acc))
    return out


def _conv_fwd(proj, dw_w, dw_b, ln_g, ln_b):
    s = proj.shape[0]
    ts = _row_tile(s)
    hb = ts // CONV_HALO

    def body(a_ref, g_ref, ha_ref, hg_ref, w_ref, b_ref, lg_ref, lb_ref, c1_ref, c3_ref, pad_ref):
        i = pl.program_id(0)
        halo = ha_ref[...] * _sigmoid(hg_ref[...])
        pad_ref[0:CONV_HALO, :] = jnp.where(i > 0, halo, 0.0)
        pad_ref[CONV_HALO:, :] = a_ref[...] * _sigmoid(g_ref[...])
        first = CONV_HALO - (CONV_K - 1)
        for (r0, c0), acc in _taps(pad_ref, w_ref, CONV_K, first, ts):
            c1_ref[r0:r0 + CONV_ROWS, c0:c0 + acc.shape[1]] = acc + b_ref[:, c0:c0 + acc.shape[1]]
        c1 = c1_ref[...]
        xc = c1 - jnp.mean(c1, axis=-1, keepdims=True)
        c2 = xc * _rstd(xc) * lg_ref[...] + lb_ref[...]
        c3_ref[...] = (c2 * _sigmoid(c2)).astype(BF16)

    cur = lambda c: pl.BlockSpec((ts, CONV_W), lambda i: (i, c))
    halo = lambda c: pl.BlockSpec((CONV_HALO, CONV_W), lambda i: (jnp.maximum(i * hb - 1, 0), c))
    vec = pl.BlockSpec((1, CONV_W), lambda i: (0, 0))
    row = pl.BlockSpec((ts, CONV_W), lambda i: (i, 0))
    return pl.pallas_call(
        body, name="conv_fwd", grid=(s // ts,),
        in_specs=[cur(3), cur(4), halo(3), halo(4), pl.BlockSpec((CONV_HALO, CONV_W), lambda i: (0, 0)), vec, vec, vec],
        out_specs=[row, row],
        out_shape=[jax.ShapeDtypeStruct((s, CONV_W), F32), jax.ShapeDtypeStruct((s, CONV_W), BF16)],
        scratch_shapes=[pltpu.VMEM((ts + CONV_HALO, CONV_W), F32)],
        compiler_params=_params(("parallel",)))(proj, proj, proj, proj, dw_w, dw_b, ln_g, ln_b)


def _conv_bwd_ln(dmix, c1, ln_g, ln_b):
    s = c1.shape[0]
    ts = _row_tile(s)

    def body(d_ref, c1_ref, lg_ref, lb_ref, dc1_ref, glg_ref, glb_ref, gb_ref):
        c1v = c1_ref[...]
        xc = c1v - jnp.mean(c1v, axis=-1, keepdims=True)
        r = _rstd(xc)
        xhat = xc * r
        c2 = xhat * lg_ref[...] + lb_ref[...]
        sg = _sigmoid(c2)
        dc2 = d_ref[...] * (sg * (1.0 + c2 * (1.0 - sg)))
        dxhat = dc2 * lg_ref[...]
        dc1 = r * (dxhat - jnp.mean(dxhat, axis=-1, keepdims=True)
                   - xhat * jnp.mean(dxhat * xhat, axis=-1, keepdims=True))
        dc1_ref[...] = dc1
        parts = [(glg_ref, jnp.sum(dc2 * xhat, axis=0, keepdims=True)),
                 (glb_ref, jnp.sum(dc2, axis=0, keepdims=True)),
                 (gb_ref, jnp.sum(dc1, axis=0, keepdims=True))]

        @pl.when(pl.program_id(0) == 0)
        def _():
            for ref, part in parts:
                ref[...] = part

        @pl.when(pl.program_id(0) > 0)
        def _():
            for ref, part in parts:
                ref[...] += part

    row = pl.BlockSpec((ts, CONV_W), lambda i: (i, 0))
    vec = pl.BlockSpec((1, CONV_W), lambda i: (0, 0))
    vsd = jax.ShapeDtypeStruct((1, CONV_W), F32)
    return pl.pallas_call(
        body, name="conv_bwd_ln", grid=(s // ts,),
        in_specs=[pl.BlockSpec((ts, CONV_W), lambda i: (i, 1)), row, vec, vec],
        out_specs=[row, vec, vec, vec],
        out_shape=[jax.ShapeDtypeStruct((s, CONV_W), F32), vsd, vsd, vsd],
        compiler_params=_params(("arbitrary",)))(dmix, c1, ln_g, ln_b)


def _conv_bwd_taps(proj, dc1, dw_w):
    s = proj.shape[0]
    ts = _row_tile(s)
    hb = ts // CONV_HALO
    last = s // CONV_HALO - 1
    nsteps = s // ts

    def body(a_ref, g_ref, ha_ref, hg_ref, d_ref, hd_ref, w_ref, out_ref, gw_ref, pad_ref, dpad_ref):
        i = pl.program_id(0)
        av = a_ref[...]
        sg = _sigmoid(g_ref[...])
        halo = ha_ref[...] * _sigmoid(hg_ref[...])
        pad_ref[0:CONV_HALO, :] = jnp.where(i > 0, halo, 0.0)
        pad_ref[CONV_HALO:, :] = av * sg
        dpad_ref[0:ts, :] = d_ref[...]
        dpad_ref[ts:, :] = jnp.where(i < nsteps - 1, hd_ref[...], 0.0)

        @pl.when(i == 0)
        def _():
            gw_ref[...] = jnp.zeros_like(gw_ref)

        first = CONV_HALO - (CONV_K - 1)
        for k in range(CONV_K):
            gw_ref[k:k + 1, :] += jnp.sum(d_ref[...] * pad_ref[first + k:first + k + ts, :], axis=0, keepdims=True)
        for (r0, c0), dc0 in _taps(dpad_ref, w_ref, CONV_K, 0, ts, reverse=True):
            cs = slice(c0, c0 + dc0.shape[1])
            a_c = a_ref[r0:r0 + CONV_ROWS, cs]
            sg_c = _sigmoid(g_ref[r0:r0 + CONV_ROWS, cs])
            out_ref[r0:r0 + CONV_ROWS, cs] = (dc0 * sg_c).astype(BF16)
            out_ref[r0:r0 + CONV_ROWS, CONV_W + c0:CONV_W + c0 + dc0.shape[1]] = (
                dc0 * a_c * sg_c * (1.0 - sg_c)).astype(BF16)

    cur = lambda c: pl.BlockSpec((ts, CONV_W), lambda i: (i, c))
    halo = lambda c: pl.BlockSpec((CONV_HALO, CONV_W), lambda i: (jnp.maximum(i * hb - 1, 0), c))
    row = pl.BlockSpec((ts, CONV_W), lambda i: (i, 0))
    nxt = pl.BlockSpec((CONV_HALO, CONV_W), lambda i: (jnp.minimum((i + 1) * hb, last), 0))
    wspec = pl.BlockSpec((CONV_HALO, CONV_W), lambda i: (0, 0))
    return pl.pallas_call(
        body, name="conv_bwd_taps", grid=(nsteps,),
        in_specs=[cur(3), cur(4), halo(3), halo(4), row, nxt, wspec],
        out_specs=[pl.BlockSpec((ts, 2 * CONV_W), lambda i: (i, 0)), wspec],
        out_shape=[jax.ShapeDtypeStruct((s, 2 * CONV_W), BF16), jax.ShapeDtypeStruct((CONV_HALO, CONV_W), F32)],
        scratch_shapes=[pltpu.VMEM((ts + CONV_HALO, CONV_W), F32), pltpu.VMEM((ts + CONV_HALO, CONV_W), F32)],
        compiler_params=_params(("arbitrary",)))(proj, proj, proj, proj, dc1, dc1, dw_w)


SQRT_HALF = 0.7071067811865476
INV_SQRT_2PI = 0.3989422804014327


def _gelu_parts(x):
    cdf = 0.5 * (1.0 + lax.erf(x * SQRT_HALF))
    return x * cdf, cdf + x * (INV_SQRT_2PI * jnp.exp(-0.5 * x * x))


def _ffn_tile(dff):
    return dff // 2


def _ffn_gate2(pad_ref, w_ref, b_ref, ts):
    first = FFN_HALO - (FFN_K - 1)
    g2 = b_ref[...] + w_ref[0:1, :] * pad_ref[first:first + ts, :]
    for k in range(1, FFN_K):
        g2 = g2 + w_ref[k:k + 1, :] * pad_ref[first + k:first + k + ts, :]
    return g2


def _ffn_act(up, fw, fb):
    s = up.shape[0]
    dff = up.shape[1] // 2
    tc = _ffn_tile(dff)
    nj = dff // tc
    ts = _row_tile(s) // 2
    hb = ts // FFN_HALO

    def body(g_ref, v_ref, hg_ref, w_ref, b_ref, act_ref, pad_ref):
        i = pl.program_id(0)
        pad_ref[0:FFN_HALO, :] = jnp.where(i > 0, hg_ref[...], 0.0)
        pad_ref[FFN_HALO:, :] = g_ref[...]
        gelu, _ = _gelu_parts(_ffn_gate2(pad_ref, w_ref, b_ref, ts))
        act_ref[...] = (gelu * v_ref[...]).astype(BF16)

    return pl.pallas_call(
        body, name="ffn_act", grid=(s // ts, nj),
        in_specs=[pl.BlockSpec((ts, tc), lambda i, j: (i, j)), pl.BlockSpec((ts, tc), lambda i, j: (i, j + nj)),
                  pl.BlockSpec((FFN_HALO, tc), lambda i, j: (jnp.maximum(i * hb - 1, 0), j)),
                  pl.BlockSpec((FFN_HALO, tc), lambda i, j: (0, j)), pl.BlockSpec((1, tc), lambda i, j: (0, j))],
        out_specs=pl.BlockSpec((ts, tc), lambda i, j: (i, j)),
        out_shape=jax.ShapeDtypeStruct((s, dff), BF16),
        scratch_shapes=[pltpu.VMEM((ts + FFN_HALO, tc), F32)],
        compiler_params=_params(("parallel", "parallel")))(up, up, up, fw, fb)


def _ffn_bwd_act(dact, up, fw, fb):
    s = up.shape[0]
    dff = up.shape[1] // 2
    tc = _ffn_tile(dff)
    nj = dff // tc
    ts = _row_tile(s) // 2
    hb = ts // FFN_HALO

    def body(d_ref, g_ref, v_ref, hg_ref, w_ref, b_ref, dg2_ref, dval_ref, gw_ref, gb_ref, pad_ref):
        i = pl.program_id(1)
        pad_ref[0:FFN_HALO, :] = jnp.where(i > 0, hg_ref[...], 0.0)
        pad_ref[FFN_HALO:, :] = g_ref[...]
        gelu, dgelu = _gelu_parts(_ffn_gate2(pad_ref, w_ref, b_ref, ts))
        dactv = d_ref[...]
        dval_ref[...] = (dactv * gelu).astype(BF16)
        dg2 = dactv * v_ref[...] * dgelu
        dg2_ref[...] = dg2

        @pl.when(i == 0)
        def _():
            gw_ref[...] = jnp.zeros_like(gw_ref)
            gb_ref[...] = jnp.zeros_like(gb_ref)

        gb_ref[...] += jnp.sum(dg2, axis=0, keepdims=True)
        first = FFN_HALO - (FFN_K - 1)
        for k in range(FFN_K):
            gw_ref[k:k + 1, :] += jnp.sum(dg2 * pad_ref[first + k:first + k + ts, :], axis=0, keepdims=True)

    blk = pl.BlockSpec((ts, tc), lambda j, i: (i, j))
    wspec = pl.BlockSpec((FFN_HALO, tc), lambda j, i: (0, j))
    bspec = pl.BlockSpec((1, tc), lambda j, i: (0, j))
    return pl.pallas_call(
        body, name="ffn_bwd_act", grid=(nj, s // ts),
        in_specs=[blk, blk, pl.BlockSpec((ts, tc), lambda j, i: (i, j + nj)),
                  pl.BlockSpec((FFN_HALO, tc), lambda j, i: (jnp.maximum(i * hb - 1, 0), j)), wspec, bspec],
        out_specs=[blk, blk, wspec, bspec],
        out_shape=[jax.ShapeDtypeStruct((s, dff), F32), jax.ShapeDtypeStruct((s, dff), BF16),
                   jax.ShapeDtypeStruct((FFN_HALO, dff), F32), jax.ShapeDtypeStruct((1, dff), F32)],
        scratch_shapes=[pltpu.VMEM((ts + FFN_HALO, tc), F32)],
        compiler_params=_params(("parallel", "arbitrary")))(dact, up, up, up, fw, fb)


def _ffn_bwd_conv(dg2, fw):
    s, dff = dg2.shape
    tc = _ffn_tile(dff)
    ts = _row_tile(s) // 2
    hb = ts // FFN_HALO
    last = s // FFN_HALO - 1
    nsteps = s // ts

    def body(d_ref, hd_ref, w_ref, out_ref, pad_ref):
        i = pl.program_id(0)
        pad_ref[0:ts, :] = d_ref[...]
        pad_ref[ts:, :] = jnp.where(i < nsteps - 1, hd_ref[...], 0.0)
        dg = w_ref[0:1, :] * pad_ref[FFN_K - 1:FFN_K - 1 + ts, :]
        for k in range(1, FFN_K):
            dg = dg + w_ref[k:k + 1, :] * pad_ref[FFN_K - 1 - k:FFN_K - 1 - k + ts, :]
        out_ref[...] = dg.astype(BF16)

    blk = pl.BlockSpec((ts, tc), lambda i, j: (i, j))
    return pl.pallas_call(
        body, name="ffn_bwd_conv", grid=(nsteps, dff // tc),
        in_specs=[blk, pl.BlockSpec((FFN_HALO, tc), lambda i, j: (jnp.minimum((i + 1) * hb, last), j)),
                  pl.BlockSpec((FFN_HALO, tc), lambda i, j: (0, j))],
        out_specs=blk, out_shape=jax.ShapeDtypeStruct((s, dff), BF16),
        scratch_shapes=[pltpu.VMEM((ts + FFN_HALO, tc), F32)],
        compiler_params=_params(("parallel", "parallel")))(dg2, dg2, fw)


def _ple_loss(h2, zg, pp, target):
    s, d = h2.shape
    ts = _row_tile(s)

    def body(h_ref, z_ref, p_ref, t_ref, dh_ref, dpp_ref, dz_ref, loss_ref):
        pg = _sigmoid(z_ref[...])
        ppv = p_ref[...]
        diff = h_ref[...] + pg * ppv - t_ref[...]
        dh = diff * (1.0 / d)
        dh_ref[...] = dh
        dpp_ref[...] = (dh * pg).astype(BF16)
        dz_ref[...] = (dh * ppv * pg * (1.0 - pg)).astype(BF16)
        part = jnp.sum(jnp.sum(diff * diff, axis=0, keepdims=True), axis=1, keepdims=True)

        @pl.when(pl.program_id(0) == 0)
        def _():
            loss_ref[...] = jnp.zeros_like(loss_ref)

        loss_ref[...] += jnp.broadcast_to(part, loss_ref.shape)

    row = pl.BlockSpec((ts, d), lambda i: (i, 0))
    return pl.pallas_call(
        body, name="ple_loss", grid=(s // ts,), in_specs=[row, row, row, row],
        out_specs=[row, row, row, pl.BlockSpec((8, LANES), lambda i: (0, 0))],
        out_shape=[jax.ShapeDtypeStruct((s, d), F32), jax.ShapeDtypeStruct((s, d), BF16),
                   jax.ShapeDtypeStruct((s, d), BF16), jax.ShapeDtypeStruct((8, LANES), F32)],
        compiler_params=_params(("arbitrary",)))(h2, zg, pp, target)


def _local_step(x, p, target, w):
    s = x.shape[0]
    t = min(256, s)
    tri = jnp.tril(jnp.ones((t, t), F32))
    tri_incl = tri.astype(BF16)
    tri_excl = jnp.tril(jnp.ones((t, t), F32), -1).astype(BF16)
    bd = jnp.kron(jnp.eye(N_HEADS, dtype=F32), jnp.ones((HEAD_DIM, HEAD_DIM), F32))
    qg = jnp.tile(w["q_gain"], (1, N_HEADS))
    kg = jnp.tile(w["k_gain"], (1, N_HEADS))
    dw_w = jnp.pad(w["dw_w"], ((0, CONV_HALO - CONV_K), (0, 0)))
    fw = jnp.pad(w["ffn_conv_w"], ((0, FFN_HALO - FFN_K), (0, 0)))
    pb = p.astype(BF16)

    u1 = _rms_fwd(x, w["g_mix"], "rms_mix")
    proj = _mm_nn_sharded(u1, w["w_in"], "mm_in")
    qs, kh, vb = _qkv_prep(proj, qg, kg, bd)
    o, ob = _attn_fwd(qs, kh, vb, tri_excl)
    c1, c3 = _conv_fwd(proj, dw_w, w["dw_b"], w["conv_ln_g"], w["conv_ln_b"])
    mix = jnp.concatenate([ob, c3], axis=1)
    h1 = _mm_nn_full(mix, w["w_out"], "mm_out", res=x)
    u2 = _rms_fwd(h1, w["g_ffn"], "rms_ffn")
    up = _mm_nn_sharded(u2, w["w_up"], "mm_up")
    act = _ffn_act(up, fw, w["ffn_conv_b"])
    h2 = _mm_nn_full(act, w["w_down"], "mm_down", res=h1)
    u3 = _rms_fwd(h2, w["g_ple"], "rms_ple")
    zg = _mm_nn_full(u3, w["w_ple_gate"], "mm_ple_gate")
    pp = _mm_nn_sharded(pb, w["w_ple_proj"], "mm_ple_proj")
    dh3, dpp, dz, sq = _ple_loss(h2, zg, pp, target)

    big = {}
    small = {}
    big["w_ple_proj"] = _mm_tn_sharded(pb, dpp, N_CHIPS, "mm_g_ple_proj")
    big["w_ple_gate"] = _mm_tn_full(u3, dz, "mm_g_ple_gate", tm=u3.shape[1])
    du3 = _mm_nt_full(dz, w["w_ple_gate"], "mm_d_ple_gate")
    dh2, dh2b, small["g_ple"] = _rms_bwd(h2, du3, w["g_ple"], dh3, "rms_ple_bwd")
    big["w_down"] = _mm_tn_full(act, dh2b, "mm_g_down", tm=act.shape[1] // 2)
    dact = _mm_nt_full(dh2b, w["w_down"], "mm_d_down")
    dg2, dval, gfw, small["ffn_conv_b"] = _ffn_bwd_act(dact, up, fw, w["ffn_conv_b"])
    small["ffn_conv_w"] = gfw[:FFN_K]
    dgate = _ffn_bwd_conv(dg2, fw)
    dup = jnp.concatenate([dgate, dval], axis=1)
    big["w_up"] = _mm_tn_sharded(u2, dup, N_CHIPS, "mm_g_up")
    du2 = _mm_nt_sharded(dup, w["w_up"], "mm_d_up")
    dh1, dh1b, small["g_ffn"] = _rms_bwd(h1, du2, w["g_ffn"], dh2, "rms_ffn_bwd")
    big["w_out"] = _mm_tn_full(mix, dh1b, "mm_g_out", tm=mix.shape[1])
    dmix = _mm_nt_full(dh1b, w["w_out"], "mm_d_out")
    dc1, small["conv_ln_g"], small["conv_ln_b"], small["dw_b"] = _conv_bwd_ln(dmix, c1, w["conv_ln_g"], w["conv_ln_b"])
    dcacg, gdw = _conv_bwd_taps(proj, dc1, dw_w)
    small["dw_w"] = gdw[:CONV_K]
    dqh, dkh, dv = _attn_bwd(qs, kh, vb, o, dmix, tri_excl, tri_incl)
    dqkv, gq, gk = _qk_bwd(proj, dqh, dkh, dv, qg, kg, bd)
    small["q_gain"] = gq.reshape(N_HEADS, HEAD_DIM).sum(axis=0, keepdims=True)
    small["k_gain"] = gk.reshape(N_HEADS, HEAD_DIM).sum(axis=0, keepdims=True)
    dproj = jnp.concatenate([dqkv, dcacg], axis=1)
    big["w_in"] = _mm_tn_sharded(u1, dproj, N_CHIPS, "mm_g_in")
    du1 = _mm_nt_sharded(dproj, w["w_in"], "mm_d_in")
    grad_x, _, small["g_mix"] = _rms_bwd(x, du1, w["g_mix"], dh1, "rms_mix_bwd")
    return sq[0, 0], grad_x, big, small


def _position():
    x, y, c = lax.axis_index("x"), lax.axis_index("y"), lax.axis_index("c")
    return x, y, c, [(1 - x, y), (x, 1 - y), (1 - x, 1 - y)]


def _remote(src, dst, send_sems, recv_sems, k, to):
    return pltpu.make_async_remote_copy(src_ref=src, dst_ref=dst, send_sem=send_sems.at[k], recv_sem=recv_sems.at[k],
                                        device_id=to, device_id_type=MESH)


def _gather_chips(shards):
    n = len(shards)

    def body(*refs):
        ins, outs = refs[:n], refs[n:2 * n]
        send_sems, recv_sems, local_sems = refs[2 * n:]
        x, y, c, chips = _position()
        me = 2 * x + y
        local = [pltpu.make_async_copy(ins[a], outs[a].at[me], local_sems.at[a]) for a in range(n)]
        for cp in local:
            cp.start()
        sends = [_remote(ins[a], outs[a].at[me], send_sems, recv_sems, 3 * a + j, (px, py, c))
                 for a in range(n) for j, (px, py) in enumerate(chips)]
        for cp in sends:
            cp.start()
        for a in range(n):
            for j, (px, py) in enumerate(chips):
                _remote(ins[a], outs[a].at[2 * px + py], send_sems, recv_sems, 3 * a + j, (px, py, c)).wait_recv()
        for cp in sends:
            cp.wait_send()
        for cp in local:
            cp.wait()

    return pl.pallas_call(
        body, name="gather_weights", in_specs=[ANY] * n, out_specs=[ANY] * n,
        out_shape=[jax.ShapeDtypeStruct((N_CHIPS,) + a.shape, a.dtype) for a in shards],
        scratch_shapes=[pltpu.SemaphoreType.DMA((3 * n,)), pltpu.SemaphoreType.DMA((3 * n,)),
                        pltpu.SemaphoreType.DMA((n,))])(*shards)


def _scatter_grads(grads, small):
    n = len(grads)

    def body(*refs):
        gin, sin = refs[:n], refs[n]
        gout, sout = refs[n + 1:2 * n + 1], refs[2 * n + 1]
        send_sems, recv_sems, local_sems, small_send, small_recv = refs[2 * n + 2:]
        x, y, c, chips = _position()
        me = 2 * x + y
        dev = 4 * x + 2 * y + c
        flip = lambda v, bit: 1 - v if bit else v
        others = [(flip(x, k & 4), flip(y, k & 2), flip(c, k & 1)) for k in range(1, N_DEV)]
        local = [pltpu.make_async_copy(gin[a].at[me], gout[a].at[me], local_sems.at[a]) for a in range(n)]
        local.append(pltpu.make_async_copy(sin, sout.at[dev], local_sems.at[n]))
        for cp in local:
            cp.start()
        sends = [_remote(gin[a].at[2 * px + py], gout[a].at[me], send_sems, recv_sems, 3 * a + j, (px, py, c))
                 for a in range(n) for j, (px, py) in enumerate(chips)]
        sends += [_remote(sin, sout.at[dev], small_send, small_recv, k, peer) for k, peer in enumerate(others)]
        for cp in sends:
            cp.start()
        for a in range(n):
            for j, (px, py) in enumerate(chips):
                _remote(gin[a].at[me], gout[a].at[2 * px + py], send_sems, recv_sems, 3 * a + j, (px, py, c)).wait_recv()
        for k, (px, py, pc) in enumerate(others):
            _remote(sin, sout.at[4 * px + 2 * py + pc], small_send, small_recv, k, (px, py, pc)).wait_recv()
        for cp in sends:
            cp.wait_send()
        for cp in local:
            cp.wait()

    return pl.pallas_call(
        body, name="scatter_grads", in_specs=[ANY] * (n + 1), out_specs=[ANY] * (n + 1),
        out_shape=[jax.ShapeDtypeStruct(g.shape, g.dtype) for g in grads]
        + [jax.ShapeDtypeStruct((N_DEV,) + small.shape, small.dtype)],
        scratch_shapes=[pltpu.SemaphoreType.DMA((3 * n,)), pltpu.SemaphoreType.DMA((3 * n,)),
                        pltpu.SemaphoreType.DMA((n + 1,)), pltpu.SemaphoreType.DMA((N_DEV - 1,)),
                        pltpu.SemaphoreType.DMA((N_DEV - 1,))])(*grads, small)


def _swap_sibling(arrs):
    n = len(arrs)

    def body(*refs):
        ins, outs = refs[:n], refs[n:2 * n]
        send_sems, recv_sems = refs[2 * n:]
        x, y, c, _ = _position()
        copies = [_remote(ins[a], outs[a], send_sems, recv_sems, a, (x, y, 1 - c)) for a in range(n)]
        for cp in copies:
            cp.start()
        for cp in copies:
            cp.wait_recv()
        for cp in copies:
            cp.wait_send()

    return pl.pallas_call(
        body, name="swap_sibling", in_specs=[ANY] * n, out_specs=[ANY] * n,
        out_shape=[jax.ShapeDtypeStruct(a.shape, a.dtype) for a in arrs],
        scratch_shapes=[pltpu.SemaphoreType.DMA((n,)), pltpu.SemaphoreType.DMA((n,))])(*arrs)


def _elem_tile(rows):
    return 128 if rows % 128 == 0 else (64 if rows % 64 == 0 else rows)


def _sum_slots(a, name):
    g, r, c = a.shape
    tr = _elem_tile(r)

    def body(a_ref, o_ref):
        acc = a_ref[0]
        for k in range(1, g):
            acc = acc + a_ref[k]
        o_ref[...] = acc

    return pl.pallas_call(
        body, name=name, grid=(r // tr,), in_specs=[pl.BlockSpec((g, tr, c), lambda i: (0, i, 0))],
        out_specs=pl.BlockSpec((tr, c), lambda i: (i, 0)), out_shape=jax.ShapeDtypeStruct((r, c), a.dtype),
        compiler_params=_params(("parallel",)))(a)


def _adamw(wt, ga, gb, m, v, name):
    r, c = wt.shape
    tr = _elem_tile(r)
    two = gb is not None

    def body(*refs):
        if two:
            w_ref, ga_ref, gb_ref, m_ref, v_ref, g_out, d_out, m_out, v_out = refs
            g = ga_ref[...] + gb_ref[...]
        else:
            w_ref, ga_ref, m_ref, v_ref, g_out, d_out, m_out, v_out = refs
            g = ga_ref[...]
        mn = ADAM_B1 * m_ref[...] + (1.0 - ADAM_B1) * g
        vn = ADAM_B2 * v_ref[...] + (1.0 - ADAM_B2) * (g * g)
        m_hat = mn / (1.0 - ADAM_B1 ** ADAM_STEP)
        v_hat = vn / (1.0 - ADAM_B2 ** ADAM_STEP)
        g_out[...] = g
        d_out[...] = -ADAM_LR * (m_hat / (jnp.sqrt(v_hat) + ADAM_EPS) + ADAM_WD * w_ref[...])
        m_out[...] = mn
        v_out[...] = vn

    blk = pl.BlockSpec((tr, c), lambda i: (i, 0))
    args = [wt, ga] + ([gb] if two else []) + [m, v]
    sds = jax.ShapeDtypeStruct((r, c), F32)
    return pl.pallas_call(
        body, name=name, grid=(r // tr,), in_specs=[blk] * len(args), out_specs=[blk] * 4, out_shape=[sds] * 4,
        compiler_params=_params(("parallel",)))(*args)


def _pack(arrs, rows):
    flat = jnp.concatenate([a.reshape(-1) for a in arrs])
    return jnp.pad(flat, (0, rows * LANES - flat.shape[0])).reshape(rows, LANES)


def _unpack(buf, shapes):
    flat = buf.reshape(-1)
    out, off = [], 0
    for shp in shapes:
        size = 1
        for d in shp:
            size *= d
        out.append(flat[off:off + size].reshape(shp))
        off += size
    return out


BIG = ["w_in", "w_out", "w_up", "w_down", "w_ple_gate", "w_ple_proj"]
COL_SHARDED = ["w_in", "w_up", "w_ple_proj"]
SMALL_REPL = ["g_mix", "q_gain", "k_gain", "dw_b", "conv_ln_g", "conv_ln_b", "g_ffn", "ffn_conv_b", "g_ple"]
SMALL_SHARDED = ["dw_w", "ffn_conv_w"]
WEIGHTS = ["g_mix", "w_in", "q_gain", "k_gain", "dw_w", "dw_b", "conv_ln_g", "conv_ln_b", "w_out", "g_ffn", "w_up",
           "ffn_conv_w", "ffn_conv_b", "w_down", "g_ple", "w_ple_gate", "w_ple_proj"]


def _rows_for(n_elems):
    return -(-n_elems // (8 * LANES)) * 8


def kernel(x, p, g_mix, w_in, q_gain, k_gain, dw_w, dw_b, conv_ln_g, conv_ln_b, w_out, g_ffn, w_up, ffn_conv_w, ffn_conv_b, w_down, g_ple, w_ple_gate, w_ple_proj, loss_target, m_g_mix, m_w_in, m_q_gain, m_k_gain, m_dw_w, m_dw_b, m_conv_ln_g, m_conv_ln_b, m_w_out, m_g_ffn, m_w_up, m_ffn_conv_w, m_ffn_conv_b, m_w_down, m_g_ple, m_w_ple_gate, m_w_ple_proj, v_g_mix, v_w_in, v_q_gain, v_k_gain, v_dw_w, v_dw_b, v_conv_ln_g, v_conv_ln_b, v_w_out, v_g_ffn, v_w_up, v_ffn_conv_w, v_ffn_conv_b, v_w_down, v_g_ple, v_w_ple_gate, v_w_ple_proj):
    given = dict(locals())
    strip = lambda n, a: a if n in SMALL_REPL else a[0]
    wts = {n: strip(n, given[n]) for n in WEIGHTS}
    mom = {n: strip(n, given["m_" + n]) for n in WEIGHTS}
    var = {n: strip(n, given["v_" + n]) for n in WEIGHTS}
    chip = 2 * lax.axis_index("x") + lax.axis_index("y")

    small_shard_shapes = [wts[n].shape for n in SMALL_SHARDED]
    filt_rows = _rows_for(sum(wts[n].size for n in SMALL_SHARDED))
    filt = _pack([wts[n] for n in SMALL_SHARDED], filt_rows)
    gathered = _gather_chips([wts[n].astype(BF16) for n in BIG] + [filt])
    full = {n: wts[n] for n in SMALL_REPL}
    for n, g in zip(BIG, gathered):
        full[n] = g if n in COL_SHARDED else g.reshape(-1, g.shape[-1])
    per_chip = [_unpack(gathered[-1][k], small_shard_shapes) for k in range(N_CHIPS)]
    for idx, n in enumerate(SMALL_SHARDED):
        full[n] = jnp.concatenate([per_chip[k][idx] for k in range(N_CHIPS)], axis=1)

    sq, grad_x, big, small = _local_step(x[0], p[0, 0], loss_target[0], full)
    loss = lax.psum(sq * (0.5 / x.shape[-1]), ("x", "y", "c"))

    small_names = SMALL_REPL + SMALL_SHARDED
    small_shapes = [small[n].shape for n in small_names]
    small_rows = _rows_for(sum(small[n].size for n in small_names))
    shard_major = [big[n] if n in COL_SHARDED else big[n].reshape(N_CHIPS, -1, big[n].shape[-1]) for n in BIG]
    *slots, small_slots = _scatter_grads(shard_major, _pack([small[n] for n in small_names], small_rows))
    mine = [_sum_slots(sl, "sum_" + n) for sl, n in zip(slots, BIG)]
    theirs = _swap_sibling(mine)
    small_sum = dict(zip(small_names, _unpack(_sum_slots(small_slots, "sum_small"), small_shapes)))

    outs = {}
    for n, ga, gb in zip(BIG, mine, theirs):
        outs[n] = _adamw(wts[n], ga, gb, mom[n], var[n], "adamw_" + n)
    for n in SMALL_SHARDED:
        width = wts[n].shape[1]
        small_sum[n] = lax.dynamic_slice_in_dim(small_sum[n], chip * width, width, axis=1)
    local_shapes = [wts[n].shape for n in small_names]
    local_rows = _rows_for(sum(wts[n].size for n in small_names))
    packed = _adamw(_pack([wts[n] for n in small_names], local_rows), _pack([small_sum[n] for n in small_names], local_rows),
                    None, _pack([mom[n] for n in small_names], local_rows),
                    _pack([var[n] for n in small_names], local_rows), "adamw_small")
    unpacked = [_unpack(buf, local_shapes) for buf in packed]
    for idx, n in enumerate(small_names):
        outs[n] = [u[idx] for u in unpacked]
    result = [loss, grad_x[None]]
    for part in range(4):
        result += [outs[n][part] if n in SMALL_REPL else outs[n][part][None] for n in WEIGHTS]
    return tuple(result)
```

```python
import functools

import jax
import jax.numpy as jnp
from jax import lax
from jax.experimental import pallas as pl
from jax.experimental.pallas import tpu as pltpu

F32 = jnp.float32
BF16 = jnp.bfloat16
HIGHEST = lax.Precision.HIGHEST
MESH = pl.DeviceIdType.MESH
ANY = pl.BlockSpec(memory_space=pl.ANY)

EPS = 1e-6
HEAD_DIM = 64
N_HEADS = 8
ATTN_W = 512
CONV_W = 512
CONV_K = 31
FFN_K = 3
ATTN_SCALE = 0.125
LANES = 128
CONV_HALO = 32
FFN_HALO = 8
VMEM_LIMIT = 56 * 1024 * 1024

ADAM_LR = 0.001
ADAM_B1 = 0.9
ADAM_B2 = 0.999
ADAM_EPS = 1e-08
ADAM_WD = 0.01
ADAM_STEP = 10

N_CHIPS = 4
N_DEV = 8


def _params(sem):
    return pltpu.CompilerParams(dimension_semantics=sem, vmem_limit_bytes=VMEM_LIMIT)


def _row_tile(s):
    return min(512, s)


def _rstd(x):
    return lax.rsqrt(jnp.mean(x * x, axis=-1, keepdims=True) + EPS)


def _sigmoid(x):
    return 1.0 / (1.0 + jnp.exp(-x))


def _mm(a, b, *, name, dims, grid, a_spec, b_spec, o_spec, o_tile, out_shape, res=None, res_spec=None):
    nk = grid[2]

    def body(*refs):
        if res is None:
            a_ref, b_ref, o_ref, acc_ref = refs
            r_ref = None
        else:
            a_ref, b_ref, r_ref, o_ref, acc_ref = refs
        part = lax.dot_general(a_ref[...], b_ref[...], (dims, ((), ())), preferred_element_type=F32)

        def finish(val):
            if r_ref is not None:
                val = val + r_ref[...]
            o_ref[...] = val.astype(o_ref.dtype)

        if nk == 1:
            finish(part)
        else:
            k = pl.program_id(2)

            @pl.when(k == 0)
            def _():
                acc_ref[...] = part

            @pl.when(k > 0)
            def _():
                acc_ref[...] += part

            @pl.when(k == nk - 1)
            def _():
                finish(acc_ref[...])

    in_specs = [a_spec, b_spec]
    args = [a, b]
    if res is not None:
        in_specs.append(res_spec)
        args.append(res)
    acc_tile = o_tile if nk > 1 else (8, LANES)
    return pl.pallas_call(
        body, name=name, grid=grid, in_specs=in_specs, out_specs=o_spec, out_shape=out_shape,
        scratch_shapes=[pltpu.VMEM(acc_tile, F32)],
        compiler_params=_params(("parallel", "parallel", "arbitrary")))(*args)


NN = ((1,), (0,))
NT = ((1,), (1,))
TN = ((0,), (0,))


def _mm_nn_sharded(a, bg, name, out_dtype=F32):
    s, k = a.shape
    g, _, ns = bg.shape
    tm = _row_tile(s)
    return _mm(a, bg, name=name, dims=NN, grid=(g, s // tm, 1),
               a_spec=pl.BlockSpec((tm, k), lambda j, i, kk: (i, 0)),
               b_spec=pl.BlockSpec((None, k, ns), lambda j, i, kk: (j, 0, 0)),
               o_spec=pl.BlockSpec((tm, ns), lambda j, i, kk: (i, j)), o_tile=(tm, ns),
               out_shape=jax.ShapeDtypeStruct((s, g * ns), out_dtype))


def _mm_nn_full(a, b, name, res=None):
    s, k = a.shape
    n = b.shape[1]
    tm = _row_tile(s)
    rs = pl.BlockSpec((tm, n), lambda i, j, kk: (i, 0))
    return _mm(a, b, name=name, dims=NN, grid=(s // tm, 1, 1),
               a_spec=pl.BlockSpec((tm, k), lambda i, j, kk: (i, 0)),
               b_spec=pl.BlockSpec((k, n), lambda i, j, kk: (0, 0)),
               o_spec=rs, o_tile=(tm, n), out_shape=jax.ShapeDtypeStruct((s, n), F32),
               res=res, res_spec=rs if res is not None else None)


def _mm_nt_full(a, b, name):
    s, n = a.shape
    k = b.shape[0]
    tm = _row_tile(s)
    return _mm(a, b, name=name, dims=NT, grid=(s // tm, 1, 1),
               a_spec=pl.BlockSpec((tm, n), lambda i, j, kk: (i, 0)),
               b_spec=pl.BlockSpec((k, n), lambda i, j, kk: (0, 0)),
               o_spec=pl.BlockSpec((tm, k), lambda i, j, kk: (i, 0)), o_tile=(tm, k),
               out_shape=jax.ShapeDtypeStruct((s, k), F32))


def _mm_nt_sharded(a, bg, name):
    s = a.shape[0]
    g, k, ns = bg.shape
    tm = _row_tile(s)

    def body(a_ref, b_ref, o_ref):
        acc = lax.dot_general(a_ref[:, 0:ns], b_ref[0], (NT, ((), ())), preferred_element_type=F32)
        for gi in range(1, g):
            acc = acc + lax.dot_general(a_ref[:, gi * ns:(gi + 1) * ns], b_ref[gi], (NT, ((), ())),
                                        preferred_element_type=F32)
        o_ref[...] = acc

    return pl.pallas_call(
        body, name=name, grid=(s // tm,),
        in_specs=[pl.BlockSpec((tm, g * ns), lambda i: (i, 0)), pl.BlockSpec((g, k, ns), lambda i: (0, 0, 0))],
        out_specs=pl.BlockSpec((tm, k), lambda i: (i, 0)), out_shape=jax.ShapeDtypeStruct((s, k), F32),
        compiler_params=_params(("parallel",)))(a, bg)


def _mm_tn_sharded(a, b, g, name):
    s, k = a.shape
    ns = b.shape[1] // g
    tk = _row_tile(s)
    return _mm(a, b, name=name, dims=TN, grid=(1, g, s // tk),
               a_spec=pl.BlockSpec((tk, k), lambda i, j, kk: (kk, 0)),
               b_spec=pl.BlockSpec((tk, ns), lambda i, j, kk: (kk, j)),
               o_spec=pl.BlockSpec((None, k, ns), lambda i, j, kk: (j, 0, 0)), o_tile=(k, ns),
               out_shape=jax.ShapeDtypeStruct((g, k, ns), F32))


def _mm_tn_full(a, b, name, tm):
    s, m = a.shape
    n = b.shape[1]
    tk = _row_tile(s)
    return _mm(a, b, name=name, dims=TN, grid=(m // tm, 1, s // tk),
               a_spec=pl.BlockSpec((tk, tm), lambda i, j, kk: (kk, i)),
               b_spec=pl.BlockSpec((tk, n), lambda i, j, kk: (kk, 0)),
               o_spec=pl.BlockSpec((tm, n), lambda i, j, kk: (i, 0)), o_tile=(tm, n),
               out_shape=jax.ShapeDtypeStruct((m, n), F32))


def _rms_fwd(x, g, name):
    s, d = x.shape
    ts = _row_tile(s)

    def body(x_ref, g_ref, u_ref):
        xv = x_ref[...]
        u_ref[...] = (xv * _rstd(xv) * g_ref[...]).astype(BF16)

    row = pl.BlockSpec((ts, d), lambda i: (i, 0))
    return pl.pallas_call(
        body, name=name, grid=(s // ts,), in_specs=[row, pl.BlockSpec((1, d), lambda i: (0, 0))],
        out_specs=row, out_shape=jax.ShapeDtypeStruct((s, d), BF16),
        compiler_params=_params(("parallel",)))(x, g)


def _rms_bwd(h, du, g, dh_in, name):
    s, d = h.shape
    ts = _row_tile(s)

    def body(h_ref, du_ref, g_ref, dhin_ref, dh_ref, dhb_ref, gg_ref):
        hv = h_ref[...]
        r = _rstd(hv)
        xhat = hv * r
        duv = du_ref[...]
        dxhat = duv * g_ref[...]
        m = jnp.mean(dxhat * xhat, axis=-1, keepdims=True)
        dh = dhin_ref[...] + r * (dxhat - xhat * m)
        dh_ref[...] = dh
        dhb_ref[...] = dh.astype(BF16)
        part = jnp.sum(duv * xhat, axis=0, keepdims=True)

        @pl.when(pl.program_id(0) == 0)
        def _():
            gg_ref[...] = part

        @pl.when(pl.program_id(0) > 0)
        def _():
            gg_ref[...] += part

    row = pl.BlockSpec((ts, d), lambda i: (i, 0))
    vec = pl.BlockSpec((1, d), lambda i: (0, 0))
    return pl.pallas_call(
        body, name=name, grid=(s // ts,), in_specs=[row, row, vec, row], out_specs=[row, row, vec],
        out_shape=[jax.ShapeDtypeStruct((s, d), F32), jax.ShapeDtypeStruct((s, d), BF16),
                   jax.ShapeDtypeStruct((1, d), F32)],
        compiler_params=_params(("arbitrary",)))(h, du, g, dh_in)


def _head_sum(x, bd):
    return jnp.dot(x, bd, precision=HIGHEST, preferred_element_type=F32)


def _qkv_prep(proj, qg, kg, bd):
    s = proj.shape[0]
    ts = _row_tile(s)

    def body(q_ref, k_ref, v_ref, qg_ref, kg_ref, bd_ref, qs_ref, kh_ref, vb_ref):
        def norm(x, gain):
            ms = _head_sum(x * x, bd_ref[...]) * (1.0 / HEAD_DIM)
            return x * lax.rsqrt(ms + EPS) * gain

        qs_ref[...] = (norm(q_ref[...], qg_ref[...]) * ATTN_SCALE).astype(BF16)
        kh_ref[...] = norm(k_ref[...], kg_ref[...]).astype(BF16)
        vb_ref[...] = v_ref[...].astype(BF16)

    col = lambda c: pl.BlockSpec((ts, ATTN_W), lambda i: (i, c))
    vec = pl.BlockSpec((1, ATTN_W), lambda i: (0, 0))
    out = pl.BlockSpec((ts, ATTN_W), lambda i: (i, 0))
    sds = jax.ShapeDtypeStruct((s, ATTN_W), BF16)
    return pl.pallas_call(
        body, name="qkv_prep", grid=(s // ts,),
        in_specs=[col(0), col(1), col(2), vec, vec, pl.BlockSpec((ATTN_W, ATTN_W), lambda i: (0, 0))],
        out_specs=[out, out, out], out_shape=[sds, sds, sds],
        compiler_params=_params(("parallel",)))(proj, proj, proj, qg, kg, bd)


def _qk_bwd(proj, dqh, dkh, dv, qg, kg, bd):
    s = proj.shape[0]
    ts = _row_tile(s)

    def body(q_ref, k_ref, dqh_ref, dkh_ref, dv_ref, qg_ref, kg_ref, bd_ref, out_ref, gq_ref, gk_ref):
        first = pl.program_id(0) == 0

        def bwd(x, dy, gain, gg_ref):
            ms = _head_sum(x * x, bd_ref[...]) * (1.0 / HEAD_DIM)
            r = lax.rsqrt(ms + EPS)
            xhat = x * r
            dxhat = dy * gain
            m = _head_sum(dxhat * xhat, bd_ref[...]) * (1.0 / HEAD_DIM)
            part = jnp.sum(dy * xhat, axis=0, keepdims=True)

            @pl.when(first)
            def _():
                gg_ref[...] = part

            @pl.when(jnp.logical_not(first))
            def _():
                gg_ref[...] += part

            return r * (dxhat - xhat * m)

        out_ref[:, 0:ATTN_W] = bwd(q_ref[...], dqh_ref[...], qg_ref[...], gq_ref).astype(BF16)
        out_ref[:, ATTN_W:2 * ATTN_W] = bwd(k_ref[...], dkh_ref[...], kg_ref[...], gk_ref).astype(BF16)
        out_ref[:, 2 * ATTN_W:3 * ATTN_W] = dv_ref[...].astype(BF16)

    col = lambda c: pl.BlockSpec((ts, ATTN_W), lambda i: (i, c))
    row = pl.BlockSpec((ts, ATTN_W), lambda i: (i, 0))
    vec = pl.BlockSpec((1, ATTN_W), lambda i: (0, 0))
    return pl.pallas_call(
        body, name="qk_bwd", grid=(s // ts,),
        in_specs=[col(0), col(1), row, row, row, vec, vec, pl.BlockSpec((ATTN_W, ATTN_W), lambda i: (0, 0))],
        out_specs=[pl.BlockSpec((ts, 3 * ATTN_W), lambda i: (i, 0)), vec, vec],
        out_shape=[jax.ShapeDtypeStruct((s, 3 * ATTN_W), BF16), jax.ShapeDtypeStruct((1, ATTN_W), F32),
                   jax.ShapeDtypeStruct((1, ATTN_W), F32)],
        compiler_params=_params(("arbitrary",)))(proj, proj, dqh, dkh, dv, qg, kg, bd)


def _split(x):
    hi = x.astype(BF16)
    return hi, (x - hi.astype(F32)).astype(BF16)


def _tri_dot(parts, tri):
    hi, lo = parts
    return jnp.dot(hi, tri, preferred_element_type=F32) + jnp.dot(lo, tri, preferred_element_type=F32)


def _log_sigmoids(z):
    neg_abs = lax.bitcast_convert_type(lax.bitcast_convert_type(z, jnp.uint32) | jnp.uint32(0x80000000), F32)
    lb = jnp.minimum(z, 0.0) - jnp.log(1.0 + jnp.exp(neg_abs))
    return lb, lb - z


def _head_masks():
    lane = lax.broadcasted_iota(jnp.int32, (1, LANES), 1)
    return [lane < HEAD_DIM, lane >= HEAD_DIM]


def _attn_fwd(qs, kh, vb, tri_excl):
    s = qs.shape[0]
    t = tri_excl.shape[0]
    nq = s // t

    def body(q_ref, k_ref, v_ref, tri_ref, o_ref, ob_ref, acc_ref, c_ref):
        i = pl.program_id(1)
        hmask = _head_masks()
        q = q_ref[...]
        qm = [jnp.where(hm, q, jnp.zeros_like(q)) for hm in hmask]
        acc_ref[...] = jnp.zeros_like(acc_ref)
        c_ref[...] = jnp.zeros_like(c_ref)
        causal = (lax.broadcasted_iota(jnp.int32, (t, t), 1) < lax.broadcasted_iota(jnp.int32, (t, t), 0))

        def tiles(kbs, masked):
            tri = tri_ref[...]
            starts = [pl.multiple_of(kb * t, t) for kb in kbs]
            kblks = [k_ref[pl.ds(k0, t), :] for k0 in starts]
            vblks = [v_ref[pl.ds(k0, t), :] for k0 in starts]
            chains = [(j, h) for j in range(len(kbs)) for h in range(2)]
            carry = [c_ref[h] for h in range(2)]
            pv = [None, None]
            lbs, loms, between = {}, {}, {}
            for step in range(len(chains) + 2):
                if step < len(chains):
                    j, h = chains[step]
                    z = lax.dot_general(qm[h], kblks[j], (NT, ((), ())), preferred_element_type=F32)
                    lbs[step], lom = _log_sigmoids(z)
                    loms[step] = jnp.where(causal, lom, 0.0) if masked else lom
                if 0 <= step - 1 < len(chains):
                    between[step - 1] = _tri_dot(_split(loms[step - 1]), tri)
                if 0 <= step - 2 < len(chains):
                    n = step - 2
                    j, h = chains[n]
                    w = jnp.exp(lbs[n] + between[n] + carry[h])
                    if masked:
                        w = jnp.where(causal, w, 0.0)
                    carry[h] = carry[h] + jnp.sum(loms[n], axis=-1, keepdims=True)
                    part = jnp.dot(w.astype(BF16), vblks[j], preferred_element_type=F32)
                    pv[h] = part if pv[h] is None else pv[h] + part
            for h in range(2):
                c_ref[h] = carry[h]
                acc_ref[h] += pv[h]

        tiles([i], True)

        def loop(step, carry):
            tiles([i - 1 - 2 * step, i - 2 - 2 * step], False)
            return carry

        lax.fori_loop(0, i // 2, loop, 0)

        @pl.when(i % 2 == 1)
        def _():
            tiles([0], False)

        o = jnp.where(hmask[0], acc_ref[0], acc_ref[1])
        o_ref[...] = o
        ob_ref[...] = o.astype(BF16)

    qspec = pl.BlockSpec((t, LANES), lambda hp, i: (i, hp))
    kspec = pl.BlockSpec((s, LANES), lambda hp, i: (0, hp))
    return pl.pallas_call(
        body, name="attn_fwd", grid=(ATTN_W // LANES, nq),
        in_specs=[qspec, kspec, kspec, pl.BlockSpec((t, t), lambda hp, i: (0, 0))],
        out_specs=[qspec, qspec],
        out_shape=[jax.ShapeDtypeStruct((s, ATTN_W), F32), jax.ShapeDtypeStruct((s, ATTN_W), BF16)],
        scratch_shapes=[pltpu.VMEM((2, t, LANES), F32), pltpu.VMEM((2, t, 1), F32)],
        compiler_params=_params(("parallel", "arbitrary")))(qs, kh, vb, tri_excl)


def _attn_bwd(qs, kh, vb, o, dmix, tri_excl, tri_incl):
    s = qs.shape[0]
    t = tri_excl.shape[0]
    nq = s // t

    def body(q_ref, k_ref, v_ref, o_ref, do_ref, te_ref, ti_ref, dq_ref, dk_ref, dv_ref, dqacc_ref, c_ref):
        i = pl.program_id(1)

        @pl.when(i == 0)
        def _():
            dk_ref[...] = jnp.zeros_like(dk_ref)
            dv_ref[...] = jnp.zeros_like(dv_ref)

        hmask = _head_masks()
        q = q_ref[...]
        do = do_ref[...]
        dob = do.astype(BF16)
        prod = dob.astype(F32) * o_ref[...]
        qm =[jnp.where(hm, q, jnp.zeros_like(q)) for hm in hmask]
        dom = [jnp.where(hm, dob, jnp.zeros_like(dob)) for hm in hmask]
        total = [jnp.sum(jnp.where(hm, prod, 0.0), axis=-1, keepdims=True) for hm in hmask]
        dqacc_ref[...] = jnp.zeros_like(dqacc_ref)
        c_ref[...] = jnp.zeros_like(c_ref)
        causal = (lax.broadcasted_iota(jnp.int32, (t, t), 1) < lax.broadcasted_iota(jnp.int32, (t, t), 0))

        def tiles(kbs, masked):
            te = te_ref[...]
            ti = ti_ref[...]
            starts = [pl.multiple_of(kb * t, t) for kb in kbs]
            kblks = [k_ref[pl.ds(k0, t), :] for k0 in starts]
            vblks = [v_ref[pl.ds(k0, t), :] for k0 in starts]
            chains = [(j, h) for j in range(len(kbs)) for h in range(2)]
            c_lom = [c_ref[2 * h] for h in range(2)]
            c_g = [c_ref[2 * h + 1] for h in range(2)]
            lbs, loms, dws, between, wbs, gs, g_after = {}, {}, {}, {}, {}, {}, {}
            dq = [None, None]
            dk = [None] * len(kbs)
            dv = [None] * len(kbs)
            add = lambda acc, part: part if acc is None else acc + part
            for step in range(len(chains) + 3):
                if step < len(chains):
                    j, h = chains[step]
                    z = lax.dot_general(qm[h], kblks[j], (NT, ((), ())), preferred_element_type=F32)
                    dws[step] = lax.dot_general(dom[h], vblks[j], (NT, ((), ())), preferred_element_type=F32)
                    lbs[step], lom = _log_sigmoids(z)
                    loms[step] = jnp.where(causal, lom, 0.0) if masked else lom
                if 0 <= step - 1 < len(chains):
                    between[step - 1] = _tri_dot(_split(loms[step - 1]), te)
                if 0 <= step - 2 < len(chains):
                    n = step - 2
                    h = chains[n][1]
                    w = jnp.exp(lbs[n] + between[n] + c_lom[h])
                    if masked:
                        w = jnp.where(causal, w, 0.0)
                    c_lom[h] = c_lom[h] + jnp.sum(loms[n], axis=-1, keepdims=True)
                    wbs[n] = w.astype(BF16)
                    gs[n] = dws[n] * wbs[n].astype(F32)
                    g_after[n] = _tri_dot(_split(gs[n]), ti)
                if 0 <= step - 3 < len(chains):
                    n = step - 3
                    j, h = chains[n]
                    beta = jnp.exp(lbs[n])
                    dz = gs[n] * (1.0 - beta) - beta * (total[h] - (g_after[n] + c_g[h]))
                    if masked:
                        dz = jnp.where(causal, dz, 0.0)
                    c_g[h] = c_g[h] + jnp.sum(gs[n], axis=-1, keepdims=True)
                    dzb = dz.astype(BF16)
                    dq[h] = add(dq[h], jnp.dot(dzb, kblks[j], preferred_element_type=F32))
                    dk[j] = add(dk[j], lax.dot_general(dzb, qm[h], (TN, ((), ())), preferred_element_type=F32))
                    dv[j] = add(dv[j], lax.dot_general(wbs[n], dom[h], (TN, ((), ())), preferred_element_type=F32))
            for h in range(2):
                c_ref[2 * h] = c_lom[h]
                c_ref[2 * h + 1] = c_g[h]
                dqacc_ref[h] += dq[h]
            for j, k0 in enumerate(starts):
                dk_ref[pl.ds(k0, t), :] += dk[j]
                dv_ref[pl.ds(k0, t), :] += dv[j]

        tiles([i], True)

        def loop(step, carry):
            tiles([i - 1 - 2 * step, i - 2 - 2 * step], False)
            return carry

        lax.fori_loop(0, i // 2, loop, 0)

        @pl.when(i % 2 == 1)
        def _():
            tiles([0], False)

        dq_ref[...] = jnp.where(hmask[0], dqacc_ref[0], dqacc_ref[1]) * ATTN_SCALE

    qspec = pl.BlockSpec((t, LANES), lambda hp, i: (i, hp))
    kspec = pl.BlockSpec((s, LANES), lambda hp, i: (0, hp))
    tspec = pl.BlockSpec((t, t), lambda hp, i: (0, 0))
    sds = jax.ShapeDtypeStruct((s, ATTN_W), F32)
    return pl.pallas_call(
        body, name="attn_bwd", grid=(ATTN_W // LANES, nq),
        in_specs=[qspec, kspec, kspec, qspec, qspec, tspec, tspec],
        out_specs=[qspec, kspec, kspec], out_shape=[sds, sds, sds],
        scratch_shapes=[pltpu.VMEM((2, t, LANES), F32), pltpu.VMEM((4, t, 1), F32)],
        compiler_params=_params(("parallel", "arbitrary")))(qs, kh, vb, o, dmix, tri_excl, tri_incl)


CONV_ROWS = 64
CONV_COLS = 256


def _taps(src_ref, w_ref, n_taps, first_row, rows, reverse=False):
    width = src_ref.shape[1]
    cols = min(CONV_COLS, width)
    out = []
    for r0 in range(0, rows, CONV_ROWS):
        for c0 in range(0, width, cols):
            acc = jnp.zeros((CONV_ROWS, cols), F32)
            for k in range(n_taps):
                off = (n_taps - 1 - k) if reverse else k
                acc = acc + w_ref[k:k + 1, c0:c0 + cols] * src_ref[first_row + r0 + off:first_row + r0 + off + CONV_ROWS,
                                                                   c0:c0 + cols]
            out.append(((r0, c0), acc))
    return out


def _conv_fwd(proj, dw_w, dw_b, ln_g, ln_b):
    s = proj.shape[0]
    ts = _row_tile(s)
    hb = ts // CONV_HALO

    def body(a_ref, g_ref, ha_ref, hg_ref, w_ref, b_ref, lg_ref, lb_ref, c1_ref, c3_ref, pad_ref):
        i = pl.program_id(0)
        halo = ha_ref[...] * _sigmoid(hg_ref[...])
        pad_ref[0:CONV_HALO, :] = jnp.where(i > 0, halo, 0.0)
        pad_ref[CONV_HALO:, :] = a_ref[...] * _sigmoid(g_ref[...])
        first = CONV_HALO - (CONV_K - 1)
        for (r0, c0), acc in _taps(pad_ref, w_ref, CONV_K, first, ts):
            c1_ref[r0:r0 + CONV_ROWS, c0:c0 + acc.shape[1]] = acc + b_ref[:, c0:c0 + acc.shape[1]]
        c1 = c1_ref[...]
        xc = c1 - jnp.mean(c1, axis=-1, keepdims=True)
        c2 = xc * _rstd(xc) * lg_ref[...] + lb_ref[...]
        c3_ref[...] = (c2 * _sigmoid(c2)).astype(BF16)

    cur = lambda c: pl.BlockSpec((ts, CONV_W), lambda i: (i, c))
    halo = lambda c: pl.BlockSpec((CONV_HALO, CONV_W), lambda i: (jnp.maximum(i * hb - 1, 0), c))
    vec = pl.BlockSpec((1, CONV_W), lambda i: (0, 0))
    row = pl.BlockSpec((ts, CONV_W), lambda i: (i, 0))
    return pl.pallas_call(
        body, name="conv_fwd", grid=(s // ts,),
        in_specs=[cur(3), cur(4), halo(3), halo(4), pl.BlockSpec((CONV_HALO, CONV_W), lambda i: (0, 0)), vec, vec, vec],
        out_specs=[row, row],
        out_shape=[jax.ShapeDtypeStruct((s, CONV_W), F32), jax.ShapeDtypeStruct((s, CONV_W), BF16)],
        scratch_shapes=[pltpu.VMEM((ts + CONV_HALO, CONV_W), F32)],
        compiler_params=_params(("parallel",)))(proj, proj, proj, proj, dw_w, dw_b, ln_g, ln_b)


def _conv_bwd_ln(dmix, c1, ln_g, ln_b):
    s = c1.shape[0]
    ts = _row_tile(s)

    def body(d_ref, c1_ref, lg_ref, lb_ref, dc1_ref, glg_ref, glb_ref, gb_ref):
        c1v = c1_ref[...]
        xc = c1v - jnp.mean(c1v, axis=-1, keepdims=True)
        r = _rstd(xc)
        xhat = xc * r
        c2 = xhat * lg_ref[...] + lb_ref[...]
        sg = _sigmoid(c2)
        dc2 = d_ref[...] * (sg * (1.0 + c2 * (1.0 - sg)))
        dxhat = dc2 * lg_ref[...]
        dc1 = r * (dxhat - jnp.mean(dxhat, axis=-1, keepdims=True)
                   - xhat * jnp.mean(dxhat * xhat, axis=-1, keepdims=True))
        dc1_ref[...] = dc1
        parts = [(glg_ref, jnp.sum(dc2 * xhat, axis=0, keepdims=True)),
                 (glb_ref, jnp.sum(dc2, axis=0, keepdims=True)),
                 (gb_ref, jnp.sum(dc1, axis=0, keepdims=True))]

        @pl.when(pl.program_id(0) == 0)
        def _():
            for ref, part in parts:
                ref[...] = part

        @pl.when(pl.program_id(0) > 0)
        def _():
            for ref, part in parts:
                ref[...] += part

    row = pl.BlockSpec((ts, CONV_W), lambda i: (i, 0))
    vec = pl.BlockSpec((1, CONV_W), lambda i: (0, 0))
    vsd = jax.ShapeDtypeStruct((1, CONV_W), F32)
    return pl.pallas_call(
        body, name="conv_bwd_ln", grid=(s // ts,),
        in_specs=[pl.BlockSpec((ts, CONV_W), lambda i: (i, 1)), row, vec, vec],
        out_specs=[row, vec, vec, vec],
        out_shape=[jax.ShapeDtypeStruct((s, CONV_W), F32), vsd, vsd, vsd],
        compiler_params=_params(("arbitrary",)))(dmix, c1, ln_g, ln_b)


def _conv_bwd_taps(proj, dc1, dw_w):
    s = proj.shape[0]
    ts = _row_tile(s)
    hb = ts // CONV_HALO
    last = s // CONV_HALO - 1
    nsteps = s // ts

    def body(a_ref, g_ref, ha_ref, hg_ref, d_ref, hd_ref, w_ref, out_ref, gw_ref, pad_ref, dpad_ref):
        i = pl.program_id(0)
        av = a_ref[...]
        sg = _sigmoid(g_ref[...])
        halo = ha_ref[...] * _sigmoid(hg_ref[...])
        pad_ref[0:CONV_HALO, :] = jnp.where(i > 0, halo, 0.0)
        pad_ref[CONV_HALO:, :] = av * sg
        dpad_ref[0:ts, :] = d_ref[...]
        dpad_ref[ts:, :] = jnp.where(i < nsteps - 1, hd_ref[...], 0.0)

        @pl.when(i == 0)
        def _():
            gw_ref[...] = jnp.zeros_like(gw_ref)

        first = CONV_HALO - (CONV_K - 1)
        for k in range(CONV_K):
            gw_ref[k:k + 1, :] += jnp.sum(d_ref[...] * pad_ref[first + k:first + k + ts, :], axis=0, keepdims=True)
        for (r0, c0), dc0 in _taps(dpad_ref, w_ref, CONV_K, 0, ts, reverse=True):
            cs = slice(c0, c0 + dc0.shape[1])
            a_c = a_ref[r0:r0 + CONV_ROWS, cs]
            sg_c = _sigmoid(g_ref[r0:r0 + CONV_ROWS, cs])
            out_ref[r0:r0 + CONV_ROWS, cs] = (dc0 * sg_c).astype(BF16)
            out_ref[r0:r0 + CONV_ROWS, CONV_W + c0:CONV_W + c0 + dc0.shape[1]] = (
                dc0 * a_c * sg_c * (1.0 - sg_c)).astype(BF16)

    cur = lambda c: pl.BlockSpec((ts, CONV_W), lambda i: (i, c))
    halo = lambda c: pl.BlockSpec((CONV_HALO, CONV_W), lambda i: (jnp.maximum(i * hb - 1, 0), c))
    row = pl.BlockSpec((ts, CONV_W), lambda i: (i, 0))
    nxt = pl.BlockSpec((CONV_HALO, CONV_W), lambda i: (jnp.minimum((i + 1) * hb, last), 0))
    wspec = pl.BlockSpec((CONV_HALO, CONV_W), lambda i: (0, 0))
    return pl.pallas_call(
        body, name="conv_bwd_taps", grid=(nsteps,),
        in_specs=[cur(3), cur(4), halo(3), halo(4), row, nxt, wspec],
        out_specs=[pl.BlockSpec((ts, 2 * CONV_W), lambda i: (i, 0)), wspec],
        out_shape=[jax.ShapeDtypeStruct((s, 2 * CONV_W), BF16), jax.ShapeDtypeStruct((CONV_HALO, CONV_W), F32)],
        scratch_shapes=[pltpu.VMEM((ts + CONV_HALO, CONV_W), F32), pltpu.VMEM((ts + CONV_HALO, CONV_W), F32)],
        compiler_params=_params(("arbitrary",)))(proj, proj, proj, proj, dc1, dc1, dw_w)


SQRT_HALF = 0.7071067811865476
INV_SQRT_2PI = 0.3989422804014327


def _gelu_parts(x):
    cdf = 0.5 * (1.0 + lax.erf(x * SQRT_HALF))
    return x * cdf, cdf + x * (INV_SQRT_2PI * jnp.exp(-0.5 * x * x))


def _ffn_tile(dff):
    return dff // 2


def _ffn_gate2(pad_ref, w_ref, b_ref, ts):
    first = FFN_HALO - (FFN_K - 1)
    g2 = b_ref[...] + w_ref[0:1, :] * pad_ref[first:first + ts, :]
    for k in range(1, FFN_K):
        g2 = g2 + w_ref[k:k + 1, :] * pad_ref[first + k:first + k + ts, :]
    return g2


def _ffn_act(up, fw, fb):
    s = up.shape[0]
    dff = up.shape[1] // 2
    tc = _ffn_tile(dff)
    nj = dff // tc
    ts = _row_tile(s) // 2
    hb = ts // FFN_HALO

    def body(g_ref, v_ref, hg_ref, w_ref, b_ref, act_ref, pad_ref):
        i = pl.program_id(0)
        pad_ref[0:FFN_HALO, :] = jnp.where(i > 0, hg_ref[...], 0.0)
        pad_ref[FFN_HALO:, :] = g_ref[...]
        gelu, _ = _gelu_parts(_ffn_gate2(pad_ref, w_ref, b_ref, ts))
        act_ref[...] = (gelu * v_ref[...]).astype(BF16)

    return pl.pallas_call(
        body, name="ffn_act", grid=(s // ts, nj),
        in_specs=[pl.BlockSpec((ts, tc), lambda i, j: (i, j)), pl.BlockSpec((ts, tc), lambda i, j: (i, j + nj)),
                  pl.BlockSpec((FFN_HALO, tc), lambda i, j: (jnp.maximum(i * hb - 1, 0), j)),
                  pl.BlockSpec((FFN_HALO, tc), lambda i, j: (0, j)), pl.BlockSpec((1, tc), lambda i, j: (0, j))],
        out_specs=pl.BlockSpec((ts, tc), lambda i, j: (i, j)),
        out_shape=jax.ShapeDtypeStruct((s, dff), BF16),
        scratch_shapes=[pltpu.VMEM((ts + FFN_HALO, tc), F32)],
        compiler_params=_params(("parallel", "parallel")))(up, up, up, fw, fb)


def _ffn_bwd_act(dact, up, fw, fb):
    s = up.shape[0]
    dff = up.shape[1] // 2
    tc = _ffn_tile(dff)
    nj = dff // tc
    ts = _row_tile(s) // 2
    hb = ts // FFN_HALO

    def body(d_ref, g_ref, v_ref, hg_ref, w_ref, b_ref, dg2_ref, dval_ref, gw_ref, gb_ref, pad_ref):
        i = pl.program_id(1)
        pad_ref[0:FFN_HALO, :] = jnp.where(i > 0, hg_ref[...], 0.0)
        pad_ref[FFN_HALO:, :] = g_ref[...]
        gelu, dgelu = _gelu_parts(_ffn_gate2(pad_ref, w_ref, b_ref, ts))
        dactv = d_ref[...]
        dval_ref[...] = (dactv * gelu).astype(BF16)
        dg2 = dactv * v_ref[...] * dgelu
        dg2_ref[...] = dg2

        @pl.when(i == 0)
        def _():
            gw_ref[...] = jnp.zeros_like(gw_ref)
            gb_ref[...] = jnp.zeros_like(gb_ref)

        gb_ref[...] += jnp.sum(dg2, axis=0, keepdims=True)
        first = FFN_HALO - (FFN_K - 1)
        for k in range(FFN_K):
            gw_ref[k:k + 1, :] += jnp.sum(dg2 * pad_ref[first + k:first + k + ts, :], axis=0, keepdims=True)

    blk = pl.BlockSpec((ts, tc), lambda j, i: (i, j))
    wspec = pl.BlockSpec((FFN_HALO, tc), lambda j, i: (0, j))
    bspec = pl.BlockSpec((1, tc), lambda j, i: (0, j))
    return pl.pallas_call(
        body, name="ffn_bwd_act", grid=(nj, s // ts),
        in_specs=[blk, blk, pl.BlockSpec((ts, tc), lambda j, i: (i, j + nj)),
                  pl.BlockSpec((FFN_HALO, tc), lambda j, i: (jnp.maximum(i * hb - 1, 0), j)), wspec, bspec],
        out_specs=[blk, blk, wspec, bspec],
        out_shape=[jax.ShapeDtypeStruct((s, dff), F32), jax.ShapeDtypeStruct((s, dff), BF16),
                   jax.ShapeDtypeStruct((FFN_HALO, dff), F32), jax.ShapeDtypeStruct((1, dff), F32)],
        scratch_shapes=[pltpu.VMEM((ts + FFN_HALO, tc), F32)],
        compiler_params=_params(("parallel", "arbitrary")))(dact, up, up, up, fw, fb)


def _ffn_bwd_conv(dg2, fw):
    s, dff = dg2.shape
    tc = _ffn_tile(dff)
    ts = _row_tile(s) // 2
    hb = ts // FFN_HALO
    last = s // FFN_HALO - 1
    nsteps = s // ts

    def body(d_ref, hd_ref, w_ref, out_ref, pad_ref):
        i = pl.program_id(0)
        pad_ref[0:ts, :] = d_ref[...]
        pad_ref[ts:, :] = jnp.where(i < nsteps - 1, hd_ref[...], 0.0)
        dg = w_ref[0:1, :] * pad_ref[FFN_K - 1:FFN_K - 1 + ts, :]
        for k in range(1, FFN_K):
            dg = dg + w_ref[k:k + 1, :] * pad_ref[FFN_K - 1 - k:FFN_K - 1 - k + ts, :]
        out_ref[...] = dg.astype(BF16)

    blk = pl.BlockSpec((ts, tc), lambda i, j: (i, j))
    return pl.pallas_call(
        body, name="ffn_bwd_conv", grid=(nsteps, dff // tc),
        in_specs=[blk, pl.BlockSpec((FFN_HALO, tc), lambda i, j: (jnp.minimum((i + 1) * hb, last), j)),
                  pl.BlockSpec((FFN_HALO, tc), lambda i, j: (0, j))],
        out_specs=blk, out_shape=jax.ShapeDtypeStruct((s, dff), BF16),
        scratch_shapes=[pltpu.VMEM((ts + FFN_HALO, tc), F32)],
        compiler_params=_params(("parallel", "parallel")))(dg2, dg2, fw)


def _ple_loss(h2, zg, pp, target):
    s, d = h2.shape
    ts = _row_tile(s)

    def body(h_ref, z_ref, p_ref, t_ref, dh_ref, dpp_ref, dz_ref, loss_ref):
        pg = _sigmoid(z_ref[...])
        ppv = p_ref[...]
        diff = h_ref[...] + pg * ppv - t_ref[...]
        dh = diff * (1.0 / d)
        dh_ref[...] = dh
        dpp_ref[...] = (dh * pg).astype(BF16)
        dz_ref[...] = (dh * ppv * pg * (1.0 - pg)).astype(BF16)
        part = jnp.sum(jnp.sum(diff * diff, axis=0, keepdims=True), axis=1, keepdims=True)

        @pl.when(pl.program_id(0) == 0)
        def _():
            loss_ref[...] = jnp.zeros_like(loss_ref)

        loss_ref[...] += jnp.broadcast_to(part, loss_ref.shape)

    row = pl.BlockSpec((ts, d), lambda i: (i, 0))
    return pl.pallas_call(
        body, name="ple_loss", grid=(s // ts,), in_specs=[row, row, row, row],
        out_specs=[row, row, row, pl.BlockSpec((8, LANES), lambda i: (0, 0))],
        out_shape=[jax.ShapeDtypeStruct((s, d), F32), jax.ShapeDtypeStruct((s, d), BF16),
                   jax.ShapeDtypeStruct((s, d), BF16), jax.ShapeDtypeStruct((8, LANES), F32)],
        compiler_params=_params(("arbitrary",)))(h2, zg, pp, target)


def _local_step(x, p, target, w):
    s = x.shape[0]
    t = min(256, s)
    tri = jnp.tril(jnp.ones((t, t), F32))
    tri_incl = tri.astype(BF16)
    tri_excl = jnp.tril(jnp.ones((t, t), F32), -1).astype(BF16)
    bd = jnp.kron(jnp.eye(N_HEADS, dtype=F32), jnp.ones((HEAD_DIM, HEAD_DIM), F32))
    qg = jnp.tile(w["q_gain"], (1, N_HEADS))
    kg = jnp.tile(w["k_gain"], (1, N_HEADS))
    dw_w = jnp.pad(w["dw_w"], ((0, CONV_HALO - CONV_K), (0, 0)))
    fw = jnp.pad(w["ffn_conv_w"], ((0, FFN_HALO - FFN_K), (0, 0)))
    pb = p.astype(BF16)

    u1 = _rms_fwd(x, w["g_mix"], "rms_mix")
    proj = _mm_nn_sharded(u1, w["w_in"], "mm_in")
    qs, kh, vb = _qkv_prep(proj, qg, kg, bd)
    o, ob = _attn_fwd(qs, kh, vb, tri_excl)
    c1, c3 = _conv_fwd(proj, dw_w, w["dw_b"], w["conv_ln_g"], w["conv_ln_b"])
    mix = jnp.concatenate([ob, c3], axis=1)
    h1 = _mm_nn_full(mix, w["w_out"], "mm_out", res=x)
    u2 = _rms_fwd(h1, w["g_ffn"], "rms_ffn")
    up = _mm_nn_sharded(u2, w["w_up"], "mm_up")
    act = _ffn_act(up, fw, w["ffn_conv_b"])
    h2 = _mm_nn_full(act, w["w_down"], "mm_down", res=h1)
    u3 = _rms_fwd(h2, w["g_ple"], "rms_ple")
    zg = _mm_nn_full(u3, w["w_ple_gate"], "mm_ple_gate")
    pp = _mm_nn_sharded(pb, w["w_ple_proj"], "mm_ple_proj")
    dh3, dpp, dz, sq = _ple_loss(h2, zg, pp, target)

    big = {}
    small = {}
    big["w_ple_proj"] = _mm_tn_sharded(pb, dpp, N_CHIPS, "mm_g_ple_proj")
    big["w_ple_gate"] = _mm_tn_full(u3, dz, "mm_g_ple_gate", tm=u3.shape[1])
    du3 = _mm_nt_full(dz, w["w_ple_gate"], "mm_d_ple_gate")
    dh2, dh2b, small["g_ple"] = _rms_bwd(h2, du3, w["g_ple"], dh3, "rms_ple_bwd")
    big["w_down"] = _mm_tn_full(act, dh2b, "mm_g_down", tm=act.shape[1] // 2)
    dact = _mm_nt_full(dh2b, w["w_down"], "mm_d_down")
    dg2, dval, gfw, small["ffn_conv_b"] = _ffn_bwd_act(dact, up, fw, w["ffn_conv_b"])
    small["ffn_conv_w"] = gfw[:FFN_K]
    dgate = _ffn_bwd_conv(dg2, fw)
    dup = jnp.concatenate([dgate, dval], axis=1)
    big["w_up"] = _mm_tn_sharded(u2, dup, N_CHIPS, "mm_g_up")
    du2 = _mm_nt_sharded(dup, w["w_up"], "mm_d_up")
    dh1, dh1b, small["g_ffn"] = _rms_bwd(h1, du2, w["g_ffn"], dh2, "rms_ffn_bwd")
    big["w_out"] = _mm_tn_full(mix, dh1b, "mm_g_out", tm=mix.shape[1])
    dmix = _mm_nt_full(dh1b, w["w_out"], "mm_d_out")
    dc1, small["conv_ln_g"], small["conv_ln_b"], small["dw_b"] = _conv_bwd_ln(dmix, c1, w["conv_ln_g"], w["conv_ln_b"])
    dcacg, gdw = _conv_bwd_taps(proj, dc1, dw_w)
    small["dw_w"] = gdw[:CONV_K]
    dqh, dkh, dv = _attn_bwd(qs, kh, vb, o, dmix, tri_excl, tri_incl)
    dqkv, gq, gk = _qk_bwd(proj, dqh, dkh, dv, qg, kg, bd)
    small["q_gain"] = gq.reshape(N_HEADS, HEAD_DIM).sum(axis=0, keepdims=True)
    small["k_gain"] = gk.reshape(N_HEADS, HEAD_DIM).sum(axis=0, keepdims=True)
    dproj = jnp.concatenate([dqkv, dcacg], axis=1)
    big["w_in"] = _mm_tn_sharded(u1, dproj, N_CHIPS, "mm_g_in")
    du1 = _mm_nt_sharded(dproj, w["w_in"], "mm_d_in")
    grad_x, _, small["g_mix"] = _rms_bwd(x, du1, w["g_mix"], dh1, "rms_mix_bwd")
    return sq[0, 0], grad_x, big, small


def _position():
    x, y, c = lax.axis_index("x"), lax.axis_index("y"), lax.axis_index("c")
    return x, y, c, [(1 - x, y), (x, 1 - y), (1 - x, 1 - y)]


def _remote(src, dst, send_sems, recv_sems, k, to):
    return pltpu.make_async_remote_copy(src_ref=src, dst_ref=dst, send_sem=send_sems.at[k], recv_sem=recv_sems.at[k],
                                        device_id=to, device_id_type=MESH)


def _gather_chips(shards):
    n = len(shards)

    def body(*refs):
        ins, outs = refs[:n], refs[n:2 * n]
        send_sems, recv_sems, local_sems = refs[2 * n:]
        x, y, c, chips = _position()
        me = 2 * x + y
        local = [pltpu.make_async_copy(ins[a], outs[a].at[me], local_sems.at[a]) for a in range(n)]
        for cp in local:
            cp.start()
        sends = [_remote(ins[a], outs[a].at[me], send_sems, recv_sems, 3 * a + j, (px, py, c))
                 for a in range(n) for j, (px, py) in enumerate(chips)]
        for cp in sends:
            cp.start()
        for a in range(n):
            for j, (px, py) in enumerate(chips):
                _remote(ins[a], outs[a].at[2 * px + py], send_sems, recv_sems, 3 * a + j, (px, py, c)).wait_recv()
        for cp in sends:
            cp.wait_send()
        for cp in local:
            cp.wait()

    return pl.pallas_call(
        body, name="gather_weights", in_specs=[ANY] * n, out_specs=[ANY] * n,
        out_shape=[jax.ShapeDtypeStruct((N_CHIPS,) + a.shape, a.dtype) for a in shards],
        scratch_shapes=[pltpu.SemaphoreType.DMA((3 * n,)), pltpu.SemaphoreType.DMA((3 * n,)),
                        pltpu.SemaphoreType.DMA((n,))])(*shards)


def _scatter_grads(grads, small):
    n = len(grads)

    def body(*refs):
        gin, sin = refs[:n], refs[n]
        gout, sout = refs[n + 1:2 * n + 1], refs[2 * n + 1]
        send_sems, recv_sems, local_sems, small_send, small_recv = refs[2 * n + 2:]
        x, y, c, chips = _position()
        me = 2 * x + y
        dev = 4 * x + 2 * y + c
        flip = lambda v, bit: 1 - v if bit else v
        others = [(flip(x, k & 4), flip(y, k & 2), flip(c, k & 1)) for k in range(1, N_DEV)]
        local = [pltpu.make_async_copy(gin[a].at[me], gout[a].at[me], local_sems.at[a]) for a in range(n)]
        local.append(pltpu.make_async_copy(sin, sout.at[dev], local_sems.at[n]))
        for cp in local:
            cp.start()
        sends = [_remote(gin[a].at[2 * px + py], gout[a].at[me], send_sems, recv_sems, 3 * a + j, (px, py, c))
                 for a in range(n) for j, (px, py) in enumerate(chips)]
        sends += [_remote(sin, sout.at[dev], small_send, small_recv, k, peer) for k, peer in enumerate(others)]
        for cp in sends:
            cp.start()
        for a in range(n):
            for j, (px, py) in enumerate(chips):
                _remote(gin[a].at[me], gout[a].at[2 * px + py], send_sems, recv_sems, 3 * a + j, (px, py, c)).wait_recv()
        for k, (px, py, pc) in enumerate(others):
            _remote(sin, sout.at[4 * px + 2 * py + pc], small_send, small_recv, k, (px, py, pc)).wait_recv()
        for cp in sends:
            cp.wait_send()
        for cp in local:
            cp.wait()

    return pl.pallas_call(
        body, name="scatter_grads", in_specs=[ANY] * (n + 1), out_specs=[ANY] * (n + 1),
        out_shape=[jax.ShapeDtypeStruct(g.shape, g.dtype) for g in grads]
        + [jax.ShapeDtypeStruct((N_DEV,) + small.shape, small.dtype)],
        scratch_shapes=[pltpu.SemaphoreType.DMA((3 * n,)), pltpu.SemaphoreType.DMA((3 * n,)),
                        pltpu.SemaphoreType.DMA((n + 1,)), pltpu.SemaphoreType.DMA((N_DEV - 1,)),
                        pltpu.SemaphoreType.DMA((N_DEV - 1,))])(*grads, small)


def _swap_sibling(arrs):
    n = len(arrs)

    def body(*refs):
        ins, outs = refs[:n], refs[n:2 * n]
        send_sems, recv_sems = refs[2 * n:]
        x, y, c, _ = _position()
        copies = [_remote(ins[a], outs[a], send_sems, recv_sems, a, (x, y, 1 - c)) for a in range(n)]
        for cp in copies:
            cp.start()
        for cp in copies:
            cp.wait_recv()
        for cp in copies:
            cp.wait_send()

    return pl.pallas_call(
        body, name="swap_sibling", in_specs=[ANY] * n, out_specs=[ANY] * n,
        out_shape=[jax.ShapeDtypeStruct(a.shape, a.dtype) for a in arrs],
        scratch_shapes=[pltpu.SemaphoreType.DMA((n,)), pltpu.SemaphoreType.DMA((n,))])(*arrs)


def _elem_tile(rows):
    return 128 if rows % 128 == 0 else (64 if rows % 64 == 0 else rows)


def _sum_slots(a, name):
    g, r, c = a.shape
    tr = _elem_tile(r)

    def body(a_ref, o_ref):
        acc = a_ref[0]
        for k in range(1, g):
            acc = acc + a_ref[k]
        o_ref[...] = acc

    return pl.pallas_call(
        body, name=name, grid=(r // tr,), in_specs=[pl.BlockSpec((g, tr, c), lambda i: (0, i, 0))],
        out_specs=pl.BlockSpec((tr, c), lambda i: (i, 0)), out_shape=jax.ShapeDtypeStruct((r, c), a.dtype),
        compiler_params=_params(("parallel",)))(a)


def _adamw(wt, ga, gb, m, v, name):
    r, c = wt.shape
    tr = _elem_tile(r)
    two = gb is not None

    def body(*refs):
        if two:
            w_ref, ga_ref, gb_ref, m_ref, v_ref, g_out, d_out, m_out, v_out = refs
            g = ga_ref[...] + gb_ref[...]
        else:
            w_ref, ga_ref, m_ref, v_ref, g_out, d_out, m_out, v_out = refs
            g = ga_ref[...]
        mn = ADAM_B1 * m_ref[...] + (1.0 - ADAM_B1) * g
        vn = ADAM_B2 * v_ref[...] + (1.0 - ADAM_B2) * (g * g)
        m_hat = mn / (1.0 - ADAM_B1 ** ADAM_STEP)
        v_hat = vn / (1.0 - ADAM_B2 ** ADAM_STEP)
        g_out[...] = g
        d_out[...] = -ADAM_LR * (m_hat / (jnp.sqrt(v_hat) + ADAM_EPS) + ADAM_WD * w_ref[...])
        m_out[...] = mn
        v_out[...] = vn

    blk = pl.BlockSpec((tr, c), lambda i: (i, 0))
    args = [wt, ga] + ([gb] if two else []) + [m, v]
    sds = jax.ShapeDtypeStruct((r, c), F32)
    return pl.pallas_call(
        body, name=name, grid=(r // tr,), in_specs=[blk] * len(args), out_specs=[blk] * 4, out_shape=[sds] * 4,
        compiler_params=_params(("parallel",)))(*args)


def _pack(arrs, rows):
    flat = jnp.concatenate([a.reshape(-1) for a in arrs])
    return jnp.pad(flat, (0, rows * LANES - flat.shape[0])).reshape(rows, LANES)


def _unpack(buf, shapes):
    flat = buf.reshape(-1)
    out, off = [], 0
    for shp in shapes:
        size = 1
        for d in shp:
            size *= d
        out.append(flat[off:off + size].reshape(shp))
        off += size
    return out


BIG = ["w_in", "w_out", "w_up", "w_down", "w_ple_gate", "w_ple_proj"]
COL_SHARDED = ["w_in", "w_up", "w_ple_proj"]
SMALL_REPL = ["g_mix", "q_gain", "k_gain", "dw_b", "conv_ln_g", "conv_ln_b", "g_ffn", "ffn_conv_b", "g_ple"]
SMALL_SHARDED = ["dw_w", "ffn_conv_w"]
WEIGHTS = ["g_mix", "w_in", "q_gain", "k_gain", "dw_w", "dw_b", "conv_ln_g", "conv_ln_b", "w_out", "g_ffn", "w_up",
           "ffn_conv_w", "ffn_conv_b", "w_down", "g_ple", "w_ple_gate", "w_ple_proj"]


def _rows_for(n_elems):
    return -(-n_elems // (8 * LANES)) * 8


def kernel(x, p, g_mix, w_in, q_gain, k_gain, dw_w, dw_b, conv_ln_g, conv_ln_b, w_out, g_ffn, w_up, ffn_conv_w, ffn_conv_b, w_down, g_ple, w_ple_gate, w_ple_proj, loss_target, m_g_mix, m_w_in, m_q_gain, m_k_gain, m_dw_w, m_dw_b, m_conv_ln_g, m_conv_ln_b, m_w_out, m_g_ffn, m_w_up, m_ffn_conv_w, m_ffn_conv_b, m_w_down, m_g_ple, m_w_ple_gate, m_w_ple_proj, v_g_mix, v_w_in, v_q_gain, v_k_gain, v_dw_w, v_dw_b, v_conv_ln_g, v_conv_ln_b, v_w_out, v_g_ffn, v_w_up, v_ffn_conv_w, v_ffn_conv_b, v_w_down, v_g_ple, v_w_ple_gate, v_w_ple_proj):
    given = dict(locals())
    strip = lambda n, a: a if n in SMALL_REPL else a[0]
    wts = {n: strip(n, given[n]) for n in WEIGHTS}
    mom = {n: strip(n, given["m_" + n]) for n in WEIGHTS}
    var = {n: strip(n, given["v_" + n]) for n in WEIGHTS}
    chip = 2 * lax.axis_index("x") + lax.axis_index("y")

    small_shard_shapes = [wts[n].shape for n in SMALL_SHARDED]
    filt_rows = _rows_for(sum(wts[n].size for n in SMALL_SHARDED))
    filt = _pack([wts[n] for n in SMALL_SHARDED], filt_rows)
    gathered = _gather_chips([wts[n].astype(BF16) for n in BIG] + [filt])
    full = {n: wts[n] for n in SMALL_REPL}
    for n, g in zip(BIG, gathered):
        full[n] = g if n in COL_SHARDED else g.reshape(-1, g.shape[-1])
    per_chip = [_unpack(gathered[-1][k], small_shard_shapes) for k in range(N_CHIPS)]
    for idx, n in enumerate(SMALL_SHARDED):
        full[n] = jnp.concatenate([per_chip[k][idx] for k in range(N_CHIPS)], axis=1)

    sq, grad_x, big, small = _local_step(x[0], p[0, 0], loss_target[0], full)
    loss = lax.psum(sq * (0.5 / x.shape[-1]), ("x", "y", "c"))

    small_names = SMALL_REPL + SMALL_SHARDED
    small_shapes = [small[n].shape for n in small_names]
    small_rows = _rows_for(sum(small[n].size for n in small_names))
    shard_major = [big[n] if n in COL_SHARDED else big[n].reshape(N_CHIPS, -1, big[n].shape[-1]) for n in BIG]
    *slots, small_slots = _scatter_grads(shard_major, _pack([small[n] for n in small_names], small_rows))
    mine = [_sum_slots(sl, "sum_" + n) for sl, n in zip(slots, BIG)]
    theirs = _swap_sibling(mine)
    small_sum = dict(zip(small_names, _unpack(_sum_slots(small_slots, "sum_small"), small_shapes)))

    outs = {}
    for n, ga, gb in zip(BIG, mine, theirs):
        outs[n] = _adamw(wts[n], ga, gb, mom[n], var[n], "adamw_" + n)
    for n in SMALL_SHARDED:
        width = wts[n].shape[1]
        small_sum[n] = lax.dynamic_slice_in_dim(small_sum[n], chip * width, width, axis=1)
    local_shapes = [wts[n].shape for n in small_names]
    local_rows = _rows_for(sum(wts[n].size for n in small_names))
    packed = _adamw(_pack([wts[n] for n in small_names], local_rows), _pack([small_sum[n] for n in small_names], local_rows),
                    None, _pack([mom[n] for n in small_names], local_rows),
                    _pack([var[n] for n in small_names], local_rows), "adamw_small")
    unpacked = [_unpack(buf, local_shapes) for buf in packed]
    for idx, n in enumerate(small_names):
        outs[n] = [u[idx] for u in unpacked]
    result = [loss, grad_x[None]]
    for part in range(4):
        result += [outs[n][part] if n in SMALL_REPL else outs[n][part][None] for n in WEIGHTS]
    return tuple(result)
```

```python
import functools

import jax
import jax.numpy as jnp
from jax import lax
from jax.experimental import pallas as pl
from jax.experimental.pallas import tpu as pltpu

F32 = jnp.float32
BF16 = jnp.bfloat16
HIGHEST = lax.Precision.HIGHEST
MESH = pl.DeviceIdType.MESH
ANY = pl.BlockSpec(memory_space=pl.ANY)

EPS = 1e-6
HEAD_DIM = 64
N_HEADS = 8
ATTN_W = 512
CONV_W = 512
CONV_K = 31
FFN_K = 3
ATTN_SCALE = 0.125
LANES = 128
CONV_HALO = 32
FFN_HALO = 8
VMEM_LIMIT = 56 * 1024 * 1024

ADAM_LR = 0.001
ADAM_B1 = 0.9
ADAM_B2 = 0.999
ADAM_EPS = 1e-08
ADAM_WD = 0.01
ADAM_STEP = 10

N_CHIPS = 4
N_DEV = 8


def _params(sem):
    return pltpu.CompilerParams(dimension_semantics=sem, vmem_limit_bytes=VMEM_LIMIT)


def _row_tile(s):
    return min(512, s)


def _rstd(x):
    return lax.rsqrt(jnp.mean(x * x, axis=-1, keepdims=True) + EPS)


def _sigmoid(x):
    return 1.0 / (1.0 + jnp.exp(-x))


def _mm(a, b, *, name, dims, grid, a_spec, b_spec, o_spec, o_tile, out_shape, res=None, res_spec=None):
    nk = grid[2]

    def body(*refs):
        if res is None:
            a_ref, b_ref, o_ref, acc_ref = refs
            r_ref = None
        else:
            a_ref, b_ref, r_ref, o_ref, acc_ref = refs
        part = lax.dot_general(a_ref[...], b_ref[...], (dims, ((), ())), preferred_element_type=F32)

        def finish(val):
            if r_ref is not None:
                val = val + r_ref[...]
            o_ref[...] = val.astype(o_ref.dtype)

        if nk == 1:
            finish(part)
        else:
            k = pl.program_id(2)

            @pl.when(k == 0)
            def _():
                acc_ref[...] = part

            @pl.when(k > 0)
            def _():
                acc_ref[...] += part

            @pl.when(k == nk - 1)
            def _():
                finish(acc_ref[...])

    in_specs = [a_spec, b_spec]
    args = [a, b]
    if res is not None:
        in_specs.append(res_spec)
        args.append(res)
    acc_tile = o_tile if nk > 1 else (8, LANES)
    return pl.pallas_call(
        body, name=name, grid=grid, in_specs=in_specs, out_specs=o_spec, out_shape=out_shape,
        scratch_shapes=[pltpu.VMEM(acc_tile, F32)],
        compiler_params=_params(("parallel", "parallel", "arbitrary")))(*args)


NN = ((1,), (0,))
NT = ((1,), (1,))
TN = ((0,), (0,))


def _mm_nn_sharded(a, bg, name, out_dtype=F32):
    s, k = a.shape
    g, _, ns = bg.shape
    tm = _row_tile(s)
    return _mm(a, bg, name=name, dims=NN, grid=(g, s // tm, 1),
               a_spec=pl.BlockSpec((tm, k), lambda j, i, kk: (i, 0)),
               b_spec=pl.BlockSpec((None, k, ns), lambda j, i, kk: (j, 0, 0)),
               o_spec=pl.BlockSpec((tm, ns), lambda j, i, kk: (i, j)), o_tile=(tm, ns),
               out_shape=jax.ShapeDtypeStruct((s, g * ns), out_dtype))


def _mm_nn_full(a, b, name, res=None):
    s, k = a.shape
    n = b.shape[1]
    tm = _row_tile(s)
    rs = pl.BlockSpec((tm, n), lambda i, j, kk: (i, 0))
    return _mm(a, b, name=name, dims=NN, grid=(s // tm, 1, 1),
               a_spec=pl.BlockSpec((tm, k), lambda i, j, kk: (i, 0)),
               b_spec=pl.BlockSpec((k, n), lambda i, j, kk: (0, 0)),
               o_spec=rs, o_tile=(tm, n), out_shape=jax.ShapeDtypeStruct((s, n), F32),
               res=res, res_spec=rs if res is not None else None)


def _mm_nt_full(a, b, name):
    s, n = a.shape
    k = b.shape[0]
    tm = _row_tile(s)
    return _mm(a, b, name=name, dims=NT, grid=(s // tm, 1, 1),
               a_spec=pl.BlockSpec((tm, n), lambda i, j, kk: (i, 0)),
               b_spec=pl.BlockSpec((k, n), lambda i, j, kk: (0, 0)),
               o_spec=pl.BlockSpec((tm, k), lambda i, j, kk: (i, 0)), o_tile=(tm, k),
               out_shape=jax.ShapeDtypeStruct((s, k), F32))


def _mm_nt_sharded(a, bg, name):
    s = a.shape[0]
    g, k, ns = bg.shape
    tm = _row_tile(s)

    def body(a_ref, b_ref, o_ref):
        acc = lax.dot_general(a_ref[:, 0:ns], b_ref[0], (NT, ((), ())), preferred_element_type=F32)
        for gi in range(1, g):
            acc = acc + lax.dot_general(a_ref[:, gi * ns:(gi + 1) * ns], b_ref[gi], (NT, ((), ())),
                                        preferred_element_type=F32)
        o_ref[...] = acc

    return pl.pallas_call(
        body, name=name, grid=(s // tm,),
        in_specs=[pl.BlockSpec((tm, g * ns), lambda i: (i, 0)), pl.BlockSpec((g, k, ns), lambda i: (0, 0, 0))],
        out_specs=pl.BlockSpec((tm, k), lambda i: (i, 0)), out_shape=jax.ShapeDtypeStruct((s, k), F32),
        compiler_params=_params(("parallel",)))(a, bg)


def _mm_tn_sharded(a, b, g, name):
    s, k = a.shape
    ns = b.shape[1] // g
    tk = _row_tile(s)
    return _mm(a, b, name=name, dims=TN, grid=(1, g, s // tk),
               a_spec=pl.BlockSpec((tk, k), lambda i, j, kk: (kk, 0)),
               b_spec=pl.BlockSpec((tk, ns), lambda i, j, kk: (kk, j)),
               o_spec=pl.BlockSpec((None, k, ns), lambda i, j, kk: (j, 0, 0)), o_tile=(k, ns),
               out_shape=jax.ShapeDtypeStruct((g, k, ns), F32))


def _mm_tn_full(a, b, name, tm):
    s, m = a.shape
    n = b.shape[1]
    tk = _row_tile(s)
    return _mm(a, b, name=name, dims=TN, grid=(m // tm, 1, s // tk),
               a_spec=pl.BlockSpec((tk, tm), lambda i, j, kk: (kk, i)),
               b_spec=pl.BlockSpec((tk, n), lambda i, j, kk: (kk, 0)),
               o_spec=pl.BlockSpec((tm, n), lambda i, j, kk: (i, 0)), o_tile=(tm, n),
               out_shape=jax.ShapeDtypeStruct((m, n), F32))


def _rms_fwd(x, g, name):
    s, d = x.shape
    ts = _row_tile(s)

    def body(x_ref, g_ref, u_ref):
        xv = x_ref[...]
        u_ref[...] = (xv * _rstd(xv) * g_ref[...]).astype(BF16)

    row = pl.BlockSpec((ts, d), lambda i: (i, 0))
    return pl.pallas_call(
        body, name=name, grid=(s // ts,), in_specs=[row, pl.BlockSpec((1, d), lambda i: (0, 0))],
        out_specs=row, out_shape=jax.ShapeDtypeStruct((s, d), BF16),
        compiler_params=_params(("parallel",)))(x, g)


def _rms_bwd(h, du, g, dh_in, name):
    s, d = h.shape
    ts = _row_tile(s)

    def body(h_ref, du_ref, g_ref, dhin_ref, dh_ref, dhb_ref, gg_ref):
        hv = h_ref[...]
        r = _rstd(hv)
        xhat = hv * r
        duv = du_ref[...]
        dxhat = duv * g_ref[...]
        m = jnp.mean(dxhat * xhat, axis=-1, keepdims=True)
        dh = dhin_ref[...] + r * (dxhat - xhat * m)
        dh_ref[...] = dh
        dhb_ref[...] = dh.astype(BF16)
        part = jnp.sum(duv * xhat, axis=0, keepdims=True)

        @pl.when(pl.program_id(0) == 0)
        def _():
            gg_ref[...] = part

        @pl.when(pl.program_id(0) > 0)
        def _():
            gg_ref[...] += part

    row = pl.BlockSpec((ts, d), lambda i: (i, 0))
    vec = pl.BlockSpec((1, d), lambda i: (0, 0))
    return pl.pallas_call(
        body, name=name, grid=(s // ts,), in_specs=[row, row, vec, row], out_specs=[row, row, vec],
        out_shape=[jax.ShapeDtypeStruct((s, d), F32), jax.ShapeDtypeStruct((s, d), BF16),
                   jax.ShapeDtypeStruct((1, d), F32)],
        compiler_params=_params(("arbitrary",)))(h, du, g, dh_in)


def _head_sum(x, bd):
    return jnp.dot(x, bd, precision=HIGHEST, preferred_element_type=F32)


def _qkv_prep(proj, qg, kg, bd):
    s = proj.shape[0]
    ts = _row_tile(s)

    def body(q_ref, k_ref, v_ref, qg_ref, kg_ref, bd_ref, qs_ref, kh_ref, vb_ref):
        def norm(x, gain):
            ms = _head_sum(x * x, bd_ref[...]) * (1.0 / HEAD_DIM)
            return x * lax.rsqrt(ms + EPS) * gain

        qs_ref[...] = (norm(q_ref[...], qg_ref[...]) * ATTN_SCALE).astype(BF16)
        kh_ref[...] = norm(k_ref[...], kg_ref[...]).astype(BF16)
        vb_ref[...] = v_ref[...].astype(BF16)

    col = lambda c: pl.BlockSpec((ts, ATTN_W), lambda i: (i, c))
    vec = pl.BlockSpec((1, ATTN_W), lambda i: (0, 0))
    out = pl.BlockSpec((ts, ATTN_W), lambda i: (i, 0))
    sds = jax.ShapeDtypeStruct((s, ATTN_W), BF16)
    return pl.pallas_call(
        body, name="qkv_prep", grid=(s // ts,),
        in_specs=[col(0), col(1), col(2), vec, vec, pl.BlockSpec((ATTN_W, ATTN_W), lambda i: (0, 0))],
        out_specs=[out, out, out], out_shape=[sds, sds, sds],
        compiler_params=_params(("parallel",)))(proj, proj, proj, qg, kg, bd)


def _qk_bwd(proj, dqh, dkh, dv, qg, kg, bd):
    s = proj.shape[0]
    ts = _row_tile(s)

    def body(q_ref, k_ref, dqh_ref, dkh_ref, dv_ref, qg_ref, kg_ref, bd_ref, out_ref, gq_ref, gk_ref):
        first = pl.program_id(0) == 0

        def bwd(x, dy, gain, gg_ref):
            ms = _head_sum(x * x, bd_ref[...]) * (1.0 / HEAD_DIM)
            r = lax.rsqrt(ms + EPS)
            xhat = x * r
            dxhat = dy * gain
            m = _head_sum(dxhat * xhat, bd_ref[...]) * (1.0 / HEAD_DIM)
            part = jnp.sum(dy * xhat, axis=0, keepdims=True)

            @pl.when(first)
            def _():
                gg_ref[...] = part

            @pl.when(jnp.logical_not(first))
            def _():
                gg_ref[...] += part

            return r * (dxhat - xhat * m)

        out_ref[:, 0:ATTN_W] = bwd(q_ref[...], dqh_ref[...], qg_ref[...], gq_ref).astype(BF16)
        out_ref[:, ATTN_W:2 * ATTN_W] = bwd(k_ref[...], dkh_ref[...], kg_ref[...], gk_ref).astype(BF16)
        out_ref[:, 2 * ATTN_W:3 * ATTN_W] = dv_ref[...].astype(BF16)

    col = lambda c: pl.BlockSpec((ts, ATTN_W), lambda i: (i, c))
    row = pl.BlockSpec((ts, ATTN_W), lambda i: (i, 0))
    vec = pl.BlockSpec((1, ATTN_W), lambda i: (0, 0))
    return pl.pallas_call(
        body, name="qk_bwd", grid=(s // ts,),
        in_specs=[col(0), col(1), row, row, row, vec, vec, pl.BlockSpec((ATTN_W, ATTN_W), lambda i: (0, 0))],
        out_specs=[pl.BlockSpec((ts, 3 * ATTN_W), lambda i: (i, 0)), vec, vec],
        out_shape=[jax.ShapeDtypeStruct((s, 3 * ATTN_W), BF16), jax.ShapeDtypeStruct((1, ATTN_W), F32),
                   jax.ShapeDtypeStruct((1, ATTN_W), F32)],
        compiler_params=_params(("arbitrary",)))(proj, proj, dqh, dkh, dv, qg, kg, bd)


def _split(x):
    hi = x.astype(BF16)
    return hi, (x - hi.astype(F32)).astype(BF16)


def _tri_dot(parts, tri):
    hi, lo = parts
    return jnp.dot(hi, tri, preferred_element_type=F32) + jnp.dot(lo, tri, preferred_element_type=F32)


def _log_sigmoids(z):
    neg_abs = lax.bitcast_convert_type(lax.bitcast_convert_type(z, jnp.uint32) | jnp.uint32(0x80000000), F32)
    lb = jnp.minimum(z, 0.0) - jnp.log(1.0 + jnp.exp(neg_abs))
    return lb, lb - z


def _head_masks():
    lane = lax.broadcasted_iota(jnp.int32, (1, LANES), 1)
    return [lane < HEAD_DIM, lane >= HEAD_DIM]


def _attn_fwd(qs, kh, vb, tri_excl):
    s = qs.shape[0]
    t = tri_excl.shape[0]
    nq = s // t

    def body(q_ref, k_ref, v_ref, tri_ref, o_ref, ob_ref, acc_ref, c_ref):
        i = pl.program_id(1)
        hmask = _head_masks()
        q = q_ref[...]
        qm = [jnp.where(hm, q, jnp.zeros_like(q)) for hm in hmask]
        acc_ref[...] = jnp.zeros_like(acc_ref)
        c_ref[...] = jnp.zeros_like(c_ref)
        causal = (lax.broadcasted_iota(jnp.int32, (t, t), 1) < lax.broadcasted_iota(jnp.int32, (t, t), 0))

        def tiles(kbs, masked):
            tri = tri_ref[...]
            starts = [pl.multiple_of(kb * t, t) for kb in kbs]
            kblks = [k_ref[pl.ds(k0, t), :] for k0 in starts]
            vblks = [v_ref[pl.ds(k0, t), :] for k0 in starts]
            chains = [(j, h) for j in range(len(kbs)) for h in range(2)]
            carry = [c_ref[h] for h in range(2)]
            pv = [None, None]
            lbs, loms, between = {}, {}, {}
            for step in range(len(chains) + 2):
                if step < len(chains):
                    j, h = chains[step]
                    z = lax.dot_general(qm[h], kblks[j], (NT, ((), ())), preferred_element_type=F32)
                    lbs[step], lom = _log_sigmoids(z)
                    loms[step] = jnp.where(causal, lom, 0.0) if masked else lom
                if 0 <= step - 1 < len(chains):
                    between[step - 1] = _tri_dot(_split(loms[step - 1]), tri)
                if 0 <= step - 2 < len(chains):
                    n = step - 2
                    j, h = chains[n]
                    w = jnp.exp(lbs[n] + between[n] + carry[h])
                    if masked:
                        w = jnp.where(causal, w, 0.0)
                    carry[h] = carry[h] + jnp.sum(loms[n], axis=-1, keepdims=True)
                    part = jnp.dot(w.astype(BF16), vblks[j], preferred_element_type=F32)
                    pv[h] = part if pv[h] is None else pv[h] + part
            for h in range(2):
                c_ref[h] = carry[h]
                acc_ref[h] += pv[h]

        tiles([i], True)

        def loop(step, carry):
            tiles([i - 1 - 2 * step, i - 2 - 2 * step], False)
            return carry

        lax.fori_loop(0, i // 2, loop, 0)

        @pl.when(i % 2 == 1)
        def _():
            tiles([0], False)

        o = jnp.where(hmask[0], acc_ref[0], acc_ref[1])
        o_ref[...] = o
        ob_ref[...] = o.astype(BF16)

    qspec = pl.BlockSpec((t, LANES), lambda hp, i: (i, hp))
    kspec = pl.BlockSpec((s, LANES), lambda hp, i: (0, hp))
    return pl.pallas_call(
        body, name="attn_fwd", grid=(ATTN_W // LANES, nq),
        in_specs=[qspec, kspec, kspec, pl.BlockSpec((t, t), lambda hp, i: (0, 0))],
        out_specs=[qspec, qspec],
        out_shape=[jax.ShapeDtypeStruct((s, ATTN_W), F32), jax.ShapeDtypeStruct((s, ATTN_W), BF16)],
        scratch_shapes=[pltpu.VMEM((2, t, LANES), F32), pltpu.VMEM((2, t, 1), F32)],
        compiler_params=_params(("parallel", "arbitrary")))(qs, kh, vb, tri_excl)


def _attn_bwd(qs, kh, vb, o, dmix, tri_excl, tri_incl):
    s = qs.shape[0]
    t = tri_excl.shape[0]
    nq = s // t

    def body(q_ref, k_ref, v_ref, o_ref, do_ref, te_ref, ti_ref, dq_ref, dk_ref, dv_ref, dqacc_ref, c_ref):
        i = pl.program_id(1)

        @pl.when(i == 0)
        def _():
            dk_ref[...] = jnp.zeros_like(dk_ref)
            dv_ref[...] = jnp.zeros_like(dv_ref)

        hmask = _head_masks()
        q = q_ref[...]
        do = do_ref[...]
        dob = do.astype(BF16)
        prod = dob.astype(F32) * o_ref[...]
        qm =[jnp.where(hm, q, jnp.zeros_like(q)) for hm in hmask]
        dom = [jnp.where(hm, dob, jnp.zeros_like(dob)) for hm in hmask]
        total = [jnp.sum(jnp.where(hm, prod, 0.0), axis=-1, keepdims=True) for hm in hmask]
        dqacc_ref[...] = jnp.zeros_like(dqacc_ref)
        c_ref[...] = jnp.zeros_like(c_ref)
        causal = (lax.broadcasted_iota(jnp.int32, (t, t), 1) < lax.broadcasted_iota(jnp.int32, (t, t), 0))

        def tiles(kbs, masked):
            te = te_ref[...]
            ti = ti_ref[...]
            starts = [pl.multiple_of(kb * t, t) for kb in kbs]
            kblks = [k_ref[pl.ds(k0, t), :] for k0 in starts]
            vblks = [v_ref[pl.ds(k0, t), :] for k0 in starts]
            chains = [(j, h) for j in range(len(kbs)) for h in range(2)]
            c_lom = [c_ref[2 * h] for h in range(2)]
            c_g = [c_ref[2 * h + 1] for h in range(2)]
            lbs, loms, dws, between, wbs, gs, g_after = {}, {}, {}, {}, {}, {}, {}
            dq = [None, None]
            dk = [None] * len(kbs)
            dv = [None] * len(kbs)
            add = lambda acc, part: part if acc is None else acc + part
            for step in range(len(chains) + 3):
                if step < len(chains):
                    j, h = chains[step]
                    z = lax.dot_general(qm[h], kblks[j], (NT, ((), ())), preferred_element_type=F32)
                    dws[step] = lax.dot_general(dom[h], vblks[j], (NT, ((), ())), preferred_element_type=F32)
                    lbs[step], lom = _log_sigmoids(z)
                    loms[step] = jnp.where(causal, lom, 0.0) if masked else lom
                if 0 <= step - 1 < len(chains):
                    between[step - 1] = _tri_dot(_split(loms[step - 1]), te)
                if 0 <= step - 2 < len(chains):
                    n = step - 2
                    h = chains[n][1]
                    w = jnp.exp(lbs[n] + between[n] + c_lom[h])
                    if masked:
                        w = jnp.where(causal, w, 0.0)
                    c_lom[h] = c_lom[h] + jnp.sum(loms[n], axis=-1, keepdims=True)
                    wbs[n] = w.astype(BF16)
                    gs[n] = dws[n] * wbs[n].astype(F32)
                    g_after[n] = _tri_dot(_split(gs[n]), ti)
                if 0 <= step - 3 < len(chains):
                    n = step - 3
                    j, h = chains[n]
                    beta = jnp.exp(lbs[n])
                    dz = gs[n] * (1.0 - beta) - beta * (total[h] - (g_after[n] + c_g[h]))
                    if masked:
                        dz = jnp.where(causal, dz, 0.0)
                    c_g[h] = c_g[h] + jnp.sum(gs[n], axis=-1, keepdims=True)
                    dzb = dz.astype(BF16)
                    dq[h] = add(dq[h], jnp.dot(dzb, kblks[j], preferred_element_type=F32))
                    dk[j] = add(dk[j], lax.dot_general(dzb, qm[h], (TN, ((), ())), preferred_element_type=F32))
                    dv[j] = add(dv[j], lax.dot_general(wbs[n], dom[h], (TN, ((), ())), preferred_element_type=F32))
            for h in range(2):
                c_ref[2 * h] = c_lom[h]
                c_ref[2 * h + 1] = c_g[h]
                dqacc_ref[h] += dq[h]
            for j, k0 in enumerate(starts):
                dk_ref[pl.ds(k0, t), :] += dk[j]
                dv_ref[pl.ds(k0, t), :] += dv[j]

        tiles([i], True)

        def loop(step, carry):
            tiles([i - 1 - 2 * step, i - 2 - 2 * step], False)
            return carry

        lax.fori_loop(0, i // 2, loop, 0)

        @pl.when(i % 2 == 1)
        def _():
            tiles([0], False)

        dq_ref[...] = jnp.where(hmask[0], dqacc_ref[0], dqacc_ref[1]) * ATTN_SCALE

    qspec = pl.BlockSpec((t, LANES), lambda hp, i: (i, hp))
    kspec = pl.BlockSpec((s, LANES), lambda hp, i: (0, hp))
    tspec = pl.BlockSpec((t, t), lambda hp, i: (0, 0))
    sds = jax.ShapeDtypeStruct((s, ATTN_W), F32)
    return pl.pallas_call(
        body, name="attn_bwd", grid=(ATTN_W // LANES, nq),
        in_specs=[qspec, kspec, kspec, qspec, qspec, tspec, tspec],
        out_specs=[qspec, kspec, kspec], out_shape=[sds, sds, sds],
        scratch_shapes=[pltpu.VMEM((2, t, LANES), F32), pltpu.VMEM((4, t, 1), F32)],
        compiler_params=_params(("parallel", "arbitrary")))(qs, kh, vb, o, dmix, tri_excl, tri_incl)


CONV_ROWS = 64
CONV_COLS = 256


def _taps(src_ref, w_ref, n_taps, first_row, rows, reverse=False):
    width = src_ref.shape[1]
    cols = min(CONV_COLS, width)
    out = []
    for r0 in range(0, rows, CONV_ROWS):
        for c0 in range(0, width, cols):
            acc = jnp.zeros((CONV_ROWS, cols), F32)
            for k in range(n_taps):
                off = (n_taps - 1 - k) if reverse else k
                acc = acc + w_ref[k:k + 1, c0:c0 + cols] * src_ref[first_row + r0 + off:first_row + r0 + off + CONV_ROWS,
                                                                   c0:c0 + cols]
            out.append(((r0, c0), acc))
    return out


def _conv_fwd(proj, dw_w, dw_b, ln_g, ln_b):
    s = proj.shape[0]
    ts = _row_tile(s)
    hb = ts // CONV_HALO

    def body(a_ref, g_ref, ha_ref, hg_ref, w_ref, b_ref, lg_ref, lb_ref, c1_ref, c3_ref, pad_ref):
        i = pl.program_id(0)
        halo = ha_ref[...] * _sigmoid(hg_ref[...])
        pad_ref[0:CONV_HALO, :] = jnp.where(i > 0, halo, 0.0)
        pad_ref[CONV_HALO:, :] = a_ref[...] * _sigmoid(g_ref[...])
        first = CONV_HALO - (CONV_K - 1)
        for (r0, c0), acc in _taps(pad_ref, w_ref, CONV_K, first, ts):
            c1_ref[r0:r0 + CONV_ROWS, c0:c0 + acc.shape[1]] = acc + b_ref[:, c0:c0 + acc.shape[1]]
        c1 = c1_ref[...]
        xc = c1 - jnp.mean(c1, axis=-1, keepdims=True)
        c2 = xc * _rstd(xc) * lg_ref[...] + lb_ref[...]
        c3_ref[...] = (c2 * _sigmoid(c2)).astype(BF16)

    cur = lambda c: pl.BlockSpec((ts, CONV_W), lambda i: (i, c))
    halo = lambda c: pl.BlockSpec((CONV_HALO, CONV_W), lambda i: (jnp.maximum(i * hb - 1, 0), c))
    vec = pl.BlockSpec((1, CONV_W), lambda i: (0, 0))
    row = pl.BlockSpec((ts, CONV_W), lambda i: (i, 0))
    return pl.pallas_call(
        body, name="conv_fwd", grid=(s // ts,),
        in_specs=[cur(3), cur(4), halo(3), halo(4), pl.BlockSpec((CONV_HALO, CONV_W), lambda i: (0, 0)), vec, vec, vec],
        out_specs=[row, row],
        out_shape=[jax.ShapeDtypeStruct((s, CONV_W), F32), jax.ShapeDtypeStruct((s, CONV_W), BF16)],
        scratch_shapes=[pltpu.VMEM((ts + CONV_HALO, CONV_W), F32)],
        compiler_params=_params(("parallel",)))(proj, proj, proj, proj, dw_w, dw_b, ln_g, ln_b)


def _conv_bwd_ln(dmix, c1, ln_g, ln_b):
    s = c1.shape[0]
    ts = _row_tile(s)

    def body(d_ref, c1_ref, lg_ref, lb_ref, dc1_ref, glg_ref, glb_ref, gb_ref):
        c1v = c1_ref[...]
        xc = c1v - jnp.mean(c1v, axis=-1, keepdims=True)
        r = _rstd(xc)
        xhat = xc * r
        c2 = xhat * lg_ref[...] + lb_ref[...]
        sg = _sigmoid(c2)
        dc2 = d_ref[...] * (sg * (1.0 + c2 * (1.0 - sg)))
        dxhat = dc2 * lg_ref[...]
        dc1 = r * (dxhat - jnp.mean(dxhat, axis=-1, keepdims=True)
                   - xhat * jnp.mean(dxhat * xhat, axis=-1, keepdims=True))
        dc1_ref[...] = dc1
        parts = [(glg_ref, jnp.sum(dc2 * xhat, axis=0, keepdims=True)),
                 (glb_ref, jnp.sum(dc2, axis=0, keepdims=True)),
                 (gb_ref, jnp.sum(dc1, axis=0, keepdims=True))]

        @pl.when(pl.program_id(0) == 0)
        def _():
            for ref, part in parts:
                ref[...] = part

        @pl.when(pl.program_id(0) > 0)
        def _():
            for ref, part in parts:
                ref[...] += part

    row = pl.BlockSpec((ts, CONV_W), lambda i: (i, 0))
    vec = pl.BlockSpec((1, CONV_W), lambda i: (0, 0))
    vsd = jax.ShapeDtypeStruct((1, CONV_W), F32)
    return pl.pallas_call(
        body, name="conv_bwd_ln", grid=(s // ts,),
        in_specs=[pl.BlockSpec((ts, CONV_W), lambda i: (i, 1)), row, vec, vec],
        out_specs=[row, vec, vec, vec],
        out_shape=[jax.ShapeDtypeStruct((s, CONV_W), F32), vsd, vsd, vsd],
        compiler_params=_params(("arbitrary",)))(dmix, c1, ln_g, ln_b)


def _conv_bwd_taps(proj, dc1, dw_w):
    s = proj.shape[0]
    ts = _row_tile(s)
    hb = ts // CONV_HALO
    last = s // CONV_HALO - 1
    nsteps = s // ts

    def body(a_ref, g_ref, ha_ref, hg_ref, d_ref, hd_ref, w_ref, out_ref, gw_ref, pad_ref, dpad_ref):
        i = pl.program_id(0)
        av = a_ref[...]
        sg = _sigmoid(g_ref[...])
        halo = ha_ref[...] * _sigmoid(hg_ref[...])
        pad_ref[0:CONV_HALO, :] = jnp.where(i > 0, halo, 0.0)
        pad_ref[CONV_HALO:, :] = av * sg
        dpad_ref[0:ts, :] = d_ref[...]
        dpad_ref[ts:, :] = jnp.where(i < nsteps - 1, hd_ref[...], 0.0)

        @pl.when(i == 0)
        def _():
            gw_ref[...] = jnp.zeros_like(gw_ref)

        first = CONV_HALO - (CONV_K - 1)
        for k in range(CONV_K):
            gw_ref[k:k + 1, :] += jnp.sum(d_ref[...] * pad_ref[first + k:first + k + ts, :], axis=0, keepdims=True)
        for (r0, c0), dc0 in _taps(dpad_ref, w_ref, CONV_K, 0, ts, reverse=True):
            cs = slice(c0, c0 + dc0.shape[1])
            a_c = a_ref[r0:r0 + CONV_ROWS, cs]
            sg_c = _sigmoid(g_ref[r0:r0 + CONV_ROWS, cs])
            out_ref[r0:r0 + CONV_ROWS, cs] = (dc0 * sg_c).astype(BF16)
            out_ref[r0:r0 + CONV_ROWS, CONV_W + c0:CONV_W + c0 + dc0.shape[1]] = (
                dc0 * a_c * sg_c * (1.0 - sg_c)).astype(BF16)

    cur = lambda c: pl.BlockSpec((ts, CONV_W), lambda i: (i, c))
    halo = lambda c: pl.BlockSpec((CONV_HALO, CONV_W), lambda i: (jnp.maximum(i * hb - 1, 0), c))
    row = pl.BlockSpec((ts, CONV_W), lambda i: (i, 0))
    nxt = pl.BlockSpec((CONV_HALO, CONV_W), lambda i: (jnp.minimum((i + 1) * hb, last), 0))
    wspec = pl.BlockSpec((CONV_HALO, CONV_W), lambda i: (0, 0))
    return pl.pallas_call(
        body, name="conv_bwd_taps", grid=(nsteps,),
        in_specs=[cur(3), cur(4), halo(3), halo(4), row, nxt, wspec],
        out_specs=[pl.BlockSpec((ts, 2 * CONV_W), lambda i: (i, 0)), wspec],
        out_shape=[jax.ShapeDtypeStruct((s, 2 * CONV_W), BF16), jax.ShapeDtypeStruct((CONV_HALO, CONV_W), F32)],
        scratch_shapes=[pltpu.VMEM((ts + CONV_HALO, CONV_W), F32), pltpu.VMEM((ts + CONV_HALO, CONV_W), F32)],
        compiler_params=_params(("arbitrary",)))(proj, proj, proj, proj, dc1, dc1, dw_w)


SQRT_HALF = 0.7071067811865476
INV_SQRT_2PI = 0.3989422804014327


def _gelu_parts(x):
    cdf = 0.5 * (1.0 + lax.erf(x * SQRT_HALF))
    return x * cdf, cdf + x * (INV_SQRT_2PI * jnp.exp(-0.5 * x * x))


def _ffn_tile(dff):
    return dff // 2


def _ffn_gate2(pad_ref, w_ref, b_ref, ts):
    first = FFN_HALO - (FFN_K - 1)
    g2 = b_ref[...] + w_ref[0:1, :] * pad_ref[first:first + ts, :]
    for k in range(1, FFN_K):
        g2 = g2 + w_ref[k:k + 1, :] * pad_ref[first + k:first + k + ts, :]
    return g2


def _ffn_act(up, fw, fb):
    s = up.shape[0]
    dff = up.shape[1] // 2
    tc = _ffn_tile(dff)
    nj = dff // tc
    ts = _row_tile(s) // 2
    hb = ts // FFN_HALO

    def body(g_ref, v_ref, hg_ref, w_ref, b_ref, act_ref, pad_ref):
        i = pl.program_id(0)
        pad_ref[0:FFN_HALO, :] = jnp.where(i > 0, hg_ref[...], 0.0)
        pad_ref[FFN_HALO:, :] = g_ref[...]
        gelu, _ = _gelu_parts(_ffn_gate2(pad_ref, w_ref, b_ref, ts))
        act_ref[...] = (gelu * v_ref[...]).astype(BF16)

    return pl.pallas_call(
        body, name="ffn_act", grid=(s // ts, nj),
        in_specs=[pl.BlockSpec((ts, tc), lambda i, j: (i, j)), pl.BlockSpec((ts, tc), lambda i, j: (i, j + nj)),
                  pl.BlockSpec((FFN_HALO, tc), lambda i, j: (jnp.maximum(i * hb - 1, 0), j)),
                  pl.BlockSpec((FFN_HALO, tc), lambda i, j: (0, j)), pl.BlockSpec((1, tc), lambda i, j: (0, j))],
        out_specs=pl.BlockSpec((ts, tc), lambda i, j: (i, j)),
        out_shape=jax.ShapeDtypeStruct((s, dff), BF16),
        scratch_shapes=[pltpu.VMEM((ts + FFN_HALO, tc), F32)],
        compiler_params=_params(("parallel", "parallel")))(up, up, up, fw, fb)


def _ffn_bwd_act(dact, up, fw, fb):
    s = up.shape[0]
    dff = up.shape[1] // 2
    tc = _ffn_tile(dff)
    nj = dff // tc
    ts = _row_tile(s) // 2
    hb = ts // FFN_HALO

    def body(d_ref, g_ref, v_ref, hg_ref, w_ref, b_ref, dg2_ref, dval_ref, gw_ref, gb_ref, pad_ref):
        i = pl.program_id(1)
        pad_ref[0:FFN_HALO, :] = jnp.where(i > 0, hg_ref[...], 0.0)
        pad_ref[FFN_HALO:, :] = g_ref[...]
        gelu, dgelu = _gelu_parts(_ffn_gate2(pad_ref, w_ref, b_ref, ts))
        dactv = d_ref[...]
        dval_ref[...] = (dactv * gelu).astype(BF16)
        dg2 = dactv * v_ref[...] * dgelu
        dg2_ref[...] = dg2

        @pl.when(i == 0)
        def _():
            gw_ref[...] = jnp.zeros_like(gw_ref)
            gb_ref[...] = jnp.zeros_like(gb_ref)

        gb_ref[...] += jnp.sum(dg2, axis=0, keepdims=True)
        first = FFN_HALO - (FFN_K - 1)
        for k in range(FFN_K):
            gw_ref[k:k + 1, :] += jnp.sum(dg2 * pad_ref[first + k:first + k + ts, :], axis=0, keepdims=True)

    blk = pl.BlockSpec((ts, tc), lambda j, i: (i, j))
    wspec = pl.BlockSpec((FFN_HALO, tc), lambda j, i: (0, j))
    bspec = pl.BlockSpec((1, tc), lambda j, i: (0, j))
    return pl.pallas_call(
        body, name="ffn_bwd_act", grid=(nj, s // ts),
        in_specs=[blk, blk, pl.BlockSpec((ts, tc), lambda j, i: (i, j + nj)),
                  pl.BlockSpec((FFN_HALO, tc), lambda j, i: (jnp.maximum(i * hb - 1, 0), j)), wspec, bspec],
        out_specs=[blk, blk, wspec, bspec],
        out_shape=[jax.ShapeDtypeStruct((s, dff), F32), jax.ShapeDtypeStruct((s, dff), BF16),
                   jax.ShapeDtypeStruct((FFN_HALO, dff), F32), jax.ShapeDtypeStruct((1, dff), F32)],
        scratch_shapes=[pltpu.VMEM((ts + FFN_HALO, tc), F32)],
        compiler_params=_params(("parallel", "arbitrary")))(dact, up, up, up, fw, fb)


def _ffn_bwd_conv(dg2, fw):
    s, dff = dg2.shape
    tc = _ffn_tile(dff)
    ts = _row_tile(s) // 2
    hb = ts // FFN_HALO
    last = s // FFN_HALO - 1
    nsteps = s // ts

    def body(d_ref, hd_ref, w_ref, out_ref, pad_ref):
        i = pl.program_id(0)
        pad_ref[0:ts, :] = d_ref[...]
        pad_ref[ts:, :] = jnp.where(i < nsteps - 1, hd_ref[...], 0.0)
        dg = w_ref[0:1, :] * pad_ref[FFN_K - 1:FFN_K - 1 + ts, :]
        for k in range(1, FFN_K):
            dg = dg + w_ref[k:k + 1, :] * pad_ref[FFN_K - 1 - k:FFN_K - 1 - k + ts, :]
        out_ref[...] = dg.astype(BF16)

    blk = pl.BlockSpec((ts, tc), lambda i, j: (i, j))
    return pl.pallas_call(
        body, name="ffn_bwd_conv", grid=(nsteps, dff // tc),
        in_specs=[blk, pl.BlockSpec((FFN_HALO, tc), lambda i, j: (jnp.minimum((i + 1) * hb, last), j)),
                  pl.BlockSpec((FFN_HALO, tc), lambda i, j: (0, j))],
        out_specs=blk, out_shape=jax.ShapeDtypeStruct((s, dff), BF16),
        scratch_shapes=[pltpu.VMEM((ts + FFN_HALO, tc), F32)],
        compiler_params=_params(("parallel", "parallel")))(dg2, dg2, fw)


def _ple_loss(h2, zg, pp, target):
    s, d = h2.shape
    ts = _row_tile(s)

    def body(h_ref, z_ref, p_ref, t_ref, dh_ref, dpp_ref, dz_ref, loss_ref):
        pg = _sigmoid(z_ref[...])
        ppv = p_ref[...]
        diff = h_ref[...] + pg * ppv - t_ref[...]
        dh = diff * (1.0 / d)
        dh_ref[...] = dh
        dpp_ref[...] = (dh * pg).astype(BF16)
        dz_ref[...] = (dh * ppv * pg * (1.0 - pg)).astype(BF16)
        part = jnp.sum(jnp.sum(diff * diff, axis=0, keepdims=True), axis=1, keepdims=True)

        @pl.when(pl.program_id(0) == 0)
        def _():
            loss_ref[...] = jnp.zeros_like(loss_ref)

        loss_ref[...] += jnp.broadcast_to(part, loss_ref.shape)

    row = pl.BlockSpec((ts, d), lambda i: (i, 0))
    return pl.pallas_call(
        body, name="ple_loss", grid=(s // ts,), in_specs=[row, row, row, row],
        out_specs=[row, row, row, pl.BlockSpec((8, LANES), lambda i: (0, 0))],
        out_shape=[jax.ShapeDtypeStruct((s, d), F32), jax.ShapeDtypeStruct((s, d), BF16),
                   jax.ShapeDtypeStruct((s, d), BF16), jax.ShapeDtypeStruct((8, LANES), F32)],
        compiler_params=_params(("arbitrary",)))(h2, zg, pp, target)


def _local_step(x, p, target, w, fetch_first, fetch_rest, send_first, send_last):
    s = x.shape[0]
    t = min(256, s)
    tri = jnp.tril(jnp.ones((t, t), F32))
    tri_incl = tri.astype(BF16)
    tri_excl = jnp.tril(jnp.ones((t, t), F32), -1).astype(BF16)
    bd = jnp.kron(jnp.eye(N_HEADS, dtype=F32), jnp.ones((HEAD_DIM, HEAD_DIM), F32))
    qg = jnp.tile(w["q_gain"], (1, N_HEADS))
    kg = jnp.tile(w["k_gain"], (1, N_HEADS))
    pb = p.astype(BF16)

    u1 = _rms_fwd(x, w["g_mix"], "rms_mix")
    w = {**w, **fetch_first(u1)}
    dw_w = jnp.pad(w["dw_w"], ((0, CONV_HALO - CONV_K), (0, 0)))
    fw = jnp.pad(w["ffn_conv_w"], ((0, FFN_HALO - FFN_K), (0, 0)))
    proj = _mm_nn_sharded(u1, w["w_in"], "mm_in")
    qs, kh, vb = _qkv_prep(proj, qg, kg, bd)
    o, ob = _attn_fwd(qs, kh, vb, tri_excl)
    w = {**w, **fetch_rest(o)}
    c1, c3 = _conv_fwd(proj, dw_w, w["dw_b"], w["conv_ln_g"], w["conv_ln_b"])
    mix = jnp.concatenate([ob, c3], axis=1)
    h1 = _mm_nn_full(mix, w["w_out"], "mm_out", res=x)
    u2 = _rms_fwd(h1, w["g_ffn"], "rms_ffn")
    up = _mm_nn_sharded(u2, w["w_up"], "mm_up")
    act = _ffn_act(up, fw, w["ffn_conv_b"])
    h2 = _mm_nn_full(act, w["w_down"], "mm_down", res=h1)
    u3 = _rms_fwd(h2, w["g_ple"], "rms_ple")
    zg = _mm_nn_full(u3, w["w_ple_gate"], "mm_ple_gate")
    pp = _mm_nn_sharded(pb, w["w_ple_proj"], "mm_ple_proj")
    dh3, dpp, dz, sq = _ple_loss(h2, zg, pp, target)

    big = {}
    small = {}
    big["w_ple_proj"] = _mm_tn_sharded(pb, dpp, N_CHIPS, "mm_g_ple_proj")
    big["w_ple_gate"] = _mm_tn_full(u3, dz, "mm_g_ple_gate", tm=u3.shape[1])
    du3 = _mm_nt_full(dz, w["w_ple_gate"], "mm_d_ple_gate")
    dh2, dh2b, small["g_ple"] = _rms_bwd(h2, du3, w["g_ple"], dh3, "rms_ple_bwd")
    big["w_down"] = _mm_tn_full(act, dh2b, "mm_g_down", tm=act.shape[1] // 2)
    dact = _mm_nt_full(dh2b, w["w_down"], "mm_d_down")
    dg2, dval, gfw, small["ffn_conv_b"] = _ffn_bwd_act(dact, up, fw, w["ffn_conv_b"])
    small["ffn_conv_w"] = gfw[:FFN_K]
    dgate = _ffn_bwd_conv(dg2, fw)
    dup = jnp.concatenate([dgate, dval], axis=1)
    big["w_up"] = _mm_tn_sharded(u2, dup, N_CHIPS, "mm_g_up")
    du2 = _mm_nt_sharded(dup, w["w_up"], "mm_d_up")
    dh1, dh1b, small["g_ffn"] = _rms_bwd(h1, du2, w["g_ffn"], dh2, "rms_ffn_bwd")
    big["w_out"] = _mm_tn_full(mix, dh1b, "mm_g_out", tm=mix.shape[1])
    sent = send_first({n: big[n] for n in ("w_ple_proj", "w_ple_gate", "w_down", "w_up", "w_out")})
    dmix = _mm_nt_full(dh1b, w["w_out"], "mm_d_out")
    dc1, small["conv_ln_g"], small["conv_ln_b"], small["dw_b"] = _conv_bwd_ln(dmix, c1, w["conv_ln_g"], w["conv_ln_b"])
    dcacg, gdw = _conv_bwd_taps(proj, dc1, dw_w)
    small["dw_w"] = gdw[:CONV_K]
    dqh, dkh, dv = _attn_bwd(qs, kh, vb, o, dmix, tri_excl + sent.astype(BF16), tri_incl)
    dqkv, gq, gk = _qk_bwd(proj, dqh, dkh, dv, qg, kg, bd)
    small["q_gain"] = gq.reshape(N_HEADS, HEAD_DIM).sum(axis=0, keepdims=True)
    small["k_gain"] = gk.reshape(N_HEADS, HEAD_DIM).sum(axis=0, keepdims=True)
    dproj = jnp.concatenate([dqkv, dcacg], axis=1)
    big["w_in"] = _mm_tn_sharded(u1, dproj, N_CHIPS, "mm_g_in")
    sent = send_last({"w_in": big["w_in"]})
    du1 = _mm_nt_sharded(dproj, w["w_in"], "mm_d_in")
    grad_x, _, small["g_mix"] = _rms_bwd(x, du1, w["g_mix"] + sent, dh1, "rms_mix_bwd")
    return sq[0, 0], grad_x, big, small


def _position():
    x, y, c = lax.axis_index("x"), lax.axis_index("y"), lax.axis_index("c")
    return x, y, c, [(1 - x, y), (x, 1 - y), (1 - x, 1 - y)]


def _remote(src, dst, send_sems, recv_sems, k, to):
    return pltpu.make_async_remote_copy(src_ref=src, dst_ref=dst, send_sem=send_sems.at[k], recv_sem=recv_sems.at[k],
                                        device_id=to, device_id_type=MESH)


HBM = pl.BlockSpec(memory_space=pltpu.HBM)
SEM = pl.BlockSpec(memory_space=pltpu.SEMAPHORE)
EFFECT = pltpu.SideEffectType.DATAFLOW_SIDE_EFFECTING


def _chip_copies(src_refs, land_refs, send_sems, recv_sems, per_chip, landed):
    x, y, c, chips = _position()
    me = 2 * x + y
    out = []
    for a, (src, land) in enumerate(zip(src_refs, land_refs)):
        for j, (px, py) in enumerate(chips):
            peer = 2 * px + py
            out.append(_remote(src.at[peer] if per_chip else src, land.at[peer if landed else me],
                               send_sems, recv_sems, 3 * a + j, (px, py, c)))
    return out


def _exchange_start(groups, lands, per_chip, name):
    srcs = [a for grp in groups for a in grp]
    n, ng = len(srcs), len(groups)

    def body(*refs):
        src_refs, land_refs = refs[:n], refs[n:2 * n]
        sems = refs[2 * n:2 * n + 2 * ng]
        token = refs[-1]
        at = 0
        for gi, grp in enumerate(groups):
            for cp in _chip_copies(src_refs[at:at + len(grp)], land_refs[at:at + len(grp)], sems[2 * gi],
                                   sems[2 * gi + 1], per_chip, False):
                cp.start()
            at += len(grp)
        token[...] = jnp.zeros_like(token)

    sem_shapes = [pltpu.SemaphoreType.DMA((3 * len(grp),)) for grp in groups for _ in range(2)]
    thru = [pltpu.HBM(a.shape, a.dtype) for a in srcs + lands]
    outs = pl.pallas_call(
        body, name=name, in_specs=[HBM] * (2 * n),
        out_specs=tuple([SEM] * (2 * ng) + [HBM] * (2 * n) + [pl.BlockSpec(memory_space=pltpu.VMEM)]),
        out_shape=tuple(sem_shapes + thru + [jax.ShapeDtypeStruct((8, LANES), F32)]),
        input_output_aliases={k: 2 * ng + k for k in range(2 * n)},
        compiler_params=pltpu.CompilerParams(has_side_effects=EFFECT),
    )(*[pltpu.with_memory_space_constraint(a, pltpu.HBM) for a in srcs + lands])
    sems = [(outs[2 * gi], outs[2 * gi + 1]) for gi in range(ng)]
    return sems, list(outs[2 * ng:2 * ng + n]), list(outs[2 * ng + n:2 * ng + 2 * n]), outs[-1]


def _exchange_wait(sems, srcs, lands, per_chip, after, name):
    n = len(srcs)
    send_sems, recv_sems = sems

    def body(*refs):
        src_refs, land_refs = refs[:n], refs[n:2 * n]
        send_ref, recv_ref = refs[2 * n], refs[2 * n + 1]
        for cp in _chip_copies(src_refs, land_refs, send_ref, recv_ref, per_chip, True):
            cp.wait_send()
            cp.wait_recv()

    outs = pl.pallas_call(
        body, name=name, in_specs=[HBM] * (2 * n) + [SEM, SEM, ANY], out_specs=tuple([HBM] * (2 * n)),
        out_shape=tuple(pltpu.HBM(a.shape, a.dtype) for a in srcs + lands),
        input_output_aliases={k: k for k in range(2 * n)},
        compiler_params=pltpu.CompilerParams(has_side_effects=EFFECT),
    )(*srcs, *lands, send_sems, recv_sems, after)
    return list(outs[:n]), list(outs[n:])


def _finish_exchange(mine, small):
    n = len(mine)

    def body(*refs):
        gin, sin = refs[:n], refs[n]
        gout, sout = refs[n + 1:2 * n + 1], refs[2 * n + 1]
        send_sems, recv_sems, small_send, small_recv, local_sem = refs[2 * n + 2:]
        x, y, c, _ = _position()
        dev = 4 * x + 2 * y + c
        flip = lambda v, bit: 1 - v if bit else v
        others = [(flip(x, k & 4), flip(y, k & 2), flip(c, k & 1)) for k in range(1, N_DEV)]
        local = pltpu.make_async_copy(sin, sout.at[dev], local_sem)
        local.start()
        swaps = [_remote(gin[a], gout[a], send_sems, recv_sems, a, (x, y, 1 - c)) for a in range(n)]
        sends = swaps + [_remote(sin, sout.at[dev], small_send, small_recv, k, peer) for k, peer in enumerate(others)]
        for cp in sends:
            cp.start()
        for cp in swaps:
            cp.wait_recv()
        for k, (px, py, pc) in enumerate(others):
            _remote(sin, sout.at[4 * px + 2 * py + pc], small_send, small_recv, k, (px, py, pc)).wait_recv()
        for cp in sends:
            cp.wait_send()
        local.wait()

    return pl.pallas_call(
        body, name="finish_exchange", in_specs=[ANY] * (n + 1), out_specs=[ANY] * (n + 1),
        out_shape=[jax.ShapeDtypeStruct(g.shape, g.dtype) for g in mine]
        + [jax.ShapeDtypeStruct((N_DEV,) + small.shape, small.dtype)],
        scratch_shapes=[pltpu.SemaphoreType.DMA((n,)), pltpu.SemaphoreType.DMA((n,)),
                        pltpu.SemaphoreType.DMA((N_DEV - 1,)), pltpu.SemaphoreType.DMA((N_DEV - 1,)),
                        pltpu.SemaphoreType.DMA])(*mine, small)


def _elem_tile(rows):
    return 128 if rows % 128 == 0 else (64 if rows % 64 == 0 else rows)


def _sum_chips(slots, own, chip, name):
    g, r, c = slots.shape
    tr = _elem_tile(r)

    def body(chip_ref, s_ref, own_ref, o_ref):
        acc = None
        for k in range(g):
            term = jnp.where(chip_ref[0] == k, own_ref[...], s_ref[k])
            acc = term if acc is None else acc + term
        o_ref[...] = acc

    grid_spec = pltpu.PrefetchScalarGridSpec(
        num_scalar_prefetch=1, grid=(r // tr,),
        in_specs=[pl.BlockSpec((g, tr, c), lambda i, chip_ref: (0, i, 0)),
                  pl.BlockSpec((None, tr, c), lambda i, chip_ref: (chip_ref[0], i, 0))],
        out_specs=pl.BlockSpec((tr, c), lambda i, chip_ref: (i, 0)))
    return pl.pallas_call(
        body, name=name, grid_spec=grid_spec, out_shape=jax.ShapeDtypeStruct((r, c), slots.dtype),
        compiler_params=_params(("parallel",)))(chip.reshape(1).astype(jnp.int32), slots, own)


def _sum_slots(a, name):
    g, r, c = a.shape
    tr = _elem_tile(r)

    def body(a_ref, o_ref):
        acc = a_ref[0]
        for k in range(1, g):
            acc = acc + a_ref[k]
        o_ref[...] = acc

    return pl.pallas_call(
        body, name=name, grid=(r // tr,), in_specs=[pl.BlockSpec((g, tr, c), lambda i: (0, i, 0))],
        out_specs=pl.BlockSpec((tr, c), lambda i: (i, 0)), out_shape=jax.ShapeDtypeStruct((r, c), a.dtype),
        compiler_params=_params(("parallel",)))(a)


def _adamw(wt, ga, gb, m, v, name):
    r, c = wt.shape
    tr = _elem_tile(r)
    two = gb is not None

    def body(*refs):
        if two:
            w_ref, ga_ref, gb_ref, m_ref, v_ref, g_out, d_out, m_out, v_out = refs
            g = ga_ref[...] + gb_ref[...]
        else:
            w_ref, ga_ref, m_ref, v_ref, g_out, d_out, m_out, v_out = refs
            g = ga_ref[...]
        mn = ADAM_B1 * m_ref[...] + (1.0 - ADAM_B1) * g
        vn = ADAM_B2 * v_ref[...] + (1.0 - ADAM_B2) * (g * g)
        m_hat = mn / (1.0 - ADAM_B1 ** ADAM_STEP)
        v_hat = vn / (1.0 - ADAM_B2 ** ADAM_STEP)
        g_out[...] = g
        d_out[...] = -ADAM_LR * (m_hat / (jnp.sqrt(v_hat) + ADAM_EPS) + ADAM_WD * w_ref[...])
        m_out[...] = mn
        v_out[...] = vn

    blk = pl.BlockSpec((tr, c), lambda i: (i, 0))
    args = [wt, ga] + ([gb] if two else []) + [m, v]
    sds = jax.ShapeDtypeStruct((r, c), F32)
    return pl.pallas_call(
        body, name=name, grid=(r // tr,), in_specs=[blk] * len(args), out_specs=[blk] * 4, out_shape=[sds] * 4,
        compiler_params=_params(("parallel",)))(*args)


def _pack(arrs, rows):
    flat = jnp.concatenate([a.reshape(-1) for a in arrs])
    return jnp.pad(flat, (0, rows * LANES - flat.shape[0])).reshape(rows, LANES)


def _unpack(buf, shapes):
    flat = buf.reshape(-1)
    out, off = [], 0
    for shp in shapes:
        size = 1
        for d in shp:
            size *= d
        out.append(flat[off:off + size].reshape(shp))
        off += size
    return out


BIG = ["w_in", "w_out", "w_up", "w_down", "w_ple_gate", "w_ple_proj"]
COL_SHARDED = ["w_in", "w_up", "w_ple_proj"]
SMALL_REPL = ["g_mix", "q_gain", "k_gain", "dw_b", "conv_ln_g", "conv_ln_b", "g_ffn", "ffn_conv_b", "g_ple"]
SMALL_SHARDED = ["dw_w", "ffn_conv_w"]
WEIGHTS = ["g_mix", "w_in", "q_gain", "k_gain", "dw_w", "dw_b", "conv_ln_g", "conv_ln_b", "w_out", "g_ffn", "w_up",
           "ffn_conv_w", "ffn_conv_b", "w_down", "g_ple", "w_ple_gate", "w_ple_proj"]


def _rows_for(n_elems):
    return -(-n_elems // (8 * LANES)) * 8


def kernel(x, p, g_mix, w_in, q_gain, k_gain, dw_w, dw_b, conv_ln_g, conv_ln_b, w_out, g_ffn, w_up, ffn_conv_w, ffn_conv_b, w_down, g_ple, w_ple_gate, w_ple_proj, loss_target, m_g_mix, m_w_in, m_q_gain, m_k_gain, m_dw_w, m_dw_b, m_conv_ln_g, m_conv_ln_b, m_w_out, m_g_ffn, m_w_up, m_ffn_conv_w, m_ffn_conv_b, m_w_down, m_g_ple, m_w_ple_gate, m_w_ple_proj, v_g_mix, v_w_in, v_q_gain, v_k_gain, v_dw_w, v_dw_b, v_conv_ln_g, v_conv_ln_b, v_w_out, v_g_ffn, v_w_up, v_ffn_conv_w, v_ffn_conv_b, v_w_down, v_g_ple, v_w_ple_gate, v_w_ple_proj):
    given = dict(locals())
    strip = lambda n, a: a if n in SMALL_REPL else a[0]
    wts = {n: strip(n, given[n]) for n in WEIGHTS}
    mom = {n: strip(n, given["m_" + n]) for n in WEIGHTS}
    var = {n: strip(n, given["v_" + n]) for n in WEIGHTS}
    chip = 2 * lax.axis_index("x") + lax.axis_index("y")

    small_shard_shapes = [wts[n].shape for n in SMALL_SHARDED]
    filt_rows = _rows_for(sum(wts[n].size for n in SMALL_SHARDED))
    filt = _pack([wts[n] for n in SMALL_SHARDED], filt_rows)
    first_names, rest_names = ["w_in"], ["w_out", "w_up", "w_down", "w_ple_gate", "w_ple_proj"]
    groups = [[wts[n].astype(BF16) for n in first_names] + [filt], [wts[n].astype(BF16) for n in rest_names]]
    own_slot = lambda a: lax.dynamic_update_slice(jnp.zeros((N_CHIPS,) + a.shape, a.dtype), a[None],
                                                  (chip,) + (0,) * a.ndim)
    sems, srcs, lands, token = _exchange_start(groups, [own_slot(a) for grp in groups for a in grp], False,
                                               "gather_start")
    n_first = len(groups[0])
    shard_major = lambda n, g: g if n in COL_SHARDED else g.reshape(-1, g.shape[-1])

    def fetch_first(after):
        _, got = _exchange_wait(sems[0], srcs[:n_first], lands[:n_first], False, after, "gather_wait_first")
        out ={n: shard_major(n, g) for n, g in zip(first_names, got)}
        per_chip = [_unpack(got[-1][k], small_shard_shapes) for k in range(N_CHIPS)]
        for idx, n in enumerate(SMALL_SHARDED):
            out[n] = jnp.concatenate([per_chip[k][idx] for k in range(N_CHIPS)], axis=1)
        return out

    def fetch_rest(after):
        _, got = _exchange_wait(sems[1], srcs[n_first:], lands[n_first:], False, after, "gather_wait_rest")
        return {n: shard_major(n, g) for n, g in zip(rest_names, got)}

    pending = []

    def send(name):
        def start(grads):
            names = list(grads)
            major = [grads[n] if n in COL_SHARDED else grads[n].reshape(N_CHIPS, -1, grads[n].shape[-1]) for n in names]
            sem, src, land, tok = _exchange_start([major], [lax.empty(g.shape, g.dtype) for g in major], True, name)
            pending.append((names, sem[0], src, land))
            return tok[0:1, 0:1]
        return start

    replicated = {n: wts[n] for n in SMALL_REPL}
    replicated["g_mix"] = replicated["g_mix"] + token[0:1, 0:1]
    sq, grad_x, big, small = _local_step(x[0], p[0, 0], loss_target[0], replicated, fetch_first, fetch_rest,
                                         send("scatter_start_first"), send("scatter_start_last"))
    loss = lax.psum(sq * (0.5 / x.shape[-1]), ("x", "y", "c"))

    small_names = SMALL_REPL + SMALL_SHARDED
    small_shapes = [small[n].shape for n in small_names]
    small_rows = _rows_for(sum(small[n].size for n in small_names))
    mine = {}
    for idx, (names, sem, src, land) in enumerate(pending):
        sent, slots = _exchange_wait(sem, src, land, True, grad_x, "scatter_wait_%d" % idx)
        for n, sl, own in zip(names, slots, sent):
            mine[n] = _sum_chips(sl, own, chip, "sum_" + n)
    mine = [mine[n] for n in BIG]
    *theirs, small_slots = _finish_exchange(mine, _pack([small[n] for n in small_names], small_rows))
    small_sum = dict(zip(small_names, _unpack(_sum_slots(small_slots, "sum_small"), small_shapes)))

    outs = {}
    for n, ga, gb in zip(BIG, mine, theirs):
        outs[n] = _adamw(wts[n], ga, gb, mom[n], var[n], "adamw_" + n)
    for n in SMALL_SHARDED:
        width = wts[n].shape[1]
        small_sum[n] = lax.dynamic_slice_in_dim(small_sum[n], chip * width, width, axis=1)
    local_shapes = [wts[n].shape for n in small_names]
    local_rows = _rows_for(sum(wts[n].size for n in small_names))
    packed = _adamw(_pack([wts[n] for n in small_names], local_rows), _pack([small_sum[n] for n in small_names], local_rows),
                    None, _pack([mom[n] for n in small_names], local_rows),
                    _pack([var[n] for n in small_names], local_rows), "adamw_small")
    unpacked = [_unpack(buf, local_shapes) for buf in packed]
    for idx, n in enumerate(small_names):
        outs[n] = [u[idx] for u in unpacked]
    result = [loss, grad_x[None]]
    for part in range(4):
        result += [outs[n][part] if n in SMALL_REPL else outs[n][part][None] for n in WEIGHTS]
    return tuple(result)
```

```python
import functools

import jax
import jax.numpy as jnp
from jax import lax
from jax.experimental import pallas as pl
from jax.experimental.pallas import tpu as pltpu

F32 = jnp.float32
BF16 = jnp.bfloat16
HIGHEST = lax.Precision.HIGHEST
MESH = pl.DeviceIdType.MESH
ANY = pl.BlockSpec(memory_space=pl.ANY)

EPS = 1e-6
HEAD_DIM = 64
N_HEADS = 8
ATTN_W = 512
CONV_W = 512
CONV_K = 31
FFN_K = 3
ATTN_SCALE = 0.125
LANES = 128
CONV_HALO = 32
FFN_HALO = 8
VMEM_LIMIT = 56 * 1024 * 1024

ADAM_LR = 0.001
ADAM_B1 = 0.9
ADAM_B2 = 0.999
ADAM_EPS = 1e-08
ADAM_WD = 0.01
ADAM_STEP = 10

N_CHIPS = 4
N_DEV = 8


def _params(sem):
    return pltpu.CompilerParams(dimension_semantics=sem, vmem_limit_bytes=VMEM_LIMIT)


def _row_tile(s):
    return min(512, s)


def _contract_tile(s):
    return min(1024, s)


def _rstd(x):
    return lax.rsqrt(jnp.mean(x * x, axis=-1, keepdims=True) + EPS)


def _sigmoid(x):
    return 1.0 / (1.0 + jnp.exp(-x))


def _mm(a, b, *, name, dims, grid, a_spec, b_spec, o_spec, o_tile, out_shape, res=None, res_spec=None):
    nk = grid[2]

    def body(*refs):
        if res is None:
            a_ref, b_ref, o_ref, acc_ref = refs
            r_ref = None
        else:
            a_ref, b_ref, r_ref, o_ref, acc_ref = refs
        part = lax.dot_general(a_ref[...], b_ref[...], (dims, ((), ())), preferred_element_type=F32)

        def finish(val):
            if r_ref is not None:
                val = val + r_ref[...]
            o_ref[...] = val.astype(o_ref.dtype)

        if nk == 1:
            finish(part)
        else:
            k = pl.program_id(2)

            @pl.when(k == 0)
            def _():
                acc_ref[...] = part

            @pl.when(k > 0)
            def _():
                acc_ref[...] += part

            @pl.when(k == nk - 1)
            def _():
                finish(acc_ref[...])

    in_specs = [a_spec, b_spec]
    args = [a, b]
    if res is not None:
        in_specs.append(res_spec)
        args.append(res)
    acc_tile = o_tile if nk > 1 else (8, LANES)
    return pl.pallas_call(
        body, name=name, grid=grid, in_specs=in_specs, out_specs=o_spec, out_shape=out_shape,
        scratch_shapes=[pltpu.VMEM(acc_tile, F32)],
        compiler_params=_params(("parallel", "parallel", "arbitrary")))(*args)


NN = ((1,), (0,))
NT = ((1,), (1,))
TN = ((0,), (0,))


def _mm_nn_sharded(a, bg, name, out_dtype=F32, tm=None):
    s, k = a.shape
    g, _, ns = bg.shape
    tm = tm or _row_tile(s)

    def body(a_ref, b_ref, o_ref):
        av = a_ref[...]
        for gi in range(g):
            o_ref[:, gi * ns:(gi + 1) * ns] = jnp.dot(av, b_ref[gi], preferred_element_type=F32).astype(out_dtype)

    return pl.pallas_call(
        body, name=name, grid=(s // tm,),
        in_specs=[pl.BlockSpec((tm, k), lambda i: (i, 0)), pl.BlockSpec((g, k, ns), lambda i: (0, 0, 0))],
        out_specs=pl.BlockSpec((tm, g * ns), lambda i: (i, 0)),
        out_shape=jax.ShapeDtypeStruct((s, g * ns), out_dtype), compiler_params=_params(("parallel",)))(a, bg)


def _mm_nn_full(a, b, name, res=None):
    s, k = a.shape
    n = b.shape[1]
    tm = _row_tile(s)
    rs = pl.BlockSpec((tm, n), lambda i, j, kk: (i, 0))
    return _mm(a, b, name=name, dims=NN, grid=(s // tm, 1, 1),
               a_spec=pl.BlockSpec((tm, k), lambda i, j, kk: (i, 0)),
               b_spec=pl.BlockSpec((k, n), lambda i, j, kk: (0, 0)),
               o_spec=rs, o_tile=(tm, n), out_shape=jax.ShapeDtypeStruct((s, n), F32),
               res=res, res_spec=rs if res is not None else None)


def _mm_nt_full(a, b, name):
    s, n = a.shape
    k = b.shape[0]
    tm = _row_tile(s)
    return _mm(a, b, name=name, dims=NT, grid=(s // tm, 1, 1),
               a_spec=pl.BlockSpec((tm, n), lambda i, j, kk: (i, 0)),
               b_spec=pl.BlockSpec((k, n), lambda i, j, kk: (0, 0)),
               o_spec=pl.BlockSpec((tm, k), lambda i, j, kk: (i, 0)), o_tile=(tm, k),
               out_shape=jax.ShapeDtypeStruct((s, k), F32))


def _mm_nt_sharded(a, bg, name):
    s = a.shape[0]
    g, k, ns = bg.shape
    tm = _row_tile(s)

    def body(a_ref, b_ref, o_ref):
        acc = lax.dot_general(a_ref[:, 0:ns], b_ref[0], (NT, ((), ())), preferred_element_type=F32)
        for gi in range(1, g):
            acc = acc + lax.dot_general(a_ref[:, gi * ns:(gi + 1) * ns], b_ref[gi], (NT, ((), ())),
                                        preferred_element_type=F32)
        o_ref[...] = acc

    return pl.pallas_call(
        body, name=name, grid=(s // tm,),
        in_specs=[pl.BlockSpec((tm, g * ns), lambda i: (i, 0)), pl.BlockSpec((g, k, ns), lambda i: (0, 0, 0))],
        out_specs=pl.BlockSpec((tm, k), lambda i: (i, 0)), out_shape=jax.ShapeDtypeStruct((s, k), F32),
        compiler_params=_params(("parallel",)))(a, bg)


def _mm_tn_sharded(a, b, g, name, n_split=1):
    s, k = a.shape
    ns = b.shape[1] // g
    gs = g // n_split
    tk = _contract_tile(s)

    def body(a_ref, b_ref, o_ref):
        first = pl.program_id(1) == 0
        at = a_ref[...].T
        for gi in range(gs):
            part = jnp.dot(at, b_ref[:, gi * ns:(gi + 1) * ns], preferred_element_type=F32)

            @pl.when(first)
            def _(gi=gi, part=part):
                o_ref[gi] = part

            @pl.when(jnp.logical_not(first))
            def _(gi=gi, part=part):
                o_ref[gi] += part

    return pl.pallas_call(
        body, name=name, grid=(n_split, s // tk),
        in_specs=[pl.BlockSpec((tk, k), lambda j, kk: (kk, 0)), pl.BlockSpec((tk, gs * ns), lambda j, kk: (kk, j))],
        out_specs=pl.BlockSpec((gs, k, ns), lambda j, kk: (j, 0, 0)),
        out_shape=jax.ShapeDtypeStruct((g, k, ns), F32), compiler_params=_params(("parallel", "arbitrary")))(a, b)


def _mm_tn_full(a, b, name, tm):
    s, m = a.shape
    n = b.shape[1]
    tk = _contract_tile(s)
    return _mm(a, b, name=name, dims=TN, grid=(m // tm, 1, s // tk),
               a_spec=pl.BlockSpec((tk, tm), lambda i, j, kk: (kk, i)),
               b_spec=pl.BlockSpec((tk, n), lambda i, j, kk: (kk, 0)),
               o_spec=pl.BlockSpec((tm, n), lambda i, j, kk: (i, 0)), o_tile=(tm, n),
               out_shape=jax.ShapeDtypeStruct((m, n), F32))


def _rms_fwd(x, g, name):
    s, d = x.shape
    ts = _row_tile(s)

    def body(x_ref, g_ref, u_ref):
        xv = x_ref[...]
        u_ref[...] = (xv * _rstd(xv) * g_ref[...]).astype(BF16)

    row = pl.BlockSpec((ts, d), lambda i: (i, 0))
    return pl.pallas_call(
        body, name=name, grid=(s // ts,), in_specs=[row, pl.BlockSpec((1, d), lambda i: (0, 0))],
        out_specs=row, out_shape=jax.ShapeDtypeStruct((s, d), BF16),
        compiler_params=_params(("parallel",)))(x, g)


def _rms_bwd(h, du, g, dh_in, name):
    s, d = h.shape
    ts = _row_tile(s)

    def body(h_ref, du_ref, g_ref, dhin_ref, dh_ref, dhb_ref, gg_ref):
        hv = h_ref[...]
        r = _rstd(hv)
        xhat = hv * r
        duv = du_ref[...]
        dxhat = duv * g_ref[...]
        m = jnp.mean(dxhat * xhat, axis=-1, keepdims=True)
        dh = dhin_ref[...] + r * (dxhat - xhat * m)
        dh_ref[...] = dh
        dhb_ref[...] = dh.astype(BF16)
        part = jnp.sum(duv * xhat, axis=0, keepdims=True)

        @pl.when(pl.program_id(0) == 0)
        def _():
            gg_ref[...] = part

        @pl.when(pl.program_id(0) > 0)
        def _():
            gg_ref[...] += part

    row = pl.BlockSpec((ts, d), lambda i: (i, 0))
    vec = pl.BlockSpec((1, d), lambda i: (0, 0))
    return pl.pallas_call(
        body, name=name, grid=(s // ts,), in_specs=[row, row, vec, row], out_specs=[row, row, vec],
        out_shape=[jax.ShapeDtypeStruct((s, d), F32), jax.ShapeDtypeStruct((s, d), BF16),
                   jax.ShapeDtypeStruct((1, d), F32)],
        compiler_params=_params(("arbitrary",)))(h, du, g, dh_in)


def _head_sum(x, bd):
    return _tri_dot(_split(x), bd)


def _qkv_prep(proj, qg, kg, bd):
    s = proj.shape[0]
    ts = _row_tile(s)

    def body(q_ref, k_ref, v_ref, qg_ref, kg_ref, bd_ref, qs_ref, kh_ref, vb_ref):
        def norm(x, gain):
            ms = _head_sum(x * x, bd_ref[...]) * (1.0 / HEAD_DIM)
            return x * lax.rsqrt(ms + EPS) * gain

        qs_ref[...] = (norm(q_ref[...], qg_ref[...]) * ATTN_SCALE).astype(BF16)
        kh_ref[...] = norm(k_ref[...], kg_ref[...]).astype(BF16)
        vb_ref[...] = v_ref[...].astype(BF16)

    col = lambda c: pl.BlockSpec((ts, ATTN_W), lambda i: (i, c))
    vec = pl.BlockSpec((1, ATTN_W), lambda i: (0, 0))
    out = pl.BlockSpec((ts, ATTN_W), lambda i: (i, 0))
    sds = jax.ShapeDtypeStruct((s, ATTN_W), BF16)
    return pl.pallas_call(
        body, name="qkv_prep", grid=(s // ts,),
        in_specs=[col(0), col(1), col(2), vec, vec, pl.BlockSpec((ATTN_W, ATTN_W), lambda i: (0, 0))],
        out_specs=[out, out, out], out_shape=[sds, sds, sds],
        compiler_params=_params(("parallel",)))(proj, proj, proj, qg, kg, bd)


def _qk_bwd(proj, dqh, dkh, dv, qg, kg, bd):
    s = proj.shape[0]
    ts = _row_tile(s)

    def body(q_ref, k_ref, dqh_ref, dkh_ref, dv_ref, qg_ref, kg_ref, bd_ref, out_ref, gq_ref, gk_ref):
        first = pl.program_id(0) == 0

        def bwd(x, dy, gain, gg_ref):
            ms = _head_sum(x * x, bd_ref[...]) * (1.0 / HEAD_DIM)
            r = lax.rsqrt(ms + EPS)
            xhat = x * r
            dxhat = dy * gain
            m = _head_sum(dxhat * xhat, bd_ref[...]) * (1.0 / HEAD_DIM)
            part = jnp.sum(dy * xhat, axis=0, keepdims=True)

            @pl.when(first)
            def _():
                gg_ref[...] = part

            @pl.when(jnp.logical_not(first))
            def _():
                gg_ref[...] += part

            return r * (dxhat - xhat * m)

        out_ref[:, 0:ATTN_W] = bwd(q_ref[...], dqh_ref[...], qg_ref[...], gq_ref).astype(BF16)
        out_ref[:, ATTN_W:2 * ATTN_W] = bwd(k_ref[...], dkh_ref[...], kg_ref[...], gk_ref).astype(BF16)
        out_ref[:, 2 * ATTN_W:3 * ATTN_W] = dv_ref[...].astype(BF16)

    col = lambda c: pl.BlockSpec((ts, ATTN_W), lambda i: (i, c))
    row = pl.BlockSpec((ts, ATTN_W), lambda i: (i, 0))
    vec = pl.BlockSpec((1, ATTN_W), lambda i: (0, 0))
    return pl.pallas_call(
        body, name="qk_bwd", grid=(s // ts,),
        in_specs=[col(0), col(1), row, row, row, vec, vec, pl.BlockSpec((ATTN_W, ATTN_W), lambda i: (0, 0))],
        out_specs=[pl.BlockSpec((ts, 3 * ATTN_W), lambda i: (i, 0)), vec, vec],
        out_shape=[jax.ShapeDtypeStruct((s, 3 * ATTN_W), BF16), jax.ShapeDtypeStruct((1, ATTN_W), F32),
                   jax.ShapeDtypeStruct((1, ATTN_W), F32)],
        compiler_params=_params(("arbitrary",)))(proj, proj, dqh, dkh, dv, qg, kg, bd)


def _split(x):
    hi = x.astype(BF16)
    return hi, (x - hi.astype(F32)).astype(BF16)


def _tri_dot(parts, tri):
    hi, lo = parts
    return jnp.dot(hi, tri, preferred_element_type=F32) + jnp.dot(lo, tri, preferred_element_type=F32)


def _log_sigmoids(z):
    neg_abs = lax.bitcast_convert_type(lax.bitcast_convert_type(z, jnp.uint32) | jnp.uint32(0x80000000), F32)
    lb = jnp.minimum(z, 0.0) - jnp.log(1.0 + jnp.exp(neg_abs))
    return lb, lb - z


FWD_BLOCKS_PER_TRIP = 4
BWD_BLOCKS_PER_TRIP = 2


def _sweep_key_blocks(tiles, i, per_trip):
    tiles([i], True)

    def loop(step, carry):
        tiles([i - 1 - per_trip * step - j for j in range(per_trip)], False)
        return carry

    lax.fori_loop(0, i // per_trip, loop, 0)
    left = i % per_trip
    size = per_trip // 2
    while size >= 1:
        top = left % (2 * size)

        @pl.when(top >= size)
        def _(top=top, size=size):
            tiles([top - 1 - j for j in range(size)], False)

        size //= 2


def _head_masks():
    lane = lax.broadcasted_iota(jnp.int32, (1, LANES), 1)
    return [lane < HEAD_DIM, lane >= HEAD_DIM]


def _attn_fwd(qs, kh, vb, tri_excl):
    s = qs.shape[0]
    t = tri_excl.shape[0]
    nq = s // t

    def body(q_ref, k_ref, v_ref, tri_ref, o_ref, ob_ref, acc_ref, c_ref):
        i = pl.program_id(1)
        hmask = _head_masks()
        q = q_ref[...]
        qm = [jnp.where(hm, q, jnp.zeros_like(q)) for hm in hmask]
        acc_ref[...] = jnp.zeros_like(acc_ref)
        c_ref[...] = jnp.zeros_like(c_ref)
        causal = (lax.broadcasted_iota(jnp.int32, (t, t), 1) < lax.broadcasted_iota(jnp.int32, (t, t), 0))

        def tiles(kbs, masked):
            tri = tri_ref[...]
            starts = [pl.multiple_of(kb * t, t) for kb in kbs]
            kblks = [k_ref[pl.ds(k0, t), :] for k0 in starts]
            vblks = [v_ref[pl.ds(k0, t), :] for k0 in starts]
            chains = [(j, h) for j in range(len(kbs)) for h in range(2)]
            carry = [c_ref[h] for h in range(2)]
            pv = [None, None]
            lbs, loms, between = {}, {}, {}
            for step in range(len(chains) + 2):
                if step < len(chains):
                    j, h = chains[step]
                    z = lax.dot_general(qm[h], kblks[j], (NT, ((), ())), preferred_element_type=F32)
                    lbs[step], lom = _log_sigmoids(z)
                    loms[step] = jnp.where(causal, lom, 0.0) if masked else lom
                if 0 <= step - 1 < len(chains):
                    between[step - 1] = _tri_dot(_split(loms[step - 1]), tri)
                if 0 <= step - 2 < len(chains):
                    n = step - 2
                    j, h = chains[n]
                    w = jnp.exp(lbs[n] + between[n] + carry[h])
                    if masked:
                        w = jnp.where(causal, w, 0.0)
                    carry[h] = carry[h] + jnp.sum(loms[n], axis=-1, keepdims=True)
                    part = jnp.dot(w.astype(BF16), vblks[j], preferred_element_type=F32)
                    pv[h] = part if pv[h] is None else pv[h] + part
            for h in range(2):
                c_ref[h] = carry[h]
                acc_ref[h] += pv[h]

        _sweep_key_blocks(tiles, i, FWD_BLOCKS_PER_TRIP)
        o = jnp.where(hmask[0], acc_ref[0], acc_ref[1])
        o_ref[...] = o
        ob_ref[...] = o.astype(BF16)

    qspec = pl.BlockSpec((t, LANES), lambda hp, i: (i, hp))
    kspec = pl.BlockSpec((s, LANES), lambda hp, i: (0, hp))
    return pl.pallas_call(
        body, name="attn_fwd", grid=(ATTN_W // LANES, nq),
        in_specs=[qspec, kspec, kspec, pl.BlockSpec((t, t), lambda hp, i: (0, 0))],
        out_specs=[qspec, qspec],
        out_shape=[jax.ShapeDtypeStruct((s, ATTN_W), F32), jax.ShapeDtypeStruct((s, ATTN_W), BF16)],
        scratch_shapes=[pltpu.VMEM((2, t, LANES), F32), pltpu.VMEM((2, t, 1), F32)],
        compiler_params=_params(("parallel", "arbitrary")))(qs, kh, vb, tri_excl)


def _attn_bwd(qs, kh, vb, o, dmix, tri_excl, tri_incl):
    s = qs.shape[0]
    t = tri_excl.shape[0]
    nq = s // t

    def body(q_ref, k_ref, v_ref, o_ref, do_ref, te_ref, ti_ref, dq_ref, dk_ref, dv_ref, dqacc_ref, c_ref):
        i = pl.program_id(1)

        @pl.when(i == 0)
        def _():
            dk_ref[...] = jnp.zeros_like(dk_ref)
            dv_ref[...] = jnp.zeros_like(dv_ref)

        hmask = _head_masks()
        q = q_ref[...]
        do = do_ref[...]
        dob = do.astype(BF16)
        prod = dob.astype(F32) * o_ref[...]
        qm =[jnp.where(hm, q, jnp.zeros_like(q)) for hm in hmask]
        dom = [jnp.where(hm, dob, jnp.zeros_like(dob)) for hm in hmask]
        total = [jnp.sum(jnp.where(hm, prod, 0.0), axis=-1, keepdims=True) for hm in hmask]
        dqacc_ref[...] = jnp.zeros_like(dqacc_ref)
        c_ref[...] = jnp.zeros_like(c_ref)
        causal = (lax.broadcasted_iota(jnp.int32, (t, t), 1) < lax.broadcasted_iota(jnp.int32, (t, t), 0))

        def tiles(kbs, masked):
            te = te_ref[...]
            ti = ti_ref[...]
            starts = [pl.multiple_of(kb * t, t) for kb in kbs]
            kblks = [k_ref[pl.ds(k0, t), :] for k0 in starts]
            vblks = [v_ref[pl.ds(k0, t), :] for k0 in starts]
            chains = [(j, h) for j in range(len(kbs)) for h in range(2)]
            c_lom = [c_ref[2 * h] for h in range(2)]
            c_g = [c_ref[2 * h + 1] for h in range(2)]
            lbs, loms, dws, between, wbs, gs, g_after = {}, {}, {}, {}, {}, {}, {}
            dq = [None, None]
            dk = [None] * len(kbs)
            dv = [None] * len(kbs)
            add = lambda acc, part: part if acc is None else acc + part
            for step in range(len(chains) + 3):
                if step < len(chains):
                    j, h = chains[step]
                    z = lax.dot_general(qm[h], kblks[j], (NT, ((), ())), preferred_element_type=F32)
                    dws[step] = lax.dot_general(dom[h], vblks[j], (NT, ((), ())), preferred_element_type=F32)
                    lbs[step], lom = _log_sigmoids(z)
                    loms[step] = jnp.where(causal, lom, 0.0) if masked else lom
                if 0 <= step - 1 < len(chains):
                    between[step - 1] = _tri_dot(_split(loms[step - 1]), te)
                if 0 <= step - 2 < len(chains):
                    n = step - 2
                    h = chains[n][1]
                    w = jnp.exp(lbs[n] + between[n] + c_lom[h])
                    if masked:
                        w = jnp.where(causal, w, 0.0)
                    c_lom[h] = c_lom[h] + jnp.sum(loms[n], axis=-1, keepdims=True)
                    wbs[n] = w.astype(BF16)
                    gs[n] = dws[n] * wbs[n].astype(F32)
                    g_after[n] = _tri_dot(_split(gs[n]), ti)
                if 0 <= step - 3 < len(chains):
                    n = step - 3
                    j, h = chains[n]
                    beta = jnp.exp(lbs[n])
                    dz = gs[n] * (1.0 - beta) - beta * (total[h] - (g_after[n] + c_g[h]))
                    if masked:
                        dz = jnp.where(causal, dz, 0.0)
                    c_g[h] = c_g[h] + jnp.sum(gs[n], axis=-1, keepdims=True)
                    dzb = dz.astype(BF16)
                    dq[h] = add(dq[h], jnp.dot(dzb, kblks[j], preferred_element_type=F32))
                    dk[j] = add(dk[j], lax.dot_general(dzb, qm[h], (TN, ((), ())), preferred_element_type=F32))
                    dv[j] = add(dv[j], lax.dot_general(wbs[n], dom[h], (TN, ((), ())), preferred_element_type=F32))
            for h in range(2):
                c_ref[2 * h] = c_lom[h]
                c_ref[2 * h + 1] = c_g[h]
                dqacc_ref[h] += dq[h]
            for j, k0 in enumerate(starts):
                dk_ref[pl.ds(k0, t), :] += dk[j]
                dv_ref[pl.ds(k0, t), :] += dv[j]

        _sweep_key_blocks(tiles, i, BWD_BLOCKS_PER_TRIP)
        dq_ref[...] = jnp.where(hmask[0], dqacc_ref[0], dqacc_ref[1]) * ATTN_SCALE

    qspec = pl.BlockSpec((t, LANES), lambda hp, i: (i, hp))
    kspec = pl.BlockSpec((s, LANES), lambda hp, i: (0, hp))
    tspec = pl.BlockSpec((t, t), lambda hp, i: (0, 0))
    sds = jax.ShapeDtypeStruct((s, ATTN_W), F32)
    return pl.pallas_call(
        body, name="attn_bwd", grid=(ATTN_W // LANES, nq),
        in_specs=[qspec, kspec, kspec, qspec, qspec, tspec, tspec],
        out_specs=[qspec, kspec, kspec], out_shape=[sds, sds, sds],
        scratch_shapes=[pltpu.VMEM((2, t, LANES), F32), pltpu.VMEM((4, t, 1), F32)],
        compiler_params=_params(("parallel", "arbitrary")))(qs, kh, vb, o, dmix, tri_excl, tri_incl)


CONV_ROWS = 64
CONV_COLS = 256


def _taps(src_ref, w_ref, n_taps, first_row, rows, reverse=False):
    width = src_ref.shape[1]
    cols = min(CONV_COLS, width)
    out = []
    for r0 in range(0, rows, CONV_ROWS):
        for c0 in range(0, width, cols):
            acc = jnp.zeros((CONV_ROWS, cols), F32)
            for k in range(n_taps):
                off = (n_taps - 1 - k) if reverse else k
                acc = acc + w_ref[k:k + 1, c0:c0 + cols] * src_ref[first_row + r0 + off:first_row + r0 + off + CONV_ROWS,
                                                                   c0:c0 + cols]
            out.append(((r0, c0), acc))
    return out


def _conv_fwd(proj, dw_w, dw_b, ln_g, ln_b):
    s = proj.shape[0]
    ts = _row_tile(s)
    hb = ts // CONV_HALO

    def body(a_ref, g_ref, ha_ref, hg_ref, w_ref, b_ref, lg_ref, lb_ref, c1_ref, c3_ref, pad_ref):
        i = pl.program_id(0)
        halo = ha_ref[...] * _sigmoid(hg_ref[...])
        pad_ref[0:CONV_HALO, :] = jnp.where(i > 0, halo, 0.0)
        pad_ref[CONV_HALO:, :] = a_ref[...] * _sigmoid(g_ref[...])
        first = CONV_HALO - (CONV_K - 1)
        for (r0, c0), acc in _taps(pad_ref, w_ref, CONV_K, first, ts):
            c1_ref[r0:r0 + CONV_ROWS, c0:c0 + acc.shape[1]] = acc + b_ref[:, c0:c0 + acc.shape[1]]
        c1 = c1_ref[...]
        xc = c1 - jnp.mean(c1, axis=-1, keepdims=True)
        c2 = xc * _rstd(xc) * lg_ref[...] + lb_ref[...]
        c3_ref[...] = (c2 * _sigmoid(c2)).astype(BF16)

    cur = lambda c: pl.BlockSpec((ts, CONV_W), lambda i: (i, c))
    halo = lambda c: pl.BlockSpec((CONV_HALO, CONV_W), lambda i: (jnp.maximum(i * hb - 1, 0), c))
    vec = pl.BlockSpec((1, CONV_W), lambda i: (0, 0))
    row = pl.BlockSpec((ts, CONV_W), lambda i: (i, 0))
    return pl.pallas_call(
        body, name="conv_fwd", grid=(s // ts,),
        in_specs=[cur(3), cur(4), halo(3), halo(4), pl.BlockSpec((CONV_HALO, CONV_W), lambda i: (0, 0)), vec, vec, vec],
        out_specs=[row, row],
        out_shape=[jax.ShapeDtypeStruct((s, CONV_W), F32), jax.ShapeDtypeStruct((s, CONV_W), BF16)],
        scratch_shapes=[pltpu.VMEM((ts + CONV_HALO, CONV_W), F32)],
        compiler_params=_params(("parallel",)))(proj, proj, proj, proj, dw_w, dw_b, ln_g, ln_b)


def _conv_bwd_ln(dmix, c1, ln_g, ln_b):
    s = c1.shape[0]
    ts = _row_tile(s)

    def body(d_ref, c1_ref, lg_ref, lb_ref, dc1_ref, glg_ref, glb_ref, gb_ref):
        c1v = c1_ref[...]
        xc = c1v - jnp.mean(c1v, axis=-1, keepdims=True)
        r = _rstd(xc)
        xhat = xc * r
        c2 = xhat * lg_ref[...] + lb_ref[...]
        sg = _sigmoid(c2)
        dc2 = d_ref[...] * (sg * (1.0 + c2 * (1.0 - sg)))
        dxhat = dc2 * lg_ref[...]
        dc1 = r * (dxhat - jnp.mean(dxhat, axis=-1, keepdims=True)
                   - xhat * jnp.mean(dxhat * xhat, axis=-1, keepdims=True))
        dc1_ref[...] = dc1
        parts = [(glg_ref, jnp.sum(dc2 * xhat, axis=0, keepdims=True)),
                 (glb_ref, jnp.sum(dc2, axis=0, keepdims=True)),
                 (gb_ref, jnp.sum(dc1, axis=0, keepdims=True))]

        @pl.when(pl.program_id(0) == 0)
        def _():
            for ref, part in parts:
                ref[...] = part

        @pl.when(pl.program_id(0) > 0)
        def _():
            for ref, part in parts:
                ref[...] += part

    row = pl.BlockSpec((ts, CONV_W), lambda i: (i, 0))
    vec = pl.BlockSpec((1, CONV_W), lambda i: (0, 0))
    vsd = jax.ShapeDtypeStruct((1, CONV_W), F32)
    return pl.pallas_call(
        body, name="conv_bwd_ln", grid=(s // ts,),
        in_specs=[pl.BlockSpec((ts, CONV_W), lambda i: (i, 1)), row, vec, vec],
        out_specs=[row, vec, vec, vec],
        out_shape=[jax.ShapeDtypeStruct((s, CONV_W), F32), vsd, vsd, vsd],
        compiler_params=_params(("arbitrary",)))(dmix, c1, ln_g, ln_b)


def _conv_bwd_taps(proj, dc1, dw_w):
    s = proj.shape[0]
    ts = _row_tile(s)
    hb = ts // CONV_HALO
    last = s // CONV_HALO - 1
    nsteps = s // ts

    def body(a_ref, g_ref, ha_ref, hg_ref, d_ref, hd_ref, w_ref, out_ref, gw_ref, pad_ref, dpad_ref):
        i = pl.program_id(0)
        av = a_ref[...]
        sg = _sigmoid(g_ref[...])
        halo = ha_ref[...] * _sigmoid(hg_ref[...])
        pad_ref[0:CONV_HALO, :] = jnp.where(i > 0, halo, 0.0)
        pad_ref[CONV_HALO:, :] = av * sg
        dpad_ref[0:ts, :] = d_ref[...]
        dpad_ref[ts:, :] = jnp.where(i < nsteps - 1, hd_ref[...], 0.0)

        @pl.when(i == 0)
        def _():
            gw_ref[...] = jnp.zeros_like(gw_ref)

        first = CONV_HALO - (CONV_K - 1)
        for k in range(CONV_K):
            gw_ref[k:k + 1, :] += jnp.sum(d_ref[...] * pad_ref[first + k:first + k + ts, :], axis=0, keepdims=True)
        for (r0, c0), dc0 in _taps(dpad_ref, w_ref, CONV_K, 0, ts, reverse=True):
            cs = slice(c0, c0 + dc0.shape[1])
            a_c = a_ref[r0:r0 + CONV_ROWS, cs]
            sg_c = _sigmoid(g_ref[r0:r0 + CONV_ROWS, cs])
            out_ref[r0:r0 + CONV_ROWS, cs] = (dc0 * sg_c).astype(BF16)
            out_ref[r0:r0 + CONV_ROWS, CONV_W + c0:CONV_W + c0 + dc0.shape[1]] = (
                dc0 * a_c * sg_c * (1.0 - sg_c)).astype(BF16)

    cur = lambda c: pl.BlockSpec((ts, CONV_W), lambda i: (i, c))
    halo = lambda c: pl.BlockSpec((CONV_HALO, CONV_W), lambda i: (jnp.maximum(i * hb - 1, 0), c))
    row = pl.BlockSpec((ts, CONV_W), lambda i: (i, 0))
    nxt = pl.BlockSpec((CONV_HALO, CONV_W), lambda i: (jnp.minimum((i + 1) * hb, last), 0))
    wspec = pl.BlockSpec((CONV_HALO, CONV_W), lambda i: (0, 0))
    return pl.pallas_call(
        body, name="conv_bwd_taps", grid=(nsteps,),
        in_specs=[cur(3), cur(4), halo(3), halo(4), row, nxt, wspec],
        out_specs=[pl.BlockSpec((ts, 2 * CONV_W), lambda i: (i, 0)), wspec],
        out_shape=[jax.ShapeDtypeStruct((s, 2 * CONV_W), BF16), jax.ShapeDtypeStruct((CONV_HALO, CONV_W), F32)],
        scratch_shapes=[pltpu.VMEM((ts + CONV_HALO, CONV_W), F32), pltpu.VMEM((ts + CONV_HALO, CONV_W), F32)],
        compiler_params=_params(("arbitrary",)))(proj, proj, proj, proj, dc1, dc1, dw_w)


SQRT_HALF = 0.7071067811865476
INV_SQRT_2PI = 0.3989422804014327


def _gelu_parts(x):
    cdf = 0.5 * (1.0 + lax.erf(x * SQRT_HALF))
    return x * cdf, cdf + x * (INV_SQRT_2PI * jnp.exp(-0.5 * x * x))


def _ffn_tile(dff):
    return dff // 2


def _ffn_gate2(pad_ref, w_ref, b_ref, ts):
    first = FFN_HALO - (FFN_K - 1)
    g2 = b_ref[...] + w_ref[0:1, :] * pad_ref[first:first + ts, :]
    for k in range(1, FFN_K):
        g2 = g2 + w_ref[k:k + 1, :] * pad_ref[first + k:first + k + ts, :]
    return g2


def _ffn_act(up, fw, fb):
    s = up.shape[0]
    dff = up.shape[1] // 2
    tc = _ffn_tile(dff)
    nj = dff // tc
    ts = _row_tile(s) // 2
    hb = ts // FFN_HALO

    def body(g_ref, v_ref, hg_ref, w_ref, b_ref, act_ref, pad_ref):
        i = pl.program_id(0)
        pad_ref[0:FFN_HALO, :] = jnp.where(i > 0, hg_ref[...], 0.0)
        pad_ref[FFN_HALO:, :] = g_ref[...]
        gelu, _ = _gelu_parts(_ffn_gate2(pad_ref, w_ref, b_ref, ts))
        act_ref[...] = (gelu * v_ref[...]).astype(BF16)

    return pl.pallas_call(
        body, name="ffn_act", grid=(s // ts, nj),
        in_specs=[pl.BlockSpec((ts, tc), lambda i, j: (i, j)), pl.BlockSpec((ts, tc), lambda i, j: (i, j + nj)),
                  pl.BlockSpec((FFN_HALO, tc), lambda i, j: (jnp.maximum(i * hb - 1, 0), j)),
                  pl.BlockSpec((FFN_HALO, tc), lambda i, j: (0, j)), pl.BlockSpec((1, tc), lambda i, j: (0, j))],
        out_specs=pl.BlockSpec((ts, tc), lambda i, j: (i, j)),
        out_shape=jax.ShapeDtypeStruct((s, dff), BF16),
        scratch_shapes=[pltpu.VMEM((ts + FFN_HALO, tc), F32)],
        compiler_params=_params(("parallel", "parallel")))(up, up, up, fw, fb)


def _ffn_bwd_act(dact, up, fw, fb):
    s = up.shape[0]
    dff = up.shape[1] // 2
    tc = _ffn_tile(dff)
    nj = dff // tc
    ts = _row_tile(s) // 2
    hb = ts // FFN_HALO

    def body(d_ref, g_ref, v_ref, hg_ref, w_ref, b_ref, dg2_ref, dval_ref, gw_ref, gb_ref, pad_ref):
        i = pl.program_id(1)
        pad_ref[0:FFN_HALO, :] = jnp.where(i > 0, hg_ref[...], 0.0)
        pad_ref[FFN_HALO:, :] = g_ref[...]
        gelu, dgelu = _gelu_parts(_ffn_gate2(pad_ref, w_ref, b_ref, ts))
        dactv = d_ref[...]
        dval_ref[...] = (dactv * gelu).astype(BF16)
        dg2 = dactv * v_ref[...] * dgelu
        dg2_ref[...] = dg2

        @pl.when(i == 0)
        def _():
            gw_ref[...] = jnp.zeros_like(gw_ref)
            gb_ref[...] = jnp.zeros_like(gb_ref)

        gb_ref[...] += jnp.sum(dg2, axis=0, keepdims=True)
        first = FFN_HALO - (FFN_K - 1)
        for k in range(FFN_K):
            gw_ref[k:k + 1, :] += jnp.sum(dg2 * pad_ref[first + k:first + k + ts, :], axis=0, keepdims=True)

    blk = pl.BlockSpec((ts, tc), lambda j, i: (i, j))
    wspec = pl.BlockSpec((FFN_HALO, tc), lambda j, i: (0, j))
    bspec = pl.BlockSpec((1, tc), lambda j, i: (0, j))
    return pl.pallas_call(
        body, name="ffn_bwd_act", grid=(nj, s // ts),
        in_specs=[blk, blk, pl.BlockSpec((ts, tc), lambda j, i: (i, j + nj)),
                  pl.BlockSpec((FFN_HALO, tc), lambda j, i: (jnp.maximum(i * hb - 1, 0), j)), wspec, bspec],
        out_specs=[blk, pl.BlockSpec((ts, tc), lambda j, i: (i, j + nj)), wspec, bspec],
        out_shape=[jax.ShapeDtypeStruct((s, dff), F32), jax.ShapeDtypeStruct((s, 2 * dff), BF16),
                   jax.ShapeDtypeStruct((FFN_HALO, dff), F32), jax.ShapeDtypeStruct((1, dff), F32)],
        scratch_shapes=[pltpu.VMEM((ts + FFN_HALO, tc), F32)],
        compiler_params=_params(("parallel", "arbitrary")))(dact, up, up, up, fw, fb)


def _ffn_bwd_conv(dg2, fw, dup):
    s, dff = dg2.shape
    tc = _ffn_tile(dff)
    ts = _row_tile(s) // 2
    hb = ts // FFN_HALO
    last = s // FFN_HALO - 1
    nsteps = s // ts

    def body(d_ref, hd_ref, w_ref, dup_ref, out_ref, pad_ref):
        i = pl.program_id(0)
        pad_ref[0:ts, :] = d_ref[...]
        pad_ref[ts:, :] = jnp.where(i < nsteps - 1, hd_ref[...], 0.0)
        dg = w_ref[0:1, :] * pad_ref[FFN_K - 1:FFN_K - 1 + ts, :]
        for k in range(1, FFN_K):
            dg = dg + w_ref[k:k + 1, :] * pad_ref[FFN_K - 1 - k:FFN_K - 1 - k + ts, :]
        out_ref[...] = dg.astype(BF16)

    blk = pl.BlockSpec((ts, tc), lambda i, j: (i, j))
    return pl.pallas_call(
        body, name="ffn_bwd_conv", grid=(nsteps, dff // tc),
        in_specs=[blk, pl.BlockSpec((FFN_HALO, tc), lambda i, j: (jnp.minimum((i + 1) * hb, last), j)),
                  pl.BlockSpec((FFN_HALO, tc), lambda i, j: (0, j)), ANY],
        out_specs=blk, out_shape=jax.ShapeDtypeStruct(dup.shape, BF16), input_output_aliases={3: 0},
        scratch_shapes=[pltpu.VMEM((ts + FFN_HALO, tc), F32)],
        compiler_params=_params(("parallel", "parallel")))(dg2, dg2, fw, dup)


def _ple_loss(h2, zg, pp, target):
    s, d = h2.shape
    ts = _row_tile(s)

    def body(h_ref, z_ref, p_ref, t_ref, dh_ref, dpp_ref, dz_ref, loss_ref):
        pg = _sigmoid(z_ref[...])
        ppv = p_ref[...]
        diff = h_ref[...] + pg * ppv - t_ref[...]
        dh = diff * (1.0 / d)
        dh_ref[...] = dh
        dpp_ref[...] = (dh * pg).astype(BF16)
        dz_ref[...] = (dh * ppv * pg * (1.0 - pg)).astype(BF16)
        part = jnp.sum(jnp.sum(diff * diff, axis=0, keepdims=True), axis=1, keepdims=True)

        @pl.when(pl.program_id(0) == 0)
        def _():
            loss_ref[...] = jnp.zeros_like(loss_ref)

        loss_ref[...] += jnp.broadcast_to(part, loss_ref.shape)

    row = pl.BlockSpec((ts, d), lambda i: (i, 0))
    return pl.pallas_call(
        body, name="ple_loss", grid=(s // ts,), in_specs=[row, row, row, row],
        out_specs=[row, row, row, pl.BlockSpec((8, LANES), lambda i: (0, 0))],
        out_shape=[jax.ShapeDtypeStruct((s, d), F32), jax.ShapeDtypeStruct((s, d), BF16),
                   jax.ShapeDtypeStruct((s, d), BF16), jax.ShapeDtypeStruct((8, LANES), F32)],
        compiler_params=_params(("arbitrary",)))(h2, zg, pp, target)


def _local_step(x, p, target, w, fetch_first, fetch_rest, send_first, send_last):
    s = x.shape[0]
    t = min(256, s)
    tri = jnp.tril(jnp.ones((t, t), F32))
    tri_incl = tri.astype(BF16)
    tri_excl = jnp.tril(jnp.ones((t, t), F32), -1).astype(BF16)
    bd = jnp.kron(jnp.eye(N_HEADS, dtype=F32), jnp.ones((HEAD_DIM, HEAD_DIM), F32)).astype(BF16)
    qg = jnp.tile(w["q_gain"], (1, N_HEADS))
    kg = jnp.tile(w["k_gain"], (1, N_HEADS))
    pb = p.astype(BF16)

    u1 = _rms_fwd(x, w["g_mix"], "rms_mix")
    w = {**w, **fetch_first(u1)}
    dw_w = jnp.pad(w["dw_w"], ((0, CONV_HALO - CONV_K), (0, 0)))
    fw = jnp.pad(w["ffn_conv_w"], ((0, FFN_HALO - FFN_K), (0, 0)))
    proj = _mm_nn_sharded(u1, w["w_in"], "mm_in")
    qs, kh, vb = _qkv_prep(proj, qg, kg, bd)
    o, ob = _attn_fwd(qs, kh, vb, tri_excl)
    w = {**w, **fetch_rest(o)}
    c1, c3 = _conv_fwd(proj, dw_w, w["dw_b"], w["conv_ln_g"], w["conv_ln_b"])
    mix = jnp.concatenate([ob, c3], axis=1)
    h1 = _mm_nn_full(mix, w["w_out"], "mm_out", res=x)
    u2 = _rms_fwd(h1, w["g_ffn"], "rms_ffn")
    up = _mm_nn_sharded(u2, w["w_up"], "mm_up", tm=_row_tile(s) // 2)
    act = _ffn_act(up, fw, w["ffn_conv_b"])
    h2 = _mm_nn_full(act, w["w_down"], "mm_down", res=h1)
    u3 = _rms_fwd(h2, w["g_ple"], "rms_ple")
    zg = _mm_nn_full(u3, w["w_ple_gate"], "mm_ple_gate")
    pp = _mm_nn_sharded(pb, w["w_ple_proj"], "mm_ple_proj")
    dh3, dpp, dz, sq = _ple_loss(h2, zg, pp, target)

    big = {}
    small = {}
    big["w_ple_proj"] = _mm_tn_sharded(pb, dpp, N_CHIPS, "mm_g_ple_proj")
    big["w_ple_gate"] = _mm_tn_full(u3, dz, "mm_g_ple_gate", tm=u3.shape[1])
    du3 = _mm_nt_full(dz, w["w_ple_gate"], "mm_d_ple_gate")
    dh2, dh2b, small["g_ple"] = _rms_bwd(h2, du3, w["g_ple"], dh3, "rms_ple_bwd")
    big["w_down"] = _mm_tn_full(act, dh2b, "mm_g_down", tm=act.shape[1] // 2)
    dact = _mm_nt_full(dh2b, w["w_down"], "mm_d_down")
    dg2, dup, gfw, small["ffn_conv_b"] = _ffn_bwd_act(dact, up, fw, w["ffn_conv_b"])
    small["ffn_conv_w"] = gfw[:FFN_K]
    dup = _ffn_bwd_conv(dg2, fw, dup)
    big["w_up"] = _mm_tn_sharded(u2, dup, N_CHIPS, "mm_g_up", n_split=2)
    du2 = _mm_nt_sharded(dup, w["w_up"], "mm_d_up")
    dh1, dh1b, small["g_ffn"] = _rms_bwd(h1, du2, w["g_ffn"], dh2, "rms_ffn_bwd")
    big["w_out"] = _mm_tn_full(mix, dh1b, "mm_g_out", tm=mix.shape[1])
    sent = send_first({n: big[n] for n in ("w_ple_proj", "w_ple_gate", "w_down", "w_up", "w_out")})
    dmix = _mm_nt_full(dh1b, w["w_out"], "mm_d_out")
    dc1, small["conv_ln_g"], small["conv_ln_b"], small["dw_b"] = _conv_bwd_ln(dmix, c1, w["conv_ln_g"], w["conv_ln_b"])
    dcacg, gdw = _conv_bwd_taps(proj, dc1, dw_w)
    small["dw_w"] = gdw[:CONV_K]
    dqh, dkh, dv = _attn_bwd(qs, kh, vb, o, dmix, tri_excl + sent.astype(BF16), tri_incl)
    dqkv, gq, gk = _qk_bwd(proj, dqh, dkh, dv, qg, kg, bd)
    small["q_gain"] = gq.reshape(N_HEADS, HEAD_DIM).sum(axis=0, keepdims=True)
    small["k_gain"] = gk.reshape(N_HEADS, HEAD_DIM).sum(axis=0, keepdims=True)
    dproj = jnp.concatenate([dqkv, dcacg], axis=1)
    big["w_in"] = _mm_tn_sharded(u1, dproj, N_CHIPS, "mm_g_in")
    sent = send_last({"w_in": big["w_in"]})
    du1 = _mm_nt_sharded(dproj, w["w_in"], "mm_d_in")
    grad_x, _, small["g_mix"] = _rms_bwd(x, du1, w["g_mix"] + sent, dh1, "rms_mix_bwd")
    return sq[0, 0], grad_x, big, small


def _position():
    x, y, c = lax.axis_index("x"), lax.axis_index("y"), lax.axis_index("c")
    return x, y, c, [(1 - x, y), (x, 1 - y), (1 - x, 1 - y)]


def _remote(src, dst, send_sems, recv_sems, k, to):
    return pltpu.make_async_remote_copy(src_ref=src, dst_ref=dst, send_sem=send_sems.at[k], recv_sem=recv_sems.at[k],
                                        device_id=to, device_id_type=MESH)


HBM = pl.BlockSpec(memory_space=pltpu.HBM)
SEM = pl.BlockSpec(memory_space=pltpu.SEMAPHORE)
EFFECT = pltpu.SideEffectType.DATAFLOW_SIDE_EFFECTING


def _chip_copies(src_refs, land_refs, send_sems, recv_sems, per_chip, landed):
    x, y, c, chips = _position()
    me = 2 * x + y
    out = []
    for a, (src, land) in enumerate(zip(src_refs, land_refs)):
        for j, (px, py) in enumerate(chips):
            peer = 2 * px + py
            out.append(_remote(src.at[peer] if per_chip else src, land.at[peer if landed else me],
                               send_sems, recv_sems, 3 * a + j, (px, py, c)))
    return out


def _exchange_start(groups, lands, per_chip, name):
    srcs = [a for grp in groups for a in grp]
    n, ng = len(srcs), len(groups)

    def body(*refs):
        src_refs, land_refs = refs[:n], refs[n:2 * n]
        sems = refs[2 * n:2 * n + 2 * ng]
        token = refs[-1]
        at = 0
        for gi, grp in enumerate(groups):
            for cp in _chip_copies(src_refs[at:at + len(grp)], land_refs[at:at + len(grp)], sems[2 * gi],
                                   sems[2 * gi + 1], per_chip, False):
                cp.start()
            at += len(grp)
        token[...] = jnp.zeros_like(token)

    sem_shapes = [pltpu.SemaphoreType.DMA((3 * len(grp),)) for grp in groups for _ in range(2)]
    thru = [pltpu.HBM(a.shape, a.dtype) for a in srcs + lands]
    outs = pl.pallas_call(
        body, name=name, in_specs=[HBM] * (2 * n),
        out_specs=tuple([SEM] * (2 * ng) + [HBM] * (2 * n) + [pl.BlockSpec(memory_space=pltpu.VMEM)]),
        out_shape=tuple(sem_shapes + thru + [jax.ShapeDtypeStruct((8, LANES), F32)]),
        input_output_aliases={k: 2 * ng + k for k in range(2 * n)},
        compiler_params=pltpu.CompilerParams(has_side_effects=EFFECT),
    )(*[pltpu.with_memory_space_constraint(a, pltpu.HBM) for a in srcs + lands])
    sems = [(outs[2 * gi], outs[2 * gi + 1]) for gi in range(ng)]
    return sems, list(outs[2 * ng:2 * ng + n]), list(outs[2 * ng + n:2 * ng + 2 * n]), outs[-1]


def _exchange_wait(sems, srcs, lands, per_chip, after, name):
    n = len(srcs)
    send_sems, recv_sems = sems

    def body(*refs):
        src_refs, land_refs = refs[:n], refs[n:2 * n]
        send_ref, recv_ref = refs[2 * n], refs[2 * n + 1]
        for cp in _chip_copies(src_refs, land_refs, send_ref, recv_ref, per_chip, True):
            cp.wait_send()
            cp.wait_recv()

    outs = pl.pallas_call(
        body, name=name, in_specs=[HBM] * (2 * n) + [SEM, SEM, ANY], out_specs=tuple([HBM] * (2 * n)),
        out_shape=tuple(pltpu.HBM(a.shape, a.dtype) for a in srcs + lands),
        input_output_aliases={k: k for k in range(2 * n)},
        compiler_params=pltpu.CompilerParams(has_side_effects=EFFECT),
    )(*srcs, *lands, send_sems, recv_sems, after)
    return list(outs[:n]), list(outs[n:])


def _finish_exchange(mine, small):
    n = len(mine)

    def body(*refs):
        gin, sin = refs[:n], refs[n]
        gout, sout = refs[n + 1:2 * n + 1], refs[2 * n + 1]
        send_sems, recv_sems, small_send, small_recv, local_sem = refs[2 * n + 2:]
        x, y, c, _ = _position()
        dev = 4 * x + 2 * y + c
        flip = lambda v, bit: 1 - v if bit else v
        others = [(flip(x, k & 4), flip(y, k & 2), flip(c, k & 1)) for k in range(1, N_DEV)]
        local = pltpu.make_async_copy(sin, sout.at[dev], local_sem)
        local.start()
        swaps = [_remote(gin[a], gout[a], send_sems, recv_sems, a, (x, y, 1 - c)) for a in range(n)]
        sends = swaps + [_remote(sin, sout.at[dev], small_send, small_recv, k, peer) for k, peer in enumerate(others)]
        for cp in sends:
            cp.start()
        for cp in swaps:
            cp.wait_recv()
        for k, (px, py, pc) in enumerate(others):
            _remote(sin, sout.at[4 * px + 2 * py + pc], small_send, small_recv, k, (px, py, pc)).wait_recv()
        for cp in sends:
            cp.wait_send()
        local.wait()

    return pl.pallas_call(
        body, name="finish_exchange", in_specs=[ANY] * (n + 1), out_specs=[ANY] * (n + 1),
        out_shape=[jax.ShapeDtypeStruct(g.shape, g.dtype) for g in mine]
        + [jax.ShapeDtypeStruct((N_DEV,) + small.shape, small.dtype)],
        scratch_shapes=[pltpu.SemaphoreType.DMA((n,)), pltpu.SemaphoreType.DMA((n,)),
                        pltpu.SemaphoreType.DMA((N_DEV - 1,)), pltpu.SemaphoreType.DMA((N_DEV - 1,)),
                        pltpu.SemaphoreType.DMA])(*mine, small)


def _elem_tile(rows):
    return 128 if rows % 128 == 0 else (64 if rows % 64 == 0 else rows)


def _sum_chips(slots, own, chip, name):
    g, r, c = slots.shape
    tr = _elem_tile(r)

    def body(chip_ref, s_ref, own_ref, o_ref):
        acc = None
        for k in range(g):
            term = jnp.where(chip_ref[0] == k, own_ref[...], s_ref[k])
            acc = term if acc is None else acc + term
        o_ref[...] = acc

    grid_spec = pltpu.PrefetchScalarGridSpec(
        num_scalar_prefetch=1, grid=(r // tr,),
        in_specs=[pl.BlockSpec((g, tr, c), lambda i, chip_ref: (0, i, 0)),
                  pl.BlockSpec((None, tr, c), lambda i, chip_ref: (chip_ref[0], i, 0))],
        out_specs=pl.BlockSpec((tr, c), lambda i, chip_ref: (i, 0)))
    return pl.pallas_call(
        body, name=name, grid_spec=grid_spec, out_shape=jax.ShapeDtypeStruct((r, c), slots.dtype),
        compiler_params=_params(("parallel",)))(chip.reshape(1).astype(jnp.int32), slots, own)


def _sum_slots(a, name):
    g, r, c = a.shape
    tr = _elem_tile(r)

    def body(a_ref, o_ref):
        acc = a_ref[0]
        for k in range(1, g):
            acc = acc + a_ref[k]
        o_ref[...] = acc

    return pl.pallas_call(
        body, name=name, grid=(r // tr,), in_specs=[pl.BlockSpec((g, tr, c), lambda i: (0, i, 0))],
        out_specs=pl.BlockSpec((tr, c), lambda i: (i, 0)), out_shape=jax.ShapeDtypeStruct((r, c), a.dtype),
        compiler_params=_params(("parallel",)))(a)


def _adamw(wt, ga, gb, m, v, name):
    r, c = wt.shape
    tr = _elem_tile(r)
    two = gb is not None

    def body(*refs):
        if two:
            w_ref, ga_ref, gb_ref, m_ref, v_ref, g_out, d_out, m_out, v_out = refs
            g = ga_ref[...] + gb_ref[...]
        else:
            w_ref, ga_ref, m_ref, v_ref, g_out, d_out, m_out, v_out = refs
            g = ga_ref[...]
        mn = ADAM_B1 * m_ref[...] + (1.0 - ADAM_B1) * g
        vn = ADAM_B2 * v_ref[...] + (1.0 - ADAM_B2) * (g * g)
        m_hat = mn / (1.0 - ADAM_B1 ** ADAM_STEP)
        v_hat = vn / (1.0 - ADAM_B2 ** ADAM_STEP)
        g_out[...] = g
        d_out[...] = -ADAM_LR * (m_hat / (jnp.sqrt(v_hat) + ADAM_EPS) + ADAM_WD * w_ref[...])
        m_out[...] = mn
        v_out[...] = vn

    blk = pl.BlockSpec((tr, c), lambda i: (i, 0))
    args = [wt, ga] + ([gb] if two else []) + [m, v]
    sds = jax.ShapeDtypeStruct((r, c), F32)
    return pl.pallas_call(
        body, name=name, grid=(r // tr,), in_specs=[blk] * len(args), out_specs=[blk] * 4, out_shape=[sds] * 4,
        compiler_params=_params(("parallel",)))(*args)


def _pack(arrs, rows):
    flat = jnp.concatenate([a.reshape(-1) for a in arrs])
    return jnp.pad(flat, (0, rows * LANES - flat.shape[0])).reshape(rows, LANES)


def _unpack(buf, shapes):
    flat = buf.reshape(-1)
    out, off = [], 0
    for shp in shapes:
        size = 1
        for d in shp:
            size *= d
        out.append(flat[off:off + size].reshape(shp))
        off += size
    return out


BIG = ["w_in", "w_out", "w_up", "w_down", "w_ple_gate", "w_ple_proj"]
COL_SHARDED = ["w_in", "w_up", "w_ple_proj"]
SMALL_REPL = ["g_mix", "q_gain", "k_gain", "dw_b", "conv_ln_g", "conv_ln_b", "g_ffn", "ffn_conv_b", "g_ple"]
SMALL_SHARDED = ["dw_w", "ffn_conv_w"]
WEIGHTS = ["g_mix", "w_in", "q_gain", "k_gain", "dw_w", "dw_b", "conv_ln_g", "conv_ln_b", "w_out", "g_ffn", "w_up",
           "ffn_conv_w", "ffn_conv_b", "w_down", "g_ple", "w_ple_gate", "w_ple_proj"]


def _rows_for(n_elems):
    return -(-n_elems // (8 * LANES)) * 8


def kernel(x, p, g_mix, w_in, q_gain, k_gain, dw_w, dw_b, conv_ln_g, conv_ln_b, w_out, g_ffn, w_up, ffn_conv_w, ffn_conv_b, w_down, g_ple, w_ple_gate, w_ple_proj, loss_target, m_g_mix, m_w_in, m_q_gain, m_k_gain, m_dw_w, m_dw_b, m_conv_ln_g, m_conv_ln_b, m_w_out, m_g_ffn, m_w_up, m_ffn_conv_w, m_ffn_conv_b, m_w_down, m_g_ple, m_w_ple_gate, m_w_ple_proj, v_g_mix, v_w_in, v_q_gain, v_k_gain, v_dw_w, v_dw_b, v_conv_ln_g, v_conv_ln_b, v_w_out, v_g_ffn, v_w_up, v_ffn_conv_w, v_ffn_conv_b, v_w_down, v_g_ple, v_w_ple_gate, v_w_ple_proj):
    given = dict(locals())
    strip = lambda n, a: a if n in SMALL_REPL else a[0]
    wts = {n: strip(n, given[n]) for n in WEIGHTS}
    mom = {n: strip(n, given["m_" + n]) for n in WEIGHTS}
    var = {n: strip(n, given["v_" + n]) for n in WEIGHTS}
    chip = 2 * lax.axis_index("x") + lax.axis_index("y")

    small_shard_shapes = [wts[n].shape for n in SMALL_SHARDED]
    filt_rows = _rows_for(sum(wts[n].size for n in SMALL_SHARDED))
    filt = _pack([wts[n] for n in SMALL_SHARDED], filt_rows)
    first_names, rest_names = ["w_in"], ["w_out", "w_up", "w_down", "w_ple_gate", "w_ple_proj"]
    groups = [[wts[n].astype(BF16) for n in first_names] + [filt], [wts[n].astype(BF16) for n in rest_names]]
    own_slot = lambda a: lax.dynamic_update_slice(jnp.zeros((N_CHIPS,) + a.shape, a.dtype), a[None],
                                                  (chip,) + (0,) * a.ndim)
    sems, srcs, lands, token = _exchange_start(groups, [own_slot(a) for grp in groups for a in grp], False,
                                               "gather_start")
    n_first = len(groups[0])
    shard_major = lambda n, g: g if n in COL_SHARDED else g.reshape(-1, g.shape[-1])

    def fetch_first(after):
        _, got = _exchange_wait(sems[0], srcs[:n_first], lands[:n_first], False, after, "gather_wait_first")
        out ={n: shard_major(n, g) for n, g in zip(first_names, got)}
        per_chip = [_unpack(got[-1][k], small_shard_shapes) for k in range(N_CHIPS)]
        for idx, n in enumerate(SMALL_SHARDED):
            out[n] = jnp.concatenate([per_chip[k][idx] for k in range(N_CHIPS)], axis=1)
        return out

    def fetch_rest(after):
        _, got = _exchange_wait(sems[1], srcs[n_first:], lands[n_first:], False, after, "gather_wait_rest")
        return {n: shard_major(n, g) for n, g in zip(rest_names, got)}

    pending = []

    def send(name):
        def start(grads):
            names = list(grads)
            major = [grads[n] if n in COL_SHARDED else grads[n].reshape(N_CHIPS, -1, grads[n].shape[-1]) for n in names]
            sem, src, land, tok = _exchange_start([major], [lax.empty(g.shape, g.dtype) for g in major], True, name)
            pending.append((names, sem[0], src, land))
            return tok[0:1, 0:1]
        return start

    replicated = {n: wts[n] for n in SMALL_REPL}
    replicated["g_mix"] = replicated["g_mix"] + token[0:1, 0:1]
    sq, grad_x, big, small = _local_step(x[0], p[0, 0], loss_target[0], replicated, fetch_first, fetch_rest,
                                         send("scatter_start_first"), send("scatter_start_last"))
    loss = lax.psum(sq * (0.5 / x.shape[-1]), ("x", "y", "c"))

    small_names = SMALL_REPL + SMALL_SHARDED
    small_shapes = [small[n].shape for n in small_names]
    small_rows = _rows_for(sum(small[n].size for n in small_names))
    mine = {}
    for idx, (names, sem, src, land) in enumerate(pending):
        sent, slots = _exchange_wait(sem, src, land, True, grad_x, "scatter_wait_%d" % idx)
        for n, sl, own in zip(names, slots, sent):
            mine[n] = _sum_chips(sl, own, chip, "sum_" + n)
    mine = [mine[n] for n in BIG]
    *theirs, small_slots = _finish_exchange(mine, _pack([small[n] for n in small_names], small_rows))
    small_sum = dict(zip(small_names, _unpack(_sum_slots(small_slots, "sum_small"), small_shapes)))

    outs = {}
    for n, ga, gb in zip(BIG, mine, theirs):
        outs[n] = _adamw(wts[n], ga, gb, mom[n], var[n], "adamw_" + n)
    for n in SMALL_SHARDED:
        width = wts[n].shape[1]
        small_sum[n] = lax.dynamic_slice_in_dim(small_sum[n], chip * width, width, axis=1)
    local_shapes = [wts[n].shape for n in small_names]
    local_rows = _rows_for(sum(wts[n].size for n in small_names))
    packed = _adamw(_pack([wts[n] for n in small_names], local_rows), _pack([small_sum[n] for n in small_names], local_rows),
                    None, _pack([mom[n] for n in small_names], local_rows),
                    _pack([var[n] for n in small_names], local_rows), "adamw_small")
    unpacked = [_unpack(buf, local_shapes) for buf in packed]
    for idx, n in enumerate(small_names):
        outs[n] = [u[idx] for u in unpacked]
    result = [loss, grad_x[None]]
    for part in range(4):
        result += [outs[n][part] if n in SMALL_REPL else outs[n][part][None] for n in WEIGHTS]
    return tuple(result)
```

```python
import functools

import jax
import jax.numpy as jnp
from jax import lax
from jax.experimental import pallas as pl
from jax.experimental.pallas import tpu as pltpu

F32 = jnp.float32
BF16 = jnp.bfloat16
HIGHEST = lax.Precision.HIGHEST
MESH = pl.DeviceIdType.MESH
ANY = pl.BlockSpec(memory_space=pl.ANY)

EPS = 1e-6
HEAD_DIM = 64
N_HEADS = 8
ATTN_W = 512
CONV_W = 512
CONV_K = 31
FFN_K = 3
ATTN_SCALE = 0.125
LANES = 128
CONV_HALO = 32
FFN_HALO = 8
VMEM_LIMIT = 56 * 1024 * 1024

ADAM_LR = 0.001
ADAM_B1 = 0.9
ADAM_B2 = 0.999
ADAM_EPS = 1e-08
ADAM_WD = 0.01
ADAM_STEP = 10

N_CHIPS = 4
N_DEV = 8


def _params(sem):
    return pltpu.CompilerParams(dimension_semantics=sem, vmem_limit_bytes=VMEM_LIMIT)


def _row_tile(s):
    return min(512, s)


def _contract_tile(s):
    return min(1024, s)


def _rstd(x):
    return lax.rsqrt(jnp.mean(x * x, axis=-1, keepdims=True) + EPS)


def _sigmoid(x):
    return 1.0 / (1.0 + jnp.exp(-x))


def _mm(a, b, *, name, dims, grid, a_spec, b_spec, o_spec, o_tile, out_shape, res=None, res_spec=None):
    nk = grid[2]

    def body(*refs):
        if res is None:
            a_ref, b_ref, o_ref, acc_ref = refs
            r_ref = None
        else:
            a_ref, b_ref, r_ref, o_ref, acc_ref = refs
        part = lax.dot_general(a_ref[...], b_ref[...], (dims, ((), ())), preferred_element_type=F32)

        def finish(val):
            if r_ref is not None:
                val = val + r_ref[...]
            o_ref[...] = val.astype(o_ref.dtype)

        if nk == 1:
            finish(part)
        else:
            k = pl.program_id(2)

            @pl.when(k == 0)
            def _():
                acc_ref[...] = part

            @pl.when(k > 0)
            def _():
                acc_ref[...] += part

            @pl.when(k == nk - 1)
            def _():
                finish(acc_ref[...])

    in_specs = [a_spec, b_spec]
    args = [a, b]
    if res is not None:
        in_specs.append(res_spec)
        args.append(res)
    acc_tile = o_tile if nk > 1 else (8, LANES)
    return pl.pallas_call(
        body, name=name, grid=grid, in_specs=in_specs, out_specs=o_spec, out_shape=out_shape,
        scratch_shapes=[pltpu.VMEM(acc_tile, F32)],
        compiler_params=_params(("parallel", "parallel", "arbitrary")))(*args)


NN = ((1,), (0,))
NT = ((1,), (1,))
TN = ((0,), (0,))


def _mm_nn_sharded(a, bg, name, out_dtype=F32, tm=None):
    s, k = a.shape
    g, _, ns = bg.shape
    tm = tm or _row_tile(s)

    def body(a_ref, b_ref, o_ref):
        av = a_ref[...]
        for gi in range(g):
            o_ref[:, gi * ns:(gi + 1) * ns] = jnp.dot(av, b_ref[gi], preferred_element_type=F32).astype(out_dtype)

    return pl.pallas_call(
        body, name=name, grid=(s // tm,),
        in_specs=[pl.BlockSpec((tm, k), lambda i: (i, 0)), pl.BlockSpec((g, k, ns), lambda i: (0, 0, 0))],
        out_specs=pl.BlockSpec((tm, g * ns), lambda i: (i, 0)),
        out_shape=jax.ShapeDtypeStruct((s, g * ns), out_dtype), compiler_params=_params(("parallel",)))(a, bg)


def _mm_nn_full(a, b, name, res=None):
    s, k = a.shape
    n = b.shape[1]
    tm = _row_tile(s)
    rs = pl.BlockSpec((tm, n), lambda i, j, kk: (i, 0))
    return _mm(a, b, name=name, dims=NN, grid=(s // tm, 1, 1),
               a_spec=pl.BlockSpec((tm, k), lambda i, j, kk: (i, 0)),
               b_spec=pl.BlockSpec((k, n), lambda i, j, kk: (0, 0)),
               o_spec=rs, o_tile=(tm, n), out_shape=jax.ShapeDtypeStruct((s, n), F32),
               res=res, res_spec=rs if res is not None else None)


def _mm_nt_full(a, b, name):
    s, n = a.shape
    k = b.shape[0]
    tm = _row_tile(s)
    return _mm(a, b, name=name, dims=NT, grid=(s // tm, 1, 1),
               a_spec=pl.BlockSpec((tm, n), lambda i, j, kk: (i, 0)),
               b_spec=pl.BlockSpec((k, n), lambda i, j, kk: (0, 0)),
               o_spec=pl.BlockSpec((tm, k), lambda i, j, kk: (i, 0)), o_tile=(tm, k),
               out_shape=jax.ShapeDtypeStruct((s, k), F32))


def _mm_nt_sharded(a, bg, name):
    s = a.shape[0]
    g, k, ns = bg.shape
    tm = _row_tile(s)

    def body(a_ref, b_ref, o_ref):
        acc = lax.dot_general(a_ref[:, 0:ns], b_ref[0], (NT, ((), ())), preferred_element_type=F32)
        for gi in range(1, g):
            acc = acc + lax.dot_general(a_ref[:, gi * ns:(gi + 1) * ns], b_ref[gi], (NT, ((), ())),
                                        preferred_element_type=F32)
        o_ref[...] = acc

    return pl.pallas_call(
        body, name=name, grid=(s // tm,),
        in_specs=[pl.BlockSpec((tm, g * ns), lambda i: (i, 0)), pl.BlockSpec((g, k, ns), lambda i: (0, 0, 0))],
        out_specs=pl.BlockSpec((tm, k), lambda i: (i, 0)), out_shape=jax.ShapeDtypeStruct((s, k), F32),
        compiler_params=_params(("parallel",)))(a, bg)


def _mm_tn_sharded(a, b, g, name, n_split=1):
    s, k = a.shape
    ns = b.shape[1] // g
    gs = g // n_split
    tk = _contract_tile(s)

    def body(a_ref, b_ref, o_ref):
        first = pl.program_id(1) == 0
        at = a_ref[...].T
        for gi in range(gs):
            part = jnp.dot(at, b_ref[:, gi * ns:(gi + 1) * ns], preferred_element_type=F32)

            @pl.when(first)
            def _(gi=gi, part=part):
                o_ref[gi] = part

            @pl.when(jnp.logical_not(first))
            def _(gi=gi, part=part):
                o_ref[gi] += part

    return pl.pallas_call(
        body, name=name, grid=(n_split, s // tk),
        in_specs=[pl.BlockSpec((tk, k), lambda j, kk: (kk, 0)), pl.BlockSpec((tk, gs * ns), lambda j, kk: (kk, j))],
        out_specs=pl.BlockSpec((gs, k, ns), lambda j, kk: (j, 0, 0)),
        out_shape=jax.ShapeDtypeStruct((g, k, ns), F32), compiler_params=_params(("parallel", "arbitrary")))(a, b)


def _mm_tn_full(a, b, name, tm):
    s, m = a.shape
    n = b.shape[1]
    tk = _contract_tile(s)
    return _mm(a, b, name=name, dims=TN, grid=(m // tm, 1, s // tk),
               a_spec=pl.BlockSpec((tk, tm), lambda i, j, kk: (kk, i)),
               b_spec=pl.BlockSpec((tk, n), lambda i, j, kk: (kk, 0)),
               o_spec=pl.BlockSpec((tm, n), lambda i, j, kk: (i, 0)), o_tile=(tm, n),
               out_shape=jax.ShapeDtypeStruct((m, n), F32))


def _rms_fwd(x, g, name):
    s, d = x.shape
    ts = _row_tile(s)

    def body(x_ref, g_ref, u_ref):
        xv = x_ref[...]
        u_ref[...] = (xv * _rstd(xv) * g_ref[...]).astype(BF16)

    row = pl.BlockSpec((ts, d), lambda i: (i, 0))
    return pl.pallas_call(
        body, name=name, grid=(s // ts,), in_specs=[row, pl.BlockSpec((1, d), lambda i: (0, 0))],
        out_specs=row, out_shape=jax.ShapeDtypeStruct((s, d), BF16),
        compiler_params=_params(("parallel",)))(x, g)


def _rms_bwd(h, du, g, dh_in, name):
    s, d = h.shape
    ts = _row_tile(s)

    def body(h_ref, du_ref, g_ref, dhin_ref, dh_ref, dhb_ref, gg_ref):
        hv = h_ref[...]
        r = _rstd(hv)
        xhat = hv * r
        duv = du_ref[...]
        dxhat = duv * g_ref[...]
        m = jnp.mean(dxhat * xhat, axis=-1, keepdims=True)
        dh = dhin_ref[...] + r * (dxhat - xhat * m)
        dh_ref[...] = dh
        dhb_ref[...] = dh.astype(BF16)
        part = jnp.sum(duv * xhat, axis=0, keepdims=True)

        @pl.when(pl.program_id(0) == 0)
        def _():
            gg_ref[...] = part

        @pl.when(pl.program_id(0) > 0)
        def _():
            gg_ref[...] += part

    row = pl.BlockSpec((ts, d), lambda i: (i, 0))
    vec = pl.BlockSpec((1, d), lambda i: (0, 0))
    return pl.pallas_call(
        body, name=name, grid=(s // ts,), in_specs=[row, row, vec, row], out_specs=[row, row, vec],
        out_shape=[jax.ShapeDtypeStruct((s, d), F32), jax.ShapeDtypeStruct((s, d), BF16),
                   jax.ShapeDtypeStruct((1, d), F32)],
        compiler_params=_params(("arbitrary",)))(h, du, g, dh_in)


def _head_sum(x, bd):
    return _tri_dot(_split(x), bd)


def _qkv_prep(proj, qg, kg, bd):
    s = proj.shape[0]
    ts = _row_tile(s)

    def body(q_ref, k_ref, v_ref, qg_ref, kg_ref, bd_ref, qs_ref, kh_ref, vb_ref):
        def norm(x, gain):
            ms = _head_sum(x * x, bd_ref[...]) * (1.0 / HEAD_DIM)
            return x * lax.rsqrt(ms + EPS) * gain

        qs_ref[...] = (norm(q_ref[...], qg_ref[...]) * ATTN_SCALE).astype(BF16)
        kh_ref[...] = norm(k_ref[...], kg_ref[...]).astype(BF16)
        vb_ref[...] = v_ref[...].astype(BF16)

    col = lambda c: pl.BlockSpec((ts, ATTN_W), lambda i: (i, c))
    vec = pl.BlockSpec((1, ATTN_W), lambda i: (0, 0))
    out = pl.BlockSpec((ts, ATTN_W), lambda i: (i, 0))
    sds = jax.ShapeDtypeStruct((s, ATTN_W), BF16)
    return pl.pallas_call(
        body, name="qkv_prep", grid=(s // ts,),
        in_specs=[col(0), col(1), col(2), vec, vec, pl.BlockSpec((ATTN_W, ATTN_W), lambda i: (0, 0))],
        out_specs=[out, out, out], out_shape=[sds, sds, sds],
        compiler_params=_params(("parallel",)))(proj, proj, proj, qg, kg, bd)


def _qk_bwd(proj, dqh, dkh, dv, qg, kg, bd):
    s = proj.shape[0]
    ts = _row_tile(s)

    def body(q_ref, k_ref, dqh_ref, dkh_ref, dv_ref, qg_ref, kg_ref, bd_ref, out_ref, gq_ref, gk_ref):
        first = pl.program_id(0) == 0

        def bwd(x, dy, gain, gg_ref):
            ms = _head_sum(x * x, bd_ref[...]) * (1.0 / HEAD_DIM)
            r = lax.rsqrt(ms + EPS)
            xhat = x * r
            dxhat = dy * gain
            m = _head_sum(dxhat * xhat, bd_ref[...]) * (1.0 / HEAD_DIM)
            part = jnp.sum(dy * xhat, axis=0, keepdims=True)

            @pl.when(first)
            def _():
                gg_ref[...] = part

            @pl.when(jnp.logical_not(first))
            def _():
                gg_ref[...] += part

            return r * (dxhat - xhat * m)

        out_ref[:, 0:ATTN_W] = bwd(q_ref[...], dqh_ref[...], qg_ref[...], gq_ref).astype(BF16)
        out_ref[:, ATTN_W:2 * ATTN_W] = bwd(k_ref[...], dkh_ref[...], kg_ref[...], gk_ref).astype(BF16)
        out_ref[:, 2 * ATTN_W:3 * ATTN_W] = dv_ref[...].astype(BF16)

    col = lambda c: pl.BlockSpec((ts, ATTN_W), lambda i: (i, c))
    row = pl.BlockSpec((ts, ATTN_W), lambda i: (i, 0))
    vec = pl.BlockSpec((1, ATTN_W), lambda i: (0, 0))
    return pl.pallas_call(
        body, name="qk_bwd", grid=(s // ts,),
        in_specs=[col(0), col(1), row, row, row, vec, vec, pl.BlockSpec((ATTN_W, ATTN_W), lambda i: (0, 0))],
        out_specs=[pl.BlockSpec((ts, 3 * ATTN_W), lambda i: (i, 0)), vec, vec],
        out_shape=[jax.ShapeDtypeStruct((s, 3 * ATTN_W), BF16), jax.ShapeDtypeStruct((1, ATTN_W), F32),
                   jax.ShapeDtypeStruct((1, ATTN_W), F32)],
        compiler_params=_params(("arbitrary",)))(proj, proj, dqh, dkh, dv, qg, kg, bd)


def _split(x):
    hi = x.astype(BF16)
    return hi, (x - hi.astype(F32)).astype(BF16)


def _tri_dot(parts, tri):
    hi, lo = parts
    return jnp.dot(hi, tri, preferred_element_type=F32) + jnp.dot(lo, tri, preferred_element_type=F32)


def _log_sigmoids(z):
    neg_abs = lax.bitcast_convert_type(lax.bitcast_convert_type(z, jnp.uint32) | jnp.uint32(0x80000000), F32)
    lb = jnp.minimum(z, 0.0) - jnp.log(1.0 + jnp.exp(neg_abs))
    return lb, lb - z


DEAD_LOG_WEIGHT = -106.0


def _sweep_key_blocks(tiles, alive, i):
    @pl.when(i == 0)
    def _():
        tiles([0], [True])

    @pl.when(i > 0)
    def _():
        tiles([i, i - 1], [True, False])

    def more(state):
        kb, live = state
        return jnp.logical_and(kb >= 0, live > 0)

    def step(state):
        kb, _ = state
        tiles([kb], [False])
        return kb - 1, alive().astype(jnp.int32)

    lax.while_loop(more, step, (i - 2, alive().astype(jnp.int32)))


def _head_masks():
    lane = lax.broadcasted_iota(jnp.int32, (1, LANES), 1)
    return [lane < HEAD_DIM, lane >= HEAD_DIM]


def _attn_fwd(qs, kh, vb, tri_excl):
    s = qs.shape[0]
    t = tri_excl.shape[0]
    nq = s // t

    def body(q_ref, k_ref, v_ref, tri_ref, o_ref, ob_ref, acc_ref, c_ref):
        i = pl.program_id(1)
        hmask = _head_masks()
        q = q_ref[...]
        qm = [jnp.where(hm, q, jnp.zeros_like(q)) for hm in hmask]
        acc_ref[...] = jnp.zeros_like(acc_ref)
        c_ref[...] = jnp.zeros_like(c_ref)
        causal = (lax.broadcasted_iota(jnp.int32, (t, t), 1) < lax.broadcasted_iota(jnp.int32, (t, t), 0))

        def tiles(kbs, masked):
            tri = tri_ref[...]
            starts = [pl.multiple_of(kb * t, t) for kb in kbs]
            kblks = [k_ref[pl.ds(k0, t), :] for k0 in starts]
            vblks = [v_ref[pl.ds(k0, t), :] for k0 in starts]
            chains = [(j, h) for j in range(len(kbs)) for h in range(2)]
            carry = [c_ref[h] for h in range(2)]
            pv = [None, None]
            lbs, loms, between = {}, {}, {}
            for step in range(len(chains) + 2):
                if step < len(chains):
                    j, h = chains[step]
                    z = lax.dot_general(qm[h], kblks[j], (NT, ((), ())), preferred_element_type=F32)
                    lbs[step], lom = _log_sigmoids(z)
                    loms[step] = jnp.where(causal, lom, 0.0) if masked[j] else lom
                if 0 <= step - 1 < len(chains):
                    between[step - 1] = _tri_dot(_split(loms[step - 1]), tri)
                if 0 <= step - 2 < len(chains):
                    n = step - 2
                    j, h = chains[n]
                    w = jnp.exp(lbs[n] + between[n] + carry[h])
                    if masked[j]:
                        w = jnp.where(causal, w, 0.0)
                    carry[h] = carry[h] + jnp.sum(loms[n], axis=-1, keepdims=True)
                    part = jnp.dot(w.astype(BF16), vblks[j], preferred_element_type=F32)
                    pv[h] = part if pv[h] is None else pv[h] + part
            for h in range(2):
                c_ref[h] = carry[h]
                acc_ref[h] += pv[h]

        _sweep_key_blocks(tiles, lambda: jnp.max(c_ref[...]) > DEAD_LOG_WEIGHT, i)
        o = jnp.where(hmask[0], acc_ref[0], acc_ref[1])
        o_ref[...] = o
        ob_ref[...] = o.astype(BF16)

    qspec = pl.BlockSpec((t, LANES), lambda hp, i: (i, hp))
    kspec = pl.BlockSpec((s, LANES), lambda hp, i: (0, hp))
    return pl.pallas_call(
        body, name="attn_fwd", grid=(ATTN_W // LANES, nq),
        in_specs=[qspec, kspec, kspec, pl.BlockSpec((t, t), lambda hp, i: (0, 0))],
        out_specs=[qspec, qspec],
        out_shape=[jax.ShapeDtypeStruct((s, ATTN_W), F32), jax.ShapeDtypeStruct((s, ATTN_W), BF16)],
        scratch_shapes=[pltpu.VMEM((2, t, LANES), F32), pltpu.VMEM((2, t, 1), F32)],
        compiler_params=_params(("parallel", "arbitrary")))(qs, kh, vb, tri_excl)


def _attn_bwd(qs, kh, vb, o, dmix, tri_excl, tri_incl):
    s = qs.shape[0]
    t = tri_excl.shape[0]
    nq = s // t

    def body(q_ref, k_ref, v_ref, o_ref, do_ref, te_ref, ti_ref, dq_ref, dk_ref, dv_ref, dqacc_ref, c_ref):
        i = pl.program_id(1)

        @pl.when(i == 0)
        def _():
            dk_ref[...] = jnp.zeros_like(dk_ref)
            dv_ref[...] = jnp.zeros_like(dv_ref)

        hmask = _head_masks()
        q = q_ref[...]
        do = do_ref[...]
        dob = do.astype(BF16)
        prod = dob.astype(F32) * o_ref[...]
        qm =[jnp.where(hm, q, jnp.zeros_like(q)) for hm in hmask]
        dom = [jnp.where(hm, dob, jnp.zeros_like(dob)) for hm in hmask]
        total = [jnp.sum(jnp.where(hm, prod, 0.0), axis=-1, keepdims=True) for hm in hmask]
        dqacc_ref[...] = jnp.zeros_like(dqacc_ref)
        c_ref[...] = jnp.zeros_like(c_ref)
        causal = (lax.broadcasted_iota(jnp.int32, (t, t), 1) < lax.broadcasted_iota(jnp.int32, (t, t), 0))

        def tiles(kbs, masked):
            te = te_ref[...]
            ti = ti_ref[...]
            starts = [pl.multiple_of(kb * t, t) for kb in kbs]
            kblks = [k_ref[pl.ds(k0, t), :] for k0 in starts]
            vblks = [v_ref[pl.ds(k0, t), :] for k0 in starts]
            chains = [(j, h) for j in range(len(kbs)) for h in range(2)]
            c_lom = [c_ref[2 * h] for h in range(2)]
            c_g = [c_ref[2 * h + 1] for h in range(2)]
            lbs, loms, dws, between, wbs, gs, g_after = {}, {}, {}, {}, {}, {}, {}
            dq = [None, None]
            dk = [None] * len(kbs)
            dv = [None] * len(kbs)
            add = lambda acc, part: part if acc is None else acc + part
            for step in range(len(chains) + 3):
                if step < len(chains):
                    j, h = chains[step]
                    z = lax.dot_general(qm[h], kblks[j], (NT, ((), ())), preferred_element_type=F32)
                    dws[step] = lax.dot_general(dom[h], vblks[j], (NT, ((), ())), preferred_element_type=F32)
                    lbs[step], lom = _log_sigmoids(z)
                    loms[step] = jnp.where(causal, lom, 0.0) if masked[j] else lom
                if 0 <= step - 1 < len(chains):
                    between[step - 1] = _tri_dot(_split(loms[step - 1]), te)
                if 0 <= step - 2 < len(chains):
                    n = step - 2
                    j, h = chains[n]
                    w = jnp.exp(lbs[n] + between[n] + c_lom[h])
                    if masked[j]:
                        w = jnp.where(causal, w, 0.0)
                    c_lom[h] = c_lom[h] + jnp.sum(loms[n], axis=-1, keepdims=True)
                    wbs[n] = w.astype(BF16)
                    gs[n] = dws[n] * wbs[n].astype(F32)
                    g_after[n] = _tri_dot(_split(gs[n]), ti)
                if 0 <= step - 3 < len(chains):
                    n = step - 3
                    j, h = chains[n]
                    beta = jnp.exp(lbs[n])
                    dz = gs[n] * (1.0 - beta) - beta * (total[h] - (g_after[n] + c_g[h]))
                    if masked[j]:
                        dz = jnp.where(causal, dz, 0.0)
                    c_g[h] = c_g[h] + jnp.sum(gs[n], axis=-1, keepdims=True)
                    dzb = dz.astype(BF16)
                    dq[h] = add(dq[h], jnp.dot(dzb, kblks[j], preferred_element_type=F32))
                    dk[j] = add(dk[j], lax.dot_general(dzb, qm[h], (TN, ((), ())), preferred_element_type=F32))
                    dv[j] = add(dv[j], lax.dot_general(wbs[n], dom[h], (TN, ((), ())), preferred_element_type=F32))
            for h in range(2):
                c_ref[2 * h] = c_lom[h]
                c_ref[2 * h + 1] = c_g[h]
                dqacc_ref[h] += dq[h]
            for j, k0 in enumerate(starts):
                dk_ref[pl.ds(k0, t), :] += dk[j]
                dv_ref[pl.ds(k0, t), :] += dv[j]

        _sweep_key_blocks(tiles, lambda: jnp.maximum(jnp.max(c_ref[0]), jnp.max(c_ref[2])) > DEAD_LOG_WEIGHT, i)
        dq_ref[...] = jnp.where(hmask[0], dqacc_ref[0], dqacc_ref[1]) * ATTN_SCALE

    qspec = pl.BlockSpec((t, LANES), lambda hp, i: (i, hp))
    kspec = pl.BlockSpec((s, LANES), lambda hp, i: (0, hp))
    tspec = pl.BlockSpec((t, t), lambda hp, i: (0, 0))
    sds = jax.ShapeDtypeStruct((s, ATTN_W), F32)
    return pl.pallas_call(
        body, name="attn_bwd", grid=(ATTN_W // LANES, nq),
        in_specs=[qspec, kspec, kspec, qspec, qspec, tspec, tspec],
        out_specs=[qspec, kspec, kspec], out_shape=[sds, sds, sds],
        scratch_shapes=[pltpu.VMEM((2, t, LANES), F32), pltpu.VMEM((4, t, 1), F32)],
        compiler_params=_params(("parallel", "arbitrary")))(qs, kh, vb, o, dmix, tri_excl, tri_incl)


CONV_ROWS = 64
CONV_COLS = 256


def _taps(src_ref, w_ref, n_taps, first_row, rows, reverse=False):
    width = src_ref.shape[1]
    cols = min(CONV_COLS, width)
    out = []
    for r0 in range(0, rows, CONV_ROWS):
        for c0 in range(0, width, cols):
            acc = jnp.zeros((CONV_ROWS, cols), F32)
            for k in range(n_taps):
                off = (n_taps - 1 - k) if reverse else k
                acc = acc + w_ref[k:k + 1, c0:c0 + cols] * src_ref[first_row + r0 + off:first_row + r0 + off + CONV_ROWS,
                                                                   c0:c0 + cols]
            out.append(((r0, c0), acc))
    return out


def _conv_fwd(proj, dw_w, dw_b, ln_g, ln_b):
    s = proj.shape[0]
    ts = _row_tile(s)
    hb = ts // CONV_HALO

    def body(a_ref, g_ref, ha_ref, hg_ref, w_ref, b_ref, lg_ref, lb_ref, c1_ref, c3_ref, pad_ref):
        i = pl.program_id(0)
        halo = ha_ref[...] * _sigmoid(hg_ref[...])
        pad_ref[0:CONV_HALO, :] = jnp.where(i > 0, halo, 0.0)
        pad_ref[CONV_HALO:, :] = a_ref[...] * _sigmoid(g_ref[...])
        first = CONV_HALO - (CONV_K - 1)
        for (r0, c0), acc in _taps(pad_ref, w_ref, CONV_K, first, ts):
            c1_ref[r0:r0 + CONV_ROWS, c0:c0 + acc.shape[1]] = acc + b_ref[:, c0:c0 + acc.shape[1]]
        c1 = c1_ref[...]
        xc = c1 - jnp.mean(c1, axis=-1, keepdims=True)
        c2 = xc * _rstd(xc) * lg_ref[...] + lb_ref[...]
        c3_ref[...] = (c2 * _sigmoid(c2)).astype(BF16)

    cur = lambda c: pl.BlockSpec((ts, CONV_W), lambda i: (i, c))
    halo = lambda c: pl.BlockSpec((CONV_HALO, CONV_W), lambda i: (jnp.maximum(i * hb - 1, 0), c))
    vec = pl.BlockSpec((1, CONV_W), lambda i: (0, 0))
    row = pl.BlockSpec((ts, CONV_W), lambda i: (i, 0))
    return pl.pallas_call(
        body, name="conv_fwd", grid=(s // ts,),
        in_specs=[cur(3), cur(4), halo(3), halo(4), pl.BlockSpec((CONV_HALO, CONV_W), lambda i: (0, 0)), vec, vec, vec],
        out_specs=[row, row],
        out_shape=[jax.ShapeDtypeStruct((s, CONV_W), F32), jax.ShapeDtypeStruct((s, CONV_W), BF16)],
        scratch_shapes=[pltpu.VMEM((ts + CONV_HALO, CONV_W), F32)],
        compiler_params=_params(("parallel",)))(proj, proj, proj, proj, dw_w, dw_b, ln_g, ln_b)


def _conv_bwd_ln(dmix, c1, ln_g, ln_b):
    s = c1.shape[0]
    ts = _row_tile(s)

    def body(d_ref, c1_ref, lg_ref, lb_ref, dc1_ref, glg_ref, glb_ref, gb_ref):
        c1v = c1_ref[...]
        xc = c1v - jnp.mean(c1v, axis=-1, keepdims=True)
        r = _rstd(xc)
        xhat = xc * r
        c2 = xhat * lg_ref[...] + lb_ref[...]
        sg = _sigmoid(c2)
        dc2 = d_ref[...] * (sg * (1.0 + c2 * (1.0 - sg)))
        dxhat = dc2 * lg_ref[...]
        dc1 = r * (dxhat - jnp.mean(dxhat, axis=-1, keepdims=True)
                   - xhat * jnp.mean(dxhat * xhat, axis=-1, keepdims=True))
        dc1_ref[...] = dc1
        parts = [(glg_ref, jnp.sum(dc2 * xhat, axis=0, keepdims=True)),
                 (glb_ref, jnp.sum(dc2, axis=0, keepdims=True)),
                 (gb_ref, jnp.sum(dc1, axis=0, keepdims=True))]

        @pl.when(pl.program_id(0) == 0)
        def _():
            for ref, part in parts:
                ref[...] = part

        @pl.when(pl.program_id(0) > 0)
        def _():
            for ref, part in parts:
                ref[...] += part

    row = pl.BlockSpec((ts, CONV_W), lambda i: (i, 0))
    vec = pl.BlockSpec((1, CONV_W), lambda i: (0, 0))
    vsd = jax.ShapeDtypeStruct((1, CONV_W), F32)
    return pl.pallas_call(
        body, name="conv_bwd_ln", grid=(s // ts,),
        in_specs=[pl.BlockSpec((ts, CONV_W), lambda i: (i, 1)), row, vec, vec],
        out_specs=[row, vec, vec, vec],
        out_shape=[jax.ShapeDtypeStruct((s, CONV_W), F32), vsd, vsd, vsd],
        compiler_params=_params(("arbitrary",)))(dmix, c1, ln_g, ln_b)


def _conv_bwd_taps(proj, dc1, dw_w):
    s = proj.shape[0]
    ts = _row_tile(s)
    hb = ts // CONV_HALO
    last = s // CONV_HALO - 1
    nsteps = s // ts

    def body(a_ref, g_ref, ha_ref, hg_ref, d_ref, hd_ref, w_ref, out_ref, gw_ref, pad_ref, dpad_ref):
        i = pl.program_id(0)
        av = a_ref[...]
        sg = _sigmoid(g_ref[...])
        halo = ha_ref[...] * _sigmoid(hg_ref[...])
        pad_ref[0:CONV_HALO, :] = jnp.where(i > 0, halo, 0.0)
        pad_ref[CONV_HALO:, :] = av * sg
        dpad_ref[0:ts, :] = d_ref[...]
        dpad_ref[ts:, :] = jnp.where(i < nsteps - 1, hd_ref[...], 0.0)

        @pl.when(i == 0)
        def _():
            gw_ref[...] = jnp.zeros_like(gw_ref)

        first = CONV_HALO - (CONV_K - 1)
        for k in range(CONV_K):
            gw_ref[k:k + 1, :] += jnp.sum(d_ref[...] * pad_ref[first + k:first + k + ts, :], axis=0, keepdims=True)
        for (r0, c0), dc0 in _taps(dpad_ref, w_ref, CONV_K, 0, ts, reverse=True):
            cs = slice(c0, c0 + dc0.shape[1])
            a_c = a_ref[r0:r0 + CONV_ROWS, cs]
            sg_c = _sigmoid(g_ref[r0:r0 + CONV_ROWS, cs])
            out_ref[r0:r0 + CONV_ROWS, cs] = (dc0 * sg_c).astype(BF16)
            out_ref[r0:r0 + CONV_ROWS, CONV_W + c0:CONV_W + c0 + dc0.shape[1]] = (
                dc0 * a_c * sg_c * (1.0 - sg_c)).astype(BF16)

    cur = lambda c: pl.BlockSpec((ts, CONV_W), lambda i: (i, c))
    halo = lambda c: pl.BlockSpec((CONV_HALO, CONV_W), lambda i: (jnp.maximum(i * hb - 1, 0), c))
    row = pl.BlockSpec((ts, CONV_W), lambda i: (i, 0))
    nxt = pl.BlockSpec((CONV_HALO, CONV_W), lambda i: (jnp.minimum((i + 1) * hb, last), 0))
    wspec = pl.BlockSpec((CONV_HALO, CONV_W), lambda i: (0, 0))
    return pl.pallas_call(
        body, name="conv_bwd_taps", grid=(nsteps,),
        in_specs=[cur(3), cur(4), halo(3), halo(4), row, nxt, wspec],
        out_specs=[pl.BlockSpec((ts, 2 * CONV_W), lambda i: (i, 0)), wspec],
        out_shape=[jax.ShapeDtypeStruct((s, 2 * CONV_W), BF16), jax.ShapeDtypeStruct((CONV_HALO, CONV_W), F32)],
        scratch_shapes=[pltpu.VMEM((ts + CONV_HALO, CONV_W), F32), pltpu.VMEM((ts + CONV_HALO, CONV_W), F32)],
        compiler_params=_params(("arbitrary",)))(proj, proj, proj, proj, dc1, dc1, dw_w)


SQRT_HALF = 0.7071067811865476
INV_SQRT_2PI = 0.3989422804014327


def _gelu_parts(x):
    cdf = 0.5 * (1.0 + lax.erf(x * SQRT_HALF))
    return x * cdf, cdf + x * (INV_SQRT_2PI * jnp.exp(-0.5 * x * x))


def _ffn_tile(dff):
    return dff // 2


def _ffn_gate2(pad_ref, w_ref, b_ref, ts):
    first = FFN_HALO - (FFN_K - 1)
    g2 = b_ref[...] + w_ref[0:1, :] * pad_ref[first:first + ts, :]
    for k in range(1, FFN_K):
        g2 = g2 + w_ref[k:k + 1, :] * pad_ref[first + k:first + k + ts, :]
    return g2


def _ffn_act(up, fw, fb):
    s = up.shape[0]
    dff = up.shape[1] // 2
    tc = _ffn_tile(dff)
    nj = dff // tc
    ts = _row_tile(s) // 2
    hb = ts // FFN_HALO

    def body(g_ref, v_ref, hg_ref, w_ref, b_ref, act_ref, pad_ref):
        i = pl.program_id(0)
        pad_ref[0:FFN_HALO, :] = jnp.where(i > 0, hg_ref[...], 0.0)
        pad_ref[FFN_HALO:, :] = g_ref[...]
        gelu, _ = _gelu_parts(_ffn_gate2(pad_ref, w_ref, b_ref, ts))
        act_ref[...] = (gelu * v_ref[...]).astype(BF16)

    return pl.pallas_call(
        body, name="ffn_act", grid=(s // ts, nj),
        in_specs=[pl.BlockSpec((ts, tc), lambda i, j: (i, j)), pl.BlockSpec((ts, tc), lambda i, j: (i, j + nj)),
                  pl.BlockSpec((FFN_HALO, tc), lambda i, j: (jnp.maximum(i * hb - 1, 0), j)),
                  pl.BlockSpec((FFN_HALO, tc), lambda i, j: (0, j)), pl.BlockSpec((1, tc), lambda i, j: (0, j))],
        out_specs=pl.BlockSpec((ts, tc), lambda i, j: (i, j)),
        out_shape=jax.ShapeDtypeStruct((s, dff), BF16),
        scratch_shapes=[pltpu.VMEM((ts + FFN_HALO, tc), F32)],
        compiler_params=_params(("parallel", "parallel")))(up, up, up, fw, fb)


def _ffn_bwd_act(dact, up, fw, fb):
    s = up.shape[0]
    dff = up.shape[1] // 2
    tc = _ffn_tile(dff)
    nj = dff // tc
    ts = _row_tile(s) // 2
    hb = ts // FFN_HALO

    def body(d_ref, g_ref, v_ref, hg_ref, w_ref, b_ref, dg2_ref, dval_ref, gw_ref, gb_ref, pad_ref):
        i = pl.program_id(1)
        pad_ref[0:FFN_HALO, :] = jnp.where(i > 0, hg_ref[...], 0.0)
        pad_ref[FFN_HALO:, :] = g_ref[...]
        gelu, dgelu = _gelu_parts(_ffn_gate2(pad_ref, w_ref, b_ref, ts))
        dactv = d_ref[...]
        dval_ref[...] = (dactv * gelu).astype(BF16)
        dg2 = dactv * v_ref[...] * dgelu
        dg2_ref[...] = dg2

        @pl.when(i == 0)
        def _():
            gw_ref[...] = jnp.zeros_like(gw_ref)
            gb_ref[...] = jnp.zeros_like(gb_ref)

        gb_ref[...] += jnp.sum(dg2, axis=0, keepdims=True)
        first = FFN_HALO - (FFN_K - 1)
        for k in range(FFN_K):
            gw_ref[k:k + 1, :] += jnp.sum(dg2 * pad_ref[first + k:first + k + ts, :], axis=0, keepdims=True)

    blk = pl.BlockSpec((ts, tc), lambda j, i: (i, j))
    wspec = pl.BlockSpec((FFN_HALO, tc), lambda j, i: (0, j))
    bspec = pl.BlockSpec((1, tc), lambda j, i: (0, j))
    return pl.pallas_call(
        body, name="ffn_bwd_act", grid=(nj, s // ts),
        in_specs=[blk, blk, pl.BlockSpec((ts, tc), lambda j, i: (i, j + nj)),
                  pl.BlockSpec((FFN_HALO, tc), lambda j, i: (jnp.maximum(i * hb - 1, 0), j)), wspec, bspec],
        out_specs=[blk, pl.BlockSpec((ts, tc), lambda j, i: (i, j + nj)), wspec, bspec],
        out_shape=[jax.ShapeDtypeStruct((s, dff), F32), jax.ShapeDtypeStruct((s, 2 * dff), BF16),
                   jax.ShapeDtypeStruct((FFN_HALO, dff), F32), jax.ShapeDtypeStruct((1, dff), F32)],
        scratch_shapes=[pltpu.VMEM((ts + FFN_HALO, tc), F32)],
        compiler_params=_params(("parallel", "arbitrary")))(dact, up, up, up, fw, fb)


def _ffn_bwd_conv(dg2, fw, dup):
    s, dff = dg2.shape
    tc = _ffn_tile(dff)
    ts = _row_tile(s) // 2
    hb = ts // FFN_HALO
    last = s // FFN_HALO - 1
    nsteps = s // ts

    def body(d_ref, hd_ref, w_ref, dup_ref, out_ref, pad_ref):
        i = pl.program_id(0)
        pad_ref[0:ts, :] = d_ref[...]
        pad_ref[ts:, :] = jnp.where(i < nsteps - 1, hd_ref[...], 0.0)
        dg = w_ref[0:1, :] * pad_ref[FFN_K - 1:FFN_K - 1 + ts, :]
        for k in range(1, FFN_K):
            dg = dg + w_ref[k:k + 1, :] * pad_ref[FFN_K - 1 - k:FFN_K - 1 - k + ts, :]
        out_ref[...] = dg.astype(BF16)

    blk = pl.BlockSpec((ts, tc), lambda i, j: (i, j))
    return pl.pallas_call(
        body, name="ffn_bwd_conv", grid=(nsteps, dff // tc),
        in_specs=[blk, pl.BlockSpec((FFN_HALO, tc), lambda i, j: (jnp.minimum((i + 1) * hb, last), j)),
                  pl.BlockSpec((FFN_HALO, tc), lambda i, j: (0, j)), ANY],
        out_specs=blk, out_shape=jax.ShapeDtypeStruct(dup.shape, BF16), input_output_aliases={3: 0},
        scratch_shapes=[pltpu.VMEM((ts + FFN_HALO, tc), F32)],
        compiler_params=_params(("parallel", "parallel")))(dg2, dg2, fw, dup)


def _ple_loss(h2, zg, pp, target):
    s, d = h2.shape
    ts = _row_tile(s)

    def body(h_ref, z_ref, p_ref, t_ref, dh_ref, dpp_ref, dz_ref, loss_ref):
        pg = _sigmoid(z_ref[...])
        ppv = p_ref[...]
        diff = h_ref[...] + pg * ppv - t_ref[...]
        dh = diff * (1.0 / d)
        dh_ref[...] = dh
        dpp_ref[...] = (dh * pg).astype(BF16)
        dz_ref[...] = (dh * ppv * pg * (1.0 - pg)).astype(BF16)
        part = jnp.sum(jnp.sum(diff * diff, axis=0, keepdims=True), axis=1, keepdims=True)

        @pl.when(pl.program_id(0) == 0)
        def _():
            loss_ref[...] = jnp.zeros_like(loss_ref)

        loss_ref[...] += jnp.broadcast_to(part, loss_ref.shape)

    row = pl.BlockSpec((ts, d), lambda i: (i, 0))
    return pl.pallas_call(
        body, name="ple_loss", grid=(s // ts,), in_specs=[row, row, row, row],
        out_specs=[row, row, row, pl.BlockSpec((8, LANES), lambda i: (0, 0))],
        out_shape=[jax.ShapeDtypeStruct((s, d), F32), jax.ShapeDtypeStruct((s, d), BF16),
                   jax.ShapeDtypeStruct((s, d), BF16), jax.ShapeDtypeStruct((8, LANES), F32)],
        compiler_params=_params(("arbitrary",)))(h2, zg, pp, target)


def _local_step(x, p, target, w, fetch_first, fetch_rest, send_first, send_last):
    s = x.shape[0]
    t = min(256, s)
    tri = jnp.tril(jnp.ones((t, t), F32))
    tri_incl = tri.astype(BF16)
    tri_excl = jnp.tril(jnp.ones((t, t), F32), -1).astype(BF16)
    bd = jnp.kron(jnp.eye(N_HEADS, dtype=F32), jnp.ones((HEAD_DIM, HEAD_DIM), F32)).astype(BF16)
    qg = jnp.tile(w["q_gain"], (1, N_HEADS))
    kg = jnp.tile(w["k_gain"], (1, N_HEADS))
    pb = p.astype(BF16)

    u1 = _rms_fwd(x, w["g_mix"], "rms_mix")
    w = {**w, **fetch_first(u1)}
    dw_w = jnp.pad(w["dw_w"], ((0, CONV_HALO - CONV_K), (0, 0)))
    fw = jnp.pad(w["ffn_conv_w"], ((0, FFN_HALO - FFN_K), (0, 0)))
    proj = _mm_nn_sharded(u1, w["w_in"], "mm_in")
    qs, kh, vb = _qkv_prep(proj, qg, kg, bd)
    o, ob = _attn_fwd(qs, kh, vb, tri_excl)
    w = {**w, **fetch_rest(o)}
    c1, c3 = _conv_fwd(proj, dw_w, w["dw_b"], w["conv_ln_g"], w["conv_ln_b"])
    mix = jnp.concatenate([ob, c3], axis=1)
    h1 = _mm_nn_full(mix, w["w_out"], "mm_out", res=x)
    u2 = _rms_fwd(h1, w["g_ffn"], "rms_ffn")
    up = _mm_nn_sharded(u2, w["w_up"], "mm_up", tm=_row_tile(s) // 2)
    act = _ffn_act(up, fw, w["ffn_conv_b"])
    h2 = _mm_nn_full(act, w["w_down"], "mm_down", res=h1)
    u3 = _rms_fwd(h2, w["g_ple"], "rms_ple")
    zg = _mm_nn_full(u3, w["w_ple_gate"], "mm_ple_gate")
    pp = _mm_nn_sharded(pb, w["w_ple_proj"], "mm_ple_proj")
    dh3, dpp, dz, sq = _ple_loss(h2, zg, pp, target)

    big = {}
    small = {}
    big["w_ple_proj"] = _mm_tn_sharded(pb, dpp, N_CHIPS, "mm_g_ple_proj")
    big["w_ple_gate"] = _mm_tn_full(u3, dz, "mm_g_ple_gate", tm=u3.shape[1])
    du3 = _mm_nt_full(dz, w["w_ple_gate"], "mm_d_ple_gate")
    dh2, dh2b, small["g_ple"] = _rms_bwd(h2, du3, w["g_ple"], dh3, "rms_ple_bwd")
    big["w_down"] = _mm_tn_full(act, dh2b, "mm_g_down", tm=act.shape[1] // 2)
    dact = _mm_nt_full(dh2b, w["w_down"], "mm_d_down")
    dg2, dup, gfw, small["ffn_conv_b"] = _ffn_bwd_act(dact, up, fw, w["ffn_conv_b"])
    small["ffn_conv_w"] = gfw[:FFN_K]
    dup = _ffn_bwd_conv(dg2, fw, dup)
    big["w_up"] = _mm_tn_sharded(u2, dup, N_CHIPS, "mm_g_up", n_split=2)
    du2 = _mm_nt_sharded(dup, w["w_up"], "mm_d_up")
    dh1, dh1b, small["g_ffn"] = _rms_bwd(h1, du2, w["g_ffn"], dh2, "rms_ffn_bwd")
    big["w_out"] = _mm_tn_full(mix, dh1b, "mm_g_out", tm=mix.shape[1])
    sent = send_first({n: big[n] for n in ("w_ple_proj", "w_ple_gate", "w_down", "w_up", "w_out")})
    dmix = _mm_nt_full(dh1b, w["w_out"], "mm_d_out")
    dc1, small["conv_ln_g"], small["conv_ln_b"], small["dw_b"] = _conv_bwd_ln(dmix, c1, w["conv_ln_g"], w["conv_ln_b"])
    dcacg, gdw = _conv_bwd_taps(proj, dc1, dw_w)
    small["dw_w"] = gdw[:CONV_K]
    dqh, dkh, dv = _attn_bwd(qs, kh, vb, o, dmix, tri_excl + sent.astype(BF16), tri_incl)
    dqkv, gq, gk = _qk_bwd(proj, dqh, dkh, dv, qg, kg, bd)
    small["q_gain"] = gq.reshape(N_HEADS, HEAD_DIM).sum(axis=0, keepdims=True)
    small["k_gain"] = gk.reshape(N_HEADS, HEAD_DIM).sum(axis=0, keepdims=True)
    dproj = jnp.concatenate([dqkv, dcacg], axis=1)
    big["w_in"] = _mm_tn_sharded(u1, dproj, N_CHIPS, "mm_g_in")
    sent = send_last({"w_in": big["w_in"]})
    du1 = _mm_nt_sharded(dproj, w["w_in"], "mm_d_in")
    grad_x, _, small["g_mix"] = _rms_bwd(x, du1, w["g_mix"] + sent, dh1, "rms_mix_bwd")
    return sq[0, 0], grad_x, big, small


def _position():
    x, y, c = lax.axis_index("x"), lax.axis_index("y"), lax.axis_index("c")
    return x, y, c, [(1 - x, y), (x, 1 - y), (1 - x, 1 - y)]


def _remote(src, dst, send_sems, recv_sems, k, to):
    return pltpu.make_async_remote_copy(src_ref=src, dst_ref=dst, send_sem=send_sems.at[k], recv_sem=recv_sems.at[k],
                                        device_id=to, device_id_type=MESH)


def _chip_copies(src_refs, land_refs, send_sems, recv_sems, per_chip, landed):
    x, y, c, chips = _position()
    me = 2 * x + y
    out = []
    for a, (src, land) in enumerate(zip(src_refs, land_refs)):
        for j, (px, py) in enumerate(chips):
            peer = 2 * px + py
            out.append(_remote(src.at[peer] if per_chip else src, land.at[peer if landed else me],
                               send_sems, recv_sems, 3 * a + j, (px, py, c)))
    return out


def _exchange(srcs, per_chip, name):
    n = len(srcs)

    def body(*refs):
        src_refs, land_refs = refs[:n], refs[n:2 * n]
        send_sems, recv_sems, local_sems = refs[2 * n:]
        x, y, _, _ = _position()
        me = 2 * x + y
        local = [pltpu.make_async_copy(src.at[me] if per_chip else src, land.at[me], local_sems.at[a])
                 for a, (src, land) in enumerate(zip(src_refs, land_refs))]
        for cp in local:
            cp.start()
        sends = _chip_copies(src_refs, land_refs, send_sems, recv_sems, per_chip, False)
        for cp in sends:
            cp.start()
        for cp in _chip_copies(src_refs, land_refs, send_sems, recv_sems, per_chip, True):
            cp.wait_recv()
        for cp in sends:
            cp.wait_send()
        for cp in local:
            cp.wait()

    return pl.pallas_call(
        body, name=name, in_specs=[ANY] * n, out_specs=[ANY] * n,
        out_shape=[jax.ShapeDtypeStruct(a.shape if per_chip else (N_CHIPS,) + a.shape, a.dtype) for a in srcs],
        scratch_shapes=[pltpu.SemaphoreType.DMA((3 * n,)), pltpu.SemaphoreType.DMA((3 * n,)),
                        pltpu.SemaphoreType.DMA((n,))])(*srcs)


def _finish_exchange(mine, small):
    n = len(mine)

    def body(*refs):
        gin, sin = refs[:n], refs[n]
        gout, sout = refs[n + 1:2 * n + 1], refs[2 * n + 1]
        send_sems, recv_sems, small_send, small_recv, local_sem = refs[2 * n + 2:]
        x, y, c, _ = _position()
        dev = 4 * x + 2 * y + c
        flip = lambda v, bit: 1 - v if bit else v
        others = [(flip(x, k & 4), flip(y, k & 2), flip(c, k & 1)) for k in range(1, N_DEV)]
        local = pltpu.make_async_copy(sin, sout.at[dev], local_sem)
        local.start()
        swaps = [_remote(gin[a], gout[a], send_sems, recv_sems, a, (x, y, 1 - c)) for a in range(n)]
        sends = swaps + [_remote(sin, sout.at[dev], small_send, small_recv, k, peer) for k, peer in enumerate(others)]
        for cp in sends:
            cp.start()
        for cp in swaps:
            cp.wait_recv()
        for k, (px, py, pc) in enumerate(others):
            _remote(sin, sout.at[4 * px + 2 * py + pc], small_send, small_recv, k, (px, py, pc)).wait_recv()
        for cp in sends:
            cp.wait_send()
        local.wait()

    return pl.pallas_call(
        body, name="finish_exchange", in_specs=[ANY] * (n + 1), out_specs=[ANY] * (n + 1),
        out_shape=[jax.ShapeDtypeStruct(g.shape, g.dtype) for g in mine]
        + [jax.ShapeDtypeStruct((N_DEV,) + small.shape, small.dtype)],
        scratch_shapes=[pltpu.SemaphoreType.DMA((n,)), pltpu.SemaphoreType.DMA((n,)),
                        pltpu.SemaphoreType.DMA((N_DEV - 1,)), pltpu.SemaphoreType.DMA((N_DEV - 1,)),
                        pltpu.SemaphoreType.DMA])(*mine, small)


def _elem_tile(rows):
    return 128 if rows % 128 == 0 else (64 if rows % 64 == 0 else rows)


def _sum_slots(a, name):
    g, r, c = a.shape
    tr = _elem_tile(r)

    def body(a_ref, o_ref):
        acc = a_ref[0]
        for k in range(1, g):
            acc = acc + a_ref[k]
        o_ref[...] = acc

    return pl.pallas_call(
        body, name=name, grid=(r // tr,), in_specs=[pl.BlockSpec((g, tr, c), lambda i: (0, i, 0))],
        out_specs=pl.BlockSpec((tr, c), lambda i: (i, 0)), out_shape=jax.ShapeDtypeStruct((r, c), a.dtype),
        compiler_params=_params(("parallel",)))(a)


def _adamw(wt, ga, gb, m, v, name):
    r, c = wt.shape
    tr = _elem_tile(r)
    two = gb is not None

    def body(*refs):
        if two:
            w_ref, ga_ref, gb_ref, m_ref, v_ref, g_out, d_out, m_out, v_out = refs
            g = ga_ref[...] + gb_ref[...]
        else:
            w_ref, ga_ref, m_ref, v_ref, g_out, d_out, m_out, v_out = refs
            g = ga_ref[...]
        mn = ADAM_B1 * m_ref[...] + (1.0 - ADAM_B1) * g
        vn = ADAM_B2 * v_ref[...] + (1.0 - ADAM_B2) * (g * g)
        m_hat = mn / (1.0 - ADAM_B1 ** ADAM_STEP)
        v_hat = vn / (1.0 - ADAM_B2 ** ADAM_STEP)
        g_out[...] = g
        d_out[...] = -ADAM_LR * (m_hat / (jnp.sqrt(v_hat) + ADAM_EPS) + ADAM_WD * w_ref[...])
        m_out[...] = mn
        v_out[...] = vn

    blk = pl.BlockSpec((tr, c), lambda i: (i, 0))
    args = [wt, ga] + ([gb] if two else []) + [m, v]
    sds = jax.ShapeDtypeStruct((r, c), F32)
    return pl.pallas_call(
        body, name=name, grid=(r // tr,), in_specs=[blk] * len(args), out_specs=[blk] * 4, out_shape=[sds] * 4,
        compiler_params=_params(("parallel",)))(*args)


def _pack(arrs, rows):
    flat = jnp.concatenate([a.reshape(-1) for a in arrs])
    return jnp.pad(flat, (0, rows * LANES - flat.shape[0])).reshape(rows, LANES)


def _unpack(buf, shapes):
    flat = buf.reshape(-1)
    out, off = [], 0
    for shp in shapes:
        size = 1
        for d in shp:
            size *= d
        out.append(flat[off:off + size].reshape(shp))
        off += size
    return out


BIG = ["w_in", "w_out", "w_up", "w_down", "w_ple_gate", "w_ple_proj"]
COL_SHARDED = ["w_in", "w_up", "w_ple_proj"]
SMALL_REPL = ["g_mix", "q_gain", "k_gain", "dw_b", "conv_ln_g", "conv_ln_b", "g_ffn", "ffn_conv_b", "g_ple"]
SMALL_SHARDED = ["dw_w", "ffn_conv_w"]
WEIGHTS = ["g_mix", "w_in", "q_gain", "k_gain", "dw_w", "dw_b", "conv_ln_g", "conv_ln_b", "w_out", "g_ffn", "w_up",
           "ffn_conv_w", "ffn_conv_b", "w_down", "g_ple", "w_ple_gate", "w_ple_proj"]


def _rows_for(n_elems):
    return -(-n_elems // (8 * LANES)) * 8


def kernel(x, p, g_mix, w_in, q_gain, k_gain, dw_w, dw_b, conv_ln_g, conv_ln_b, w_out, g_ffn, w_up, ffn_conv_w, ffn_conv_b, w_down, g_ple, w_ple_gate, w_ple_proj, loss_target, m_g_mix, m_w_in, m_q_gain, m_k_gain, m_dw_w, m_dw_b, m_conv_ln_g, m_conv_ln_b, m_w_out, m_g_ffn, m_w_up, m_ffn_conv_w, m_ffn_conv_b, m_w_down, m_g_ple, m_w_ple_gate, m_w_ple_proj, v_g_mix, v_w_in, v_q_gain, v_k_gain, v_dw_w, v_dw_b, v_conv_ln_g, v_conv_ln_b, v_w_out, v_g_ffn, v_w_up, v_ffn_conv_w, v_ffn_conv_b, v_w_down, v_g_ple, v_w_ple_gate, v_w_ple_proj):
    given = dict(locals())
    strip = lambda n, a: a if n in SMALL_REPL else a[0]
    wts = {n: strip(n, given[n]) for n in WEIGHTS}
    mom = {n: strip(n, given["m_" + n]) for n in WEIGHTS}
    var = {n: strip(n, given["v_" + n]) for n in WEIGHTS}
    chip = 2 * lax.axis_index("x") + lax.axis_index("y")

    small_shard_shapes = [wts[n].shape for n in SMALL_SHARDED]
    filt_rows = _rows_for(sum(wts[n].size for n in SMALL_SHARDED))
    filt = _pack([wts[n] for n in SMALL_SHARDED], filt_rows)
    gathered = _exchange([wts[n].astype(BF16) for n in BIG] + [filt], False, "gather_weights")
    full = {n: (g if n in COL_SHARDED else g.reshape(-1, g.shape[-1])) for n, g in zip(BIG, gathered)}
    per_chip = [_unpack(gathered[-1][k], small_shard_shapes) for k in range(N_CHIPS)]
    for idx, n in enumerate(SMALL_SHARDED):
        full[n] = jnp.concatenate([per_chip[k][idx] for k in range(N_CHIPS)], axis=1)
    first = {n: full.pop(n) for n in ["w_in"] + SMALL_SHARDED}
    unsent = jnp.zeros((1, 1), F32)

    sq, grad_x, big, small = _local_step(x[0], p[0, 0], loss_target[0], {n: wts[n] for n in SMALL_REPL},
                                         lambda after: first, lambda after: full,
                                         lambda grads: unsent, lambda grads: unsent)
    loss = lax.psum(sq * (0.5 / x.shape[-1]), ("x", "y", "c"))

    small_names = SMALL_REPL + SMALL_SHARDED
    small_shapes = [small[n].shape for n in small_names]
    small_rows = _rows_for(sum(small[n].size for n in small_names))
    shard_major = [big[n] if n in COL_SHARDED else big[n].reshape(N_CHIPS, -1, big[n].shape[-1]) for n in BIG]
    slots = _exchange(shard_major, True, "scatter_grads")
    mine = [_sum_slots(sl, "sum_" + n) for sl, n in zip(slots, BIG)]
    *theirs, small_slots = _finish_exchange(mine, _pack([small[n] for n in small_names], small_rows))
    small_sum = dict(zip(small_names, _unpack(_sum_slots(small_slots, "sum_small"), small_shapes)))

    outs = {}
    for n, ga, gb in zip(BIG, mine, theirs):
        outs[n] = _adamw(wts[n], ga, gb, mom[n], var[n], "adamw_" + n)
    for n in SMALL_SHARDED:
        width = wts[n].shape[1]
        small_sum[n] = lax.dynamic_slice_in_dim(small_sum[n], chip * width, width, axis=1)
    local_shapes = [wts[n].shape for n in small_names]
    local_rows = _rows_for(sum(wts[n].size for n in small_names))
    packed = _adamw(_pack([wts[n] for n in small_names], local_rows), _pack([small_sum[n] for n in small_names], local_rows),
                    None, _pack([mom[n] for n in small_names], local_rows),
                    _pack([var[n] for n in small_names], local_rows), "adamw_small")
    unpacked = [_unpack(buf, local_shapes) for buf in packed]
    for idx, n in enumerate(small_names):
        outs[n] = [u[idx] for u in unpacked]
    result = [loss, grad_x[None]]
    for part in range(4):
        result += [outs[n][part] if n in SMALL_REPL else outs[n][part][None] for n in WEIGHTS]
    return tuple(result)
```

```python
import functools

import jax
import jax.numpy as jnp
from jax import lax
from jax.experimental import pallas as pl
from jax.experimental.pallas import tpu as pltpu

F32 = jnp.float32
BF16 = jnp.bfloat16
HIGHEST = lax.Precision.HIGHEST
MESH = pl.DeviceIdType.MESH
ANY = pl.BlockSpec(memory_space=pl.ANY)

EPS = 1e-6
HEAD_DIM = 64
N_HEADS = 8
ATTN_W = 512
CONV_W = 512
CONV_K = 31
FFN_K = 3
ATTN_SCALE = 0.125
LANES = 128
CONV_HALO = 32
FFN_HALO = 8
VMEM_LIMIT = 56 * 1024 * 1024

ADAM_LR = 0.001
ADAM_B1 = 0.9
ADAM_B2 = 0.999
ADAM_EPS = 1e-08
ADAM_WD = 0.01
ADAM_STEP = 10

N_CHIPS = 4
N_DEV = 8


def _params(sem):
    return pltpu.CompilerParams(dimension_semantics=sem, vmem_limit_bytes=VMEM_LIMIT)


def _row_tile(s):
    return min(512, s)


def _position():
    x, y, c = lax.axis_index("x"), lax.axis_index("y"), lax.axis_index("c")
    return x, y, c, [(1 - x, y), (x, 1 - y), (1 - x, 1 - y)]


def _remote(src, dst, send_sems, recv_sems, k, to):
    return pltpu.make_async_remote_copy(src_ref=src, dst_ref=dst, send_sem=send_sems.at[k], recv_sem=recv_sems.at[k],
                                        device_id=to, device_id_type=MESH)


def _chip_copies(src_refs, land_refs, send_sems, recv_sems, per_chip, landed):
    x, y, c, chips = _position()
    me = 2 * x + y
    out = []
    for a, (src, land) in enumerate(zip(src_refs, land_refs)):
        for j, (px, py) in enumerate(chips):
            peer = 2 * px + py
            out.append(_remote(src.at[peer] if per_chip else src, land.at[peer if landed else me],
                               send_sems, recv_sems, 3 * a + j, (px, py, c)))
    return out


def _local_copies(src_refs, land_refs, local_sems, per_chip):
    x, y, _, _ = _position()
    me = 2 * x + y
    return [pltpu.make_async_copy(src.at[me] if per_chip else src, land.at[me], local_sems.at[a])
            for a, (src, land) in enumerate(zip(src_refs, land_refs))]


def _exchanged_shapes(srcs, per_chip):
    return [jax.ShapeDtypeStruct(a.shape if per_chip else (N_CHIPS,) + a.shape, a.dtype) for a in srcs]


def _exchange_sems(n):
    return [pltpu.SemaphoreType.DMA((3 * n,)), pltpu.SemaphoreType.DMA((3 * n,)), pltpu.SemaphoreType.DMA((n,))]


def _call(body, *, name, grid, in_specs, out_specs, out_shape, scratch_shapes, args, semantics, carry=None):
    if carry is None:
        return pl.pallas_call(body, name=name, grid=grid, in_specs=in_specs, out_specs=out_specs, out_shape=out_shape,
                              scratch_shapes=scratch_shapes, compiler_params=_params(semantics))(*args)
    srcs, per_chip = carry
    n, n_in, n_out, n_scr = len(srcs), len(in_specs), len(out_specs), len(scratch_shapes)

    def wrapped(*refs):
        ins, xin = refs[:n_in], refs[n_in:n_in + n]
        outs, xout = refs[n_in + n:n_in + n + n_out], refs[n_in + n + n_out:n_in + 2 * n + n_out]
        scratch = refs[n_in + 2 * n + n_out:n_in + 2 * n + n_out + n_scr]
        send_sems, recv_sems, local_sems = refs[-3:]
        first = functools.reduce(jnp.logical_and, [pl.program_id(d) == 0 for d in range(len(grid))])
        last = functools.reduce(jnp.logical_and, [pl.program_id(d) == g - 1 for d, g in enumerate(grid)])

        @pl.when(first)
        def _():
            for cp in _local_copies(xin, xout, local_sems, per_chip):
                cp.start()
            for cp in _chip_copies(xin, xout, send_sems, recv_sems, per_chip, False):
                cp.start()

        body(*ins, *outs, *scratch)

        @pl.when(last)
        def _():
            for cp in _chip_copies(xin, xout, send_sems, recv_sems, per_chip, True):
                cp.wait_recv()
            for cp in _chip_copies(xin, xout, send_sems, recv_sems, per_chip, False):
                cp.wait_send()
            for cp in _local_copies(xin, xout, local_sems, per_chip):
                cp.wait()

    return pl.pallas_call(
        wrapped, name=name, grid=grid, in_specs=list(in_specs) + [ANY] * n, out_specs=list(out_specs) + [ANY] * n,
        out_shape=list(out_shape) + _exchanged_shapes(srcs, per_chip),
        scratch_shapes=list(scratch_shapes) + _exchange_sems(n),
        compiler_params=_params(("arbitrary",) * len(grid)))(*args, *srcs)


def _contract_tile(s):
    return min(1024, s)


def _rstd(x):
    return lax.rsqrt(jnp.mean(x * x, axis=-1, keepdims=True) + EPS)


def _sigmoid(x):
    return 1.0 / (1.0 + jnp.exp(-x))


def _mm(a, b, *, name, dims, grid, a_spec, b_spec, o_spec, o_tile, out_shape, res=None, res_spec=None):
    nk = grid[2]

    def body(*refs):
        if res is None:
            a_ref, b_ref, o_ref, acc_ref = refs
            r_ref = None
        else:
            a_ref, b_ref, r_ref, o_ref, acc_ref = refs
        part = lax.dot_general(a_ref[...], b_ref[...], (dims, ((), ())), preferred_element_type=F32)

        def finish(val):
            if r_ref is not None:
                val = val + r_ref[...]
            o_ref[...] = val.astype(o_ref.dtype)

        if nk == 1:
            finish(part)
        else:
            k = pl.program_id(2)

            @pl.when(k == 0)
            def _():
                acc_ref[...] = part

            @pl.when(k > 0)
            def _():
                acc_ref[...] += part

            @pl.when(k == nk - 1)
            def _():
                finish(acc_ref[...])

    in_specs = [a_spec, b_spec]
    args = [a, b]
    if res is not None:
        in_specs.append(res_spec)
        args.append(res)
    acc_tile = o_tile if nk > 1 else (8, LANES)
    return pl.pallas_call(
        body, name=name, grid=grid, in_specs=in_specs, out_specs=o_spec, out_shape=out_shape,
        scratch_shapes=[pltpu.VMEM(acc_tile, F32)],
        compiler_params=_params(("parallel", "parallel", "arbitrary")))(*args)


NN = ((1,), (0,))
NT = ((1,), (1,))
TN = ((0,), (0,))


def _mm_nn_sharded(a, bg, name, out_dtype=F32, tm=None):
    s, k = a.shape
    g, _, ns = bg.shape
    tm = tm or _row_tile(s)

    def body(a_ref, b_ref, o_ref):
        av = a_ref[...]
        for gi in range(g):
            o_ref[:, gi * ns:(gi + 1) * ns] = jnp.dot(av, b_ref[gi], preferred_element_type=F32).astype(out_dtype)

    return pl.pallas_call(
        body, name=name, grid=(s // tm,),
        in_specs=[pl.BlockSpec((tm, k), lambda i: (i, 0)), pl.BlockSpec((g, k, ns), lambda i: (0, 0, 0))],
        out_specs=pl.BlockSpec((tm, g * ns), lambda i: (i, 0)),
        out_shape=jax.ShapeDtypeStruct((s, g * ns), out_dtype), compiler_params=_params(("parallel",)))(a, bg)


def _mm_nn_full(a, b, name, res=None):
    s, k = a.shape
    n = b.shape[1]
    tm = _row_tile(s)
    rs = pl.BlockSpec((tm, n), lambda i, j, kk: (i, 0))
    return _mm(a, b, name=name, dims=NN, grid=(s // tm, 1, 1),
               a_spec=pl.BlockSpec((tm, k), lambda i, j, kk: (i, 0)),
               b_spec=pl.BlockSpec((k, n), lambda i, j, kk: (0, 0)),
               o_spec=rs, o_tile=(tm, n), out_shape=jax.ShapeDtypeStruct((s, n), F32),
               res=res, res_spec=rs if res is not None else None)


def _mm_nt_full(a, b, name):
    s, n = a.shape
    k = b.shape[0]
    tm = _row_tile(s)
    return _mm(a, b, name=name, dims=NT, grid=(s // tm, 1, 1),
               a_spec=pl.BlockSpec((tm, n), lambda i, j, kk: (i, 0)),
               b_spec=pl.BlockSpec((k, n), lambda i, j, kk: (0, 0)),
               o_spec=pl.BlockSpec((tm, k), lambda i, j, kk: (i, 0)), o_tile=(tm, k),
               out_shape=jax.ShapeDtypeStruct((s, k), F32))


def _mm_nt_sharded(a, bg, name):
    s = a.shape[0]
    g, k, ns = bg.shape
    tm = _row_tile(s)

    def body(a_ref, b_ref, o_ref):
        acc = lax.dot_general(a_ref[:, 0:ns], b_ref[0], (NT, ((), ())), preferred_element_type=F32)
        for gi in range(1, g):
            acc = acc + lax.dot_general(a_ref[:, gi * ns:(gi + 1) * ns], b_ref[gi], (NT, ((), ())),
                                        preferred_element_type=F32)
        o_ref[...] = acc

    return pl.pallas_call(
        body, name=name, grid=(s // tm,),
        in_specs=[pl.BlockSpec((tm, g * ns), lambda i: (i, 0)), pl.BlockSpec((g, k, ns), lambda i: (0, 0, 0))],
        out_specs=pl.BlockSpec((tm, k), lambda i: (i, 0)), out_shape=jax.ShapeDtypeStruct((s, k), F32),
        compiler_params=_params(("parallel",)))(a, bg)


def _mm_tn_sharded(a, b, g, name, n_split=1):
    s, k = a.shape
    ns = b.shape[1] // g
    gs = g // n_split
    tk = _contract_tile(s)

    def body(a_ref, b_ref, o_ref):
        first = pl.program_id(1) == 0
        at = a_ref[...].T
        for gi in range(gs):
            part = jnp.dot(at, b_ref[:, gi * ns:(gi + 1) * ns], preferred_element_type=F32)

            @pl.when(first)
            def _(gi=gi, part=part):
                o_ref[gi] = part

            @pl.when(jnp.logical_not(first))
            def _(gi=gi, part=part):
                o_ref[gi] += part

    return pl.pallas_call(
        body, name=name, grid=(n_split, s // tk),
        in_specs=[pl.BlockSpec((tk, k), lambda j, kk: (kk, 0)), pl.BlockSpec((tk, gs * ns), lambda j, kk: (kk, j))],
        out_specs=pl.BlockSpec((gs, k, ns), lambda j, kk: (j, 0, 0)),
        out_shape=jax.ShapeDtypeStruct((g, k, ns), F32), compiler_params=_params(("parallel", "arbitrary")))(a, b)


def _mm_tn_full(a, b, name, tm):
    s, m = a.shape
    n = b.shape[1]
    tk = _contract_tile(s)
    return _mm(a, b, name=name, dims=TN, grid=(m // tm, 1, s // tk),
               a_spec=pl.BlockSpec((tk, tm), lambda i, j, kk: (kk, i)),
               b_spec=pl.BlockSpec((tk, n), lambda i, j, kk: (kk, 0)),
               o_spec=pl.BlockSpec((tm, n), lambda i, j, kk: (i, 0)), o_tile=(tm, n),
               out_shape=jax.ShapeDtypeStruct((m, n), F32))


def _rms_fwd(x, g, name):
    s, d = x.shape
    ts = _row_tile(s)

    def body(x_ref, g_ref, u_ref):
        xv = x_ref[...]
        u_ref[...] = (xv * _rstd(xv) * g_ref[...]).astype(BF16)

    row = pl.BlockSpec((ts, d), lambda i: (i, 0))
    return pl.pallas_call(
        body, name=name, grid=(s // ts,), in_specs=[row, pl.BlockSpec((1, d), lambda i: (0, 0))],
        out_specs=row, out_shape=jax.ShapeDtypeStruct((s, d), BF16),
        compiler_params=_params(("parallel",)))(x, g)


def _rms_bwd(h, du, g, dh_in, name):
    s, d = h.shape
    ts = _row_tile(s)

    def body(h_ref, du_ref, g_ref, dhin_ref, dh_ref, dhb_ref, gg_ref):
        hv = h_ref[...]
        r = _rstd(hv)
        xhat = hv * r
        duv = du_ref[...]
        dxhat = duv * g_ref[...]
        m = jnp.mean(dxhat * xhat, axis=-1, keepdims=True)
        dh = dhin_ref[...] + r * (dxhat - xhat * m)
        dh_ref[...] = dh
        dhb_ref[...] = dh.astype(BF16)
        part = jnp.sum(duv * xhat, axis=0, keepdims=True)

        @pl.when(pl.program_id(0) == 0)
        def _():
            gg_ref[...] = part

        @pl.when(pl.program_id(0) > 0)
        def _():
            gg_ref[...] += part

    row = pl.BlockSpec((ts, d), lambda i: (i, 0))
    vec = pl.BlockSpec((1, d), lambda i: (0, 0))
    return pl.pallas_call(
        body, name=name, grid=(s // ts,), in_specs=[row, row, vec, row], out_specs=[row, row, vec],
        out_shape=[jax.ShapeDtypeStruct((s, d), F32), jax.ShapeDtypeStruct((s, d), BF16),
                   jax.ShapeDtypeStruct((1, d), F32)],
        compiler_params=_params(("arbitrary",)))(h, du, g, dh_in)


def _head_sum(x, bd):
    return _tri_dot(_split(x), bd)


def _qkv_prep(proj, qg, kg, bd):
    s = proj.shape[0]
    ts = _row_tile(s)

    def body(q_ref, k_ref, v_ref, qg_ref, kg_ref, bd_ref, qs_ref, kh_ref, vb_ref):
        def norm(x, gain):
            ms = _head_sum(x * x, bd_ref[...]) * (1.0 / HEAD_DIM)
            return x * lax.rsqrt(ms + EPS) * gain

        qs_ref[...] = (norm(q_ref[...], qg_ref[...]) * ATTN_SCALE).astype(BF16)
        kh_ref[...] = norm(k_ref[...], kg_ref[...]).astype(BF16)
        vb_ref[...] = v_ref[...].astype(BF16)

    col = lambda c: pl.BlockSpec((ts, ATTN_W), lambda i: (i, c))
    vec = pl.BlockSpec((1, ATTN_W), lambda i: (0, 0))
    out = pl.BlockSpec((ts, ATTN_W), lambda i: (i, 0))
    sds = jax.ShapeDtypeStruct((s, ATTN_W), BF16)
    return pl.pallas_call(
        body, name="qkv_prep", grid=(s // ts,),
        in_specs=[col(0), col(1), col(2), vec, vec, pl.BlockSpec((ATTN_W, ATTN_W), lambda i: (0, 0))],
        out_specs=[out, out, out], out_shape=[sds, sds, sds],
        compiler_params=_params(("parallel",)))(proj, proj, proj, qg, kg, bd)


def _qk_bwd(proj, dqh, dkh, dv, qg, kg, bd):
    s = proj.shape[0]
    ts = _row_tile(s)

    def body(q_ref, k_ref, dqh_ref, dkh_ref, dv_ref, qg_ref, kg_ref, bd_ref, out_ref, gq_ref, gk_ref):
        first = pl.program_id(0) == 0

        def bwd(x, dy, gain, gg_ref):
            ms = _head_sum(x * x, bd_ref[...]) * (1.0 / HEAD_DIM)
            r = lax.rsqrt(ms + EPS)
            xhat = x * r
            dxhat = dy * gain
            m = _head_sum(dxhat * xhat, bd_ref[...]) * (1.0 / HEAD_DIM)
            part = jnp.sum(dy * xhat, axis=0, keepdims=True)

            @pl.when(first)
            def _():
                gg_ref[...] = part

            @pl.when(jnp.logical_not(first))
            def _():
                gg_ref[...] += part

            return r * (dxhat - xhat * m)

        out_ref[:, 0:ATTN_W] = bwd(q_ref[...], dqh_ref[...], qg_ref[...], gq_ref).astype(BF16)
        out_ref[:, ATTN_W:2 * ATTN_W] = bwd(k_ref[...], dkh_ref[...], kg_ref[...], gk_ref).astype(BF16)
        out_ref[:, 2 * ATTN_W:3 * ATTN_W] = dv_ref[...].astype(BF16)

    col = lambda c: pl.BlockSpec((ts, ATTN_W), lambda i: (i, c))
    row = pl.BlockSpec((ts, ATTN_W), lambda i: (i, 0))
    vec = pl.BlockSpec((1, ATTN_W), lambda i: (0, 0))
    return pl.pallas_call(
        body, name="qk_bwd", grid=(s // ts,),
        in_specs=[col(0), col(1), row, row, row, vec, vec, pl.BlockSpec((ATTN_W, ATTN_W), lambda i: (0, 0))],
        out_specs=[pl.BlockSpec((ts, 3 * ATTN_W), lambda i: (i, 0)), vec, vec],
        out_shape=[jax.ShapeDtypeStruct((s, 3 * ATTN_W), BF16), jax.ShapeDtypeStruct((1, ATTN_W), F32),
                   jax.ShapeDtypeStruct((1, ATTN_W), F32)],
        compiler_params=_params(("arbitrary",)))(proj, proj, dqh, dkh, dv, qg, kg, bd)


def _split(x):
    hi = x.astype(BF16)
    return hi, (x - hi.astype(F32)).astype(BF16)


def _tri_dot(parts, tri):
    hi, lo = parts
    return jnp.dot(hi, tri, preferred_element_type=F32) + jnp.dot(lo, tri, preferred_element_type=F32)


def _log_sigmoids(z):
    neg_abs = lax.bitcast_convert_type(lax.bitcast_convert_type(z, jnp.uint32) | jnp.uint32(0x80000000), F32)
    lb = jnp.minimum(z, 0.0) - jnp.log(1.0 + jnp.exp(neg_abs))
    return lb, lb - z


DEAD_LOG_WEIGHT = -106.0


def _sweep_key_blocks(tiles, alive, i):
    @pl.when(i == 0)
    def _():
        tiles([0], [True])

    @pl.when(i > 0)
    def _():
        tiles([i, i - 1], [True, False])

    def more(state):
        kb, live = state
        return jnp.logical_and(kb >= 0, live > 0)

    def step(state):
        kb, _ = state
        tiles([kb], [False])
        return kb - 1, alive().astype(jnp.int32)

    lax.while_loop(more, step, (i - 2, alive().astype(jnp.int32)))


def _head_masks():
    lane = lax.broadcasted_iota(jnp.int32, (1, LANES), 1)
    return [lane < HEAD_DIM, lane >= HEAD_DIM]


def _attn_fwd(qs, kh, vb, tri_excl, carry=None):
    s = qs.shape[0]
    t = tri_excl.shape[0]
    nq = s // t

    def body(q_ref, k_ref, v_ref, tri_ref, o_ref, ob_ref, acc_ref, c_ref):
        i = pl.program_id(1)
        hmask = _head_masks()
        q = q_ref[...]
        qm = [jnp.where(hm, q, jnp.zeros_like(q)) for hm in hmask]
        acc_ref[...] = jnp.zeros_like(acc_ref)
        c_ref[...] = jnp.zeros_like(c_ref)
        causal = (lax.broadcasted_iota(jnp.int32, (t, t), 1) < lax.broadcasted_iota(jnp.int32, (t, t), 0))

        def tiles(kbs, masked):
            tri = tri_ref[...]
            starts = [pl.multiple_of(kb * t, t) for kb in kbs]
            kblks = [k_ref[pl.ds(k0, t), :] for k0 in starts]
            vblks = [v_ref[pl.ds(k0, t), :] for k0 in starts]
            chains = [(j, h) for j in range(len(kbs)) for h in range(2)]
            carry = [c_ref[h] for h in range(2)]
            pv = [None, None]
            lbs, loms, between = {}, {}, {}
            for step in range(len(chains) + 2):
                if step < len(chains):
                    j, h = chains[step]
                    z = lax.dot_general(qm[h], kblks[j], (NT, ((), ())), preferred_element_type=F32)
                    lbs[step], lom = _log_sigmoids(z)
                    loms[step] = jnp.where(causal, lom, 0.0) if masked[j] else lom
                if 0 <= step - 1 < len(chains):
                    between[step - 1] = _tri_dot(_split(loms[step - 1]), tri)
                if 0 <= step - 2 < len(chains):
                    n = step - 2
                    j, h = chains[n]
                    w = jnp.exp(lbs[n] + between[n] + carry[h])
                    if masked[j]:
                        w = jnp.where(causal, w, 0.0)
                    carry[h] = carry[h] + jnp.sum(loms[n], axis=-1, keepdims=True)
                    part = jnp.dot(w.astype(BF16), vblks[j], preferred_element_type=F32)
                    pv[h] = part if pv[h] is None else pv[h] + part
            for h in range(2):
                c_ref[h] = carry[h]
                acc_ref[h] += pv[h]

        _sweep_key_blocks(tiles, lambda: jnp.max(c_ref[...]) > DEAD_LOG_WEIGHT, i)
        o = jnp.where(hmask[0], acc_ref[0], acc_ref[1])
        o_ref[...] = o
        ob_ref[...] = o.astype(BF16)

    qspec = pl.BlockSpec((t, LANES), lambda hp, i: (i, hp))
    kspec = pl.BlockSpec((s, LANES), lambda hp, i: (0, hp))
    return _call(
        body, name="attn_fwd", grid=(ATTN_W // LANES, nq),
        in_specs=[qspec, kspec, kspec, pl.BlockSpec((t, t), lambda hp, i: (0, 0))],
        out_specs=[qspec, qspec],
        out_shape=[jax.ShapeDtypeStruct((s, ATTN_W), F32), jax.ShapeDtypeStruct((s, ATTN_W), BF16)],
        scratch_shapes=[pltpu.VMEM((2, t, LANES), F32), pltpu.VMEM((2, t, 1), F32)],
        args=(qs, kh, vb, tri_excl), semantics=("parallel", "arbitrary"), carry=carry)


def _attn_bwd(qs, kh, vb, o, dmix, tri_excl, tri_incl, carry=None):
    s = qs.shape[0]
    t = tri_excl.shape[0]
    nq = s // t

    def body(q_ref, k_ref, v_ref, o_ref, do_ref, te_ref, ti_ref, dq_ref, dk_ref, dv_ref, dqacc_ref, c_ref):
        i = pl.program_id(1)

        @pl.when(i == 0)
        def _():
            dk_ref[...] = jnp.zeros_like(dk_ref)
            dv_ref[...] = jnp.zeros_like(dv_ref)

        hmask = _head_masks()
        q = q_ref[...]
        do = do_ref[...]
        dob = do.astype(BF16)
        prod = dob.astype(F32) * o_ref[...]
        qm =[jnp.where(hm, q, jnp.zeros_like(q)) for hm in hmask]
        dom = [jnp.where(hm, dob, jnp.zeros_like(dob)) for hm in hmask]
        total = [jnp.sum(jnp.where(hm, prod, 0.0), axis=-1, keepdims=True) for hm in hmask]
        dqacc_ref[...] = jnp.zeros_like(dqacc_ref)
        c_ref[...] = jnp.zeros_like(c_ref)
        causal = (lax.broadcasted_iota(jnp.int32, (t, t), 1) < lax.broadcasted_iota(jnp.int32, (t, t), 0))

        def tiles(kbs, masked):
            te = te_ref[...]
            ti = ti_ref[...]
            starts = [pl.multiple_of(kb * t, t) for kb in kbs]
            kblks = [k_ref[pl.ds(k0, t), :] for k0 in starts]
            vblks = [v_ref[pl.ds(k0, t), :] for k0 in starts]
            chains = [(j, h) for j in range(len(kbs)) for h in range(2)]
            c_lom = [c_ref[2 * h] for h in range(2)]
            c_g = [c_ref[2 * h + 1] for h in range(2)]
            lbs, loms, dws, between, wbs, gs, g_after = {}, {}, {}, {}, {}, {}, {}
            dq = [None, None]
            dk = [None] * len(kbs)
            dv = [None] * len(kbs)
            add = lambda acc, part: part if acc is None else acc + part
            for step in range(len(chains) + 3):
                if step < len(chains):
                    j, h = chains[step]
                    z = lax.dot_general(qm[h], kblks[j], (NT, ((), ())), preferred_element_type=F32)
                    dws[step] = lax.dot_general(dom[h], vblks[j], (NT, ((), ())), preferred_element_type=F32)
                    lbs[step], lom = _log_sigmoids(z)
                    loms[step] = jnp.where(causal, lom, 0.0) if masked[j] else lom
                if 0 <= step - 1 < len(chains):
                    between[step - 1] = _tri_dot(_split(loms[step - 1]), te)
                if 0 <= step - 2 < len(chains):
                    n = step - 2
                    j, h = chains[n]
                    w = jnp.exp(lbs[n] + between[n] + c_lom[h])
                    if masked[j]:
                        w = jnp.where(causal, w, 0.0)
                    c_lom[h] = c_lom[h] + jnp.sum(loms[n], axis=-1, keepdims=True)
                    wbs[n] = w.astype(BF16)
                    gs[n] = dws[n] * wbs[n].astype(F32)
                    g_after[n] = _tri_dot(_split(gs[n]), ti)
                if 0 <= step - 3 < len(chains):
                    n = step - 3
                    j, h = chains[n]
                    beta = jnp.exp(lbs[n])
                    dz = gs[n] * (1.0 - beta) - beta * (total[h] - (g_after[n] + c_g[h]))
                    if masked[j]:
                        dz = jnp.where(causal, dz, 0.0)
                    c_g[h] = c_g[h] + jnp.sum(gs[n], axis=-1, keepdims=True)
                    dzb = dz.astype(BF16)
                    dq[h] = add(dq[h], jnp.dot(dzb, kblks[j], preferred_element_type=F32))
                    dk[j] = add(dk[j], lax.dot_general(dzb, qm[h], (TN, ((), ())), preferred_element_type=F32))
                    dv[j] = add(dv[j], lax.dot_general(wbs[n], dom[h], (TN, ((), ())), preferred_element_type=F32))
            for h in range(2):
                c_ref[2 * h] = c_lom[h]
                c_ref[2 * h + 1] = c_g[h]
                dqacc_ref[h] += dq[h]
            for j, k0 in enumerate(starts):
                dk_ref[pl.ds(k0, t), :] += dk[j]
                dv_ref[pl.ds(k0, t), :] += dv[j]

        _sweep_key_blocks(tiles, lambda: jnp.maximum(jnp.max(c_ref[0]), jnp.max(c_ref[2])) > DEAD_LOG_WEIGHT, i)
        dq_ref[...] = jnp.where(hmask[0], dqacc_ref[0], dqacc_ref[1]) * ATTN_SCALE

    qspec = pl.BlockSpec((t, LANES), lambda hp, i: (i, hp))
    kspec = pl.BlockSpec((s, LANES), lambda hp, i: (0, hp))
    tspec = pl.BlockSpec((t, t), lambda hp, i: (0, 0))
    sds = jax.ShapeDtypeStruct((s, ATTN_W), F32)
    return _call(
        body, name="attn_bwd", grid=(ATTN_W // LANES, nq),
        in_specs=[qspec, kspec, kspec, qspec, qspec, tspec, tspec],
        out_specs=[qspec, kspec, kspec], out_shape=[sds, sds, sds],
        scratch_shapes=[pltpu.VMEM((2, t, LANES), F32), pltpu.VMEM((4, t, 1), F32)],
        args=(qs, kh, vb, o, dmix, tri_excl, tri_incl), semantics=("parallel", "arbitrary"), carry=carry)


CONV_ROWS = 64
CONV_COLS = 256


def _taps(src_ref, w_ref, n_taps, first_row, rows, reverse=False):
    width = src_ref.shape[1]
    cols = min(CONV_COLS, width)
    out = []
    for r0 in range(0, rows, CONV_ROWS):
        for c0 in range(0, width, cols):
            acc = jnp.zeros((CONV_ROWS, cols), F32)
            for k in range(n_taps):
                off = (n_taps - 1 - k) if reverse else k
                acc = acc + w_ref[k:k + 1, c0:c0 + cols] * src_ref[first_row + r0 + off:first_row + r0 + off + CONV_ROWS,
                                                                   c0:c0 + cols]
            out.append(((r0, c0), acc))
    return out


def _conv_fwd(proj, dw_w, dw_b, ln_g, ln_b, carry=None):
    s = proj.shape[0]
    ts = _row_tile(s)
    hb = ts // CONV_HALO

    def body(a_ref, g_ref, ha_ref, hg_ref, w_ref, b_ref, lg_ref, lb_ref, c1_ref, c3_ref, pad_ref):
        i = pl.program_id(0)
        halo = ha_ref[...] * _sigmoid(hg_ref[...])
        pad_ref[0:CONV_HALO, :] = jnp.where(i > 0, halo, 0.0)
        pad_ref[CONV_HALO:, :] = a_ref[...] * _sigmoid(g_ref[...])
        first = CONV_HALO - (CONV_K - 1)
        for (r0, c0), acc in _taps(pad_ref, w_ref, CONV_K, first, ts):
            c1_ref[r0:r0 + CONV_ROWS, c0:c0 + acc.shape[1]] = acc + b_ref[:, c0:c0 + acc.shape[1]]
        c1 = c1_ref[...]
        xc = c1 - jnp.mean(c1, axis=-1, keepdims=True)
        c2 = xc * _rstd(xc) * lg_ref[...] + lb_ref[...]
        c3_ref[...] = (c2 * _sigmoid(c2)).astype(BF16)

    cur = lambda c: pl.BlockSpec((ts, CONV_W), lambda i: (i, c))
    halo = lambda c: pl.BlockSpec((CONV_HALO, CONV_W), lambda i: (jnp.maximum(i * hb - 1, 0), c))
    vec = pl.BlockSpec((1, CONV_W), lambda i: (0, 0))
    row = pl.BlockSpec((ts, CONV_W), lambda i: (i, 0))
    return _call(
        body, name="conv_fwd", grid=(s // ts,),
        in_specs=[cur(3), cur(4), halo(3), halo(4), pl.BlockSpec((CONV_HALO, CONV_W), lambda i: (0, 0)), vec, vec, vec],
        out_specs=[row, row],
        out_shape=[jax.ShapeDtypeStruct((s, CONV_W), F32), jax.ShapeDtypeStruct((s, CONV_W), BF16)],
        scratch_shapes=[pltpu.VMEM((ts + CONV_HALO, CONV_W), F32)],
        args=(proj, proj, proj, proj, dw_w, dw_b, ln_g, ln_b), semantics=("parallel",), carry=carry)


def _conv_bwd_ln(dmix, c1, ln_g, ln_b):
    s = c1.shape[0]
    ts = _row_tile(s)

    def body(d_ref, c1_ref, lg_ref, lb_ref, dc1_ref, glg_ref, glb_ref, gb_ref):
        c1v = c1_ref[...]
        xc = c1v - jnp.mean(c1v, axis=-1, keepdims=True)
        r = _rstd(xc)
        xhat = xc * r
        c2 = xhat * lg_ref[...] + lb_ref[...]
        sg = _sigmoid(c2)
        dc2 = d_ref[...] * (sg * (1.0 + c2 * (1.0 - sg)))
        dxhat = dc2 * lg_ref[...]
        dc1 = r * (dxhat - jnp.mean(dxhat, axis=-1, keepdims=True)
                   - xhat * jnp.mean(dxhat * xhat, axis=-1, keepdims=True))
        dc1_ref[...] = dc1
        parts = [(glg_ref, jnp.sum(dc2 * xhat, axis=0, keepdims=True)),
                 (glb_ref, jnp.sum(dc2, axis=0, keepdims=True)),
                 (gb_ref, jnp.sum(dc1, axis=0, keepdims=True))]

        @pl.when(pl.program_id(0) == 0)
        def _():
            for ref, part in parts:
                ref[...] = part

        @pl.when(pl.program_id(0) > 0)
        def _():
            for ref, part in parts:
                ref[...] += part

    row = pl.BlockSpec((ts, CONV_W), lambda i: (i, 0))
    vec = pl.BlockSpec((1, CONV_W), lambda i: (0, 0))
    vsd = jax.ShapeDtypeStruct((1, CONV_W), F32)
    return pl.pallas_call(
        body, name="conv_bwd_ln", grid=(s // ts,),
        in_specs=[pl.BlockSpec((ts, CONV_W), lambda i: (i, 1)), row, vec, vec],
        out_specs=[row, vec, vec, vec],
        out_shape=[jax.ShapeDtypeStruct((s, CONV_W), F32), vsd, vsd, vsd],
        compiler_params=_params(("arbitrary",)))(dmix, c1, ln_g, ln_b)


def _conv_bwd_taps(proj, dc1, dw_w):
    s = proj.shape[0]
    ts = _row_tile(s)
    hb = ts // CONV_HALO
    last = s // CONV_HALO - 1
    nsteps = s // ts

    def body(a_ref, g_ref, ha_ref, hg_ref, d_ref, hd_ref, w_ref, out_ref, gw_ref, pad_ref, dpad_ref):
        i = pl.program_id(0)
        av = a_ref[...]
        sg = _sigmoid(g_ref[...])
        halo = ha_ref[...] * _sigmoid(hg_ref[...])
        pad_ref[0:CONV_HALO, :] = jnp.where(i > 0, halo, 0.0)
        pad_ref[CONV_HALO:, :] = av * sg
        dpad_ref[0:ts, :] = d_ref[...]
        dpad_ref[ts:, :] = jnp.where(i < nsteps - 1, hd_ref[...], 0.0)

        @pl.when(i == 0)
        def _():
            gw_ref[...] = jnp.zeros_like(gw_ref)

        first = CONV_HALO - (CONV_K - 1)
        for k in range(CONV_K):
            gw_ref[k:k + 1, :] += jnp.sum(d_ref[...] * pad_ref[first + k:first + k + ts, :], axis=0, keepdims=True)
        for (r0, c0), dc0 in _taps(dpad_ref, w_ref, CONV_K, 0, ts, reverse=True):
            cs = slice(c0, c0 + dc0.shape[1])
            a_c = a_ref[r0:r0 + CONV_ROWS, cs]
            sg_c = _sigmoid(g_ref[r0:r0 + CONV_ROWS, cs])
            out_ref[r0:r0 + CONV_ROWS, cs] = (dc0 * sg_c).astype(BF16)
            out_ref[r0:r0 + CONV_ROWS, CONV_W + c0:CONV_W + c0 + dc0.shape[1]] = (
                dc0 * a_c * sg_c * (1.0 - sg_c)).astype(BF16)

    cur = lambda c: pl.BlockSpec((ts, CONV_W), lambda i: (i, c))
    halo = lambda c: pl.BlockSpec((CONV_HALO, CONV_W), lambda i: (jnp.maximum(i * hb - 1, 0), c))
    row = pl.BlockSpec((ts, CONV_W), lambda i: (i, 0))
    nxt = pl.BlockSpec((CONV_HALO, CONV_W), lambda i: (jnp.minimum((i + 1) * hb, last), 0))
    wspec = pl.BlockSpec((CONV_HALO, CONV_W), lambda i: (0, 0))
    return pl.pallas_call(
        body, name="conv_bwd_taps", grid=(nsteps,),
        in_specs=[cur(3), cur(4), halo(3), halo(4), row, nxt, wspec],
        out_specs=[pl.BlockSpec((ts, 2 * CONV_W), lambda i: (i, 0)), wspec],
        out_shape=[jax.ShapeDtypeStruct((s, 2 * CONV_W), BF16), jax.ShapeDtypeStruct((CONV_HALO, CONV_W), F32)],
        scratch_shapes=[pltpu.VMEM((ts + CONV_HALO, CONV_W), F32), pltpu.VMEM((ts + CONV_HALO, CONV_W), F32)],
        compiler_params=_params(("arbitrary",)))(proj, proj, proj, proj, dc1, dc1, dw_w)


SQRT_HALF = 0.7071067811865476
INV_SQRT_2PI = 0.3989422804014327


def _gelu_parts(x):
    cdf = 0.5 * (1.0 + lax.erf(x * SQRT_HALF))
    return x * cdf, cdf + x * (INV_SQRT_2PI * jnp.exp(-0.5 * x * x))


def _ffn_tile(dff):
    return dff // 2


def _ffn_gate2(pad_ref, w_ref, b_ref, ts):
    first = FFN_HALO - (FFN_K - 1)
    g2 = b_ref[...] + w_ref[0:1, :] * pad_ref[first:first + ts, :]
    for k in range(1, FFN_K):
        g2 = g2 + w_ref[k:k + 1, :] * pad_ref[first + k:first + k + ts, :]
    return g2


def _ffn_act(up, fw, fb):
    s = up.shape[0]
    dff = up.shape[1] // 2
    tc = _ffn_tile(dff)
    nj = dff // tc
    ts = _row_tile(s) // 2
    hb = ts // FFN_HALO

    def body(g_ref, v_ref, hg_ref, w_ref, b_ref, act_ref, pad_ref):
        i = pl.program_id(0)
        pad_ref[0:FFN_HALO, :] = jnp.where(i > 0, hg_ref[...], 0.0)
        pad_ref[FFN_HALO:, :] = g_ref[...]
        gelu, _ = _gelu_parts(_ffn_gate2(pad_ref, w_ref, b_ref, ts))
        act_ref[...] = (gelu * v_ref[...]).astype(BF16)

    return pl.pallas_call(
        body, name="ffn_act", grid=(s // ts, nj),
        in_specs=[pl.BlockSpec((ts, tc), lambda i, j: (i, j)), pl.BlockSpec((ts, tc), lambda i, j: (i, j + nj)),
                  pl.BlockSpec((FFN_HALO, tc), lambda i, j: (jnp.maximum(i * hb - 1, 0), j)),
                  pl.BlockSpec((FFN_HALO, tc), lambda i, j: (0, j)), pl.BlockSpec((1, tc), lambda i, j: (0, j))],
        out_specs=pl.BlockSpec((ts, tc), lambda i, j: (i, j)),
        out_shape=jax.ShapeDtypeStruct((s, dff), BF16),
        scratch_shapes=[pltpu.VMEM((ts + FFN_HALO, tc), F32)],
        compiler_params=_params(("parallel", "parallel")))(up, up, up, fw, fb)


def _ffn_bwd_act(dact, up, fw, fb, carry=None):
    s = up.shape[0]
    dff = up.shape[1] // 2
    tc = _ffn_tile(dff)
    nj = dff // tc
    ts = _row_tile(s) // 2
    hb = ts // FFN_HALO

    def body(d_ref, g_ref, v_ref, hg_ref, w_ref, b_ref, dg2_ref, dval_ref, gw_ref, gb_ref, pad_ref):
        i = pl.program_id(1)
        pad_ref[0:FFN_HALO, :] = jnp.where(i > 0, hg_ref[...], 0.0)
        pad_ref[FFN_HALO:, :] = g_ref[...]
        gelu, dgelu = _gelu_parts(_ffn_gate2(pad_ref, w_ref, b_ref, ts))
        dactv = d_ref[...]
        dval_ref[...] = (dactv * gelu).astype(BF16)
        dg2 = dactv * v_ref[...] * dgelu
        dg2_ref[...] = dg2

        @pl.when(i == 0)
        def _():
            gw_ref[...] = jnp.zeros_like(gw_ref)
            gb_ref[...] = jnp.zeros_like(gb_ref)

        gb_ref[...] += jnp.sum(dg2, axis=0, keepdims=True)
        first = FFN_HALO - (FFN_K - 1)
        for k in range(FFN_K):
            gw_ref[k:k + 1, :] += jnp.sum(dg2 * pad_ref[first + k:first + k + ts, :], axis=0, keepdims=True)

    blk = pl.BlockSpec((ts, tc), lambda j, i: (i, j))
    wspec = pl.BlockSpec((FFN_HALO, tc), lambda j, i: (0, j))
    bspec = pl.BlockSpec((1, tc), lambda j, i: (0, j))
    return _call(
        body, name="ffn_bwd_act", grid=(nj, s // ts),
        in_specs=[blk, blk, pl.BlockSpec((ts, tc), lambda j, i: (i, j + nj)),
                  pl.BlockSpec((FFN_HALO, tc), lambda j, i: (jnp.maximum(i * hb - 1, 0), j)), wspec, bspec],
        out_specs=[blk, pl.BlockSpec((ts, tc), lambda j, i: (i, j + nj)), wspec, bspec],
        out_shape=[jax.ShapeDtypeStruct((s, dff), F32), jax.ShapeDtypeStruct((s, 2 * dff), BF16),
                   jax.ShapeDtypeStruct((FFN_HALO, dff), F32), jax.ShapeDtypeStruct((1, dff), F32)],
        scratch_shapes=[pltpu.VMEM((ts + FFN_HALO, tc), F32)],
        args=(dact, up, up, up, fw, fb), semantics=("parallel", "arbitrary"), carry=carry)


def _ffn_bwd_conv(dg2, fw, dup):
    s, dff = dg2.shape
    tc = _ffn_tile(dff)
    ts = _row_tile(s) // 2
    hb = ts // FFN_HALO
    last = s // FFN_HALO - 1
    nsteps = s // ts

    def body(d_ref, hd_ref, w_ref, dup_ref, out_ref, pad_ref):
        i = pl.program_id(0)
        pad_ref[0:ts, :] = d_ref[...]
        pad_ref[ts:, :] = jnp.where(i < nsteps - 1, hd_ref[...], 0.0)
        dg = w_ref[0:1, :] * pad_ref[FFN_K - 1:FFN_K - 1 + ts, :]
        for k in range(1, FFN_K):
            dg = dg + w_ref[k:k + 1, :] * pad_ref[FFN_K - 1 - k:FFN_K - 1 - k + ts, :]
        out_ref[...] = dg.astype(BF16)

    blk = pl.BlockSpec((ts, tc), lambda i, j: (i, j))
    return pl.pallas_call(
        body, name="ffn_bwd_conv", grid=(nsteps, dff // tc),
        in_specs=[blk, pl.BlockSpec((FFN_HALO, tc), lambda i, j: (jnp.minimum((i + 1) * hb, last), j)),
                  pl.BlockSpec((FFN_HALO, tc), lambda i, j: (0, j)), ANY],
        out_specs=blk, out_shape=jax.ShapeDtypeStruct(dup.shape, BF16), input_output_aliases={3: 0},
        scratch_shapes=[pltpu.VMEM((ts + FFN_HALO, tc), F32)],
        compiler_params=_params(("parallel", "parallel")))(dg2, dg2, fw, dup)


def _ple_loss(h2, zg, pp, target):
    s, d = h2.shape
    ts = _row_tile(s)

    def body(h_ref, z_ref, p_ref, t_ref, dh_ref, dpp_ref, dz_ref, loss_ref):
        pg = _sigmoid(z_ref[...])
        ppv = p_ref[...]
        diff = h_ref[...] + pg * ppv - t_ref[...]
        dh = diff * (1.0 / d)
        dh_ref[...] = dh
        dpp_ref[...] = (dh * pg).astype(BF16)
        dz_ref[...] = (dh * ppv * pg * (1.0 - pg)).astype(BF16)
        part = jnp.sum(jnp.sum(diff * diff, axis=0, keepdims=True), axis=1, keepdims=True)

        @pl.when(pl.program_id(0) == 0)
        def _():
            loss_ref[...] = jnp.zeros_like(loss_ref)

        loss_ref[...] += jnp.broadcast_to(part, loss_ref.shape)

    row = pl.BlockSpec((ts, d), lambda i: (i, 0))
    return pl.pallas_call(
        body, name="ple_loss", grid=(s // ts,), in_specs=[row, row, row, row],
        out_specs=[row, row, row, pl.BlockSpec((8, LANES), lambda i: (0, 0))],
        out_shape=[jax.ShapeDtypeStruct((s, d), F32), jax.ShapeDtypeStruct((s, d), BF16),
                   jax.ShapeDtypeStruct((s, d), BF16), jax.ShapeDtypeStruct((8, LANES), F32)],
        compiler_params=_params(("arbitrary",)))(h2, zg, pp, target)


def _local_step(x, p, target, w, shards=None):
    riding = shards is not None
    major = lambda n, g: g if n in COL_SHARDED else g.reshape(N_CHIPS, -1, g.shape[-1])
    full = lambda n, g: g if n in COL_SHARDED else g.reshape(-1, g.shape[-1])

    def ride(names, arrays):
        return ([arrays[n] for n in names], False) if riding else None

    def land(names, got):
        return {n: full(n, g) for n, g in zip(names, got)}
    s = x.shape[0]
    t = min(256, s)
    tri = jnp.tril(jnp.ones((t, t), F32))
    tri_incl = tri.astype(BF16)
    tri_excl = jnp.tril(jnp.ones((t, t), F32), -1).astype(BF16)
    bd = jnp.kron(jnp.eye(N_HEADS, dtype=F32), jnp.ones((HEAD_DIM, HEAD_DIM), F32)).astype(BF16)
    qg = jnp.tile(w["q_gain"], (1, N_HEADS))
    kg = jnp.tile(w["k_gain"], (1, N_HEADS))
    pb = p.astype(BF16)

    u1 = _rms_fwd(x, w["g_mix"], "rms_mix")
    dw_w = jnp.pad(w["dw_w"], ((0, CONV_HALO - CONV_K), (0, 0)))
    fw = jnp.pad(w["ffn_conv_w"], ((0, FFN_HALO - FFN_K), (0, 0)))
    proj = _mm_nn_sharded(u1, w["w_in"], "mm_in")
    qs, kh, vb = _qkv_prep(proj, qg, kg, bd)
    with_attn, with_conv = ["w_out", "w_up"], ["w_down", "w_ple_gate", "w_ple_proj"]
    o, ob, *got = _attn_fwd(qs, kh, vb, tri_excl, carry=ride(with_attn, shards))
    w = {**w, **land(with_attn, got)}
    c1, c3, *got = _conv_fwd(proj, dw_w, w["dw_b"], w["conv_ln_g"], w["conv_ln_b"], carry=ride(with_conv, shards))
    w = {**w, **land(with_conv, got)}
    mix = jnp.concatenate([ob, c3], axis=1)
    h1 = _mm_nn_full(mix, w["w_out"], "mm_out", res=x)
    u2 = _rms_fwd(h1, w["g_ffn"], "rms_ffn")
    up = _mm_nn_sharded(u2, w["w_up"], "mm_up", tm=_row_tile(s) // 2)
    act = _ffn_act(up, fw, w["ffn_conv_b"])
    h2 = _mm_nn_full(act, w["w_down"], "mm_down", res=h1)
    u3 = _rms_fwd(h2, w["g_ple"], "rms_ple")
    zg = _mm_nn_full(u3, w["w_ple_gate"], "mm_ple_gate")
    pp = _mm_nn_sharded(pb, w["w_ple_proj"], "mm_ple_proj")
    dh3, dpp, dz, sq = _ple_loss(h2, zg, pp, target)

    big = {}
    small = {}
    big["w_ple_proj"] = _mm_tn_sharded(pb, dpp, N_CHIPS, "mm_g_ple_proj")
    big["w_ple_gate"] = _mm_tn_full(u3, dz, "mm_g_ple_gate", tm=u3.shape[1])
    du3 = _mm_nt_full(dz, w["w_ple_gate"], "mm_d_ple_gate")
    dh2, dh2b, small["g_ple"] = _rms_bwd(h2, du3, w["g_ple"], dh3, "rms_ple_bwd")
    big["w_down"] = _mm_tn_full(act, dh2b, "mm_g_down", tm=act.shape[1] // 2)
    dact = _mm_nt_full(dh2b, w["w_down"], "mm_d_down")
    slots = {}
    with_ffn, with_attn = ["w_ple_proj", "w_ple_gate", "w_down"], ["w_up", "w_out"]
    leaving = ([major(n, big.pop(n)) for n in with_ffn], True) if riding else None
    dg2, dup, gfw, small["ffn_conv_b"], *got = _ffn_bwd_act(dact, up, fw, w["ffn_conv_b"], carry=leaving)
    slots.update(zip(with_ffn, got))
    small["ffn_conv_w"] = gfw[:FFN_K]
    dup = _ffn_bwd_conv(dg2, fw, dup)
    big["w_up"] = _mm_tn_sharded(u2, dup, N_CHIPS, "mm_g_up", n_split=2)
    du2 = _mm_nt_sharded(dup, w["w_up"], "mm_d_up")
    dh1, dh1b, small["g_ffn"] = _rms_bwd(h1, du2, w["g_ffn"], dh2, "rms_ffn_bwd")
    big["w_out"] = _mm_tn_full(mix, dh1b, "mm_g_out", tm=mix.shape[1])
    dmix = _mm_nt_full(dh1b, w["w_out"], "mm_d_out")
    dc1, small["conv_ln_g"], small["conv_ln_b"], small["dw_b"] = _conv_bwd_ln(dmix, c1, w["conv_ln_g"], w["conv_ln_b"])
    dcacg, gdw = _conv_bwd_taps(proj, dc1, dw_w)
    small["dw_w"] = gdw[:CONV_K]
    leaving = ([major(n, big.pop(n)) for n in with_attn], True) if riding else None
    dqh, dkh, dv, *got = _attn_bwd(qs, kh, vb, o, dmix, tri_excl, tri_incl, carry=leaving)
    slots.update(zip(with_attn, got))
    dqkv, gq, gk = _qk_bwd(proj, dqh, dkh, dv, qg, kg, bd)
    small["q_gain"] = gq.reshape(N_HEADS, HEAD_DIM).sum(axis=0, keepdims=True)
    small["k_gain"] = gk.reshape(N_HEADS, HEAD_DIM).sum(axis=0, keepdims=True)
    dproj = jnp.concatenate([dqkv, dcacg], axis=1)
    big["w_in"] = _mm_tn_sharded(u1, dproj, N_CHIPS, "mm_g_in")
    du1 = _mm_nt_sharded(dproj, w["w_in"], "mm_d_in")
    grad_x, _, small["g_mix"] = _rms_bwd(x, du1, w["g_mix"], dh1, "rms_mix_bwd")
    return sq[0, 0], grad_x, big, slots, small


def _exchange(srcs, per_chip, name):
    n = len(srcs)

    def body(*refs):
        src_refs, land_refs = refs[:n], refs[n:2 * n]
        send_sems, recv_sems, local_sems = refs[2 * n:]
        local = _local_copies(src_refs, land_refs, local_sems, per_chip)
        sends = _chip_copies(src_refs, land_refs, send_sems, recv_sems, per_chip, False)
        for cp in local + sends:
            cp.start()
        for cp in _chip_copies(src_refs, land_refs, send_sems, recv_sems, per_chip, True):
            cp.wait_recv()
        for cp in sends:
            cp.wait_send()
        for cp in local:
            cp.wait()

    return pl.pallas_call(
        body, name=name, in_specs=[ANY] * n, out_specs=[ANY] * n, out_shape=_exchanged_shapes(srcs, per_chip),
        scratch_shapes=_exchange_sems(n))(*srcs)


def _finish_exchange(mine, small):
    n = len(mine)

    def body(*refs):
        gin, sin = refs[:n], refs[n]
        gout, sout = refs[n + 1:2 * n + 1], refs[2 * n + 1]
        send_sems, recv_sems, small_send, small_recv, local_sem = refs[2 * n + 2:]
        x, y, c, _ = _position()
        dev = 4 * x + 2 * y + c
        flip = lambda v, bit: 1 - v if bit else v
        others = [(flip(x, k & 4), flip(y, k & 2), flip(c, k & 1)) for k in range(1, N_DEV)]
        local = pltpu.make_async_copy(sin, sout.at[dev], local_sem)
        local.start()
        swaps = [_remote(gin[a], gout[a], send_sems, recv_sems, a, (x, y, 1 - c)) for a in range(n)]
        sends = swaps + [_remote(sin, sout.at[dev], small_send, small_recv, k, peer) for k, peer in enumerate(others)]
        for cp in sends:
            cp.start()
        for cp in swaps:
            cp.wait_recv()
        for k, (px, py, pc) in enumerate(others):
            _remote(sin, sout.at[4 * px + 2 * py + pc], small_send, small_recv, k, (px, py, pc)).wait_recv()
        for cp in sends:
            cp.wait_send()
        local.wait()

    return pl.pallas_call(
        body, name="finish_exchange", in_specs=[ANY] * (n + 1), out_specs=[ANY] * (n + 1),
        out_shape=[jax.ShapeDtypeStruct(g.shape, g.dtype) for g in mine]
        + [jax.ShapeDtypeStruct((N_DEV,) + small.shape, small.dtype)],
        scratch_shapes=[pltpu.SemaphoreType.DMA((n,)), pltpu.SemaphoreType.DMA((n,)),
                        pltpu.SemaphoreType.DMA((N_DEV - 1,)), pltpu.SemaphoreType.DMA((N_DEV - 1,)),
                        pltpu.SemaphoreType.DMA])(*mine, small)


def _elem_tile(rows):
    return 128 if rows % 128 == 0 else (64 if rows % 64 == 0 else rows)


def _sum_slots(a, name):
    g, r, c = a.shape
    tr = _elem_tile(r)

    def body(a_ref, o_ref):
        acc = a_ref[0]
        for k in range(1, g):
            acc = acc + a_ref[k]
        o_ref[...] = acc

    return pl.pallas_call(
        body, name=name, grid=(r // tr,), in_specs=[pl.BlockSpec((g, tr, c), lambda i: (0, i, 0))],
        out_specs=pl.BlockSpec((tr, c), lambda i: (i, 0)), out_shape=jax.ShapeDtypeStruct((r, c), a.dtype),
        compiler_params=_params(("parallel",)))(a)


def _adamw(wt, ga, gb, m, v, name):
    r, c = wt.shape
    tr = _elem_tile(r)
    two = gb is not None

    def body(*refs):
        if two:
            w_ref, ga_ref, gb_ref, m_ref, v_ref, g_out, d_out, m_out, v_out = refs
            g = ga_ref[...] + gb_ref[...]
        else:
            w_ref, ga_ref, m_ref, v_ref, g_out, d_out, m_out, v_out = refs
            g = ga_ref[...]
        mn = ADAM_B1 * m_ref[...] + (1.0 - ADAM_B1) * g
        vn = ADAM_B2 * v_ref[...] + (1.0 - ADAM_B2) * (g * g)
        m_hat = mn / (1.0 - ADAM_B1 ** ADAM_STEP)
        v_hat = vn / (1.0 - ADAM_B2 ** ADAM_STEP)
        g_out[...] = g
        d_out[...] = -ADAM_LR * (m_hat / (jnp.sqrt(v_hat) + ADAM_EPS) + ADAM_WD * w_ref[...])
        m_out[...] = mn
        v_out[...] = vn

    blk = pl.BlockSpec((tr, c), lambda i: (i, 0))
    args = [wt, ga] + ([gb] if two else []) + [m, v]
    sds = jax.ShapeDtypeStruct((r, c), F32)
    return pl.pallas_call(
        body, name=name, grid=(r // tr,), in_specs=[blk] * len(args), out_specs=[blk] * 4, out_shape=[sds] * 4,
        compiler_params=_params(("parallel",)))(*args)


def _pack(arrs, rows):
    flat = jnp.concatenate([a.reshape(-1) for a in arrs])
    return jnp.pad(flat, (0, rows * LANES - flat.shape[0])).reshape(rows, LANES)


def _unpack(buf, shapes):
    flat = buf.reshape(-1)
    out, off = [], 0
    for shp in shapes:
        size = 1
        for d in shp:
            size *= d
        out.append(flat[off:off + size].reshape(shp))
        off += size
    return out


BIG = ["w_in", "w_out", "w_up", "w_down", "w_ple_gate", "w_ple_proj"]
COL_SHARDED = ["w_in", "w_up", "w_ple_proj"]
SMALL_REPL = ["g_mix", "q_gain", "k_gain", "dw_b", "conv_ln_g", "conv_ln_b", "g_ffn", "ffn_conv_b", "g_ple"]
SMALL_SHARDED = ["dw_w", "ffn_conv_w"]
WEIGHTS = ["g_mix", "w_in", "q_gain", "k_gain", "dw_w", "dw_b", "conv_ln_g", "conv_ln_b", "w_out", "g_ffn", "w_up",
           "ffn_conv_w", "ffn_conv_b", "w_down", "g_ple", "w_ple_gate", "w_ple_proj"]


def _rows_for(n_elems):
    return -(-n_elems // (8 * LANES)) * 8


def kernel(x, p, g_mix, w_in, q_gain, k_gain, dw_w, dw_b, conv_ln_g, conv_ln_b, w_out, g_ffn, w_up, ffn_conv_w, ffn_conv_b, w_down, g_ple, w_ple_gate, w_ple_proj, loss_target, m_g_mix, m_w_in, m_q_gain, m_k_gain, m_dw_w, m_dw_b, m_conv_ln_g, m_conv_ln_b, m_w_out, m_g_ffn, m_w_up, m_ffn_conv_w, m_ffn_conv_b, m_w_down, m_g_ple, m_w_ple_gate, m_w_ple_proj, v_g_mix, v_w_in, v_q_gain, v_k_gain, v_dw_w, v_dw_b, v_conv_ln_g, v_conv_ln_b, v_w_out, v_g_ffn, v_w_up, v_ffn_conv_w, v_ffn_conv_b, v_w_down, v_g_ple, v_w_ple_gate, v_w_ple_proj):
    given = dict(locals())
    strip = lambda n, a: a if n in SMALL_REPL else a[0]
    wts = {n: strip(n, given[n]) for n in WEIGHTS}
    mom = {n: strip(n, given["m_" + n]) for n in WEIGHTS}
    var = {n: strip(n, given["v_" + n]) for n in WEIGHTS}
    chip = 2 * lax.axis_index("x") + lax.axis_index("y")

    small_shard_shapes = [wts[n].shape for n in SMALL_SHARDED]
    filt_rows = _rows_for(sum(wts[n].size for n in SMALL_SHARDED))
    filt = _pack([wts[n] for n in SMALL_SHARDED], filt_rows)
    w_in_all, filt_all = _exchange([wts["w_in"].astype(BF16), filt], False, "gather_first")
    known = {n: wts[n] for n in SMALL_REPL}
    known["w_in"] = w_in_all
    per_chip = [_unpack(filt_all[k], small_shard_shapes) for k in range(N_CHIPS)]
    for idx, n in enumerate(SMALL_SHARDED):
        known[n] = jnp.concatenate([per_chip[k][idx] for k in range(N_CHIPS)], axis=1)
    shards = {n: wts[n].astype(BF16) for n in BIG if n != "w_in"}

    sq, grad_x, big, slots, small = _local_step(x[0], p[0, 0], loss_target[0], known, shards)
    loss = lax.psum(sq * (0.5 / x.shape[-1]), ("x", "y", "c"))

    small_names = SMALL_REPL + SMALL_SHARDED
    small_shapes = [small[n].shape for n in small_names]
    small_rows = _rows_for(sum(small[n].size for n in small_names))
    slots["w_in"], = _exchange([big["w_in"]], True, "scatter_last")
    mine = [_sum_slots(slots[n], "sum_" + n) for n in BIG]
    *theirs, small_slots = _finish_exchange(mine, _pack([small[n] for n in small_names], small_rows))
    small_sum = dict(zip(small_names, _unpack(_sum_slots(small_slots, "sum_small"), small_shapes)))

    outs = {}
    for n, ga, gb in zip(BIG, mine, theirs):
        outs[n] = _adamw(wts[n], ga, gb, mom[n], var[n], "adamw_" + n)
    for n in SMALL_SHARDED:
        width = wts[n].shape[1]
        small_sum[n] = lax.dynamic_slice_in_dim(small_sum[n], chip * width, width, axis=1)
    local_shapes = [wts[n].shape for n in small_names]
    local_rows = _rows_for(sum(wts[n].size for n in small_names))
    packed = _adamw(_pack([wts[n] for n in small_names], local_rows), _pack([small_sum[n] for n in small_names], local_rows),
                    None, _pack([mom[n] for n in small_names], local_rows),
                    _pack([var[n] for n in small_names], local_rows), "adamw_small")
    unpacked = [_unpack(buf, local_shapes) for buf in packed]
    for idx, n in enumerate(small_names):
        outs[n] = [u[idx] for u in unpacked]
    result = [loss, grad_x[None]]
    for part in range(4):
        result += [outs[n][part] if n in SMALL_REPL else outs[n][part][None] for n in WEIGHTS]
    return tuple(result)
```

```python
import functools

import jax
import jax.numpy as jnp
from jax import lax
from jax.experimental import pallas as pl
from jax.experimental.pallas import tpu as pltpu

F32 = jnp.float32
BF16 = jnp.bfloat16
HIGHEST = lax.Precision.HIGHEST
MESH = pl.DeviceIdType.MESH
ANY = pl.BlockSpec(memory_space=pl.ANY)

EPS = 1e-6
HEAD_DIM = 64
N_HEADS = 8
ATTN_W = 512
CONV_W = 512
CONV_K = 31
FFN_K = 3
ATTN_SCALE = 0.125
LANES = 128
CONV_HALO = 32
FFN_HALO = 8
VMEM_LIMIT = 56 * 1024 * 1024

ADAM_LR = 0.001
ADAM_B1 = 0.9
ADAM_B2 = 0.999
ADAM_EPS = 1e-08
ADAM_WD = 0.01
ADAM_STEP = 10

N_CHIPS = 4
N_DEV = 8


def _params(sem):
    return pltpu.CompilerParams(dimension_semantics=sem, vmem_limit_bytes=VMEM_LIMIT)


def _row_tile(s):
    return min(512, s)


def _position():
    x, y, c = lax.axis_index("x"), lax.axis_index("y"), lax.axis_index("c")
    return x, y, c, [(1 - x, y), (x, 1 - y), (1 - x, 1 - y)]


def _remote(src, dst, send_sems, recv_sems, k, to):
    return pltpu.make_async_remote_copy(src_ref=src, dst_ref=dst, send_sem=send_sems.at[k], recv_sem=recv_sems.at[k],
                                        device_id=to, device_id_type=MESH)


def _chip_copies(src_refs, land_refs, send_sems, recv_sems, per_chip, landed):
    x, y, c, chips = _position()
    me = 2 * x + y
    out = []
    for a, (src, land) in enumerate(zip(src_refs, land_refs)):
        for j, (px, py) in enumerate(chips):
            peer = 2 * px + py
            out.append(_remote(src.at[peer] if per_chip else src, land.at[peer if landed else me],
                               send_sems, recv_sems, 3 * a + j, (px, py, c)))
    return out


def _local_copies(src_refs, land_refs, local_sems, per_chip):
    x, y, _, _ = _position()
    me = 2 * x + y
    return [pltpu.make_async_copy(src.at[me] if per_chip else src, land.at[me], local_sems.at[a])
            for a, (src, land) in enumerate(zip(src_refs, land_refs))]


def _exchanged_shapes(srcs, per_chip):
    return [jax.ShapeDtypeStruct(a.shape if per_chip else (N_CHIPS,) + a.shape, a.dtype) for a in srcs]


def _exchange_sems(n):
    return [pltpu.SemaphoreType.DMA((3 * n,)), pltpu.SemaphoreType.DMA((3 * n,)), pltpu.SemaphoreType.DMA((n,))]


def _call(body, *, name, grid, in_specs, out_specs, out_shape, scratch_shapes, args, semantics, carry=None):
    if carry is None:
        return pl.pallas_call(body, name=name, grid=grid, in_specs=in_specs, out_specs=out_specs, out_shape=out_shape,
                              scratch_shapes=scratch_shapes, compiler_params=_params(semantics))(*args)
    srcs, per_chip = carry
    n, n_in, n_out, n_scr = len(srcs), len(in_specs), len(out_specs), len(scratch_shapes)

    def wrapped(*refs):
        ins, xin = refs[:n_in], refs[n_in:n_in + n]
        outs, xout = refs[n_in + n:n_in + n + n_out], refs[n_in + n + n_out:n_in + 2 * n + n_out]
        scratch = refs[n_in + 2 * n + n_out:n_in + 2 * n + n_out + n_scr]
        send_sems, recv_sems, local_sems = refs[-3:]
        first = functools.reduce(jnp.logical_and, [pl.program_id(d) == 0 for d in range(len(grid))])
        last = functools.reduce(jnp.logical_and, [pl.program_id(d) == g - 1 for d, g in enumerate(grid)])

        @pl.when(first)
        def _():
            for cp in _local_copies(xin, xout, local_sems, per_chip):
                cp.start()
            for cp in _chip_copies(xin, xout, send_sems, recv_sems, per_chip, False):
                cp.start()

        body(*ins, *outs, *scratch)

        @pl.when(last)
        def _():
            for cp in _chip_copies(xin, xout, send_sems, recv_sems, per_chip, True):
                cp.wait_recv()
            for cp in _chip_copies(xin, xout, send_sems, recv_sems, per_chip, False):
                cp.wait_send()
            for cp in _local_copies(xin, xout, local_sems, per_chip):
                cp.wait()

    return pl.pallas_call(
        wrapped, name=name, grid=grid, in_specs=list(in_specs) + [ANY] * n, out_specs=list(out_specs) + [ANY] * n,
        out_shape=list(out_shape) + _exchanged_shapes(srcs, per_chip),
        scratch_shapes=list(scratch_shapes) + _exchange_sems(n),
        compiler_params=_params(("arbitrary",) * len(grid)))(*args, *srcs)


def _contract_tile(s):
    return min(1024, s)


def _rstd(x):
    return lax.rsqrt(jnp.mean(x * x, axis=-1, keepdims=True) + EPS)


def _sigmoid(x):
    return 1.0 / (1.0 + jnp.exp(-x))


def _mm(a, b, *, name, dims, grid, a_spec, b_spec, o_spec, o_tile, out_shape, res=None, res_spec=None):
    nk = grid[2]

    def body(*refs):
        if res is None:
            a_ref, b_ref, o_ref, acc_ref = refs
            r_ref = None
        else:
            a_ref, b_ref, r_ref, o_ref, acc_ref = refs
        part = lax.dot_general(a_ref[...], b_ref[...], (dims, ((), ())), preferred_element_type=F32)

        def finish(val):
            if r_ref is not None:
                val = val + r_ref[...]
            o_ref[...] = val.astype(o_ref.dtype)

        if nk == 1:
            finish(part)
        else:
            k = pl.program_id(2)

            @pl.when(k == 0)
            def _():
                acc_ref[...] = part

            @pl.when(k > 0)
            def _():
                acc_ref[...] += part

            @pl.when(k == nk - 1)
            def _():
                finish(acc_ref[...])

    in_specs = [a_spec, b_spec]
    args = [a, b]
    if res is not None:
        in_specs.append(res_spec)
        args.append(res)
    acc_tile = o_tile if nk > 1 else (8, LANES)
    return pl.pallas_call(
        body, name=name, grid=grid, in_specs=in_specs, out_specs=o_spec, out_shape=out_shape,
        scratch_shapes=[pltpu.VMEM(acc_tile, F32)],
        compiler_params=_params(("parallel", "parallel", "arbitrary")))(*args)


NN = ((1,), (0,))
NT = ((1,), (1,))
TN = ((0,), (0,))


def _mm_nn_sharded(a, bg, name, out_dtype=F32, tm=None):
    s, k = a.shape
    g, _, ns = bg.shape
    tm = tm or _row_tile(s)

    def body(a_ref, b_ref, o_ref):
        av = a_ref[...]
        for gi in range(g):
            o_ref[:, gi * ns:(gi + 1) * ns] = jnp.dot(av, b_ref[gi], preferred_element_type=F32).astype(out_dtype)

    return pl.pallas_call(
        body, name=name, grid=(s // tm,),
        in_specs=[pl.BlockSpec((tm, k), lambda i: (i, 0)), pl.BlockSpec((g, k, ns), lambda i: (0, 0, 0))],
        out_specs=pl.BlockSpec((tm, g * ns), lambda i: (i, 0)),
        out_shape=jax.ShapeDtypeStruct((s, g * ns), out_dtype), compiler_params=_params(("parallel",)))(a, bg)


def _mm_nn_full(a, b, name, res=None):
    s, k = a.shape
    n = b.shape[1]
    tm = _row_tile(s)
    rs = pl.BlockSpec((tm, n), lambda i, j, kk: (i, 0))
    return _mm(a, b, name=name, dims=NN, grid=(s // tm, 1, 1),
               a_spec=pl.BlockSpec((tm, k), lambda i, j, kk: (i, 0)),
               b_spec=pl.BlockSpec((k, n), lambda i, j, kk: (0, 0)),
               o_spec=rs, o_tile=(tm, n), out_shape=jax.ShapeDtypeStruct((s, n), F32),
               res=res, res_spec=rs if res is not None else None)


def _mm_nt_full(a, b, name):
    s, n = a.shape
    k = b.shape[0]
    tm = _row_tile(s)
    return _mm(a, b, name=name, dims=NT, grid=(s // tm, 1, 1),
               a_spec=pl.BlockSpec((tm, n), lambda i, j, kk: (i, 0)),
               b_spec=pl.BlockSpec((k, n), lambda i, j, kk: (0, 0)),
               o_spec=pl.BlockSpec((tm, k), lambda i, j, kk: (i, 0)), o_tile=(tm, k),
               out_shape=jax.ShapeDtypeStruct((s, k), F32))


def _mm_nt_sharded(a, bg, name, carry=None):
    s = a.shape[0]
    g, k, ns = bg.shape
    tm = _row_tile(s)

    def body(a_ref, b_ref, o_ref):
        acc = lax.dot_general(a_ref[:, 0:ns], b_ref[0], (NT, ((), ())), preferred_element_type=F32)
        for gi in range(1, g):
            acc = acc + lax.dot_general(a_ref[:, gi * ns:(gi + 1) * ns], b_ref[gi], (NT, ((), ())),
                                        preferred_element_type=F32)
        o_ref[...] = acc

    return _call(
        body, name=name, grid=(s // tm,),
        in_specs=[pl.BlockSpec((tm, g * ns), lambda i: (i, 0)), pl.BlockSpec((g, k, ns), lambda i: (0, 0, 0))],
        out_specs=[pl.BlockSpec((tm, k), lambda i: (i, 0))], out_shape=[jax.ShapeDtypeStruct((s, k), F32)],
        scratch_shapes=[], args=(a, bg), semantics=("parallel",), carry=carry)


def _mm_tn_sharded(a, b, g, name, n_split=1):
    s, k = a.shape
    ns = b.shape[1] // g
    gs = g // n_split
    tk = _contract_tile(s)

    def body(a_ref, b_ref, o_ref):
        first = pl.program_id(1) == 0
        at = a_ref[...].T
        for gi in range(gs):
            part = jnp.dot(at, b_ref[:, gi * ns:(gi + 1) * ns], preferred_element_type=F32)

            @pl.when(first)
            def _(gi=gi, part=part):
                o_ref[gi] = part

            @pl.when(jnp.logical_not(first))
            def _(gi=gi, part=part):
                o_ref[gi] += part

    return pl.pallas_call(
        body, name=name, grid=(n_split, s // tk),
        in_specs=[pl.BlockSpec((tk, k), lambda j, kk: (kk, 0)), pl.BlockSpec((tk, gs * ns), lambda j, kk: (kk, j))],
        out_specs=pl.BlockSpec((gs, k, ns), lambda j, kk: (j, 0, 0)),
        out_shape=jax.ShapeDtypeStruct((g, k, ns), F32), compiler_params=_params(("parallel", "arbitrary")))(a, b)


def _mm_tn_full(a, b, name, tm):
    s, m = a.shape
    n = b.shape[1]
    tk = _contract_tile(s)
    return _mm(a, b, name=name, dims=TN, grid=(m // tm, 1, s // tk),
               a_spec=pl.BlockSpec((tk, tm), lambda i, j, kk: (kk, i)),
               b_spec=pl.BlockSpec((tk, n), lambda i, j, kk: (kk, 0)),
               o_spec=pl.BlockSpec((tm, n), lambda i, j, kk: (i, 0)), o_tile=(tm, n),
               out_shape=jax.ShapeDtypeStruct((m, n), F32))


def _rms_fwd(x, g, name):
    s, d = x.shape
    ts = _row_tile(s)

    def body(x_ref, g_ref, u_ref):
        xv = x_ref[...]
        u_ref[...] = (xv * _rstd(xv) * g_ref[...]).astype(BF16)

    row = pl.BlockSpec((ts, d), lambda i: (i, 0))
    return pl.pallas_call(
        body, name=name, grid=(s // ts,), in_specs=[row, pl.BlockSpec((1, d), lambda i: (0, 0))],
        out_specs=row, out_shape=jax.ShapeDtypeStruct((s, d), BF16),
        compiler_params=_params(("parallel",)))(x, g)


def _rms_bwd(h, du, g, dh_in, name):
    s, d = h.shape
    ts = _row_tile(s)

    def body(h_ref, du_ref, g_ref, dhin_ref, dh_ref, dhb_ref, gg_ref):
        hv = h_ref[...]
        r = _rstd(hv)
        xhat = hv * r
        duv = du_ref[...]
        dxhat = duv * g_ref[...]
        m = jnp.mean(dxhat * xhat, axis=-1, keepdims=True)
        dh = dhin_ref[...] + r * (dxhat - xhat * m)
        dh_ref[...] = dh
        dhb_ref[...] = dh.astype(BF16)
        part = jnp.sum(duv * xhat, axis=0, keepdims=True)

        @pl.when(pl.program_id(0) == 0)
        def _():
            gg_ref[...] = part

        @pl.when(pl.program_id(0) > 0)
        def _():
            gg_ref[...] += part

    row = pl.BlockSpec((ts, d), lambda i: (i, 0))
    vec = pl.BlockSpec((1, d), lambda i: (0, 0))
    return pl.pallas_call(
        body, name=name, grid=(s // ts,), in_specs=[row, row, vec, row], out_specs=[row, row, vec],
        out_shape=[jax.ShapeDtypeStruct((s, d), F32), jax.ShapeDtypeStruct((s, d), BF16),
                   jax.ShapeDtypeStruct((1, d), F32)],
        compiler_params=_params(("arbitrary",)))(h, du, g, dh_in)


def _head_sum(x, bd):
    return _tri_dot(_split(x), bd)


def _qkv_prep(proj, qg, kg, bd):
    s = proj.shape[0]
    ts = _row_tile(s)

    def body(q_ref, k_ref, v_ref, qg_ref, kg_ref, bd_ref, qs_ref, kh_ref, vb_ref):
        def norm(x, gain):
            ms = _head_sum(x * x, bd_ref[...]) * (1.0 / HEAD_DIM)
            return x * lax.rsqrt(ms + EPS) * gain

        qs_ref[...] = (norm(q_ref[...], qg_ref[...]) * ATTN_SCALE).astype(BF16)
        kh_ref[...] = norm(k_ref[...], kg_ref[...]).astype(BF16)
        vb_ref[...] = v_ref[...].astype(BF16)

    col = lambda c: pl.BlockSpec((ts, ATTN_W), lambda i: (i, c))
    vec = pl.BlockSpec((1, ATTN_W), lambda i: (0, 0))
    out = pl.BlockSpec((ts, ATTN_W), lambda i: (i, 0))
    sds = jax.ShapeDtypeStruct((s, ATTN_W), BF16)
    return pl.pallas_call(
        body, name="qkv_prep", grid=(s // ts,),
        in_specs=[col(0), col(1), col(2), vec, vec, pl.BlockSpec((ATTN_W, ATTN_W), lambda i: (0, 0))],
        out_specs=[out, out, out], out_shape=[sds, sds, sds],
        compiler_params=_params(("parallel",)))(proj, proj, proj, qg, kg, bd)


def _qk_bwd(proj, dqh, dkh, dv, qg, kg, bd):
    s = proj.shape[0]
    ts = _row_tile(s)

    def body(q_ref, k_ref, dqh_ref, dkh_ref, dv_ref, qg_ref, kg_ref, bd_ref, out_ref, gq_ref, gk_ref):
        first = pl.program_id(0) == 0

        def bwd(x, dy, gain, gg_ref):
            ms = _head_sum(x * x, bd_ref[...]) * (1.0 / HEAD_DIM)
            r = lax.rsqrt(ms + EPS)
            xhat = x * r
            dxhat = dy * gain
            m = _head_sum(dxhat * xhat, bd_ref[...]) * (1.0 / HEAD_DIM)
            part = jnp.sum(dy * xhat, axis=0, keepdims=True)

            @pl.when(first)
            def _():
                gg_ref[...] = part

            @pl.when(jnp.logical_not(first))
            def _():
                gg_ref[...] += part

            return r * (dxhat - xhat * m)

        out_ref[:, 0:ATTN_W] = bwd(q_ref[...], dqh_ref[...], qg_ref[...], gq_ref).astype(BF16)
        out_ref[:, ATTN_W:2 * ATTN_W] = bwd(k_ref[...], dkh_ref[...], kg_ref[...], gk_ref).astype(BF16)
        out_ref[:, 2 * ATTN_W:3 * ATTN_W] = dv_ref[...].astype(BF16)

    col = lambda c: pl.BlockSpec((ts, ATTN_W), lambda i: (i, c))
    row = pl.BlockSpec((ts, ATTN_W), lambda i: (i, 0))
    vec = pl.BlockSpec((1, ATTN_W), lambda i: (0, 0))
    return pl.pallas_call(
        body, name="qk_bwd", grid=(s // ts,),
        in_specs=[col(0), col(1), row, row, row, vec, vec, pl.BlockSpec((ATTN_W, ATTN_W), lambda i: (0, 0))],
        out_specs=[pl.BlockSpec((ts, 3 * ATTN_W), lambda i: (i, 0)), vec, vec],
        out_shape=[jax.ShapeDtypeStruct((s, 3 * ATTN_W), BF16), jax.ShapeDtypeStruct((1, ATTN_W), F32),
                   jax.ShapeDtypeStruct((1, ATTN_W), F32)],
        compiler_params=_params(("arbitrary",)))(proj, proj, dqh, dkh, dv, qg, kg, bd)


def _split(x):
    hi = x.astype(BF16)
    return hi, (x - hi.astype(F32)).astype(BF16)


def _tri_dot(parts, tri):
    hi, lo = parts
    return jnp.dot(hi, tri, preferred_element_type=F32) + jnp.dot(lo, tri, preferred_element_type=F32)


def _log_sigmoids(z):
    neg_abs = lax.bitcast_convert_type(lax.bitcast_convert_type(z, jnp.uint32) | jnp.uint32(0x80000000), F32)
    lb = jnp.minimum(z, 0.0) - jnp.log(1.0 + jnp.exp(neg_abs))
    return lb, lb - z


DEAD_LOG_WEIGHT = -106.0


def _sweep_key_blocks(tiles, alive, i):
    @pl.when(i == 0)
    def _():
        tiles([0], [True])

    @pl.when(i > 0)
    def _():
        tiles([i, i - 1], [True, False])

    def more(state):
        kb, live = state
        return jnp.logical_and(kb >= 0, live > 0)

    def step(state):
        kb, _ = state
        tiles([kb], [False])
        return kb - 1, alive().astype(jnp.int32)

    lax.while_loop(more, step, (i - 2, alive().astype(jnp.int32)))


def _head_masks():
    lane = lax.broadcasted_iota(jnp.int32, (1, LANES), 1)
    return [lane < HEAD_DIM, lane >= HEAD_DIM]


def _attn_fwd(qs, kh, vb, tri_excl, carry=None):
    s = qs.shape[0]
    t = tri_excl.shape[0]
    nq = s // t

    def body(q_ref, k_ref, v_ref, tri_ref, o_ref, ob_ref, acc_ref, c_ref):
        i = pl.program_id(1)
        hmask = _head_masks()
        q = q_ref[...]
        qm = [jnp.where(hm, q, jnp.zeros_like(q)) for hm in hmask]
        acc_ref[...] = jnp.zeros_like(acc_ref)
        c_ref[...] = jnp.zeros_like(c_ref)
        causal = (lax.broadcasted_iota(jnp.int32, (t, t), 1) < lax.broadcasted_iota(jnp.int32, (t, t), 0))

        def tiles(kbs, masked):
            tri = tri_ref[...]
            starts = [pl.multiple_of(kb * t, t) for kb in kbs]
            kblks = [k_ref[pl.ds(k0, t), :] for k0 in starts]
            vblks = [v_ref[pl.ds(k0, t), :] for k0 in starts]
            chains = [(j, h) for j in range(len(kbs)) for h in range(2)]
            carry = [c_ref[h] for h in range(2)]
            pv = [None, None]
            lbs, loms, between = {}, {}, {}
            for step in range(len(chains) + 2):
                if step < len(chains):
                    j, h = chains[step]
                    z = lax.dot_general(qm[h], kblks[j], (NT, ((), ())), preferred_element_type=F32)
                    lbs[step], lom = _log_sigmoids(z)
                    loms[step] = jnp.where(causal, lom, 0.0) if masked[j] else lom
                if 0 <= step - 1 < len(chains):
                    between[step - 1] = _tri_dot(_split(loms[step - 1]), tri)
                if 0 <= step - 2 < len(chains):
                    n = step - 2
                    j, h = chains[n]
                    w = jnp.exp(lbs[n] + between[n] + carry[h])
                    if masked[j]:
                        w = jnp.where(causal, w, 0.0)
                    carry[h] = carry[h] + jnp.sum(loms[n], axis=-1, keepdims=True)
                    part = jnp.dot(w.astype(BF16), vblks[j], preferred_element_type=F32)
                    pv[h] = part if pv[h] is None else pv[h] + part
            for h in range(2):
                c_ref[h] = carry[h]
                acc_ref[h] += pv[h]

        _sweep_key_blocks(tiles, lambda: jnp.max(c_ref[...]) > DEAD_LOG_WEIGHT, i)
        o = jnp.where(hmask[0], acc_ref[0], acc_ref[1])
        o_ref[...] = o
        ob_ref[...] = o.astype(BF16)

    qspec = pl.BlockSpec((t, LANES), lambda hp, i: (i, hp))
    kspec = pl.BlockSpec((s, LANES), lambda hp, i: (0, hp))
    return _call(
        body, name="attn_fwd", grid=(ATTN_W // LANES, nq),
        in_specs=[qspec, kspec, kspec, pl.BlockSpec((t, t), lambda hp, i: (0, 0))],
        out_specs=[qspec, qspec],
        out_shape=[jax.ShapeDtypeStruct((s, ATTN_W), F32), jax.ShapeDtypeStruct((s, ATTN_W), BF16)],
        scratch_shapes=[pltpu.VMEM((2, t, LANES), F32), pltpu.VMEM((2, t, 1), F32)],
        args=(qs, kh, vb, tri_excl), semantics=("parallel", "arbitrary"), carry=carry)


def _attn_bwd(qs, kh, vb, o, dmix, tri_excl, tri_incl, carry=None):
    s = qs.shape[0]
    t = tri_excl.shape[0]
    nq = s // t

    def body(q_ref, k_ref, v_ref, o_ref, do_ref, te_ref, ti_ref, dq_ref, dk_ref, dv_ref, dqacc_ref, c_ref):
        i = pl.program_id(1)

        @pl.when(i == 0)
        def _():
            dk_ref[...] = jnp.zeros_like(dk_ref)
            dv_ref[...] = jnp.zeros_like(dv_ref)

        hmask = _head_masks()
        q = q_ref[...]
        do = do_ref[...]
        dob = do.astype(BF16)
        prod = dob.astype(F32) * o_ref[...]
        qm =[jnp.where(hm, q, jnp.zeros_like(q)) for hm in hmask]
        dom = [jnp.where(hm, dob, jnp.zeros_like(dob)) for hm in hmask]
        total = [jnp.sum(jnp.where(hm, prod, 0.0), axis=-1, keepdims=True) for hm in hmask]
        dqacc_ref[...] = jnp.zeros_like(dqacc_ref)
        c_ref[...] = jnp.zeros_like(c_ref)
        causal = (lax.broadcasted_iota(jnp.int32, (t, t), 1) < lax.broadcasted_iota(jnp.int32, (t, t), 0))

        def tiles(kbs, masked):
            te = te_ref[...]
            ti = ti_ref[...]
            starts = [pl.multiple_of(kb * t, t) for kb in kbs]
            kblks = [k_ref[pl.ds(k0, t), :] for k0 in starts]
            vblks = [v_ref[pl.ds(k0, t), :] for k0 in starts]
            chains = [(j, h) for j in range(len(kbs)) for h in range(2)]
            c_lom = [c_ref[2 * h] for h in range(2)]
            c_g = [c_ref[2 * h + 1] for h in range(2)]
            lbs, loms, dws, between, wbs, gs, g_after = {}, {}, {}, {}, {}, {}, {}
            dq = [None, None]
            dk = [None] * len(kbs)
            dv = [None] * len(kbs)
            add = lambda acc, part: part if acc is None else acc + part
            for step in range(len(chains) + 3):
                if step < len(chains):
                    j, h = chains[step]
                    z = lax.dot_general(qm[h], kblks[j], (NT, ((), ())), preferred_element_type=F32)
                    dws[step] = lax.dot_general(dom[h], vblks[j], (NT, ((), ())), preferred_element_type=F32)
                    lbs[step], lom = _log_sigmoids(z)
                    loms[step] = jnp.where(causal, lom, 0.0) if masked[j] else lom
                if 0 <= step - 1 < len(chains):
                    between[step - 1] = _tri_dot(_split(loms[step - 1]), te)
                if 0 <= step - 2 < len(chains):
                    n = step - 2
                    j, h = chains[n]
                    w = jnp.exp(lbs[n] + between[n] + c_lom[h])
                    if masked[j]:
                        w = jnp.where(causal, w, 0.0)
                    c_lom[h] = c_lom[h] + jnp.sum(loms[n], axis=-1, keepdims=True)
                    wbs[n] = w.astype(BF16)
                    gs[n] = dws[n] * wbs[n].astype(F32)
                    g_after[n] = _tri_dot(_split(gs[n]), ti)
                if 0 <= step - 3 < len(chains):
                    n = step - 3
                    j, h = chains[n]
                    beta = jnp.exp(lbs[n])
                    dz = gs[n] * (1.0 - beta) - beta * (total[h] - (g_after[n] + c_g[h]))
                    if masked[j]:
                        dz = jnp.where(causal, dz, 0.0)
                    c_g[h] = c_g[h] + jnp.sum(gs[n], axis=-1, keepdims=True)
                    dzb = dz.astype(BF16)
                    dq[h] = add(dq[h], jnp.dot(dzb, kblks[j], preferred_element_type=F32))
                    dk[j] = add(dk[j], lax.dot_general(dzb, qm[h], (TN, ((), ())), preferred_element_type=F32))
                    dv[j] = add(dv[j], lax.dot_general(wbs[n], dom[h], (TN, ((), ())), preferred_element_type=F32))
            for h in range(2):
                c_ref[2 * h] = c_lom[h]
                c_ref[2 * h + 1] = c_g[h]
                dqacc_ref[h] += dq[h]
            for j, k0 in enumerate(starts):
                dk_ref[pl.ds(k0, t), :] += dk[j]
                dv_ref[pl.ds(k0, t), :] += dv[j]

        _sweep_key_blocks(tiles, lambda: jnp.maximum(jnp.max(c_ref[0]), jnp.max(c_ref[2])) > DEAD_LOG_WEIGHT, i)
        dq_ref[...] = jnp.where(hmask[0], dqacc_ref[0], dqacc_ref[1]) * ATTN_SCALE

    qspec = pl.BlockSpec((t, LANES), lambda hp, i: (i, hp))
    kspec = pl.BlockSpec((s, LANES), lambda hp, i: (0, hp))
    tspec = pl.BlockSpec((t, t), lambda hp, i: (0, 0))
    sds = jax.ShapeDtypeStruct((s, ATTN_W), F32)
    return _call(
        body, name="attn_bwd", grid=(ATTN_W // LANES, nq),
        in_specs=[qspec, kspec, kspec, qspec, qspec, tspec, tspec],
        out_specs=[qspec, kspec, kspec], out_shape=[sds, sds, sds],
        scratch_shapes=[pltpu.VMEM((2, t, LANES), F32), pltpu.VMEM((4, t, 1), F32)],
        args=(qs, kh, vb, o, dmix, tri_excl, tri_incl), semantics=("parallel", "arbitrary"), carry=carry)


CONV_ROWS = 64
CONV_COLS = 256


SUBLANES = 8


def _shift_copies(src_ref, sh_ref):
    length = sh_ref.shape[1]
    for r in range(1, SUBLANES):
        sh_ref[r - 1] = src_ref[r:r + length, :]


def _shift_scratch(ts):
    return pltpu.VMEM((SUBLANES - 1, ts + CONV_HALO - SUBLANES, CONV_W), F32)


def _rows_at(src_ref, sh_ref, offset, r0, rows, cols):
    r = offset % SUBLANES
    base = offset - r + r0
    return src_ref[base:base + rows, cols] if r == 0 else sh_ref[r - 1, base:base + rows, cols]


def _taps(src_ref, sh_ref, w_ref, n_taps, first_row, rows, reverse=False):
    width = src_ref.shape[1]
    cols = min(CONV_COLS, width)
    out = []
    for r0 in range(0, rows, CONV_ROWS):
        for c0 in range(0, width, cols):
            acc = jnp.zeros((CONV_ROWS, cols), F32)
            for k in range(n_taps):
                off = (n_taps - 1 - k) if reverse else k
                acc = acc + w_ref[k:k + 1, c0:c0 + cols] * _rows_at(src_ref, sh_ref, first_row + off, r0, CONV_ROWS,
                                                                    slice(c0, c0 + cols))
            out.append(((r0, c0), acc))
    return out


def _conv_fwd(proj, dw_w, dw_b, ln_g, ln_b, carry=None):
    s = proj.shape[0]
    ts = _row_tile(s)
    hb = ts // CONV_HALO

    def body(a_ref, g_ref, ha_ref, hg_ref, w_ref, b_ref, lg_ref, lb_ref, c1_ref, c3_ref, pad_ref, sh_ref):
        i = pl.program_id(0)
        halo = ha_ref[...] * _sigmoid(hg_ref[...])
        pad_ref[0:CONV_HALO, :] = jnp.where(i > 0, halo, 0.0)
        pad_ref[CONV_HALO:, :] = a_ref[...] * _sigmoid(g_ref[...])
        _shift_copies(pad_ref, sh_ref)
        first = CONV_HALO - (CONV_K - 1)
        for (r0, c0), acc in _taps(pad_ref, sh_ref, w_ref, CONV_K, first, ts):
            c1_ref[r0:r0 + CONV_ROWS, c0:c0 + acc.shape[1]] = acc + b_ref[:, c0:c0 + acc.shape[1]]
        c1 = c1_ref[...]
        xc = c1 - jnp.mean(c1, axis=-1, keepdims=True)
        c2 = xc * _rstd(xc) * lg_ref[...] + lb_ref[...]
        c3_ref[...] = (c2 * _sigmoid(c2)).astype(BF16)

    cur = lambda c: pl.BlockSpec((ts, CONV_W), lambda i: (i, c))
    halo = lambda c: pl.BlockSpec((CONV_HALO, CONV_W), lambda i: (jnp.maximum(i * hb - 1, 0), c))
    vec = pl.BlockSpec((1, CONV_W), lambda i: (0, 0))
    row = pl.BlockSpec((ts, CONV_W), lambda i: (i, 0))
    return _call(
        body, name="conv_fwd", grid=(s // ts,),
        in_specs=[cur(3), cur(4), halo(3), halo(4), pl.BlockSpec((CONV_HALO, CONV_W), lambda i: (0, 0)), vec, vec, vec],
        out_specs=[row, row],
        out_shape=[jax.ShapeDtypeStruct((s, CONV_W), F32), jax.ShapeDtypeStruct((s, CONV_W), BF16)],
        scratch_shapes=[pltpu.VMEM((ts + CONV_HALO, CONV_W), F32), _shift_scratch(ts)],
        args=(proj, proj, proj, proj, dw_w, dw_b, ln_g, ln_b), semantics=("parallel",), carry=carry)


def _conv_bwd_ln(dmix, c1, ln_g, ln_b):
    s = c1.shape[0]
    ts = _row_tile(s)

    def body(d_ref, c1_ref, lg_ref, lb_ref, dc1_ref, glg_ref, glb_ref, gb_ref):
        c1v = c1_ref[...]
        xc = c1v - jnp.mean(c1v, axis=-1, keepdims=True)
        r = _rstd(xc)
        xhat = xc * r
        c2 = xhat * lg_ref[...] + lb_ref[...]
        sg = _sigmoid(c2)
        dc2 = d_ref[...] * (sg * (1.0 + c2 * (1.0 - sg)))
        dxhat = dc2 * lg_ref[...]
        dc1 = r * (dxhat - jnp.mean(dxhat, axis=-1, keepdims=True)
                   - xhat * jnp.mean(dxhat * xhat, axis=-1, keepdims=True))
        dc1_ref[...] = dc1
        parts = [(glg_ref, jnp.sum(dc2 * xhat, axis=0, keepdims=True)),
                 (glb_ref, jnp.sum(dc2, axis=0, keepdims=True)),
                 (gb_ref, jnp.sum(dc1, axis=0, keepdims=True))]

        @pl.when(pl.program_id(0) == 0)
        def _():
            for ref, part in parts:
                ref[...] = part

        @pl.when(pl.program_id(0) > 0)
        def _():
            for ref, part in parts:
                ref[...] += part

    row = pl.BlockSpec((ts, CONV_W), lambda i: (i, 0))
    vec = pl.BlockSpec((1, CONV_W), lambda i: (0, 0))
    vsd = jax.ShapeDtypeStruct((1, CONV_W), F32)
    return pl.pallas_call(
        body, name="conv_bwd_ln", grid=(s // ts,),
        in_specs=[pl.BlockSpec((ts, CONV_W), lambda i: (i, 1)), row, vec, vec],
        out_specs=[row, vec, vec, vec],
        out_shape=[jax.ShapeDtypeStruct((s, CONV_W), F32), vsd, vsd, vsd],
        compiler_params=_params(("arbitrary",)))(dmix, c1, ln_g, ln_b)


def _conv_bwd_taps(proj, dc1, dw_w):
    s = proj.shape[0]
    ts = _row_tile(s)
    hb = ts // CONV_HALO
    last = s // CONV_HALO - 1
    nsteps = s // ts

    def body(a_ref, g_ref, ha_ref, hg_ref, d_ref, hd_ref, w_ref, out_ref, gw_ref, pad_ref, dpad_ref, sh_ref):
        i = pl.program_id(0)
        av = a_ref[...]
        sg = _sigmoid(g_ref[...])
        halo = ha_ref[...] * _sigmoid(hg_ref[...])
        pad_ref[0:CONV_HALO, :] = jnp.where(i > 0, halo, 0.0)
        pad_ref[CONV_HALO:, :] = av * sg
        dpad_ref[0:ts, :] = d_ref[...]
        dpad_ref[ts:, :] = jnp.where(i < nsteps - 1, hd_ref[...], 0.0)

        @pl.when(i == 0)
        def _():
            gw_ref[...] = jnp.zeros_like(gw_ref)

        _shift_copies(pad_ref, sh_ref)
        first = CONV_HALO - (CONV_K - 1)
        for k in range(CONV_K):
            shifted = _rows_at(pad_ref, sh_ref, first + k, 0, ts, slice(None))
            gw_ref[k:k + 1, :] += jnp.sum(d_ref[...] * shifted, axis=0, keepdims=True)
        _shift_copies(dpad_ref, sh_ref)
        for (r0, c0), dc0 in _taps(dpad_ref, sh_ref, w_ref, CONV_K, 0, ts, reverse=True):
            cs = slice(c0, c0 + dc0.shape[1])
            a_c = a_ref[r0:r0 + CONV_ROWS, cs]
            sg_c = _sigmoid(g_ref[r0:r0 + CONV_ROWS, cs])
            out_ref[r0:r0 + CONV_ROWS, cs] = (dc0 * sg_c).astype(BF16)
            out_ref[r0:r0 + CONV_ROWS, CONV_W + c0:CONV_W + c0 + dc0.shape[1]] = (
                dc0 * a_c * sg_c * (1.0 - sg_c)).astype(BF16)

    cur = lambda c: pl.BlockSpec((ts, CONV_W), lambda i: (i, c))
    halo = lambda c: pl.BlockSpec((CONV_HALO, CONV_W), lambda i: (jnp.maximum(i * hb - 1, 0), c))
    row = pl.BlockSpec((ts, CONV_W), lambda i: (i, 0))
    nxt = pl.BlockSpec((CONV_HALO, CONV_W), lambda i: (jnp.minimum((i + 1) * hb, last), 0))
    wspec = pl.BlockSpec((CONV_HALO, CONV_W), lambda i: (0, 0))
    return pl.pallas_call(
        body, name="conv_bwd_taps", grid=(nsteps,),
        in_specs=[cur(3), cur(4), halo(3), halo(4), row, nxt, wspec],
        out_specs=[pl.BlockSpec((ts, 2 * CONV_W), lambda i: (i, 0)), wspec],
        out_shape=[jax.ShapeDtypeStruct((s, 2 * CONV_W), BF16), jax.ShapeDtypeStruct((CONV_HALO, CONV_W), F32)],
        scratch_shapes=[pltpu.VMEM((ts + CONV_HALO, CONV_W), F32), pltpu.VMEM((ts + CONV_HALO, CONV_W), F32),
                        _shift_scratch(ts)],
        compiler_params=_params(("arbitrary",)))(proj, proj, proj, proj, dc1, dc1, dw_w)


SQRT_HALF = 0.7071067811865476
INV_SQRT_2PI = 0.3989422804014327


def _gelu_parts(x):
    cdf = 0.5 * (1.0 + lax.erf(x * SQRT_HALF))
    return x * cdf, cdf + x * (INV_SQRT_2PI * jnp.exp(-0.5 * x * x))


def _ffn_tile(dff):
    return dff // 2


FFN_ROWS = 64
FFN_COLS = LANES


def _ffn_chunks(ts, tc):
    return [(r0, slice(c0, c0 + FFN_COLS)) for c0 in range(0, tc, FFN_COLS) for r0 in range(0, ts, FFN_ROWS)]


def _ffn_gate2(pad_ref, w_ref, b_ref, r0, cs):
    first = FFN_HALO - (FFN_K - 1) + r0
    g2 = b_ref[:, cs] + w_ref[0:1, cs] * pad_ref[first:first + FFN_ROWS, cs]
    for k in range(1, FFN_K):
        g2 = g2 + w_ref[k:k + 1, cs] * pad_ref[first + k:first + k + FFN_ROWS, cs]
    return g2


def _ffn_act(up, fw, fb):
    s = up.shape[0]
    dff = up.shape[1] // 2
    tc = _ffn_tile(dff)
    nj = dff // tc
    ts = _row_tile(s) // 2
    hb = ts // FFN_HALO

    def body(g_ref, v_ref, hg_ref, w_ref, b_ref, act_ref, pad_ref):
        i = pl.program_id(0)
        pad_ref[0:FFN_HALO, :] = jnp.where(i > 0, hg_ref[...], 0.0)
        pad_ref[FFN_HALO:, :] = g_ref[...]
        for r0, cs in _ffn_chunks(ts, tc):
            gelu, _ = _gelu_parts(_ffn_gate2(pad_ref, w_ref, b_ref, r0, cs))
            act_ref[r0:r0 + FFN_ROWS, cs] = (gelu * v_ref[r0:r0 + FFN_ROWS, cs]).astype(BF16)

    return pl.pallas_call(
        body, name="ffn_act", grid=(s // ts, nj),
        in_specs=[pl.BlockSpec((ts, tc), lambda i, j: (i, j)), pl.BlockSpec((ts, tc), lambda i, j: (i, j + nj)),
                  pl.BlockSpec((FFN_HALO, tc), lambda i, j: (jnp.maximum(i * hb - 1, 0), j)),
                  pl.BlockSpec((FFN_HALO, tc), lambda i, j: (0, j)), pl.BlockSpec((1, tc), lambda i, j: (0, j))],
        out_specs=pl.BlockSpec((ts, tc), lambda i, j: (i, j)),
        out_shape=jax.ShapeDtypeStruct((s, dff), BF16),
        scratch_shapes=[pltpu.VMEM((ts + FFN_HALO, tc), F32)],
        compiler_params=_params(("parallel", "parallel")))(up, up, up, fw, fb)


def _ffn_bwd_act(dact, up, fw, fb, carry=None):
    s = up.shape[0]
    dff = up.shape[1] // 2
    tc = _ffn_tile(dff)
    nj = dff // tc
    ts = _row_tile(s) // 2
    hb = ts // FFN_HALO

    def body(d_ref, g_ref, v_ref, hg_ref, w_ref, b_ref, dg2_ref, dval_ref, gw_ref, gb_ref, pad_ref):
        i = pl.program_id(1)
        pad_ref[0:FFN_HALO, :] = jnp.where(i > 0, hg_ref[...], 0.0)
        pad_ref[FFN_HALO:, :] = g_ref[...]

        @pl.when(i == 0)
        def _():
            gw_ref[...] = jnp.zeros_like(gw_ref)
            gb_ref[...] = jnp.zeros_like(gb_ref)

        fold = lambda v: jnp.sum(v.reshape(FFN_ROWS // SUBLANES, SUBLANES, FFN_COLS), axis=0)
        first = FFN_HALO - (FFN_K - 1)
        sums = {}
        for r0, cs in _ffn_chunks(ts, tc):
            rows = slice(r0, r0 + FFN_ROWS)
            gelu, dgelu = _gelu_parts(_ffn_gate2(pad_ref, w_ref, b_ref, r0, cs))
            dactv = d_ref[rows, cs]
            dval_ref[rows, cs] = (dactv * gelu).astype(BF16)
            dg2 = dactv * v_ref[rows, cs] * dgelu
            dg2_ref[rows, cs] = dg2
            parts = [fold(dg2)] + [fold(dg2 * pad_ref[first + k + r0:first + k + r0 + FFN_ROWS, cs])
                                   for k in range(FFN_K)]
            sums = {n: part + sums[n] if r0 else part for n, part in enumerate(parts)}
            if r0 + FFN_ROWS == ts:
                gb_ref[:, cs] += jnp.sum(sums[0], axis=0, keepdims=True)
                for k in range(FFN_K):
                    gw_ref[k:k + 1, cs] += jnp.sum(sums[1 + k], axis=0, keepdims=True)

    blk = pl.BlockSpec((ts, tc), lambda j, i: (i, j))
    wspec = pl.BlockSpec((FFN_HALO, tc), lambda j, i: (0, j))
    bspec = pl.BlockSpec((1, tc), lambda j, i: (0, j))
    return _call(
        body, name="ffn_bwd_act", grid=(nj, s // ts),
        in_specs=[blk, blk, pl.BlockSpec((ts, tc), lambda j, i: (i, j + nj)),
                  pl.BlockSpec((FFN_HALO, tc), lambda j, i: (jnp.maximum(i * hb - 1, 0), j)), wspec, bspec],
        out_specs=[blk, pl.BlockSpec((ts, tc), lambda j, i: (i, j + nj)), wspec, bspec],
        out_shape=[jax.ShapeDtypeStruct((s, dff), F32), jax.ShapeDtypeStruct((s, 2 * dff), BF16),
                   jax.ShapeDtypeStruct((FFN_HALO, dff), F32), jax.ShapeDtypeStruct((1, dff), F32)],
        scratch_shapes=[pltpu.VMEM((ts + FFN_HALO, tc), F32)],
        args=(dact, up, up, up, fw, fb), semantics=("parallel", "arbitrary"), carry=carry)


def _ffn_bwd_conv(dg2, fw, dup):
    s, dff = dg2.shape
    tc = _ffn_tile(dff)
    ts = _row_tile(s) // 2
    hb = ts // FFN_HALO
    last = s // FFN_HALO - 1
    nsteps = s // ts

    def body(d_ref, hd_ref, w_ref, dup_ref, out_ref, pad_ref):
        i = pl.program_id(0)
        pad_ref[0:ts, :] = d_ref[...]
        pad_ref[ts:, :] = jnp.where(i < nsteps - 1, hd_ref[...], 0.0)
        for r0, cs in _ffn_chunks(ts, tc):
            dg = w_ref[0:1, cs] * pad_ref[r0 + FFN_K - 1:r0 + FFN_K - 1 + FFN_ROWS, cs]
            for k in range(1, FFN_K):
                dg = dg + w_ref[k:k + 1, cs] * pad_ref[r0 + FFN_K - 1 - k:r0 + FFN_K - 1 - k + FFN_ROWS, cs]
            out_ref[r0:r0 + FFN_ROWS, cs] = dg.astype(BF16)

    blk = pl.BlockSpec((ts, tc), lambda i, j: (i, j))
    return pl.pallas_call(
        body, name="ffn_bwd_conv", grid=(nsteps, dff // tc),
        in_specs=[blk, pl.BlockSpec((FFN_HALO, tc), lambda i, j: (jnp.minimum((i + 1) * hb, last), j)),
                  pl.BlockSpec((FFN_HALO, tc), lambda i, j: (0, j)), ANY],
        out_specs=blk, out_shape=jax.ShapeDtypeStruct(dup.shape, BF16), input_output_aliases={3: 0},
        scratch_shapes=[pltpu.VMEM((ts + FFN_HALO, tc), F32)],
        compiler_params=_params(("parallel", "parallel")))(dg2, dg2, fw, dup)


def _ple_loss(h2, zg, pp, target):
    s, d = h2.shape
    ts = _row_tile(s)

    def body(h_ref, z_ref, p_ref, t_ref, dh_ref, dpp_ref, dz_ref, loss_ref):
        pg = _sigmoid(z_ref[...])
        ppv = p_ref[...]
        diff = h_ref[...] + pg * ppv - t_ref[...]
        dh = diff * (1.0 / d)
        dh_ref[...] = dh
        dpp_ref[...] = (dh * pg).astype(BF16)
        dz_ref[...] = (dh * ppv * pg * (1.0 - pg)).astype(BF16)
        part = jnp.sum(jnp.sum(diff * diff, axis=0, keepdims=True), axis=1, keepdims=True)

        @pl.when(pl.program_id(0) == 0)
        def _():
            loss_ref[...] = jnp.zeros_like(loss_ref)

        loss_ref[...] += jnp.broadcast_to(part, loss_ref.shape)

    row = pl.BlockSpec((ts, d), lambda i: (i, 0))
    return pl.pallas_call(
        body, name="ple_loss", grid=(s // ts,), in_specs=[row, row, row, row],
        out_specs=[row, row, row, pl.BlockSpec((8, LANES), lambda i: (0, 0))],
        out_shape=[jax.ShapeDtypeStruct((s, d), F32), jax.ShapeDtypeStruct((s, d), BF16),
                   jax.ShapeDtypeStruct((s, d), BF16), jax.ShapeDtypeStruct((8, LANES), F32)],
        compiler_params=_params(("arbitrary",)))(h2, zg, pp, target)


def _local_step(x, p, target, w, shards=None):
    riding = shards is not None
    major = lambda n, g: g if n in COL_SHARDED else g.reshape(N_CHIPS, -1, g.shape[-1])
    full = lambda n, g: g if n in COL_SHARDED else g.reshape(-1, g.shape[-1])

    def ride(names, arrays):
        return ([arrays[n] for n in names], False) if riding else None

    def land(names, got):
        return {n: full(n, g) for n, g in zip(names, got)}
    s = x.shape[0]
    t = min(256, s)
    tri = jnp.tril(jnp.ones((t, t), F32))
    tri_incl = tri.astype(BF16)
    tri_excl = jnp.tril(jnp.ones((t, t), F32), -1).astype(BF16)
    bd = jnp.kron(jnp.eye(N_HEADS, dtype=F32), jnp.ones((HEAD_DIM, HEAD_DIM), F32)).astype(BF16)
    qg = jnp.tile(w["q_gain"], (1, N_HEADS))
    kg = jnp.tile(w["k_gain"], (1, N_HEADS))
    pb = p.astype(BF16)

    u1 = _rms_fwd(x, w["g_mix"], "rms_mix")
    dw_w = jnp.pad(w["dw_w"], ((0, CONV_HALO - CONV_K), (0, 0)))
    fw = jnp.pad(w["ffn_conv_w"], ((0, FFN_HALO - FFN_K), (0, 0)))
    proj = _mm_nn_sharded(u1, w["w_in"], "mm_in")
    qs, kh, vb = _qkv_prep(proj, qg, kg, bd)
    with_attn, with_conv = ["w_out", "w_up"], ["w_down", "w_ple_gate", "w_ple_proj"]
    o, ob, *got = _attn_fwd(qs, kh, vb, tri_excl, carry=ride(with_attn, shards))
    w = {**w, **land(with_attn, got)}
    c1, c3, *got = _conv_fwd(proj, dw_w, w["dw_b"], w["conv_ln_g"], w["conv_ln_b"], carry=ride(with_conv, shards))
    w = {**w, **land(with_conv, got)}
    mix = jnp.concatenate([ob, c3], axis=1)
    h1 = _mm_nn_full(mix, w["w_out"], "mm_out", res=x)
    u2 = _rms_fwd(h1, w["g_ffn"], "rms_ffn")
    up = _mm_nn_sharded(u2, w["w_up"], "mm_up", tm=_row_tile(s) // 2)
    act = _ffn_act(up, fw, w["ffn_conv_b"])
    h2 = _mm_nn_full(act, w["w_down"], "mm_down", res=h1)
    u3 = _rms_fwd(h2, w["g_ple"], "rms_ple")
    zg = _mm_nn_full(u3, w["w_ple_gate"], "mm_ple_gate")
    pp = _mm_nn_sharded(pb, w["w_ple_proj"], "mm_ple_proj")
    dh3, dpp, dz, sq = _ple_loss(h2, zg, pp, target)

    big = {}
    small = {}
    big["w_ple_proj"] = _mm_tn_sharded(pb, dpp, N_CHIPS, "mm_g_ple_proj")
    big["w_ple_gate"] = _mm_tn_full(u3, dz, "mm_g_ple_gate", tm=u3.shape[1])
    du3 = _mm_nt_full(dz, w["w_ple_gate"], "mm_d_ple_gate")
    dh2, dh2b, small["g_ple"] = _rms_bwd(h2, du3, w["g_ple"], dh3, "rms_ple_bwd")
    big["w_down"] = _mm_tn_full(act, dh2b, "mm_g_down", tm=act.shape[1] // 2)
    dact = _mm_nt_full(dh2b, w["w_down"], "mm_d_down")
    slots = {}
    with_ffn, with_attn = ["w_ple_proj", "w_ple_gate", "w_down"], ["w_up", "w_out"]
    leaving = ([major(n, big.pop(n)) for n in with_ffn], True) if riding else None
    dg2, dup, gfw, small["ffn_conv_b"], *got = _ffn_bwd_act(dact, up, fw, w["ffn_conv_b"], carry=leaving)
    slots.update(zip(with_ffn, got))
    small["ffn_conv_w"] = gfw[:FFN_K]
    dup = _ffn_bwd_conv(dg2, fw, dup)
    big["w_up"] = _mm_tn_sharded(u2, dup, N_CHIPS, "mm_g_up", n_split=2)
    du2, = _mm_nt_sharded(dup, w["w_up"], "mm_d_up")
    dh1, dh1b, small["g_ffn"] = _rms_bwd(h1, du2, w["g_ffn"], dh2, "rms_ffn_bwd")
    big["w_out"] = _mm_tn_full(mix, dh1b, "mm_g_out", tm=mix.shape[1])
    dmix = _mm_nt_full(dh1b, w["w_out"], "mm_d_out")
    dc1, small["conv_ln_g"], small["conv_ln_b"], small["dw_b"] = _conv_bwd_ln(dmix, c1, w["conv_ln_g"], w["conv_ln_b"])
    dcacg, gdw = _conv_bwd_taps(proj, dc1, dw_w)
    small["dw_w"] = gdw[:CONV_K]
    leaving = ([major(n, big.pop(n)) for n in with_attn], True) if riding else None
    dqh, dkh, dv, *got = _attn_bwd(qs, kh, vb, o, dmix, tri_excl, tri_incl, carry=leaving)
    slots.update(zip(with_attn, got))
    dqkv, gq, gk = _qk_bwd(proj, dqh, dkh, dv, qg, kg, bd)
    small["q_gain"] = gq.reshape(N_HEADS, HEAD_DIM).sum(axis=0, keepdims=True)
    small["k_gain"] = gk.reshape(N_HEADS, HEAD_DIM).sum(axis=0, keepdims=True)
    dproj = jnp.concatenate([dqkv, dcacg], axis=1)
    big["w_in"] = _mm_tn_sharded(u1, dproj, N_CHIPS, "mm_g_in")
    leaving = ([big.pop("w_in")], True) if riding else None
    du1, *got = _mm_nt_sharded(dproj, w["w_in"], "mm_d_in", carry=leaving)
    slots.update(zip(["w_in"], got))
    grad_x, _, small["g_mix"] = _rms_bwd(x, du1, w["g_mix"], dh1, "rms_mix_bwd")
    return sq[0, 0], grad_x, big, slots, small


def _exchange(srcs, per_chip, name):
    n = len(srcs)

    def body(*refs):
        src_refs, land_refs = refs[:n], refs[n:2 * n]
        send_sems, recv_sems, local_sems = refs[2 * n:]
        local = _local_copies(src_refs, land_refs, local_sems, per_chip)
        sends = _chip_copies(src_refs, land_refs, send_sems, recv_sems, per_chip, False)
        for cp in local + sends:
            cp.start()
        for cp in _chip_copies(src_refs, land_refs, send_sems, recv_sems, per_chip, True):
            cp.wait_recv()
        for cp in sends:
            cp.wait_send()
        for cp in local:
            cp.wait()

    return pl.pallas_call(
        body, name=name, in_specs=[ANY] * n, out_specs=[ANY] * n, out_shape=_exchanged_shapes(srcs, per_chip),
        scratch_shapes=_exchange_sems(n))(*srcs)


def _finish_exchange(mine, small):
    n = len(mine)

    def body(*refs):
        gin, sin = refs[:n], refs[n]
        gout, sout = refs[n + 1:2 * n + 1], refs[2 * n + 1]
        send_sems, recv_sems, small_send, small_recv, local_sem = refs[2 * n + 2:]
        x, y, c, _ = _position()
        dev = 4 * x + 2 * y + c
        flip = lambda v, bit: 1 - v if bit else v
        others = [(flip(x, k & 4), flip(y, k & 2), flip(c, k & 1)) for k in range(1, N_DEV)]
        local = pltpu.make_async_copy(sin, sout.at[dev], local_sem)
        local.start()
        swaps = [_remote(gin[a], gout[a], send_sems, recv_sems, a, (x, y, 1 - c)) for a in range(n)]
        sends = swaps + [_remote(sin, sout.at[dev], small_send, small_recv, k, peer) for k, peer in enumerate(others)]
        for cp in sends:
            cp.start()
        for cp in swaps:
            cp.wait_recv()
        for k, (px, py, pc) in enumerate(others):
            _remote(sin, sout.at[4 * px + 2 * py + pc], small_send, small_recv, k, (px, py, pc)).wait_recv()
        for cp in sends:
            cp.wait_send()
        local.wait()

    return pl.pallas_call(
        body, name="finish_exchange", in_specs=[ANY] * (n + 1), out_specs=[ANY] * (n + 1),
        out_shape=[jax.ShapeDtypeStruct(g.shape, g.dtype) for g in mine]
        + [jax.ShapeDtypeStruct((N_DEV,) + small.shape, small.dtype)],
        scratch_shapes=[pltpu.SemaphoreType.DMA((n,)), pltpu.SemaphoreType.DMA((n,)),
                        pltpu.SemaphoreType.DMA((N_DEV - 1,)), pltpu.SemaphoreType.DMA((N_DEV - 1,)),
                        pltpu.SemaphoreType.DMA])(*mine, small)


def _elem_tile(rows):
    return 128 if rows % 128 == 0 else (64 if rows % 64 == 0 else rows)


def _sum_slots(a, name):
    g, r, c = a.shape
    tr = _elem_tile(r)

    def body(a_ref, o_ref):
        acc = a_ref[0]
        for k in range(1, g):
            acc = acc + a_ref[k]
        o_ref[...] = acc

    return pl.pallas_call(
        body, name=name, grid=(r // tr,), in_specs=[pl.BlockSpec((g, tr, c), lambda i: (0, i, 0))],
        out_specs=pl.BlockSpec((tr, c), lambda i: (i, 0)), out_shape=jax.ShapeDtypeStruct((r, c), a.dtype),
        compiler_params=_params(("parallel",)))(a)


def _adamw(wt, ga, gb, m, v, name):
    r, c = wt.shape
    tr = _elem_tile(r)
    two = gb is not None

    def body(*refs):
        if two:
            w_ref, ga_ref, gb_ref, m_ref, v_ref, g_out, d_out, m_out, v_out = refs
            g = ga_ref[...] + gb_ref[...]
        else:
            w_ref, ga_ref, m_ref, v_ref, g_out, d_out, m_out, v_out = refs
            g = ga_ref[...]
        mn = ADAM_B1 * m_ref[...] + (1.0 - ADAM_B1) * g
        vn = ADAM_B2 * v_ref[...] + (1.0 - ADAM_B2) * (g * g)
        m_hat = mn / (1.0 - ADAM_B1 ** ADAM_STEP)
        v_hat = vn / (1.0 - ADAM_B2 ** ADAM_STEP)
        g_out[...] = g
        d_out[...] = -ADAM_LR * (m_hat / (jnp.sqrt(v_hat) + ADAM_EPS) + ADAM_WD * w_ref[...])
        m_out[...] = mn
        v_out[...] = vn

    blk = pl.BlockSpec((tr, c), lambda i: (i, 0))
    args = [wt, ga] + ([gb] if two else []) + [m, v]
    sds = jax.ShapeDtypeStruct((r, c), F32)
    return pl.pallas_call(
        body, name=name, grid=(r // tr,), in_specs=[blk] * len(args), out_specs=[blk] * 4, out_shape=[sds] * 4,
        compiler_params=_params(("parallel",)))(*args)


def _pack(arrs, rows):
    flat = jnp.concatenate([a.reshape(-1) for a in arrs])
    return jnp.pad(flat, (0, rows * LANES - flat.shape[0])).reshape(rows, LANES)


def _unpack(buf, shapes):
    flat = buf.reshape(-1)
    out, off = [], 0
    for shp in shapes:
        size = 1
        for d in shp:
            size *= d
        out.append(flat[off:off + size].reshape(shp))
        off += size
    return out


BIG = ["w_in", "w_out", "w_up", "w_down", "w_ple_gate", "w_ple_proj"]
COL_SHARDED = ["w_in", "w_up", "w_ple_proj"]
SMALL_REPL = ["g_mix", "q_gain", "k_gain", "dw_b", "conv_ln_g", "conv_ln_b", "g_ffn", "ffn_conv_b", "g_ple"]
SMALL_SHARDED = ["dw_w", "ffn_conv_w"]
WEIGHTS = ["g_mix", "w_in", "q_gain", "k_gain", "dw_w", "dw_b", "conv_ln_g", "conv_ln_b", "w_out", "g_ffn", "w_up",
           "ffn_conv_w", "ffn_conv_b", "w_down", "g_ple", "w_ple_gate", "w_ple_proj"]


def _rows_for(n_elems):
    return -(-n_elems // (8 * LANES)) * 8


def kernel(x, p, g_mix, w_in, q_gain, k_gain, dw_w, dw_b, conv_ln_g, conv_ln_b, w_out, g_ffn, w_up, ffn_conv_w, ffn_conv_b, w_down, g_ple, w_ple_gate, w_ple_proj, loss_target, m_g_mix, m_w_in, m_q_gain, m_k_gain, m_dw_w, m_dw_b, m_conv_ln_g, m_conv_ln_b, m_w_out, m_g_ffn, m_w_up, m_ffn_conv_w, m_ffn_conv_b, m_w_down, m_g_ple, m_w_ple_gate, m_w_ple_proj, v_g_mix, v_w_in, v_q_gain, v_k_gain, v_dw_w, v_dw_b, v_conv_ln_g, v_conv_ln_b, v_w_out, v_g_ffn, v_w_up, v_ffn_conv_w, v_ffn_conv_b, v_w_down, v_g_ple, v_w_ple_gate, v_w_ple_proj):
    given = dict(locals())
    strip = lambda n, a: a if n in SMALL_REPL else a[0]
    wts = {n: strip(n, given[n]) for n in WEIGHTS}
    mom = {n: strip(n, given["m_" + n]) for n in WEIGHTS}
    var = {n: strip(n, given["v_" + n]) for n in WEIGHTS}
    chip = 2 * lax.axis_index("x") + lax.axis_index("y")

    small_shard_shapes = [wts[n].shape for n in SMALL_SHARDED]
    filt_rows = _rows_for(sum(wts[n].size for n in SMALL_SHARDED))
    filt = _pack([wts[n] for n in SMALL_SHARDED], filt_rows)
    w_in_all, filt_all = _exchange([wts["w_in"].astype(BF16), filt], False, "gather_first")
    known = {n: wts[n] for n in SMALL_REPL}
    known["w_in"] = w_in_all
    per_chip = [_unpack(filt_all[k], small_shard_shapes) for k in range(N_CHIPS)]
    for idx, n in enumerate(SMALL_SHARDED):
        known[n] = jnp.concatenate([per_chip[k][idx] for k in range(N_CHIPS)], axis=1)
    shards = {n: wts[n].astype(BF16) for n in BIG if n != "w_in"}

    sq, grad_x, big, slots, small = _local_step(x[0], p[0, 0], loss_target[0], known, shards)
    loss = lax.psum(sq * (0.5 / x.shape[-1]), ("x", "y", "c"))

    small_names = SMALL_REPL + SMALL_SHARDED
    small_shapes = [small[n].shape for n in small_names]
    small_rows = _rows_for(sum(small[n].size for n in small_names))
    mine = [_sum_slots(slots[n], "sum_" + n) for n in BIG]
    *theirs, small_slots = _finish_exchange(mine, _pack([small[n] for n in small_names], small_rows))
    small_sum = dict(zip(small_names, _unpack(_sum_slots(small_slots, "sum_small"), small_shapes)))

    outs = {}
    for n, ga, gb in zip(BIG, mine, theirs):
        outs[n] = _adamw(wts[n], ga, gb, mom[n], var[n], "adamw_" + n)
    for n in SMALL_SHARDED:
        width = wts[n].shape[1]
        small_sum[n] = lax.dynamic_slice_in_dim(small_sum[n], chip * width, width, axis=1)
    local_shapes = [wts[n].shape for n in small_names]
    local_rows = _rows_for(sum(wts[n].size for n in small_names))
    packed = _adamw(_pack([wts[n] for n in small_names], local_rows), _pack([small_sum[n] for n in small_names], local_rows),
                    None, _pack([mom[n] for n in small_names], local_rows),
                    _pack([var[n] for n in small_names], local_rows), "adamw_small")
    unpacked = [_unpack(buf, local_shapes) for buf in packed]
    for idx, n in enumerate(small_names):
        outs[n] = [u[idx] for u in unpacked]
    result = [loss, grad_x[None]]
    for part in range(4):
        result += [outs[n][part] if n in SMALL_REPL else outs[n][part][None] for n in WEIGHTS]
    return tuple(result)
```

```python
import functools

import jax
import jax.numpy as jnp
from jax import lax
from jax.experimental import pallas as pl
from jax.experimental.pallas import tpu as pltpu

F32 = jnp.float32
BF16 = jnp.bfloat16
HIGHEST = lax.Precision.HIGHEST
MESH = pl.DeviceIdType.MESH
ANY = pl.BlockSpec(memory_space=pl.ANY)

EPS = 1e-6
HEAD_DIM = 64
N_HEADS = 8
ATTN_W = 512
CONV_W = 512
CONV_K = 31
FFN_K = 3
ATTN_SCALE = 0.125
LANES = 128
CONV_HALO = 32
FFN_HALO = 8
VMEM_LIMIT = 56 * 1024 * 1024

ADAM_LR = 0.001
ADAM_B1 = 0.9
ADAM_B2 = 0.999
ADAM_EPS = 1e-08
ADAM_WD = 0.01
ADAM_STEP = 10

N_CHIPS = 4
N_DEV = 8


def _params(sem):
    return pltpu.CompilerParams(dimension_semantics=sem, vmem_limit_bytes=VMEM_LIMIT)


def _row_tile(s):
    return min(512, s)


def _position():
    x, y, c = lax.axis_index("x"), lax.axis_index("y"), lax.axis_index("c")
    return x, y, c, [(1 - x, y), (x, 1 - y), (1 - x, 1 - y)]


def _remote(src, dst, send_sems, recv_sems, k, to):
    return pltpu.make_async_remote_copy(src_ref=src, dst_ref=dst, send_sem=send_sems.at[k], recv_sem=recv_sems.at[k],
                                        device_id=to, device_id_type=MESH)


def _chip_copies(src_refs, land_refs, send_sems, recv_sems, per_chip, landed):
    x, y, c, chips = _position()
    me = 2 * x + y
    out = []
    for a, (src, land) in enumerate(zip(src_refs, land_refs)):
        for j, (px, py) in enumerate(chips):
            peer = 2 * px + py
            out.append(_remote(src.at[peer] if per_chip else src, land.at[peer if landed else me],
                               send_sems, recv_sems, 3 * a + j, (px, py, c)))
    return out


def _local_copies(src_refs, land_refs, local_sems, per_chip):
    x, y, _, _ = _position()
    me = 2 * x + y
    return [pltpu.make_async_copy(src.at[me] if per_chip else src, land.at[me], local_sems.at[a])
            for a, (src, land) in enumerate(zip(src_refs, land_refs))]


def _exchanged_shapes(srcs, per_chip):
    return [jax.ShapeDtypeStruct(a.shape if per_chip else (N_CHIPS,) + a.shape, a.dtype) for a in srcs]


def _exchange_sems(n):
    return [pltpu.SemaphoreType.DMA((3 * n,)), pltpu.SemaphoreType.DMA((3 * n,)), pltpu.SemaphoreType.DMA((n,))]


def _call(body, *, name, grid, in_specs, out_specs, out_shape, scratch_shapes, args, semantics, carry=None):
    if carry is None:
        return pl.pallas_call(body, name=name, grid=grid, in_specs=in_specs, out_specs=out_specs, out_shape=out_shape,
                              scratch_shapes=scratch_shapes, compiler_params=_params(semantics))(*args)
    srcs, per_chip = carry
    n, n_in, n_out, n_scr = len(srcs), len(in_specs), len(out_specs), len(scratch_shapes)

    def wrapped(*refs):
        ins, xin = refs[:n_in], refs[n_in:n_in + n]
        outs, xout = refs[n_in + n:n_in + n + n_out], refs[n_in + n + n_out:n_in + 2 * n + n_out]
        scratch = refs[n_in + 2 * n + n_out:n_in + 2 * n + n_out + n_scr]
        send_sems, recv_sems, local_sems = refs[-3:]
        first = functools.reduce(jnp.logical_and, [pl.program_id(d) == 0 for d in range(len(grid))])
        last = functools.reduce(jnp.logical_and, [pl.program_id(d) == g - 1 for d, g in enumerate(grid)])

        @pl.when(first)
        def _():
            for cp in _local_copies(xin, xout, local_sems, per_chip):
                cp.start()
            for cp in _chip_copies(xin, xout, send_sems, recv_sems, per_chip, False):
                cp.start()

        body(*ins, *outs, *scratch)

        @pl.when(last)
        def _():
            for cp in _chip_copies(xin, xout, send_sems, recv_sems, per_chip, True):
                cp.wait_recv()
            for cp in _chip_copies(xin, xout, send_sems, recv_sems, per_chip, False):
                cp.wait_send()
            for cp in _local_copies(xin, xout, local_sems, per_chip):
                cp.wait()

    return pl.pallas_call(
        wrapped, name=name, grid=grid, in_specs=list(in_specs) + [ANY] * n, out_specs=list(out_specs) + [ANY] * n,
        out_shape=list(out_shape) + _exchanged_shapes(srcs, per_chip),
        scratch_shapes=list(scratch_shapes) + _exchange_sems(n),
        compiler_params=_params(("arbitrary",) * len(grid)))(*args, *srcs)


def _contract_tile(s):
    return min(1024, s)


def _rstd(x):
    return lax.rsqrt(jnp.mean(x * x, axis=-1, keepdims=True) + EPS)


def _sigmoid(x):
    return 1.0 / (1.0 + jnp.exp(-x))


def _mm(a, b, *, name, dims, grid, a_spec, b_spec, o_spec, o_tile, out_shape, res=None, res_spec=None, carry=None):
    nk = grid[2]

    def body(*refs):
        if res is None:
            a_ref, b_ref, o_ref, acc_ref = refs
            r_ref = None
        else:
            a_ref, b_ref, r_ref, o_ref, acc_ref = refs
        part = lax.dot_general(a_ref[...], b_ref[...], (dims, ((), ())), preferred_element_type=F32)

        def finish(val):
            if r_ref is not None:
                val = val + r_ref[...]
            o_ref[...] = val.astype(o_ref.dtype)

        if nk == 1:
            finish(part)
        else:
            k = pl.program_id(2)

            @pl.when(k == 0)
            def _():
                acc_ref[...] = part

            @pl.when(k > 0)
            def _():
                acc_ref[...] += part

            @pl.when(k == nk - 1)
            def _():
                finish(acc_ref[...])

    in_specs = [a_spec, b_spec]
    args = [a, b]
    if res is not None:
        in_specs.append(res_spec)
        args.append(res)
    acc_tile = o_tile if nk > 1 else (8, LANES)
    out = _call(body, name=name, grid=grid, in_specs=in_specs, out_specs=[o_spec], out_shape=[out_shape],
                scratch_shapes=[pltpu.VMEM(acc_tile, F32)], args=args,
                semantics=("parallel", "parallel", "arbitrary"), carry=carry)
    return out[0] if carry is None else out


NN = ((1,), (0,))
NT = ((1,), (1,))
TN = ((0,), (0,))


def _mm_nn_sharded(a, bg, name, out_dtype=F32, tm=None):
    s, k = a.shape
    g, _, ns = bg.shape
    tm = tm or _row_tile(s)

    def body(a_ref, b_ref, o_ref):
        av = a_ref[...]
        for gi in range(g):
            o_ref[:, gi * ns:(gi + 1) * ns] = jnp.dot(av, b_ref[gi], preferred_element_type=F32).astype(out_dtype)

    return pl.pallas_call(
        body, name=name, grid=(s // tm,),
        in_specs=[pl.BlockSpec((tm, k), lambda i: (i, 0)), pl.BlockSpec((g, k, ns), lambda i: (0, 0, 0))],
        out_specs=pl.BlockSpec((tm, g * ns), lambda i: (i, 0)),
        out_shape=jax.ShapeDtypeStruct((s, g * ns), out_dtype), compiler_params=_params(("parallel",)))(a, bg)


def _mm_nn_full(a, b, name, res=None):
    s, k = a.shape
    n = b.shape[1]
    tm = _row_tile(s)
    rs = pl.BlockSpec((tm, n), lambda i, j, kk: (i, 0))
    return _mm(a, b, name=name, dims=NN, grid=(s // tm, 1, 1),
               a_spec=pl.BlockSpec((tm, k), lambda i, j, kk: (i, 0)),
               b_spec=pl.BlockSpec((k, n), lambda i, j, kk: (0, 0)),
               o_spec=rs, o_tile=(tm, n), out_shape=jax.ShapeDtypeStruct((s, n), F32),
               res=res, res_spec=rs if res is not None else None)


def _mm_nt_full(a, b, name):
    s, n = a.shape
    k = b.shape[0]
    tm = _row_tile(s)
    return _mm(a, b, name=name, dims=NT, grid=(s // tm, 1, 1),
               a_spec=pl.BlockSpec((tm, n), lambda i, j, kk: (i, 0)),
               b_spec=pl.BlockSpec((k, n), lambda i, j, kk: (0, 0)),
               o_spec=pl.BlockSpec((tm, k), lambda i, j, kk: (i, 0)), o_tile=(tm, k),
               out_shape=jax.ShapeDtypeStruct((s, k), F32))


def _mm_nt_sharded(a, bg, name, carry=None):
    s = a.shape[0]
    g, k, ns = bg.shape
    tm = _row_tile(s)

    def body(a_ref, b_ref, o_ref):
        acc = lax.dot_general(a_ref[:, 0:ns], b_ref[0], (NT, ((), ())), preferred_element_type=F32)
        for gi in range(1, g):
            acc = acc + lax.dot_general(a_ref[:, gi * ns:(gi + 1) * ns], b_ref[gi], (NT, ((), ())),
                                        preferred_element_type=F32)
        o_ref[...] = acc

    return _call(
        body, name=name, grid=(s // tm,),
        in_specs=[pl.BlockSpec((tm, g * ns), lambda i: (i, 0)), pl.BlockSpec((g, k, ns), lambda i: (0, 0, 0))],
        out_specs=[pl.BlockSpec((tm, k), lambda i: (i, 0))], out_shape=[jax.ShapeDtypeStruct((s, k), F32)],
        scratch_shapes=[], args=(a, bg), semantics=("parallel",), carry=carry)


def _mm_tn_sharded(a, b, g, name, n_split=1):
    s, k = a.shape
    ns = b.shape[1] // g
    gs = g // n_split
    tk = _contract_tile(s)

    def body(a_ref, b_ref, o_ref):
        first = pl.program_id(1) == 0
        at = a_ref[...].T
        for gi in range(gs):
            part = jnp.dot(at, b_ref[:, gi * ns:(gi + 1) * ns], preferred_element_type=F32)

            @pl.when(first)
            def _(gi=gi, part=part):
                o_ref[gi] = part

            @pl.when(jnp.logical_not(first))
            def _(gi=gi, part=part):
                o_ref[gi] += part

    return pl.pallas_call(
        body, name=name, grid=(n_split, s // tk),
        in_specs=[pl.BlockSpec((tk, k), lambda j, kk: (kk, 0)), pl.BlockSpec((tk, gs * ns), lambda j, kk: (kk, j))],
        out_specs=pl.BlockSpec((gs, k, ns), lambda j, kk: (j, 0, 0)),
        out_shape=jax.ShapeDtypeStruct((g, k, ns), F32), compiler_params=_params(("parallel", "arbitrary")))(a, b)


def _mm_tn_full(a, b, name, tm, carry=None):
    s, m = a.shape
    n = b.shape[1]
    tk = _contract_tile(s)
    return _mm(a, b, name=name, dims=TN, grid=(m // tm, 1, s // tk),
               a_spec=pl.BlockSpec((tk, tm), lambda i, j, kk: (kk, i)),
               b_spec=pl.BlockSpec((tk, n), lambda i, j, kk: (kk, 0)),
               o_spec=pl.BlockSpec((tm, n), lambda i, j, kk: (i, 0)), o_tile=(tm, n),
               out_shape=jax.ShapeDtypeStruct((m, n), F32), carry=carry)


def _rms_fwd(x, g, name, carry=None):
    s, d = x.shape
    ts = _row_tile(s)

    def body(x_ref, g_ref, u_ref):
        xv = x_ref[...]
        u_ref[...] = (xv * _rstd(xv) * g_ref[...]).astype(BF16)

    row = pl.BlockSpec((ts, d), lambda i: (i, 0))
    out = _call(body, name=name, grid=(s // ts,), in_specs=[row, pl.BlockSpec((1, d), lambda i: (0, 0))],
                out_specs=[row], out_shape=[jax.ShapeDtypeStruct((s, d), BF16)], scratch_shapes=[], args=(x, g),
                semantics=("parallel",), carry=carry)
    return out[0] if carry is None else out


def _rms_bwd(h, du, g, dh_in, name, carry=None):
    s, d = h.shape
    ts = _row_tile(s)

    def body(h_ref, du_ref, g_ref, dhin_ref, dh_ref, dhb_ref, gg_ref):
        hv = h_ref[...]
        r = _rstd(hv)
        xhat = hv * r
        duv = du_ref[...]
        dxhat = duv * g_ref[...]
        m = jnp.mean(dxhat * xhat, axis=-1, keepdims=True)
        dh = dhin_ref[...] + r * (dxhat - xhat * m)
        dh_ref[...] = dh
        dhb_ref[...] = dh.astype(BF16)
        part = jnp.sum(duv * xhat, axis=0, keepdims=True)

        @pl.when(pl.program_id(0) == 0)
        def _():
            gg_ref[...] = part

        @pl.when(pl.program_id(0) > 0)
        def _():
            gg_ref[...] += part

    row = pl.BlockSpec((ts, d), lambda i: (i, 0))
    vec = pl.BlockSpec((1, d), lambda i: (0, 0))
    return _call(
        body, name=name, grid=(s // ts,), in_specs=[row, row, vec, row], out_specs=[row, row, vec],
        out_shape=[jax.ShapeDtypeStruct((s, d), F32), jax.ShapeDtypeStruct((s, d), BF16),
                   jax.ShapeDtypeStruct((1, d), F32)],
        scratch_shapes=[], args=(h, du, g, dh_in), semantics=("arbitrary",), carry=carry)


def _head_sum(x, bd):
    return _tri_dot(_split(x), bd)


def _qkv_prep(proj, qg, kg, bd):
    s = proj.shape[0]
    ts = _row_tile(s)

    def body(q_ref, k_ref, v_ref, qg_ref, kg_ref, bd_ref, qs_ref, kh_ref, vb_ref):
        def norm(x, gain):
            ms = _head_sum(x * x, bd_ref[...]) * (1.0 / HEAD_DIM)
            return x * lax.rsqrt(ms + EPS) * gain

        qs_ref[...] = (norm(q_ref[...], qg_ref[...]) * ATTN_SCALE).astype(BF16)
        kh_ref[...] = norm(k_ref[...], kg_ref[...]).astype(BF16)
        vb_ref[...] = v_ref[...].astype(BF16)

    col = lambda c: pl.BlockSpec((ts, ATTN_W), lambda i: (i, c))
    vec = pl.BlockSpec((1, ATTN_W), lambda i: (0, 0))
    out = pl.BlockSpec((ts, ATTN_W), lambda i: (i, 0))
    sds = jax.ShapeDtypeStruct((s, ATTN_W), BF16)
    return pl.pallas_call(
        body, name="qkv_prep", grid=(s // ts,),
        in_specs=[col(0), col(1), col(2), vec, vec, pl.BlockSpec((ATTN_W, ATTN_W), lambda i: (0, 0))],
        out_specs=[out, out, out], out_shape=[sds, sds, sds],
        compiler_params=_params(("parallel",)))(proj, proj, proj, qg, kg, bd)


def _qk_bwd(proj, dqh, dkh, dv, qg, kg, bd):
    s = proj.shape[0]
    ts = _row_tile(s)

    def body(q_ref, k_ref, dqh_ref, dkh_ref, dv_ref, qg_ref, kg_ref, bd_ref, out_ref, gq_ref, gk_ref):
        first = pl.program_id(0) == 0

        def bwd(x, dy, gain, gg_ref):
            ms = _head_sum(x * x, bd_ref[...]) * (1.0 / HEAD_DIM)
            r = lax.rsqrt(ms + EPS)
            xhat = x * r
            dxhat = dy * gain
            m = _head_sum(dxhat * xhat, bd_ref[...]) * (1.0 / HEAD_DIM)
            part = jnp.sum(dy * xhat, axis=0, keepdims=True)

            @pl.when(first)
            def _():
                gg_ref[...] = part

            @pl.when(jnp.logical_not(first))
            def _():
                gg_ref[...] += part

            return r * (dxhat - xhat * m)

        out_ref[:, 0:ATTN_W] = bwd(q_ref[...], dqh_ref[...], qg_ref[...], gq_ref).astype(BF16)
        out_ref[:, ATTN_W:2 * ATTN_W] = bwd(k_ref[...], dkh_ref[...], kg_ref[...], gk_ref).astype(BF16)
        out_ref[:, 2 * ATTN_W:3 * ATTN_W] = dv_ref[...].astype(BF16)

    col = lambda c: pl.BlockSpec((ts, ATTN_W), lambda i: (i, c))
    row = pl.BlockSpec((ts, ATTN_W), lambda i: (i, 0))
    vec = pl.BlockSpec((1, ATTN_W), lambda i: (0, 0))
    return pl.pallas_call(
        body, name="qk_bwd", grid=(s // ts,),
        in_specs=[col(0), col(1), row, row, row, vec, vec, pl.BlockSpec((ATTN_W, ATTN_W), lambda i: (0, 0))],
        out_specs=[pl.BlockSpec((ts, 3 * ATTN_W), lambda i: (i, 0)), vec, vec],
        out_shape=[jax.ShapeDtypeStruct((s, 3 * ATTN_W), BF16), jax.ShapeDtypeStruct((1, ATTN_W), F32),
                   jax.ShapeDtypeStruct((1, ATTN_W), F32)],
        compiler_params=_params(("arbitrary",)))(proj, proj, dqh, dkh, dv, qg, kg, bd)


def _split(x):
    hi = x.astype(BF16)
    return hi, (x - hi.astype(F32)).astype(BF16)


def _tri_dot(parts, tri):
    hi, lo = parts
    return jnp.dot(hi, tri, preferred_element_type=F32) + jnp.dot(lo, tri, preferred_element_type=F32)


def _log_sigmoids(z):
    neg_abs = lax.bitcast_convert_type(lax.bitcast_convert_type(z, jnp.uint32) | jnp.uint32(0x80000000), F32)
    lb = jnp.minimum(z, 0.0) - jnp.log(1.0 + jnp.exp(neg_abs))
    return lb, lb - z


DEAD_LOG_WEIGHT = -106.0


def _sweep_key_blocks(tiles, alive, i):
    @pl.when(i == 0)
    def _():
        tiles([0], [True])

    @pl.when(i > 0)
    def _():
        tiles([i, i - 1], [True, False])

    def more(state):
        kb, live = state
        return jnp.logical_and(kb >= 0, live > 0)

    def step(state):
        kb, _ = state
        tiles([kb], [False])
        return kb - 1, alive().astype(jnp.int32)

    lax.while_loop(more, step, (i - 2, alive().astype(jnp.int32)))


def _head_masks():
    lane = lax.broadcasted_iota(jnp.int32, (1, LANES), 1)
    return [lane < HEAD_DIM, lane >= HEAD_DIM]


def _attn_fwd(qs, kh, vb, tri_excl, carry=None):
    s = qs.shape[0]
    t = tri_excl.shape[0]
    nq = s // t

    def body(q_ref, k_ref, v_ref, tri_ref, o_ref, ob_ref, acc_ref, c_ref):
        i = pl.program_id(1)
        hmask = _head_masks()
        q = q_ref[...]
        qm = [jnp.where(hm, q, jnp.zeros_like(q)) for hm in hmask]
        acc_ref[...] = jnp.zeros_like(acc_ref)
        c_ref[...] = jnp.zeros_like(c_ref)
        causal = (lax.broadcasted_iota(jnp.int32, (t, t), 1) < lax.broadcasted_iota(jnp.int32, (t, t), 0))

        def tiles(kbs, masked):
            tri = tri_ref[...]
            starts = [pl.multiple_of(kb * t, t) for kb in kbs]
            kblks = [k_ref[pl.ds(k0, t), :] for k0 in starts]
            vblks = [v_ref[pl.ds(k0, t), :] for k0 in starts]
            chains = [(j, h) for j in range(len(kbs)) for h in range(2)]
            carry = [c_ref[h] for h in range(2)]
            pv = [None, None]
            lbs, loms, between = {}, {}, {}
            for step in range(len(chains) + 2):
                if step < len(chains):
                    j, h = chains[step]
                    z = lax.dot_general(qm[h], kblks[j], (NT, ((), ())), preferred_element_type=F32)
                    lbs[step], lom = _log_sigmoids(z)
                    loms[step] = jnp.where(causal, lom, 0.0) if masked[j] else lom
                if 0 <= step - 1 < len(chains):
                    between[step - 1] = _tri_dot(_split(loms[step - 1]), tri)
                if 0 <= step - 2 < len(chains):
                    n = step - 2
                    j, h = chains[n]
                    w = jnp.exp(lbs[n] + between[n] + carry[h])
                    if masked[j]:
                        w = jnp.where(causal, w, 0.0)
                    carry[h] = carry[h] + jnp.sum(loms[n], axis=-1, keepdims=True)
                    part = jnp.dot(w.astype(BF16), vblks[j], preferred_element_type=F32)
                    pv[h] = part if pv[h] is None else pv[h] + part
            for h in range(2):
                c_ref[h] = carry[h]
                acc_ref[h] += pv[h]

        _sweep_key_blocks(tiles, lambda: jnp.max(c_ref[...]) > DEAD_LOG_WEIGHT, i)
        o = jnp.where(hmask[0], acc_ref[0], acc_ref[1])
        o_ref[...] = o
        ob_ref[...] = o.astype(BF16)

    qspec = pl.BlockSpec((t, LANES), lambda hp, i: (i, hp))
    kspec = pl.BlockSpec((s, LANES), lambda hp, i: (0, hp))
    return _call(
        body, name="attn_fwd", grid=(ATTN_W // LANES, nq),
        in_specs=[qspec, kspec, kspec, pl.BlockSpec((t, t), lambda hp, i: (0, 0))],
        out_specs=[qspec, qspec],
        out_shape=[jax.ShapeDtypeStruct((s, ATTN_W), F32), jax.ShapeDtypeStruct((s, ATTN_W), BF16)],
        scratch_shapes=[pltpu.VMEM((2, t, LANES), F32), pltpu.VMEM((2, t, 1), F32)],
        args=(qs, kh, vb, tri_excl), semantics=("parallel", "arbitrary"), carry=carry)


def _attn_bwd(qs, kh, vb, o, dmix, tri_excl, tri_incl, carry=None):
    s = qs.shape[0]
    t = tri_excl.shape[0]
    nq = s // t

    def body(q_ref, k_ref, v_ref, o_ref, do_ref, te_ref, ti_ref, dq_ref, dk_ref, dv_ref, dqacc_ref, c_ref):
        i = pl.program_id(1)

        @pl.when(i == 0)
        def _():
            dk_ref[...] = jnp.zeros_like(dk_ref)
            dv_ref[...] = jnp.zeros_like(dv_ref)

        hmask = _head_masks()
        q = q_ref[...]
        do = do_ref[...]
        dob = do.astype(BF16)
        prod = dob.astype(F32) * o_ref[...]
        qm =[jnp.where(hm, q, jnp.zeros_like(q)) for hm in hmask]
        dom = [jnp.where(hm, dob, jnp.zeros_like(dob)) for hm in hmask]
        total = [jnp.sum(jnp.where(hm, prod, 0.0), axis=-1, keepdims=True) for hm in hmask]
        dqacc_ref[...] = jnp.zeros_like(dqacc_ref)
        c_ref[...] = jnp.zeros_like(c_ref)
        causal = (lax.broadcasted_iota(jnp.int32, (t, t), 1) < lax.broadcasted_iota(jnp.int32, (t, t), 0))

        def tiles(kbs, masked):
            te = te_ref[...]
            ti = ti_ref[...]
            starts = [pl.multiple_of(kb * t, t) for kb in kbs]
            kblks = [k_ref[pl.ds(k0, t), :] for k0 in starts]
            vblks = [v_ref[pl.ds(k0, t), :] for k0 in starts]
            chains = [(j, h) for j in range(len(kbs)) for h in range(2)]
            c_lom = [c_ref[2 * h] for h in range(2)]
            c_g = [c_ref[2 * h + 1] for h in range(2)]
            lbs, loms, dws, between, wbs, gs, g_after = {}, {}, {}, {}, {}, {}, {}
            dq = [None, None]
            dk = [None] * len(kbs)
            dv = [None] * len(kbs)
            add = lambda acc, part: part if acc is None else acc + part
            for step in range(len(chains) + 3):
                if step < len(chains):
                    j, h = chains[step]
                    z = lax.dot_general(qm[h], kblks[j], (NT, ((), ())), preferred_element_type=F32)
                    dws[step] = lax.dot_general(dom[h], vblks[j], (NT, ((), ())), preferred_element_type=F32)
                    lbs[step], lom = _log_sigmoids(z)
                    loms[step] = jnp.where(causal, lom, 0.0) if masked[j] else lom
                if 0 <= step - 1 < len(chains):
                    between[step - 1] = _tri_dot(_split(loms[step - 1]), te)
                if 0 <= step - 2 < len(chains):
                    n = step - 2
                    j, h = chains[n]
                    w = jnp.exp(lbs[n] + between[n] + c_lom[h])
                    if masked[j]:
                        w = jnp.where(causal, w, 0.0)
                    c_lom[h] = c_lom[h] + jnp.sum(loms[n], axis=-1, keepdims=True)
                    wbs[n] = w.astype(BF16)
                    gs[n] = dws[n] * wbs[n].astype(F32)
                    g_after[n] = _tri_dot(_split(gs[n]), ti)
                if 0 <= step - 3 < len(chains):
                    n = step - 3
                    j, h = chains[n]
                    beta = jnp.exp(lbs[n])
                    dz = gs[n] * (1.0 - beta) - beta * (total[h] - (g_after[n] + c_g[h]))
                    if masked[j]:
                        dz = jnp.where(causal, dz, 0.0)
                    c_g[h] = c_g[h] + jnp.sum(gs[n], axis=-1, keepdims=True)
                    dzb = dz.astype(BF16)
                    dq[h] = add(dq[h], jnp.dot(dzb, kblks[j], preferred_element_type=F32))
                    dk[j] = add(dk[j], lax.dot_general(dzb, qm[h], (TN, ((), ())), preferred_element_type=F32))
                    dv[j] = add(dv[j], lax.dot_general(wbs[n], dom[h], (TN, ((), ())), preferred_element_type=F32))
            for h in range(2):
                c_ref[2 * h] = c_lom[h]
                c_ref[2 * h + 1] = c_g[h]
                dqacc_ref[h] += dq[h]
            for j, k0 in enumerate(starts):
                dk_ref[pl.ds(k0, t), :] += dk[j]
                dv_ref[pl.ds(k0, t), :] += dv[j]

        _sweep_key_blocks(tiles, lambda: jnp.maximum(jnp.max(c_ref[0]), jnp.max(c_ref[2])) > DEAD_LOG_WEIGHT, i)
        dq_ref[...] = jnp.where(hmask[0], dqacc_ref[0], dqacc_ref[1]) * ATTN_SCALE

    qspec = pl.BlockSpec((t, LANES), lambda hp, i: (i, hp))
    kspec = pl.BlockSpec((s, LANES), lambda hp, i: (0, hp))
    tspec = pl.BlockSpec((t, t), lambda hp, i: (0, 0))
    sds = jax.ShapeDtypeStruct((s, ATTN_W), F32)
    return _call(
        body, name="attn_bwd", grid=(ATTN_W // LANES, nq),
        in_specs=[qspec, kspec, kspec, qspec, qspec, tspec, tspec],
        out_specs=[qspec, kspec, kspec], out_shape=[sds, sds, sds],
        scratch_shapes=[pltpu.VMEM((2, t, LANES), F32), pltpu.VMEM((4, t, 1), F32)],
        args=(qs, kh, vb, o, dmix, tri_excl, tri_incl), semantics=("parallel", "arbitrary"), carry=carry)


CONV_ROWS = 64
CONV_COLS = 256


SUBLANES = 8


def _shift_copies(src_ref, sh_ref):
    length = sh_ref.shape[1]
    for r in range(1, SUBLANES):
        sh_ref[r - 1] = src_ref[r:r + length, :]


def _shift_scratch(ts):
    return pltpu.VMEM((SUBLANES - 1, ts + CONV_HALO - SUBLANES, CONV_W), F32)


def _rows_at(src_ref, sh_ref, offset, r0, rows, cols):
    r = offset % SUBLANES
    base = offset - r + r0
    return src_ref[base:base + rows, cols] if r == 0 else sh_ref[r - 1, base:base + rows, cols]


def _taps(src_ref, sh_ref, w_ref, n_taps, first_row, rows, reverse=False):
    width = src_ref.shape[1]
    cols = min(CONV_COLS, width)
    out = []
    for r0 in range(0, rows, CONV_ROWS):
        for c0 in range(0, width, cols):
            acc = jnp.zeros((CONV_ROWS, cols), F32)
            for k in range(n_taps):
                off = (n_taps - 1 - k) if reverse else k
                acc = acc + w_ref[k:k + 1, c0:c0 + cols] * _rows_at(src_ref, sh_ref, first_row + off, r0, CONV_ROWS,
                                                                    slice(c0, c0 + cols))
            out.append(((r0, c0), acc))
    return out


def _conv_fwd(proj, dw_w, dw_b, ln_g, ln_b, carry=None):
    s = proj.shape[0]
    ts = _row_tile(s)
    hb = ts // CONV_HALO

    def body(a_ref, g_ref, ha_ref, hg_ref, w_ref, b_ref, lg_ref, lb_ref, c1_ref, c3_ref, pad_ref, sh_ref):
        i = pl.program_id(0)
        halo = ha_ref[...] * _sigmoid(hg_ref[...])
        pad_ref[0:CONV_HALO, :] = jnp.where(i > 0, halo, 0.0)
        pad_ref[CONV_HALO:, :] = a_ref[...] * _sigmoid(g_ref[...])
        _shift_copies(pad_ref, sh_ref)
        first = CONV_HALO - (CONV_K - 1)
        for (r0, c0), acc in _taps(pad_ref, sh_ref, w_ref, CONV_K, first, ts):
            c1_ref[r0:r0 + CONV_ROWS, c0:c0 + acc.shape[1]] = acc + b_ref[:, c0:c0 + acc.shape[1]]
        c1 = c1_ref[...]
        xc = c1 - jnp.mean(c1, axis=-1, keepdims=True)
        c2 = xc * _rstd(xc) * lg_ref[...] + lb_ref[...]
        c3_ref[...] = (c2 * _sigmoid(c2)).astype(BF16)

    cur = lambda c: pl.BlockSpec((ts, CONV_W), lambda i: (i, c))
    halo = lambda c: pl.BlockSpec((CONV_HALO, CONV_W), lambda i: (jnp.maximum(i * hb - 1, 0), c))
    vec = pl.BlockSpec((1, CONV_W), lambda i: (0, 0))
    row = pl.BlockSpec((ts, CONV_W), lambda i: (i, 0))
    return _call(
        body, name="conv_fwd", grid=(s // ts,),
        in_specs=[cur(3), cur(4), halo(3), halo(4), pl.BlockSpec((CONV_HALO, CONV_W), lambda i: (0, 0)), vec, vec, vec],
        out_specs=[row, row],
        out_shape=[jax.ShapeDtypeStruct((s, CONV_W), F32), jax.ShapeDtypeStruct((s, CONV_W), BF16)],
        scratch_shapes=[pltpu.VMEM((ts + CONV_HALO, CONV_W), F32), _shift_scratch(ts)],
        args=(proj, proj, proj, proj, dw_w, dw_b, ln_g, ln_b), semantics=("parallel",), carry=carry)


def _conv_bwd_ln(dmix, c1, ln_g, ln_b):
    s = c1.shape[0]
    ts = _row_tile(s)

    def body(d_ref, c1_ref, lg_ref, lb_ref, dc1_ref, glg_ref, glb_ref, gb_ref):
        c1v = c1_ref[...]
        xc = c1v - jnp.mean(c1v, axis=-1, keepdims=True)
        r = _rstd(xc)
        xhat = xc * r
        c2 = xhat * lg_ref[...] + lb_ref[...]
        sg = _sigmoid(c2)
        dc2 = d_ref[...] * (sg * (1.0 + c2 * (1.0 - sg)))
        dxhat = dc2 * lg_ref[...]
        dc1 = r * (dxhat - jnp.mean(dxhat, axis=-1, keepdims=True)
                   - xhat * jnp.mean(dxhat * xhat, axis=-1, keepdims=True))
        dc1_ref[...] = dc1
        parts = [(glg_ref, jnp.sum(dc2 * xhat, axis=0, keepdims=True)),
                 (glb_ref, jnp.sum(dc2, axis=0, keepdims=True)),
                 (gb_ref, jnp.sum(dc1, axis=0, keepdims=True))]

        @pl.when(pl.program_id(0) == 0)
        def _():
            for ref, part in parts:
                ref[...] = part

        @pl.when(pl.program_id(0) > 0)
        def _():
            for ref, part in parts:
                ref[...] += part

    row = pl.BlockSpec((ts, CONV_W), lambda i: (i, 0))
    vec = pl.BlockSpec((1, CONV_W), lambda i: (0, 0))
    vsd = jax.ShapeDtypeStruct((1, CONV_W), F32)
    return pl.pallas_call(
        body, name="conv_bwd_ln", grid=(s // ts,),
        in_specs=[pl.BlockSpec((ts, CONV_W), lambda i: (i, 1)), row, vec, vec],
        out_specs=[row, vec, vec, vec],
        out_shape=[jax.ShapeDtypeStruct((s, CONV_W), F32), vsd, vsd, vsd],
        compiler_params=_params(("arbitrary",)))(dmix, c1, ln_g, ln_b)


def _conv_bwd_taps(proj, dc1, dw_w):
    s = proj.shape[0]
    ts = _row_tile(s)
    hb = ts // CONV_HALO
    last = s // CONV_HALO - 1
    nsteps = s // ts

    def body(a_ref, g_ref, ha_ref, hg_ref, d_ref, hd_ref, w_ref, out_ref, gw_ref, pad_ref, dpad_ref, sh_ref):
        i = pl.program_id(0)
        av = a_ref[...]
        sg = _sigmoid(g_ref[...])
        halo = ha_ref[...] * _sigmoid(hg_ref[...])
        pad_ref[0:CONV_HALO, :] = jnp.where(i > 0, halo, 0.0)
        pad_ref[CONV_HALO:, :] = av * sg
        dpad_ref[0:ts, :] = d_ref[...]
        dpad_ref[ts:, :] = jnp.where(i < nsteps - 1, hd_ref[...], 0.0)

        @pl.when(i == 0)
        def _():
            gw_ref[...] = jnp.zeros_like(gw_ref)

        _shift_copies(pad_ref, sh_ref)
        first = CONV_HALO - (CONV_K - 1)
        for k in range(CONV_K):
            shifted = _rows_at(pad_ref, sh_ref, first + k, 0, ts, slice(None))
            gw_ref[k:k + 1, :] += jnp.sum(d_ref[...] * shifted, axis=0, keepdims=True)
        _shift_copies(dpad_ref, sh_ref)
        for (r0, c0), dc0 in _taps(dpad_ref, sh_ref, w_ref, CONV_K, 0, ts, reverse=True):
            cs = slice(c0, c0 + dc0.shape[1])
            a_c = a_ref[r0:r0 + CONV_ROWS, cs]
            sg_c = _sigmoid(g_ref[r0:r0 + CONV_ROWS, cs])
            out_ref[r0:r0 + CONV_ROWS, cs] = (dc0 * sg_c).astype(BF16)
            out_ref[r0:r0 + CONV_ROWS, CONV_W + c0:CONV_W + c0 + dc0.shape[1]] = (
                dc0 * a_c * sg_c * (1.0 - sg_c)).astype(BF16)

    cur = lambda c: pl.BlockSpec((ts, CONV_W), lambda i: (i, c))
    halo = lambda c: pl.BlockSpec((CONV_HALO, CONV_W), lambda i: (jnp.maximum(i * hb - 1, 0), c))
    row = pl.BlockSpec((ts, CONV_W), lambda i: (i, 0))
    nxt = pl.BlockSpec((CONV_HALO, CONV_W), lambda i: (jnp.minimum((i + 1) * hb, last), 0))
    wspec = pl.BlockSpec((CONV_HALO, CONV_W), lambda i: (0, 0))
    return pl.pallas_call(
        body, name="conv_bwd_taps", grid=(nsteps,),
        in_specs=[cur(3), cur(4), halo(3), halo(4), row, nxt, wspec],
        out_specs=[pl.BlockSpec((ts, 2 * CONV_W), lambda i: (i, 0)), wspec],
        out_shape=[jax.ShapeDtypeStruct((s, 2 * CONV_W), BF16), jax.ShapeDtypeStruct((CONV_HALO, CONV_W), F32)],
        scratch_shapes=[pltpu.VMEM((ts + CONV_HALO, CONV_W), F32), pltpu.VMEM((ts + CONV_HALO, CONV_W), F32),
                        _shift_scratch(ts)],
        compiler_params=_params(("arbitrary",)))(proj, proj, proj, proj, dc1, dc1, dw_w)


SQRT_HALF = 0.7071067811865476
INV_SQRT_2PI = 0.3989422804014327


def _gelu_parts(x):
    cdf = 0.5 * (1.0 + lax.erf(x * SQRT_HALF))
    return x * cdf, cdf + x * (INV_SQRT_2PI * jnp.exp(-0.5 * x * x))


def _ffn_tile(dff):
    return dff // 2


FFN_ROWS = 64
FFN_COLS = LANES


def _ffn_chunks(ts, tc):
    return [(r0, slice(c0, c0 + FFN_COLS)) for c0 in range(0, tc, FFN_COLS) for r0 in range(0, ts, FFN_ROWS)]


def _ffn_gate2(pad_ref, w_ref, b_ref, r0, cs):
    first = FFN_HALO - (FFN_K - 1) + r0
    g2 = b_ref[:, cs] + w_ref[0:1, cs] * pad_ref[first:first + FFN_ROWS, cs]
    for k in range(1, FFN_K):
        g2 = g2 + w_ref[k:k + 1, cs] * pad_ref[first + k:first + k + FFN_ROWS, cs]
    return g2


def _ffn_act(up, fw, fb):
    s = up.shape[0]
    dff = up.shape[1] // 2
    tc = _ffn_tile(dff)
    nj = dff // tc
    ts = _row_tile(s) // 2
    hb = ts // FFN_HALO

    def body(g_ref, v_ref, hg_ref, w_ref, b_ref, act_ref, pad_ref):
        i = pl.program_id(0)
        pad_ref[0:FFN_HALO, :] = jnp.where(i > 0, hg_ref[...], 0.0)
        pad_ref[FFN_HALO:, :] = g_ref[...]
        for r0, cs in _ffn_chunks(ts, tc):
            gelu, _ = _gelu_parts(_ffn_gate2(pad_ref, w_ref, b_ref, r0, cs))
            act_ref[r0:r0 + FFN_ROWS, cs] = (gelu * v_ref[r0:r0 + FFN_ROWS, cs]).astype(BF16)

    return pl.pallas_call(
        body, name="ffn_act", grid=(s // ts, nj),
        in_specs=[pl.BlockSpec((ts, tc), lambda i, j: (i, j)), pl.BlockSpec((ts, tc), lambda i, j: (i, j + nj)),
                  pl.BlockSpec((FFN_HALO, tc), lambda i, j: (jnp.maximum(i * hb - 1, 0), j)),
                  pl.BlockSpec((FFN_HALO, tc), lambda i, j: (0, j)), pl.BlockSpec((1, tc), lambda i, j: (0, j))],
        out_specs=pl.BlockSpec((ts, tc), lambda i, j: (i, j)),
        out_shape=jax.ShapeDtypeStruct((s, dff), BF16),
        scratch_shapes=[pltpu.VMEM((ts + FFN_HALO, tc), F32)],
        compiler_params=_params(("parallel", "parallel")))(up, up, up, fw, fb)


def _ffn_bwd_act(dact, up, fw, fb, carry=None):
    s = up.shape[0]
    dff = up.shape[1] // 2
    tc = _ffn_tile(dff)
    nj = dff // tc
    ts = _row_tile(s) // 2
    hb = ts // FFN_HALO

    def body(d_ref, g_ref, v_ref, hg_ref, w_ref, b_ref, dg2_ref, dval_ref, gw_ref, gb_ref, pad_ref):
        i = pl.program_id(1)
        pad_ref[0:FFN_HALO, :] = jnp.where(i > 0, hg_ref[...], 0.0)
        pad_ref[FFN_HALO:, :] = g_ref[...]

        @pl.when(i == 0)
        def _():
            gw_ref[...] = jnp.zeros_like(gw_ref)
            gb_ref[...] = jnp.zeros_like(gb_ref)

        fold = lambda v: jnp.sum(v.reshape(FFN_ROWS // SUBLANES, SUBLANES, FFN_COLS), axis=0)
        first = FFN_HALO - (FFN_K - 1)
        sums = {}
        for r0, cs in _ffn_chunks(ts, tc):
            rows = slice(r0, r0 + FFN_ROWS)
            gelu, dgelu = _gelu_parts(_ffn_gate2(pad_ref, w_ref, b_ref, r0, cs))
            dactv = d_ref[rows, cs]
            dval_ref[rows, cs] = (dactv * gelu).astype(BF16)
            dg2 = dactv * v_ref[rows, cs] * dgelu
            dg2_ref[rows, cs] = dg2
            parts = [fold(dg2)] + [fold(dg2 * pad_ref[first + k + r0:first + k + r0 + FFN_ROWS, cs])
                                   for k in range(FFN_K)]
            sums = {n: part + sums[n] if r0 else part for n, part in enumerate(parts)}
            if r0 + FFN_ROWS == ts:
                gb_ref[:, cs] += jnp.sum(sums[0], axis=0, keepdims=True)
                for k in range(FFN_K):
                    gw_ref[k:k + 1, cs] += jnp.sum(sums[1 + k], axis=0, keepdims=True)

    blk = pl.BlockSpec((ts, tc), lambda j, i: (i, j))
    wspec = pl.BlockSpec((FFN_HALO, tc), lambda j, i: (0, j))
    bspec = pl.BlockSpec((1, tc), lambda j, i: (0, j))
    return _call(
        body, name="ffn_bwd_act", grid=(nj, s // ts),
        in_specs=[blk, blk, pl.BlockSpec((ts, tc), lambda j, i: (i, j + nj)),
                  pl.BlockSpec((FFN_HALO, tc), lambda j, i: (jnp.maximum(i * hb - 1, 0), j)), wspec, bspec],
        out_specs=[blk, pl.BlockSpec((ts, tc), lambda j, i: (i, j + nj)), wspec, bspec],
        out_shape=[jax.ShapeDtypeStruct((s, dff), F32), jax.ShapeDtypeStruct((s, 2 * dff), BF16),
                   jax.ShapeDtypeStruct((FFN_HALO, dff), F32), jax.ShapeDtypeStruct((1, dff), F32)],
        scratch_shapes=[pltpu.VMEM((ts + FFN_HALO, tc), F32)],
        args=(dact, up, up, up, fw, fb), semantics=("parallel", "arbitrary"), carry=carry)


def _ffn_bwd_conv(dg2, fw, dup):
    s, dff = dg2.shape
    tc = _ffn_tile(dff)
    ts = _row_tile(s) // 2
    hb = ts // FFN_HALO
    last = s // FFN_HALO - 1
    nsteps = s // ts

    def body(d_ref, hd_ref, w_ref, dup_ref, out_ref, pad_ref):
        i = pl.program_id(0)
        pad_ref[0:ts, :] = d_ref[...]
        pad_ref[ts:, :] = jnp.where(i < nsteps - 1, hd_ref[...], 0.0)
        for r0, cs in _ffn_chunks(ts, tc):
            dg = w_ref[0:1, cs] * pad_ref[r0 + FFN_K - 1:r0 + FFN_K - 1 + FFN_ROWS, cs]
            for k in range(1, FFN_K):
                dg = dg + w_ref[k:k + 1, cs] * pad_ref[r0 + FFN_K - 1 - k:r0 + FFN_K - 1 - k + FFN_ROWS, cs]
            out_ref[r0:r0 + FFN_ROWS, cs] = dg.astype(BF16)

    blk = pl.BlockSpec((ts, tc), lambda i, j: (i, j))
    return pl.pallas_call(
        body, name="ffn_bwd_conv", grid=(nsteps, dff // tc),
        in_specs=[blk, pl.BlockSpec((FFN_HALO, tc), lambda i, j: (jnp.minimum((i + 1) * hb, last), j)),
                  pl.BlockSpec((FFN_HALO, tc), lambda i, j: (0, j)), ANY],
        out_specs=blk, out_shape=jax.ShapeDtypeStruct(dup.shape, BF16), input_output_aliases={3: 0},
        scratch_shapes=[pltpu.VMEM((ts + FFN_HALO, tc), F32)],
        compiler_params=_params(("parallel", "parallel")))(dg2, dg2, fw, dup)


def _ple_loss(h2, zg, pp, target):
    s, d = h2.shape
    ts = _row_tile(s)

    def body(h_ref, z_ref, p_ref, t_ref, dh_ref, dpp_ref, dz_ref, loss_ref):
        pg = _sigmoid(z_ref[...])
        ppv = p_ref[...]
        diff = h_ref[...] + pg * ppv - t_ref[...]
        dh = diff * (1.0 / d)
        dh_ref[...] = dh
        dpp_ref[...] = (dh * pg).astype(BF16)
        dz_ref[...] = (dh * ppv * pg * (1.0 - pg)).astype(BF16)
        part = jnp.sum(jnp.sum(diff * diff, axis=0, keepdims=True), axis=1, keepdims=True)

        @pl.when(pl.program_id(0) == 0)
        def _():
            loss_ref[...] = jnp.zeros_like(loss_ref)

        loss_ref[...] += jnp.broadcast_to(part, loss_ref.shape)

    row = pl.BlockSpec((ts, d), lambda i: (i, 0))
    return pl.pallas_call(
        body, name="ple_loss", grid=(s // ts,), in_specs=[row, row, row, row],
        out_specs=[row, row, row, pl.BlockSpec((8, LANES), lambda i: (0, 0))],
        out_shape=[jax.ShapeDtypeStruct((s, d), F32), jax.ShapeDtypeStruct((s, d), BF16),
                   jax.ShapeDtypeStruct((s, d), BF16), jax.ShapeDtypeStruct((8, LANES), F32)],
        compiler_params=_params(("arbitrary",)))(h2, zg, pp, target)


def _local_step(x, p, target, w, shards=None, unpack_filters=None):
    riding = shards is not None
    major = lambda n, g: g if n in COL_SHARDED else g.reshape(N_CHIPS, -1, g.shape[-1])
    full = lambda n, g: g if n in COL_SHARDED else g.reshape(-1, g.shape[-1])
    first = lambda res: (res[0], res[1:]) if riding else (res, [])

    def ride(names, arrays):
        return ([arrays[n] for n in names], False) if riding else None

    def leave(grads):
        return (grads, True) if riding else None

    def land(names, got):
        return {n: full(n, g) for n, g in zip(names, got)}
    s = x.shape[0]
    t = min(256, s)
    tri = jnp.tril(jnp.ones((t, t), F32))
    tri_incl = tri.astype(BF16)
    tri_excl = jnp.tril(jnp.ones((t, t), F32), -1).astype(BF16)
    bd = jnp.kron(jnp.eye(N_HEADS, dtype=F32), jnp.ones((HEAD_DIM, HEAD_DIM), F32)).astype(BF16)
    qg = jnp.tile(w["q_gain"], (1, N_HEADS))
    kg = jnp.tile(w["k_gain"], (1, N_HEADS))
    pb = p.astype(BF16)

    u1, got = first(_rms_fwd(x, w["g_mix"], "rms_mix", carry=ride(["w_in", "filters"], shards)))
    if riding:
        w = {**w, "w_in": got[0], **unpack_filters(got[1])}
    dw_w = jnp.pad(w["dw_w"], ((0, CONV_HALO - CONV_K), (0, 0)))
    fw = jnp.pad(w["ffn_conv_w"], ((0, FFN_HALO - FFN_K), (0, 0)))
    proj = _mm_nn_sharded(u1, w["w_in"], "mm_in")
    qs, kh, vb = _qkv_prep(proj, qg, kg, bd)
    with_attn, with_conv = ["w_out", "w_up", "w_ple_gate", "w_ple_proj"], ["w_down"]
    o, ob, *got = _attn_fwd(qs, kh, vb, tri_excl, carry=ride(with_attn, shards))
    w = {**w, **land(with_attn, got)}
    c1, c3, *got = _conv_fwd(proj, dw_w, w["dw_b"], w["conv_ln_g"], w["conv_ln_b"], carry=ride(with_conv, shards))
    w = {**w, **land(with_conv, got)}
    mix = jnp.concatenate([ob, c3], axis=1)
    h1 = _mm_nn_full(mix, w["w_out"], "mm_out", res=x)
    u2 = _rms_fwd(h1, w["g_ffn"], "rms_ffn")
    up = _mm_nn_sharded(u2, w["w_up"], "mm_up", tm=_row_tile(s) // 2)
    act = _ffn_act(up, fw, w["ffn_conv_b"])
    h2 = _mm_nn_full(act, w["w_down"], "mm_down", res=h1)
    u3 = _rms_fwd(h2, w["g_ple"], "rms_ple")
    zg = _mm_nn_full(u3, w["w_ple_gate"], "mm_ple_gate")
    pp = _mm_nn_sharded(pb, w["w_ple_proj"], "mm_ple_proj")
    dh3, dpp, dz, sq = _ple_loss(h2, zg, pp, target)

    big = {}
    small = {}
    big["w_ple_proj"] = _mm_tn_sharded(pb, dpp, N_CHIPS, "mm_g_ple_proj")
    big["w_ple_gate"] = _mm_tn_full(u3, dz, "mm_g_ple_gate", tm=u3.shape[1])
    du3 = _mm_nt_full(dz, w["w_ple_gate"], "mm_d_ple_gate")
    dh2, dh2b, small["g_ple"] = _rms_bwd(h2, du3, w["g_ple"], dh3, "rms_ple_bwd")
    slots = {}
    with_down, with_ffn, with_attn = ["w_ple_proj", "w_ple_gate"], ["w_down"], ["w_up", "w_out"]
    leaving = leave([major(n, big.pop(n)) for n in with_down]) if riding else None
    big["w_down"], got = first(_mm_tn_full(act, dh2b, "mm_g_down", tm=act.shape[1] // 2, carry=leaving))
    slots.update(zip(with_down, got))
    dact = _mm_nt_full(dh2b, w["w_down"], "mm_d_down")
    leaving = leave([major(n, big.pop(n)) for n in with_ffn]) if riding else None
    dg2, dup, gfw, small["ffn_conv_b"], *got = _ffn_bwd_act(dact, up, fw, w["ffn_conv_b"], carry=leaving)
    slots.update(zip(with_ffn, got))
    small["ffn_conv_w"] = gfw[:FFN_K]
    dup = _ffn_bwd_conv(dg2, fw, dup)
    big["w_up"] = _mm_tn_sharded(u2, dup, N_CHIPS, "mm_g_up", n_split=2)
    du2, = _mm_nt_sharded(dup, w["w_up"], "mm_d_up")
    dh1, dh1b, small["g_ffn"] = _rms_bwd(h1, du2, w["g_ffn"], dh2, "rms_ffn_bwd")
    big["w_out"] = _mm_tn_full(mix, dh1b, "mm_g_out", tm=mix.shape[1])
    dmix = _mm_nt_full(dh1b, w["w_out"], "mm_d_out")
    dc1, small["conv_ln_g"], small["conv_ln_b"], small["dw_b"] = _conv_bwd_ln(dmix, c1, w["conv_ln_g"], w["conv_ln_b"])
    dcacg, gdw = _conv_bwd_taps(proj, dc1, dw_w)
    small["dw_w"] = gdw[:CONV_K]
    leaving = leave([major(n, big.pop(n)) for n in with_attn]) if riding else None
    dqh, dkh, dv, *got = _attn_bwd(qs, kh, vb, o, dmix, tri_excl, tri_incl, carry=leaving)
    slots.update(zip(with_attn, got))
    dqkv, gq, gk = _qk_bwd(proj, dqh, dkh, dv, qg, kg, bd)
    small["q_gain"] = gq.reshape(N_HEADS, HEAD_DIM).sum(axis=0, keepdims=True)
    small["k_gain"] = gk.reshape(N_HEADS, HEAD_DIM).sum(axis=0, keepdims=True)
    dproj = jnp.concatenate([dqkv, dcacg], axis=1)
    big["w_in"] = _mm_tn_sharded(u1, dproj, N_CHIPS, "mm_g_in")
    half = big["w_in"].shape[1] // 2
    halves = [big["w_in"][:, :half], big.pop("w_in")[:, half:]] if riding else [None, None]
    du1, *got_a = _mm_nt_sharded(dproj, w["w_in"], "mm_d_in", carry=leave([halves[0]]) if riding else None)
    grad_x, _, small["g_mix"], *got_b = _rms_bwd(x, du1, w["g_mix"], dh1, "rms_mix_bwd",
                                                 carry=leave([halves[1]]) if riding else None)
    if riding:
        slots["w_in"] = jnp.concatenate([got_a[0], got_b[0]], axis=1)
    return sq[0, 0], grad_x, big, slots, small


def _exchange(srcs, per_chip, name):
    n = len(srcs)

    def body(*refs):
        src_refs, land_refs = refs[:n], refs[n:2 * n]
        send_sems, recv_sems, local_sems = refs[2 * n:]
        local = _local_copies(src_refs, land_refs, local_sems, per_chip)
        sends = _chip_copies(src_refs, land_refs, send_sems, recv_sems, per_chip, False)
        for cp in local + sends:
            cp.start()
        for cp in _chip_copies(src_refs, land_refs, send_sems, recv_sems, per_chip, True):
            cp.wait_recv()
        for cp in sends:
            cp.wait_send()
        for cp in local:
            cp.wait()

    return pl.pallas_call(
        body, name=name, in_specs=[ANY] * n, out_specs=[ANY] * n, out_shape=_exchanged_shapes(srcs, per_chip),
        scratch_shapes=_exchange_sems(n))(*srcs)


def _finish_exchange(mine, small):
    n = len(mine)

    def body(*refs):
        gin, sin = refs[:n], refs[n]
        gout, sout = refs[n + 1:2 * n + 1], refs[2 * n + 1]
        send_sems, recv_sems, small_send, small_recv, local_sem = refs[2 * n + 2:]
        x, y, c, _ = _position()
        dev = 4 * x + 2 * y + c
        flip = lambda v, bit: 1 - v if bit else v
        others = [(flip(x, k & 4), flip(y, k & 2), flip(c, k & 1)) for k in range(1, N_DEV)]
        local = pltpu.make_async_copy(sin, sout.at[dev], local_sem)
        local.start()
        swaps = [_remote(gin[a], gout[a], send_sems, recv_sems, a, (x, y, 1 - c)) for a in range(n)]
        sends = swaps + [_remote(sin, sout.at[dev], small_send, small_recv, k, peer) for k, peer in enumerate(others)]
        for cp in sends:
            cp.start()
        for cp in swaps:
            cp.wait_recv()
        for k, (px, py, pc) in enumerate(others):
            _remote(sin, sout.at[4 * px + 2 * py + pc], small_send, small_recv, k, (px, py, pc)).wait_recv()
        for cp in sends:
            cp.wait_send()
        local.wait()

    return pl.pallas_call(
        body, name="finish_exchange", in_specs=[ANY] * (n + 1), out_specs=[ANY] * (n + 1),
        out_shape=[jax.ShapeDtypeStruct(g.shape, g.dtype) for g in mine]
        + [jax.ShapeDtypeStruct((N_DEV,) + small.shape, small.dtype)],
        scratch_shapes=[pltpu.SemaphoreType.DMA((n,)), pltpu.SemaphoreType.DMA((n,)),
                        pltpu.SemaphoreType.DMA((N_DEV - 1,)), pltpu.SemaphoreType.DMA((N_DEV - 1,)),
                        pltpu.SemaphoreType.DMA])(*mine, small)


def _elem_tile(rows):
    return 128 if rows % 128 == 0 else (64 if rows % 64 == 0 else rows)


def _sum_slots(a, name):
    g, r, c = a.shape
    tr = _elem_tile(r)

    def body(a_ref, o_ref):
        acc = a_ref[0]
        for k in range(1, g):
            acc = acc + a_ref[k]
        o_ref[...] = acc

    return pl.pallas_call(
        body, name=name, grid=(r // tr,), in_specs=[pl.BlockSpec((g, tr, c), lambda i: (0, i, 0))],
        out_specs=pl.BlockSpec((tr, c), lambda i: (i, 0)), out_shape=jax.ShapeDtypeStruct((r, c), a.dtype),
        compiler_params=_params(("parallel",)))(a)


def _adamw(wt, ga, gb, m, v, name):
    r, c = wt.shape
    tr = _elem_tile(r)
    two = gb is not None

    def body(*refs):
        if two:
            w_ref, ga_ref, gb_ref, m_ref, v_ref, g_out, d_out, m_out, v_out = refs
            g = ga_ref[...] + gb_ref[...]
        else:
            w_ref, ga_ref, m_ref, v_ref, g_out, d_out, m_out, v_out = refs
            g = ga_ref[...]
        mn = ADAM_B1 * m_ref[...] + (1.0 - ADAM_B1) * g
        vn = ADAM_B2 * v_ref[...] + (1.0 - ADAM_B2) * (g * g)
        m_hat = mn / (1.0 - ADAM_B1 ** ADAM_STEP)
        v_hat = vn / (1.0 - ADAM_B2 ** ADAM_STEP)
        g_out[...] = g
        d_out[...] = -ADAM_LR * (m_hat / (jnp.sqrt(v_hat) + ADAM_EPS) + ADAM_WD * w_ref[...])
        m_out[...] = mn
        v_out[...] = vn

    blk = pl.BlockSpec((tr, c), lambda i: (i, 0))
    args = [wt, ga] + ([gb] if two else []) + [m, v]
    sds = jax.ShapeDtypeStruct((r, c), F32)
    return pl.pallas_call(
        body, name=name, grid=(r // tr,), in_specs=[blk] * len(args), out_specs=[blk] * 4, out_shape=[sds] * 4,
        compiler_params=_params(("parallel",)))(*args)


def _pack(arrs, rows):
    flat = jnp.concatenate([a.reshape(-1) for a in arrs])
    return jnp.pad(flat, (0, rows * LANES - flat.shape[0])).reshape(rows, LANES)


def _unpack(buf, shapes):
    flat = buf.reshape(-1)
    out, off = [], 0
    for shp in shapes:
        size = 1
        for d in shp:
            size *= d
        out.append(flat[off:off + size].reshape(shp))
        off += size
    return out


BIG = ["w_in", "w_out", "w_up", "w_down", "w_ple_gate", "w_ple_proj"]
COL_SHARDED = ["w_in", "w_up", "w_ple_proj"]
SMALL_REPL = ["g_mix", "q_gain", "k_gain", "dw_b", "conv_ln_g", "conv_ln_b", "g_ffn", "ffn_conv_b", "g_ple"]
SMALL_SHARDED = ["dw_w", "ffn_conv_w"]
WEIGHTS = ["g_mix", "w_in", "q_gain", "k_gain", "dw_w", "dw_b", "conv_ln_g", "conv_ln_b", "w_out", "g_ffn", "w_up",
           "ffn_conv_w", "ffn_conv_b", "w_down", "g_ple", "w_ple_gate", "w_ple_proj"]


def _rows_for(n_elems):
    return -(-n_elems // (8 * LANES)) * 8


def kernel(x, p, g_mix, w_in, q_gain, k_gain, dw_w, dw_b, conv_ln_g, conv_ln_b, w_out, g_ffn, w_up, ffn_conv_w, ffn_conv_b, w_down, g_ple, w_ple_gate, w_ple_proj, loss_target, m_g_mix, m_w_in, m_q_gain, m_k_gain, m_dw_w, m_dw_b, m_conv_ln_g, m_conv_ln_b, m_w_out, m_g_ffn, m_w_up, m_ffn_conv_w, m_ffn_conv_b, m_w_down, m_g_ple, m_w_ple_gate, m_w_ple_proj, v_g_mix, v_w_in, v_q_gain, v_k_gain, v_dw_w, v_dw_b, v_conv_ln_g, v_conv_ln_b, v_w_out, v_g_ffn, v_w_up, v_ffn_conv_w, v_ffn_conv_b, v_w_down, v_g_ple, v_w_ple_gate, v_w_ple_proj):
    given = dict(locals())
    strip = lambda n, a: a if n in SMALL_REPL else a[0]
    wts = {n: strip(n, given[n]) for n in WEIGHTS}
    mom = {n: strip(n, given["m_" + n]) for n in WEIGHTS}
    var = {n: strip(n, given["v_" + n]) for n in WEIGHTS}
    chip = 2 * lax.axis_index("x") + lax.axis_index("y")

    small_shard_shapes = [wts[n].shape for n in SMALL_SHARDED]
    filt_rows = _rows_for(sum(wts[n].size for n in SMALL_SHARDED))
    shards = {n: wts[n].astype(BF16) for n in BIG}
    shards["filters"] = _pack([wts[n] for n in SMALL_SHARDED], filt_rows)

    def unpack_filters(filt_all):
        per_chip = [_unpack(filt_all[k], small_shard_shapes) for k in range(N_CHIPS)]
        return {n: jnp.concatenate([per_chip[k][idx] for k in range(N_CHIPS)], axis=1)
                for idx, n in enumerate(SMALL_SHARDED)}

    sq, grad_x, big, slots, small = _local_step(x[0], p[0, 0], loss_target[0], {n: wts[n] for n in SMALL_REPL},
                                                shards, unpack_filters)
    loss = lax.psum(sq * (0.5 / x.shape[-1]), ("x", "y", "c"))

    small_names = SMALL_REPL + SMALL_SHARDED
    small_shapes = [small[n].shape for n in small_names]
    small_rows = _rows_for(sum(small[n].size for n in small_names))
    mine = [_sum_slots(slots[n], "sum_" + n) for n in BIG]
    *theirs, small_slots = _finish_exchange(mine, _pack([small[n] for n in small_names], small_rows))
    small_sum = dict(zip(small_names, _unpack(_sum_slots(small_slots, "sum_small"), small_shapes)))

    outs = {}
    for n, ga, gb in zip(BIG, mine, theirs):
        outs[n] = _adamw(wts[n], ga, gb, mom[n], var[n], "adamw_" + n)
    for n in SMALL_SHARDED:
        width = wts[n].shape[1]
        small_sum[n] = lax.dynamic_slice_in_dim(small_sum[n], chip * width, width, axis=1)
    local_shapes = [wts[n].shape for n in small_names]
    local_rows = _rows_for(sum(wts[n].size for n in small_names))
    packed = _adamw(_pack([wts[n] for n in small_names], local_rows), _pack([small_sum[n] for n in small_names], local_rows),
                    None, _pack([mom[n] for n in small_names], local_rows),
                    _pack([var[n] for n in small_names], local_rows), "adamw_small")
    unpacked = [_unpack(buf, local_shapes) for buf in packed]
    for idx, n in enumerate(small_names):
        outs[n] = [u[idx] for u in unpacked]
    result = [loss, grad_x[None]]
    for part in range(4):
        result += [outs[n][part] if n in SMALL_REPL else outs[n][part][None] for n in WEIGHTS]
    return tuple(result)
```

```python
import functools

import jax
import jax.numpy as jnp
from jax import lax
from jax.experimental import pallas as pl
from jax.experimental.pallas import tpu as pltpu

F32 = jnp.float32
BF16 = jnp.bfloat16
HIGHEST = lax.Precision.HIGHEST
MESH = pl.DeviceIdType.MESH
ANY = pl.BlockSpec(memory_space=pl.ANY)

EPS = 1e-6
HEAD_DIM = 64
N_HEADS = 8
ATTN_W = 512
CONV_W = 512
CONV_K = 31
FFN_K = 3
ATTN_SCALE = 0.125
LANES = 128
CONV_HALO = 32
FFN_HALO = 16
VMEM_LIMIT = 56 * 1024 * 1024

ADAM_LR = 0.001
ADAM_B1 = 0.9
ADAM_B2 = 0.999
ADAM_EPS = 1e-08
ADAM_WD = 0.01
ADAM_STEP = 10

N_CHIPS = 4
N_DEV = 8


def _params(sem):
    return pltpu.CompilerParams(dimension_semantics=sem, vmem_limit_bytes=VMEM_LIMIT)


def _row_tile(s):
    return min(512, s)


def _position():
    x, y, c = lax.axis_index("x"), lax.axis_index("y"), lax.axis_index("c")
    return x, y, c, [(1 - x, y), (x, 1 - y), (1 - x, 1 - y)]


def _remote(src, dst, send_sems, recv_sems, k, to):
    return pltpu.make_async_remote_copy(src_ref=src, dst_ref=dst, send_sem=send_sems.at[k], recv_sem=recv_sems.at[k],
                                        device_id=to, device_id_type=MESH)


def _chip_copies(src_refs, land_refs, send_sems, recv_sems, per_chip, landed):
    x, y, c, chips = _position()
    me = 2 * x + y
    out = []
    for a, (src, land) in enumerate(zip(src_refs, land_refs)):
        for j, (px, py) in enumerate(chips):
            peer = 2 * px + py
            out.append(_remote(src.at[peer] if per_chip else src, land.at[peer if landed else me],
                               send_sems, recv_sems, 3 * a + j, (px, py, c)))
    return out


def _local_copies(src_refs, land_refs, local_sems, per_chip):
    x, y, _, _ = _position()
    me = 2 * x + y
    return [pltpu.make_async_copy(src.at[me] if per_chip else src, land.at[me], local_sems.at[a])
            for a, (src, land) in enumerate(zip(src_refs, land_refs))]


def _exchanged_shapes(srcs, per_chip):
    return [jax.ShapeDtypeStruct(a.shape if per_chip else (N_CHIPS,) + a.shape, a.dtype) for a in srcs]


def _exchange_sems(n):
    return [pltpu.SemaphoreType.DMA((3 * n,)), pltpu.SemaphoreType.DMA((3 * n,)), pltpu.SemaphoreType.DMA((n,))]


def _call(body, *, name, grid, in_specs, out_specs, out_shape, scratch_shapes, args, semantics, carry=None):
    if carry is None:
        return pl.pallas_call(body, name=name, grid=grid, in_specs=in_specs, out_specs=out_specs, out_shape=out_shape,
                              scratch_shapes=scratch_shapes, compiler_params=_params(semantics))(*args)
    srcs, per_chip = carry
    n, n_in, n_out, n_scr = len(srcs), len(in_specs), len(out_specs), len(scratch_shapes)

    def wrapped(*refs):
        ins, xin = refs[:n_in], refs[n_in:n_in + n]
        outs, xout = refs[n_in + n:n_in + n + n_out], refs[n_in + n + n_out:n_in + 2 * n + n_out]
        scratch = refs[n_in + 2 * n + n_out:n_in + 2 * n + n_out + n_scr]
        send_sems, recv_sems, local_sems = refs[-3:]
        first = functools.reduce(jnp.logical_and, [pl.program_id(d) == 0 for d in range(len(grid))])
        last = functools.reduce(jnp.logical_and, [pl.program_id(d) == g - 1 for d, g in enumerate(grid)])

        @pl.when(first)
        def _():
            for cp in _local_copies(xin, xout, local_sems, per_chip):
                cp.start()
            for cp in _chip_copies(xin, xout, send_sems, recv_sems, per_chip, False):
                cp.start()

        body(*ins, *outs, *scratch)

        @pl.when(last)
        def _():
            for cp in _chip_copies(xin, xout, send_sems, recv_sems, per_chip, True):
                cp.wait_recv()
            for cp in _chip_copies(xin, xout, send_sems, recv_sems, per_chip, False):
                cp.wait_send()
            for cp in _local_copies(xin, xout, local_sems, per_chip):
                cp.wait()

    return pl.pallas_call(
        wrapped, name=name, grid=grid, in_specs=list(in_specs) + [ANY] * n, out_specs=list(out_specs) + [ANY] * n,
        out_shape=list(out_shape) + _exchanged_shapes(srcs, per_chip),
        scratch_shapes=list(scratch_shapes) + _exchange_sems(n),
        compiler_params=_params(("arbitrary",) * len(grid)))(*args, *srcs)


def _contract_tile(s):
    return min(1024, s)


def _rstd(x):
    return lax.rsqrt(jnp.mean(x * x, axis=-1, keepdims=True) + EPS)


def _sigmoid(x):
    return 1.0 / (1.0 + jnp.exp(-x))


def _mm(a, b, *, name, dims, grid, a_spec, b_spec, o_spec, o_tile, out_shape, res=None, res_spec=None, carry=None):
    nk = grid[2]

    def body(*refs):
        if res is None:
            a_ref, b_ref, o_ref, acc_ref = refs
            r_ref = None
        else:
            a_ref, b_ref, r_ref, o_ref, acc_ref = refs
        part = lax.dot_general(a_ref[...], b_ref[...], (dims, ((), ())), preferred_element_type=F32)

        def finish(val):
            if r_ref is not None:
                val = val + r_ref[...]
            o_ref[...] = val.astype(o_ref.dtype)

        if nk == 1:
            finish(part)
        else:
            k = pl.program_id(2)

            @pl.when(k == 0)
            def _():
                acc_ref[...] = part

            @pl.when(k > 0)
            def _():
                acc_ref[...] += part

            @pl.when(k == nk - 1)
            def _():
                finish(acc_ref[...])

    in_specs = [a_spec, b_spec]
    args = [a, b]
    if res is not None:
        in_specs.append(res_spec)
        args.append(res)
    acc_tile = o_tile if nk > 1 else (8, LANES)
    out = _call(body, name=name, grid=grid, in_specs=in_specs, out_specs=[o_spec], out_shape=[out_shape],
                scratch_shapes=[pltpu.VMEM(acc_tile, F32)], args=args,
                semantics=("parallel", "parallel", "arbitrary"), carry=carry)
    return out[0] if carry is None else out


NN = ((1,), (0,))
NT = ((1,), (1,))
TN = ((0,), (0,))


def _mm_nn_sharded(a, bg, name, out_dtype=F32, tm=None):
    s, k = a.shape
    g, _, ns = bg.shape
    tm = tm or _row_tile(s)

    def body(a_ref, b_ref, o_ref):
        av = a_ref[...]
        for gi in range(g):
            o_ref[:, gi * ns:(gi + 1) * ns] = jnp.dot(av, b_ref[gi], preferred_element_type=F32).astype(out_dtype)

    return pl.pallas_call(
        body, name=name, grid=(s // tm,),
        in_specs=[pl.BlockSpec((tm, k), lambda i: (i, 0)), pl.BlockSpec((g, k, ns), lambda i: (0, 0, 0))],
        out_specs=pl.BlockSpec((tm, g * ns), lambda i: (i, 0)),
        out_shape=jax.ShapeDtypeStruct((s, g * ns), out_dtype), compiler_params=_params(("parallel",)))(a, bg)


def _mm_nn_full(a, b, name, res=None):
    s, k = a.shape
    n = b.shape[1]
    tm = _row_tile(s)
    rs = pl.BlockSpec((tm, n), lambda i, j, kk: (i, 0))
    return _mm(a, b, name=name, dims=NN, grid=(s // tm, 1, 1),
               a_spec=pl.BlockSpec((tm, k), lambda i, j, kk: (i, 0)),
               b_spec=pl.BlockSpec((k, n), lambda i, j, kk: (0, 0)),
               o_spec=rs, o_tile=(tm, n), out_shape=jax.ShapeDtypeStruct((s, n), F32),
               res=res, res_spec=rs if res is not None else None)


def _mm_nt_full(a, b, name):
    s, n = a.shape
    k = b.shape[0]
    tm = _row_tile(s)
    return _mm(a, b, name=name, dims=NT, grid=(s // tm, 1, 1),
               a_spec=pl.BlockSpec((tm, n), lambda i, j, kk: (i, 0)),
               b_spec=pl.BlockSpec((k, n), lambda i, j, kk: (0, 0)),
               o_spec=pl.BlockSpec((tm, k), lambda i, j, kk: (i, 0)), o_tile=(tm, k),
               out_shape=jax.ShapeDtypeStruct((s, k), F32))


def _mm_nt_sharded(a, bg, name, carry=None):
    s = a.shape[0]
    g, k, ns = bg.shape
    tm = _row_tile(s)

    def body(a_ref, b_ref, o_ref):
        acc = lax.dot_general(a_ref[:, 0:ns], b_ref[0], (NT, ((), ())), preferred_element_type=F32)
        for gi in range(1, g):
            acc = acc + lax.dot_general(a_ref[:, gi * ns:(gi + 1) * ns], b_ref[gi], (NT, ((), ())),
                                        preferred_element_type=F32)
        o_ref[...] = acc

    return _call(
        body, name=name, grid=(s // tm,),
        in_specs=[pl.BlockSpec((tm, g * ns), lambda i: (i, 0)), pl.BlockSpec((g, k, ns), lambda i: (0, 0, 0))],
        out_specs=[pl.BlockSpec((tm, k), lambda i: (i, 0))], out_shape=[jax.ShapeDtypeStruct((s, k), F32)],
        scratch_shapes=[], args=(a, bg), semantics=("parallel",), carry=carry)


def _mm_tn_sharded(a, b, g, name, n_split=1):
    s, k = a.shape
    ns = b.shape[1] // g
    gs = g // n_split
    tk = _contract_tile(s)

    def body(a_ref, b_ref, o_ref):
        first = pl.program_id(1) == 0
        at = a_ref[...].T
        for gi in range(gs):
            part = jnp.dot(at, b_ref[:, gi * ns:(gi + 1) * ns], preferred_element_type=F32)

            @pl.when(first)
            def _(gi=gi, part=part):
                o_ref[gi] = part

            @pl.when(jnp.logical_not(first))
            def _(gi=gi, part=part):
                o_ref[gi] += part

    return pl.pallas_call(
        body, name=name, grid=(n_split, s // tk),
        in_specs=[pl.BlockSpec((tk, k), lambda j, kk: (kk, 0)), pl.BlockSpec((tk, gs * ns), lambda j, kk: (kk, j))],
        out_specs=pl.BlockSpec((gs, k, ns), lambda j, kk: (j, 0, 0)),
        out_shape=jax.ShapeDtypeStruct((g, k, ns), F32), compiler_params=_params(("parallel", "arbitrary")))(a, b)


def _mm_tn_full(a, b, name, tm, carry=None):
    s, m = a.shape
    n = b.shape[1]
    tk = _contract_tile(s)
    return _mm(a, b, name=name, dims=TN, grid=(m // tm, 1, s // tk),
               a_spec=pl.BlockSpec((tk, tm), lambda i, j, kk: (kk, i)),
               b_spec=pl.BlockSpec((tk, n), lambda i, j, kk: (kk, 0)),
               o_spec=pl.BlockSpec((tm, n), lambda i, j, kk: (i, 0)), o_tile=(tm, n),
               out_shape=jax.ShapeDtypeStruct((m, n), F32), carry=carry)


def _rms_fwd(x, g, name, carry=None):
    s, d = x.shape
    ts = _row_tile(s)

    def body(x_ref, g_ref, u_ref):
        xv = x_ref[...]
        u_ref[...] = (xv * _rstd(xv) * g_ref[...]).astype(BF16)

    row = pl.BlockSpec((ts, d), lambda i: (i, 0))
    out = _call(body, name=name, grid=(s // ts,), in_specs=[row, pl.BlockSpec((1, d), lambda i: (0, 0))],
                out_specs=[row], out_shape=[jax.ShapeDtypeStruct((s, d), BF16)], scratch_shapes=[], args=(x, g),
                semantics=("parallel",), carry=carry)
    return out[0] if carry is None else out


def _rms_bwd(h, du, g, dh_in, name, carry=None):
    s, d = h.shape
    ts = _row_tile(s)

    def body(h_ref, du_ref, g_ref, dhin_ref, dh_ref, dhb_ref, gg_ref):
        hv = h_ref[...]
        r = _rstd(hv)
        xhat = hv * r
        duv = du_ref[...]
        dxhat = duv * g_ref[...]
        m = jnp.mean(dxhat * xhat, axis=-1, keepdims=True)
        dh = dhin_ref[...] + r * (dxhat - xhat * m)
        dh_ref[...] = dh
        dhb_ref[...] = dh.astype(BF16)
        part = jnp.sum(duv * xhat, axis=0, keepdims=True)

        @pl.when(pl.program_id(0) == 0)
        def _():
            gg_ref[...] = part

        @pl.when(pl.program_id(0) > 0)
        def _():
            gg_ref[...] += part

    row = pl.BlockSpec((ts, d), lambda i: (i, 0))
    vec = pl.BlockSpec((1, d), lambda i: (0, 0))
    return _call(
        body, name=name, grid=(s // ts,), in_specs=[row, row, vec, row], out_specs=[row, row, vec],
        out_shape=[jax.ShapeDtypeStruct((s, d), F32), jax.ShapeDtypeStruct((s, d), BF16),
                   jax.ShapeDtypeStruct((1, d), F32)],
        scratch_shapes=[], args=(h, du, g, dh_in), semantics=("arbitrary",), carry=carry)


def _head_sum(x, bd):
    return _tri_dot(_split(x), bd)


def _qkv_prep(proj, qg, kg, bd):
    s = proj.shape[0]
    ts = _row_tile(s)

    def body(q_ref, k_ref, v_ref, qg_ref, kg_ref, bd_ref, qs_ref, kh_ref, vb_ref):
        def norm(x, gain):
            ms = _head_sum(x * x, bd_ref[...]) * (1.0 / HEAD_DIM)
            return x * lax.rsqrt(ms + EPS) * gain

        qs_ref[...] = (norm(q_ref[...], qg_ref[...]) * ATTN_SCALE).astype(BF16)
        kh_ref[...] = norm(k_ref[...], kg_ref[...]).astype(BF16)
        vb_ref[...] = v_ref[...].astype(BF16)

    col = lambda c: pl.BlockSpec((ts, ATTN_W), lambda i: (i, c))
    vec = pl.BlockSpec((1, ATTN_W), lambda i: (0, 0))
    out = pl.BlockSpec((ts, ATTN_W), lambda i: (i, 0))
    sds = jax.ShapeDtypeStruct((s, ATTN_W), BF16)
    return pl.pallas_call(
        body, name="qkv_prep", grid=(s // ts,),
        in_specs=[col(0), col(1), col(2), vec, vec, pl.BlockSpec((ATTN_W, ATTN_W), lambda i: (0, 0))],
        out_specs=[out, out, out], out_shape=[sds, sds, sds],
        compiler_params=_params(("parallel",)))(proj, proj, proj, qg, kg, bd)


def _qk_bwd(proj, dqh, dkh, dv, qg, kg, bd):
    s = proj.shape[0]
    ts = _row_tile(s)

    def body(q_ref, k_ref, dqh_ref, dkh_ref, dv_ref, qg_ref, kg_ref, bd_ref, out_ref, gq_ref, gk_ref):
        first = pl.program_id(0) == 0

        def bwd(x, dy, gain, gg_ref):
            ms = _head_sum(x * x, bd_ref[...]) * (1.0 / HEAD_DIM)
            r = lax.rsqrt(ms + EPS)
            xhat = x * r
            dxhat = dy * gain
            m = _head_sum(dxhat * xhat, bd_ref[...]) * (1.0 / HEAD_DIM)
            part = jnp.sum(dy * xhat, axis=0, keepdims=True)

            @pl.when(first)
            def _():
                gg_ref[...] = part

            @pl.when(jnp.logical_not(first))
            def _():
                gg_ref[...] += part

            return r * (dxhat - xhat * m)

        out_ref[:, 0:ATTN_W] = bwd(q_ref[...], dqh_ref[...], qg_ref[...], gq_ref).astype(BF16)
        out_ref[:, ATTN_W:2 * ATTN_W] = bwd(k_ref[...], dkh_ref[...], kg_ref[...], gk_ref).astype(BF16)
        out_ref[:, 2 * ATTN_W:3 * ATTN_W] = dv_ref[...].astype(BF16)

    col = lambda c: pl.BlockSpec((ts, ATTN_W), lambda i: (i, c))
    row = pl.BlockSpec((ts, ATTN_W), lambda i: (i, 0))
    vec = pl.BlockSpec((1, ATTN_W), lambda i: (0, 0))
    return pl.pallas_call(
        body, name="qk_bwd", grid=(s // ts,),
        in_specs=[col(0), col(1), row, row, row, vec, vec, pl.BlockSpec((ATTN_W, ATTN_W), lambda i: (0, 0))],
        out_specs=[pl.BlockSpec((ts, 3 * ATTN_W), lambda i: (i, 0)), vec, vec],
        out_shape=[jax.ShapeDtypeStruct((s, 3 * ATTN_W), BF16), jax.ShapeDtypeStruct((1, ATTN_W), F32),
                   jax.ShapeDtypeStruct((1, ATTN_W), F32)],
        compiler_params=_params(("arbitrary",)))(proj, proj, dqh, dkh, dv, qg, kg, bd)


def _split(x):
    hi = x.astype(BF16)
    return hi, (x - hi.astype(F32)).astype(BF16)


def _tri_dot(parts, tri):
    hi, lo = parts
    return jnp.dot(hi, tri, preferred_element_type=F32) + jnp.dot(lo, tri, preferred_element_type=F32)


def _log_sigmoids(z):
    neg_abs = lax.bitcast_convert_type(lax.bitcast_convert_type(z, jnp.uint32) | jnp.uint32(0x80000000), F32)
    lb = jnp.minimum(z, 0.0) - jnp.log(1.0 + jnp.exp(neg_abs))
    return lb, lb - z


DEAD_LOG_WEIGHT = -106.0


def _sweep_key_blocks(tiles, alive, i):
    @pl.when(i == 0)
    def _():
        tiles([0], [True])

    @pl.when(i > 0)
    def _():
        tiles([i, i - 1], [True, False])

    def more(state):
        kb, live = state
        return jnp.logical_and(kb >= 0, live > 0)

    def step(state):
        kb, _ = state
        tiles([kb], [False])
        return kb - 1, alive().astype(jnp.int32)

    lax.while_loop(more, step, (i - 2, alive().astype(jnp.int32)))


def _head_masks():
    lane = lax.broadcasted_iota(jnp.int32, (1, LANES), 1)
    return [lane < HEAD_DIM, lane >= HEAD_DIM]


def _attn_fwd(qs, kh, vb, tri_excl, carry=None):
    s = qs.shape[0]
    t = tri_excl.shape[0]
    nq = s // t

    def body(q_ref, k_ref, v_ref, tri_ref, o_ref, ob_ref, acc_ref, c_ref):
        i = pl.program_id(1)
        hmask = _head_masks()
        q = q_ref[...]
        qm = [jnp.where(hm, q, jnp.zeros_like(q)) for hm in hmask]
        acc_ref[...] = jnp.zeros_like(acc_ref)
        c_ref[...] = jnp.zeros_like(c_ref)
        causal = (lax.broadcasted_iota(jnp.int32, (t, t), 1) < lax.broadcasted_iota(jnp.int32, (t, t), 0))

        def tiles(kbs, masked):
            tri = tri_ref[...]
            starts = [pl.multiple_of(kb * t, t) for kb in kbs]
            kblks = [k_ref[pl.ds(k0, t), :] for k0 in starts]
            vblks = [v_ref[pl.ds(k0, t), :] for k0 in starts]
            chains = [(j, h) for j in range(len(kbs)) for h in range(2)]
            carry = [c_ref[h] for h in range(2)]
            pv = [None, None]
            lbs, loms, between = {}, {}, {}
            for step in range(len(chains) + 2):
                if step < len(chains):
                    j, h = chains[step]
                    z = lax.dot_general(qm[h], kblks[j], (NT, ((), ())), preferred_element_type=F32)
                    lbs[step], lom = _log_sigmoids(z)
                    loms[step] = jnp.where(causal, lom, 0.0) if masked[j] else lom
                if 0 <= step - 1 < len(chains):
                    between[step - 1] = _tri_dot(_split(loms[step - 1]), tri)
                if 0 <= step - 2 < len(chains):
                    n = step - 2
                    j, h = chains[n]
                    w = jnp.exp(lbs[n] + between[n] + carry[h])
                    if masked[j]:
                        w = jnp.where(causal, w, 0.0)
                    carry[h] = carry[h] + jnp.sum(loms[n], axis=-1, keepdims=True)
                    part = jnp.dot(w.astype(BF16), vblks[j], preferred_element_type=F32)
                    pv[h] = part if pv[h] is None else pv[h] + part
            for h in range(2):
                c_ref[h] = carry[h]
                acc_ref[h] += pv[h]

        _sweep_key_blocks(tiles, lambda: jnp.max(c_ref[...]) > DEAD_LOG_WEIGHT, i)
        o = jnp.where(hmask[0], acc_ref[0], acc_ref[1])
        o_ref[...] = o
        ob_ref[...] = o.astype(BF16)

    qspec = pl.BlockSpec((t, LANES), lambda hp, i: (i, hp))
    kspec = pl.BlockSpec((s, LANES), lambda hp, i: (0, hp))
    return _call(
        body, name="attn_fwd", grid=(ATTN_W // LANES, nq),
        in_specs=[qspec, kspec, kspec, pl.BlockSpec((t, t), lambda hp, i: (0, 0))],
        out_specs=[qspec, qspec],
        out_shape=[jax.ShapeDtypeStruct((s, ATTN_W), F32), jax.ShapeDtypeStruct((s, ATTN_W), BF16)],
        scratch_shapes=[pltpu.VMEM((2, t, LANES), F32), pltpu.VMEM((2, t, 1), F32)],
        args=(qs, kh, vb, tri_excl), semantics=("parallel", "arbitrary"), carry=carry)


def _attn_bwd(qs, kh, vb, o, dmix, tri_excl, tri_incl, carry=None):
    s = qs.shape[0]
    t = tri_excl.shape[0]
    nq = s // t

    def body(q_ref, k_ref, v_ref, o_ref, do_ref, te_ref, ti_ref, dq_ref, dk_ref, dv_ref, dqacc_ref, c_ref):
        i = pl.program_id(1)

        @pl.when(i == 0)
        def _():
            dk_ref[...] = jnp.zeros_like(dk_ref)
            dv_ref[...] = jnp.zeros_like(dv_ref)

        hmask = _head_masks()
        q = q_ref[...]
        do = do_ref[...]
        dob = do.astype(BF16)
        prod = dob.astype(F32) * o_ref[...]
        qm =[jnp.where(hm, q, jnp.zeros_like(q)) for hm in hmask]
        dom = [jnp.where(hm, dob, jnp.zeros_like(dob)) for hm in hmask]
        total = [jnp.sum(jnp.where(hm, prod, 0.0), axis=-1, keepdims=True) for hm in hmask]
        dqacc_ref[...] = jnp.zeros_like(dqacc_ref)
        c_ref[...] = jnp.zeros_like(c_ref)
        causal = (lax.broadcasted_iota(jnp.int32, (t, t), 1) < lax.broadcasted_iota(jnp.int32, (t, t), 0))

        def tiles(kbs, masked):
            te = te_ref[...]
            ti = ti_ref[...]
            starts = [pl.multiple_of(kb * t, t) for kb in kbs]
            kblks = [k_ref[pl.ds(k0, t), :] for k0 in starts]
            vblks = [v_ref[pl.ds(k0, t), :] for k0 in starts]
            chains = [(j, h) for j in range(len(kbs)) for h in range(2)]
            c_lom = [c_ref[2 * h] for h in range(2)]
            c_g = [c_ref[2 * h + 1] for h in range(2)]
            lbs, loms, dws, between, wbs, gs, g_after = {}, {}, {}, {}, {}, {}, {}
            dq = [None, None]
            dk = [None] * len(kbs)
            dv = [None] * len(kbs)
            add = lambda acc, part: part if acc is None else acc + part
            for step in range(len(chains) + 3):
                if step < len(chains):
                    j, h = chains[step]
                    z = lax.dot_general(qm[h], kblks[j], (NT, ((), ())), preferred_element_type=F32)
                    dws[step] = lax.dot_general(dom[h], vblks[j], (NT, ((), ())), preferred_element_type=F32)
                    lbs[step], lom = _log_sigmoids(z)
                    loms[step] = jnp.where(causal, lom, 0.0) if masked[j] else lom
                if 0 <= step - 1 < len(chains):
                    between[step - 1] = _tri_dot(_split(loms[step - 1]), te)
                if 0 <= step - 2 < len(chains):
                    n = step - 2
                    j, h = chains[n]
                    w = jnp.exp(lbs[n] + between[n] + c_lom[h])
                    if masked[j]:
                        w = jnp.where(causal, w, 0.0)
                    c_lom[h] = c_lom[h] + jnp.sum(loms[n], axis=-1, keepdims=True)
                    wbs[n] = w.astype(BF16)
                    gs[n] = dws[n] * wbs[n].astype(F32)
                    g_after[n] = _tri_dot(_split(gs[n]), ti)
                if 0 <= step - 3 < len(chains):
                    n = step - 3
                    j, h = chains[n]
                    beta = jnp.exp(lbs[n])
                    dz = gs[n] * (1.0 - beta) - beta * (total[h] - (g_after[n] + c_g[h]))
                    if masked[j]:
                        dz = jnp.where(causal, dz, 0.0)
                    c_g[h] = c_g[h] + jnp.sum(gs[n], axis=-1, keepdims=True)
                    dzb = dz.astype(BF16)
                    dq[h] = add(dq[h], jnp.dot(dzb, kblks[j], preferred_element_type=F32))
                    dk[j] = add(dk[j], lax.dot_general(dzb, qm[h], (TN, ((), ())), preferred_element_type=F32))
                    dv[j] = add(dv[j], lax.dot_general(wbs[n], dom[h], (TN, ((), ())), preferred_element_type=F32))
            for h in range(2):
                c_ref[2 * h] = c_lom[h]
                c_ref[2 * h + 1] = c_g[h]
                dqacc_ref[h] += dq[h]
            for j, k0 in enumerate(starts):
                dk_ref[pl.ds(k0, t), :] += dk[j]
                dv_ref[pl.ds(k0, t), :] += dv[j]

        _sweep_key_blocks(tiles, lambda: jnp.maximum(jnp.max(c_ref[0]), jnp.max(c_ref[2])) > DEAD_LOG_WEIGHT, i)
        dq_ref[...] = jnp.where(hmask[0], dqacc_ref[0], dqacc_ref[1]) * ATTN_SCALE

    qspec = pl.BlockSpec((t, LANES), lambda hp, i: (i, hp))
    kspec = pl.BlockSpec((s, LANES), lambda hp, i: (0, hp))
    tspec = pl.BlockSpec((t, t), lambda hp, i: (0, 0))
    sds = jax.ShapeDtypeStruct((s, ATTN_W), F32)
    return _call(
        body, name="attn_bwd", grid=(ATTN_W // LANES, nq),
        in_specs=[qspec, kspec, kspec, qspec, qspec, tspec, tspec],
        out_specs=[qspec, kspec, kspec], out_shape=[sds, sds, sds],
        scratch_shapes=[pltpu.VMEM((2, t, LANES), F32), pltpu.VMEM((4, t, 1), F32)],
        args=(qs, kh, vb, o, dmix, tri_excl, tri_incl), semantics=("parallel", "arbitrary"), carry=carry)


CONV_ROWS = 64
CONV_COLS = 256


SUBLANES = 8


def _shift_copies(src_ref, sh_ref):
    length = sh_ref.shape[1]
    for r in range(1, SUBLANES):
        sh_ref[r - 1] = src_ref[r:r + length, :]


def _shift_scratch(ts):
    return pltpu.VMEM((SUBLANES - 1, ts + CONV_HALO - SUBLANES, CONV_W), F32)


def _rows_at(src_ref, sh_ref, offset, r0, rows, cols):
    r = offset % SUBLANES
    base = offset - r + r0
    return src_ref[base:base + rows, cols] if r == 0 else sh_ref[r - 1, base:base + rows, cols]


def _taps(src_ref, sh_ref, w_ref, n_taps, first_row, rows, reverse=False):
    width = src_ref.shape[1]
    cols = min(CONV_COLS, width)
    out = []
    for r0 in range(0, rows, CONV_ROWS):
        for c0 in range(0, width, cols):
            acc = jnp.zeros((CONV_ROWS, cols), F32)
            for k in range(n_taps):
                off = (n_taps - 1 - k) if reverse else k
                acc = acc + w_ref[k:k + 1, c0:c0 + cols] * _rows_at(src_ref, sh_ref, first_row + off, r0, CONV_ROWS,
                                                                    slice(c0, c0 + cols))
            out.append(((r0, c0), acc))
    return out


def _conv_fwd(proj, dw_w, dw_b, ln_g, ln_b, carry=None):
    s = proj.shape[0]
    ts = _row_tile(s)
    hb = ts // CONV_HALO

    def body(a_ref, g_ref, ha_ref, hg_ref, w_ref, b_ref, lg_ref, lb_ref, c1_ref, c3_ref, pad_ref, sh_ref):
        i = pl.program_id(0)
        halo = ha_ref[...] * _sigmoid(hg_ref[...])
        pad_ref[0:CONV_HALO, :] = jnp.where(i > 0, halo, 0.0)
        pad_ref[CONV_HALO:, :] = a_ref[...] * _sigmoid(g_ref[...])
        _shift_copies(pad_ref, sh_ref)
        first = CONV_HALO - (CONV_K - 1)
        for (r0, c0), acc in _taps(pad_ref, sh_ref, w_ref, CONV_K, first, ts):
            c1_ref[r0:r0 + CONV_ROWS, c0:c0 + acc.shape[1]] = acc + b_ref[:, c0:c0 + acc.shape[1]]
        c1 = c1_ref[...]
        xc = c1 - jnp.mean(c1, axis=-1, keepdims=True)
        c2 = xc * _rstd(xc) * lg_ref[...] + lb_ref[...]
        c3_ref[...] = (c2 * _sigmoid(c2)).astype(BF16)

    cur = lambda c: pl.BlockSpec((ts, CONV_W), lambda i: (i, c))
    halo = lambda c: pl.BlockSpec((CONV_HALO, CONV_W), lambda i: (jnp.maximum(i * hb - 1, 0), c))
    vec = pl.BlockSpec((1, CONV_W), lambda i: (0, 0))
    row = pl.BlockSpec((ts, CONV_W), lambda i: (i, 0))
    return _call(
        body, name="conv_fwd", grid=(s // ts,),
        in_specs=[cur(3), cur(4), halo(3), halo(4), pl.BlockSpec((CONV_HALO, CONV_W), lambda i: (0, 0)), vec, vec, vec],
        out_specs=[row, row],
        out_shape=[jax.ShapeDtypeStruct((s, CONV_W), F32), jax.ShapeDtypeStruct((s, CONV_W), BF16)],
        scratch_shapes=[pltpu.VMEM((ts + CONV_HALO, CONV_W), F32), _shift_scratch(ts)],
        args=(proj, proj, proj, proj, dw_w, dw_b, ln_g, ln_b), semantics=("parallel",), carry=carry)


def _conv_bwd_ln(dmix, c1, ln_g, ln_b):
    s = c1.shape[0]
    ts = _row_tile(s)

    def body(d_ref, c1_ref, lg_ref, lb_ref, dc1_ref, glg_ref, glb_ref, gb_ref):
        c1v = c1_ref[...]
        xc = c1v - jnp.mean(c1v, axis=-1, keepdims=True)
        r = _rstd(xc)
        xhat = xc * r
        c2 = xhat * lg_ref[...] + lb_ref[...]
        sg = _sigmoid(c2)
        dc2 = d_ref[...] * (sg * (1.0 + c2 * (1.0 - sg)))
        dxhat = dc2 * lg_ref[...]
        dc1 = r * (dxhat - jnp.mean(dxhat, axis=-1, keepdims=True)
                   - xhat * jnp.mean(dxhat * xhat, axis=-1, keepdims=True))
        dc1_ref[...] = dc1
        parts = [(glg_ref, jnp.sum(dc2 * xhat, axis=0, keepdims=True)),
                 (glb_ref, jnp.sum(dc2, axis=0, keepdims=True)),
                 (gb_ref, jnp.sum(dc1, axis=0, keepdims=True))]

        @pl.when(pl.program_id(0) == 0)
        def _():
            for ref, part in parts:
                ref[...] = part

        @pl.when(pl.program_id(0) > 0)
        def _():
            for ref, part in parts:
                ref[...] += part

    row = pl.BlockSpec((ts, CONV_W), lambda i: (i, 0))
    vec = pl.BlockSpec((1, CONV_W), lambda i: (0, 0))
    vsd = jax.ShapeDtypeStruct((1, CONV_W), F32)
    return pl.pallas_call(
        body, name="conv_bwd_ln", grid=(s // ts,),
        in_specs=[pl.BlockSpec((ts, CONV_W), lambda i: (i, 1)), row, vec, vec],
        out_specs=[row, vec, vec, vec],
        out_shape=[jax.ShapeDtypeStruct((s, CONV_W), F32), vsd, vsd, vsd],
        compiler_params=_params(("arbitrary",)))(dmix, c1, ln_g, ln_b)


def _conv_bwd_taps(proj, dc1, dw_w):
    s = proj.shape[0]
    ts = _row_tile(s)
    hb = ts // CONV_HALO
    last = s // CONV_HALO - 1
    nsteps = s // ts

    def body(a_ref, g_ref, ha_ref, hg_ref, d_ref, hd_ref, w_ref, out_ref, gw_ref, pad_ref, dpad_ref, sh_ref):
        i = pl.program_id(0)
        av = a_ref[...]
        sg = _sigmoid(g_ref[...])
        halo = ha_ref[...] * _sigmoid(hg_ref[...])
        pad_ref[0:CONV_HALO, :] = jnp.where(i > 0, halo, 0.0)
        pad_ref[CONV_HALO:, :] = av * sg
        dpad_ref[0:ts, :] = d_ref[...]
        dpad_ref[ts:, :] = jnp.where(i < nsteps - 1, hd_ref[...], 0.0)

        @pl.when(i == 0)
        def _():
            gw_ref[...] = jnp.zeros_like(gw_ref)

        _shift_copies(pad_ref, sh_ref)
        first = CONV_HALO - (CONV_K - 1)
        for k in range(CONV_K):
            shifted = _rows_at(pad_ref, sh_ref, first + k, 0, ts, slice(None))
            gw_ref[k:k + 1, :] += jnp.sum(d_ref[...] * shifted, axis=0, keepdims=True)
        _shift_copies(dpad_ref, sh_ref)
        for (r0, c0), dc0 in _taps(dpad_ref, sh_ref, w_ref, CONV_K, 0, ts, reverse=True):
            cs = slice(c0, c0 + dc0.shape[1])
            a_c = a_ref[r0:r0 + CONV_ROWS, cs]
            sg_c = _sigmoid(g_ref[r0:r0 + CONV_ROWS, cs])
            out_ref[r0:r0 + CONV_ROWS, cs] = (dc0 * sg_c).astype(BF16)
            out_ref[r0:r0 + CONV_ROWS, CONV_W + c0:CONV_W + c0 + dc0.shape[1]] = (
                dc0 * a_c * sg_c * (1.0 - sg_c)).astype(BF16)

    cur = lambda c: pl.BlockSpec((ts, CONV_W), lambda i: (i, c))
    halo = lambda c: pl.BlockSpec((CONV_HALO, CONV_W), lambda i: (jnp.maximum(i * hb - 1, 0), c))
    row = pl.BlockSpec((ts, CONV_W), lambda i: (i, 0))
    nxt = pl.BlockSpec((CONV_HALO, CONV_W), lambda i: (jnp.minimum((i + 1) * hb, last), 0))
    wspec = pl.BlockSpec((CONV_HALO, CONV_W), lambda i: (0, 0))
    return pl.pallas_call(
        body, name="conv_bwd_taps", grid=(nsteps,),
        in_specs=[cur(3), cur(4), halo(3), halo(4), row, nxt, wspec],
        out_specs=[pl.BlockSpec((ts, 2 * CONV_W), lambda i: (i, 0)), wspec],
        out_shape=[jax.ShapeDtypeStruct((s, 2 * CONV_W), BF16), jax.ShapeDtypeStruct((CONV_HALO, CONV_W), F32)],
        scratch_shapes=[pltpu.VMEM((ts + CONV_HALO, CONV_W), F32), pltpu.VMEM((ts + CONV_HALO, CONV_W), F32),
                        _shift_scratch(ts)],
        compiler_params=_params(("arbitrary",)))(proj, proj, proj, proj, dc1, dc1, dw_w)


SQRT_HALF = 0.7071067811865476
INV_SQRT_2PI = 0.3989422804014327


def _gelu_parts(x):
    cdf = 0.5 * (1.0 + lax.erf(x * SQRT_HALF))
    return x * cdf, cdf + x * (INV_SQRT_2PI * jnp.exp(-0.5 * x * x))


def _ffn_tile(dff):
    return dff // 2


FFN_ROWS = 64
FFN_COLS = LANES


def _ffn_chunks(ts, tc):
    return [(r0, slice(c0, c0 + FFN_COLS)) for c0 in range(0, tc, FFN_COLS) for r0 in range(0, ts, FFN_ROWS)]


def _ffn_gate2(pad_ref, w_ref, b_ref, r0, cs):
    first = FFN_HALO - (FFN_K - 1) + r0
    g2 = b_ref[:, cs] + w_ref[0:1, cs] * pad_ref[first:first + FFN_ROWS, cs]
    for k in range(1, FFN_K):
        g2 = g2 + w_ref[k:k + 1, cs] * pad_ref[first + k:first + k + FFN_ROWS, cs]
    return g2


def _ffn_act(up, fw, fb):
    s = up.shape[0]
    dff = up.shape[1] // 2
    tc = _ffn_tile(dff)
    nj = dff // tc
    ts = _row_tile(s) // 2
    hb = ts // FFN_HALO

    def body(g_ref, v_ref, hg_ref, w_ref, b_ref, act_ref, pad_ref):
        i = pl.program_id(0)
        pad_ref[0:FFN_HALO, :] = jnp.where(i > 0, hg_ref[...].astype(F32), 0.0)
        pad_ref[FFN_HALO:, :] = g_ref[...].astype(F32)
        for r0, cs in _ffn_chunks(ts, tc):
            gelu, _ = _gelu_parts(_ffn_gate2(pad_ref, w_ref, b_ref, r0, cs))
            act_ref[r0:r0 + FFN_ROWS, cs] = (gelu * v_ref[r0:r0 + FFN_ROWS, cs].astype(F32)).astype(BF16)

    return pl.pallas_call(
        body, name="ffn_act", grid=(s // ts, nj),
        in_specs=[pl.BlockSpec((ts, tc), lambda i, j: (i, j)), pl.BlockSpec((ts, tc), lambda i, j: (i, j + nj)),
                  pl.BlockSpec((FFN_HALO, tc), lambda i, j: (jnp.maximum(i * hb - 1, 0), j)),
                  pl.BlockSpec((FFN_HALO, tc), lambda i, j: (0, j)), pl.BlockSpec((1, tc), lambda i, j: (0, j))],
        out_specs=pl.BlockSpec((ts, tc), lambda i, j: (i, j)),
        out_shape=jax.ShapeDtypeStruct((s, dff), BF16),
        scratch_shapes=[pltpu.VMEM((ts + FFN_HALO, tc), F32)],
        compiler_params=_params(("parallel", "parallel")))(up, up, up, fw, fb)


def _ffn_bwd_act(dact, up, fw, fb, carry=None):
    s = up.shape[0]
    dff = up.shape[1] // 2
    tc = _ffn_tile(dff)
    nj = dff // tc
    ts = _row_tile(s) // 2
    hb = ts // FFN_HALO

    def body(d_ref, g_ref, v_ref, hg_ref, w_ref, b_ref, dg2_ref, dval_ref, gw_ref, gb_ref, pad_ref):
        i = pl.program_id(1)
        pad_ref[0:FFN_HALO, :] = jnp.where(i > 0, hg_ref[...].astype(F32), 0.0)
        pad_ref[FFN_HALO:, :] = g_ref[...].astype(F32)

        @pl.when(i == 0)
        def _():
            gw_ref[...] = jnp.zeros_like(gw_ref)
            gb_ref[...] = jnp.zeros_like(gb_ref)

        fold = lambda v: jnp.sum(v.reshape(FFN_ROWS // SUBLANES, SUBLANES, FFN_COLS), axis=0)
        first = FFN_HALO - (FFN_K - 1)
        sums = {}
        for r0, cs in _ffn_chunks(ts, tc):
            rows = slice(r0, r0 + FFN_ROWS)
            shifted = [pad_ref[first + k + r0:first + k + r0 + FFN_ROWS, cs] for k in range(FFN_K)]
            g2 = b_ref[:, cs] + w_ref[0:1, cs] * shifted[0]
            for k in range(1, FFN_K):
                g2 = g2 + w_ref[k:k + 1, cs] * shifted[k]
            gelu, dgelu = _gelu_parts(g2)
            dactv = d_ref[rows, cs]
            dval_ref[rows, cs] = (dactv * gelu).astype(BF16)
            dg2 = dactv * v_ref[rows, cs].astype(F32) * dgelu
            dg2_ref[rows, cs] = dg2
            parts = [fold(dg2)] + [fold(dg2 * shifted[k]) for k in range(FFN_K)]
            sums = {n: part + sums[n] if r0 else part for n, part in enumerate(parts)}
            if r0 + FFN_ROWS == ts:
                gb_ref[:, cs] += jnp.sum(sums[0], axis=0, keepdims=True)
                for k in range(FFN_K):
                    gw_ref[k:k + 1, cs] += jnp.sum(sums[1 + k], axis=0, keepdims=True)

    blk = pl.BlockSpec((ts, tc), lambda j, i: (i, j))
    wspec = pl.BlockSpec((FFN_HALO, tc), lambda j, i: (0, j))
    bspec = pl.BlockSpec((1, tc), lambda j, i: (0, j))
    return _call(
        body, name="ffn_bwd_act", grid=(nj, s // ts),
        in_specs=[blk, blk, pl.BlockSpec((ts, tc), lambda j, i: (i, j + nj)),
                  pl.BlockSpec((FFN_HALO, tc), lambda j, i: (jnp.maximum(i * hb - 1, 0), j)), wspec, bspec],
        out_specs=[blk, pl.BlockSpec((ts, tc), lambda j, i: (i, j + nj)), wspec, bspec],
        out_shape=[jax.ShapeDtypeStruct((s, dff), F32), jax.ShapeDtypeStruct((s, 2 * dff), BF16),
                   jax.ShapeDtypeStruct((FFN_HALO, dff), F32), jax.ShapeDtypeStruct((1, dff), F32)],
        scratch_shapes=[pltpu.VMEM((ts + FFN_HALO, tc), F32)],
        args=(dact, up, up, up, fw, fb), semantics=("parallel", "arbitrary"), carry=carry)


def _ffn_bwd_conv(dg2, fw, dup):
    s, dff = dg2.shape
    tc = _ffn_tile(dff)
    ts = _row_tile(s) // 2
    hb = ts // FFN_HALO
    last = s // FFN_HALO - 1
    nsteps = s // ts

    def body(d_ref, hd_ref, w_ref, dup_ref, out_ref, pad_ref):
        i = pl.program_id(0)
        pad_ref[0:ts, :] = d_ref[...]
        pad_ref[ts:, :] = jnp.where(i < nsteps - 1, hd_ref[...], 0.0)
        for r0, cs in _ffn_chunks(ts, tc):
            dg = w_ref[0:1, cs] * pad_ref[r0 + FFN_K - 1:r0 + FFN_K - 1 + FFN_ROWS, cs]
            for k in range(1, FFN_K):
                dg = dg + w_ref[k:k + 1, cs] * pad_ref[r0 + FFN_K - 1 - k:r0 + FFN_K - 1 - k + FFN_ROWS, cs]
            out_ref[r0:r0 + FFN_ROWS, cs] = dg.astype(BF16)

    blk = pl.BlockSpec((ts, tc), lambda i, j: (i, j))
    return pl.pallas_call(
        body, name="ffn_bwd_conv", grid=(nsteps, dff // tc),
        in_specs=[blk, pl.BlockSpec((FFN_HALO, tc), lambda i, j: (jnp.minimum((i + 1) * hb, last), j)),
                  pl.BlockSpec((FFN_HALO, tc), lambda i, j: (0, j)), ANY],
        out_specs=blk, out_shape=jax.ShapeDtypeStruct(dup.shape, BF16), input_output_aliases={3: 0},
        scratch_shapes=[pltpu.VMEM((ts + FFN_HALO, tc), F32)],
        compiler_params=_params(("parallel", "parallel")))(dg2, dg2, fw, dup)


def _ple_loss(h2, zg, pp, target):
    s, d = h2.shape
    ts = _row_tile(s)

    def body(h_ref, z_ref, p_ref, t_ref, dh_ref, dpp_ref, dz_ref, loss_ref):
        pg = _sigmoid(z_ref[...])
        ppv = p_ref[...]
        diff = h_ref[...] + pg * ppv - t_ref[...]
        dh = diff * (1.0 / d)
        dh_ref[...] = dh
        dpp_ref[...] = (dh * pg).astype(BF16)
        dz_ref[...] = (dh * ppv * pg * (1.0 - pg)).astype(BF16)
        part = jnp.sum(jnp.sum(diff * diff, axis=0, keepdims=True), axis=1, keepdims=True)

        @pl.when(pl.program_id(0) == 0)
        def _():
            loss_ref[...] = jnp.zeros_like(loss_ref)

        loss_ref[...] += jnp.broadcast_to(part, loss_ref.shape)

    row = pl.BlockSpec((ts, d), lambda i: (i, 0))
    return pl.pallas_call(
        body, name="ple_loss", grid=(s // ts,), in_specs=[row, row, row, row],
        out_specs=[row, row, row, pl.BlockSpec((8, LANES), lambda i: (0, 0))],
        out_shape=[jax.ShapeDtypeStruct((s, d), F32), jax.ShapeDtypeStruct((s, d), BF16),
                   jax.ShapeDtypeStruct((s, d), BF16), jax.ShapeDtypeStruct((8, LANES), F32)],
        compiler_params=_params(("arbitrary",)))(h2, zg, pp, target)


def _local_step(x, p, target, w, shards=None, unpack_filters=None):
    riding = shards is not None
    major = lambda n, g: g if n in COL_SHARDED else g.reshape(N_CHIPS, -1, g.shape[-1])
    full = lambda n, g: g if n in COL_SHARDED else g.reshape(-1, g.shape[-1])
    first = lambda res: (res[0], res[1:]) if riding else (res, [])

    def ride(names, arrays):
        return ([arrays[n] for n in names], False) if riding else None

    def leave(grads):
        return (grads, True) if riding else None

    def land(names, got):
        return {n: full(n, g) for n, g in zip(names, got)}
    s = x.shape[0]
    t = min(256, s)
    tri = jnp.tril(jnp.ones((t, t), F32))
    tri_incl = tri.astype(BF16)
    tri_excl = jnp.tril(jnp.ones((t, t), F32), -1).astype(BF16)
    bd = jnp.kron(jnp.eye(N_HEADS, dtype=F32), jnp.ones((HEAD_DIM, HEAD_DIM), F32)).astype(BF16)
    qg = jnp.tile(w["q_gain"], (1, N_HEADS))
    kg = jnp.tile(w["k_gain"], (1, N_HEADS))
    pb = p.astype(BF16)

    u1, got = first(_rms_fwd(x, w["g_mix"], "rms_mix", carry=ride(["w_in", "filters"], shards)))
    if riding:
        w = {**w, "w_in": got[0], **unpack_filters(got[1])}
    dw_w = jnp.pad(w["dw_w"], ((0, CONV_HALO - CONV_K), (0, 0)))
    fw = jnp.pad(w["ffn_conv_w"], ((0, FFN_HALO - FFN_K), (0, 0)))
    proj = _mm_nn_sharded(u1, w["w_in"], "mm_in")
    qs, kh, vb = _qkv_prep(proj, qg, kg, bd)
    with_attn, with_conv = ["w_out", "w_up", "w_ple_gate", "w_ple_proj"], ["w_down"]
    o, ob, *got = _attn_fwd(qs, kh, vb, tri_excl, carry=ride(with_attn, shards))
    w = {**w, **land(with_attn, got)}
    c1, c3, *got = _conv_fwd(proj, dw_w, w["dw_b"], w["conv_ln_g"], w["conv_ln_b"], carry=ride(with_conv, shards))
    w = {**w, **land(with_conv, got)}
    mix = jnp.concatenate([ob, c3], axis=1)
    h1 = _mm_nn_full(mix, w["w_out"], "mm_out", res=x)
    u2 = _rms_fwd(h1, w["g_ffn"], "rms_ffn")
    up = _mm_nn_sharded(u2, w["w_up"], "mm_up", out_dtype=BF16)
    act = _ffn_act(up, fw, w["ffn_conv_b"])
    h2 = _mm_nn_full(act, w["w_down"], "mm_down", res=h1)
    u3 = _rms_fwd(h2, w["g_ple"], "rms_ple")
    zg = _mm_nn_full(u3, w["w_ple_gate"], "mm_ple_gate")
    pp = _mm_nn_sharded(pb, w["w_ple_proj"], "mm_ple_proj")
    dh3, dpp, dz, sq = _ple_loss(h2, zg, pp, target)

    big = {}
    small = {}
    big["w_ple_proj"] = _mm_tn_sharded(pb, dpp, N_CHIPS, "mm_g_ple_proj")
    big["w_ple_gate"] = _mm_tn_full(u3, dz, "mm_g_ple_gate", tm=u3.shape[1])
    du3 = _mm_nt_full(dz, w["w_ple_gate"], "mm_d_ple_gate")
    dh2, dh2b, small["g_ple"] = _rms_bwd(h2, du3, w["g_ple"], dh3, "rms_ple_bwd")
    slots = {}
    with_down, with_ffn, with_attn = ["w_ple_proj", "w_ple_gate"], ["w_down"], ["w_up", "w_out"]
    leaving = leave([major(n, big.pop(n)) for n in with_down]) if riding else None
    big["w_down"], got = first(_mm_tn_full(act, dh2b, "mm_g_down", tm=act.shape[1] // 2, carry=leaving))
    slots.update(zip(with_down, got))
    dact = _mm_nt_full(dh2b, w["w_down"], "mm_d_down")
    leaving = leave([major(n, big.pop(n)) for n in with_ffn]) if riding else None
    dg2, dup, gfw, small["ffn_conv_b"], *got = _ffn_bwd_act(dact, up, fw, w["ffn_conv_b"], carry=leaving)
    slots.update(zip(with_ffn, got))
    small["ffn_conv_w"] = gfw[:FFN_K]
    dup = _ffn_bwd_conv(dg2, fw, dup)
    big["w_up"] = _mm_tn_sharded(u2, dup, N_CHIPS, "mm_g_up", n_split=2)
    du2, = _mm_nt_sharded(dup, w["w_up"], "mm_d_up")
    dh1, dh1b, small["g_ffn"] = _rms_bwd(h1, du2, w["g_ffn"], dh2, "rms_ffn_bwd")
    big["w_out"] = _mm_tn_full(mix, dh1b, "mm_g_out", tm=mix.shape[1])
    dmix = _mm_nt_full(dh1b, w["w_out"], "mm_d_out")
    dc1, small["conv_ln_g"], small["conv_ln_b"], small["dw_b"] = _conv_bwd_ln(dmix, c1, w["conv_ln_g"], w["conv_ln_b"])
    dcacg, gdw = _conv_bwd_taps(proj, dc1, dw_w)
    small["dw_w"] = gdw[:CONV_K]
    leaving = leave([major(n, big.pop(n)) for n in with_attn]) if riding else None
    dqh, dkh, dv, *got = _attn_bwd(qs, kh, vb, o, dmix, tri_excl, tri_incl, carry=leaving)
    slots.update(zip(with_attn, got))
    dqkv, gq, gk = _qk_bwd(proj, dqh, dkh, dv, qg, kg, bd)
    small["q_gain"] = gq.reshape(N_HEADS, HEAD_DIM).sum(axis=0, keepdims=True)
    small["k_gain"] = gk.reshape(N_HEADS, HEAD_DIM).sum(axis=0, keepdims=True)
    dproj = jnp.concatenate([dqkv, dcacg], axis=1)
    big["w_in"] = _mm_tn_sharded(u1, dproj, N_CHIPS, "mm_g_in")
    half = big["w_in"].shape[1] // 2
    halves = [big["w_in"][:, :half], big.pop("w_in")[:, half:]] if riding else [None, None]
    du1, *got_a = _mm_nt_sharded(dproj, w["w_in"], "mm_d_in", carry=leave([halves[0]]) if riding else None)
    grad_x, _, small["g_mix"], *got_b = _rms_bwd(x, du1, w["g_mix"], dh1, "rms_mix_bwd",
                                                 carry=leave([halves[1]]) if riding else None)
    if riding:
        slots["w_in"] = jnp.concatenate([got_a[0], got_b[0]], axis=1)
    return sq[0, 0], grad_x, big, slots, small


def _exchange(srcs, per_chip, name):
    n = len(srcs)

    def body(*refs):
        src_refs, land_refs = refs[:n], refs[n:2 * n]
        send_sems, recv_sems, local_sems = refs[2 * n:]
        local = _local_copies(src_refs, land_refs, local_sems, per_chip)
        sends = _chip_copies(src_refs, land_refs, send_sems, recv_sems, per_chip, False)
        for cp in local + sends:
            cp.start()
        for cp in _chip_copies(src_refs, land_refs, send_sems, recv_sems, per_chip, True):
            cp.wait_recv()
        for cp in sends:
            cp.wait_send()
        for cp in local:
            cp.wait()

    return pl.pallas_call(
        body, name=name, in_specs=[ANY] * n, out_specs=[ANY] * n, out_shape=_exchanged_shapes(srcs, per_chip),
        scratch_shapes=_exchange_sems(n))(*srcs)


def _finish_exchange(mine, small):
    n = len(mine)

    def body(*refs):
        gin, sin = refs[:n], refs[n]
        gout, sout = refs[n + 1:2 * n + 1], refs[2 * n + 1]
        send_sems, recv_sems, small_send, small_recv, local_sem = refs[2 * n + 2:]
        x, y, c, _ = _position()
        dev = 4 * x + 2 * y + c
        flip = lambda v, bit: 1 - v if bit else v
        others = [(flip(x, k & 4), flip(y, k & 2), flip(c, k & 1)) for k in range(1, N_DEV)]
        local = pltpu.make_async_copy(sin, sout.at[dev], local_sem)
        local.start()
        swaps = [_remote(gin[a], gout[a], send_sems, recv_sems, a, (x, y, 1 - c)) for a in range(n)]
        sends = swaps + [_remote(sin, sout.at[dev], small_send, small_recv, k, peer) for k, peer in enumerate(others)]
        for cp in sends:
            cp.start()
        for cp in swaps:
            cp.wait_recv()
        for k, (px, py, pc) in enumerate(others):
            _remote(sin, sout.at[4 * px + 2 * py + pc], small_send, small_recv, k, (px, py, pc)).wait_recv()
        for cp in sends:
            cp.wait_send()
        local.wait()

    return pl.pallas_call(
        body, name="finish_exchange", in_specs=[ANY] * (n + 1), out_specs=[ANY] * (n + 1),
        out_shape=[jax.ShapeDtypeStruct(g.shape, g.dtype) for g in mine]
        + [jax.ShapeDtypeStruct((N_DEV,) + small.shape, small.dtype)],
        scratch_shapes=[pltpu.SemaphoreType.DMA((n,)), pltpu.SemaphoreType.DMA((n,)),
                        pltpu.SemaphoreType.DMA((N_DEV - 1,)), pltpu.SemaphoreType.DMA((N_DEV - 1,)),
                        pltpu.SemaphoreType.DMA])(*mine, small)


def _elem_tile(rows):
    return 128 if rows % 128 == 0 else (64 if rows % 64 == 0 else rows)


def _sum_slots(a, name):
    g, r, c = a.shape
    tr = _elem_tile(r)

    def body(a_ref, o_ref):
        acc = a_ref[0]
        for k in range(1, g):
            acc = acc + a_ref[k]
        o_ref[...] = acc

    return pl.pallas_call(
        body, name=name, grid=(r // tr,), in_specs=[pl.BlockSpec((g, tr, c), lambda i: (0, i, 0))],
        out_specs=pl.BlockSpec((tr, c), lambda i: (i, 0)), out_shape=jax.ShapeDtypeStruct((r, c), a.dtype),
        compiler_params=_params(("parallel",)))(a)


def _adamw(wt, ga, gb, m, v, name):
    r, c = wt.shape
    tr = _elem_tile(r)
    two = gb is not None

    def body(*refs):
        if two:
            w_ref, ga_ref, gb_ref, m_ref, v_ref, g_out, d_out, m_out, v_out = refs
            g = ga_ref[...] + gb_ref[...]
        else:
            w_ref, ga_ref, m_ref, v_ref, g_out, d_out, m_out, v_out = refs
            g = ga_ref[...]
        mn = ADAM_B1 * m_ref[...] + (1.0 - ADAM_B1) * g
        vn = ADAM_B2 * v_ref[...] + (1.0 - ADAM_B2) * (g * g)
        m_hat = mn / (1.0 - ADAM_B1 ** ADAM_STEP)
        v_hat = vn / (1.0 - ADAM_B2 ** ADAM_STEP)
        g_out[...] = g
        d_out[...] = -ADAM_LR * (m_hat / (jnp.sqrt(v_hat) + ADAM_EPS) + ADAM_WD * w_ref[...])
        m_out[...] = mn
        v_out[...] = vn

    blk = pl.BlockSpec((tr, c), lambda i: (i, 0))
    args = [wt, ga] + ([gb] if two else []) + [m, v]
    sds = jax.ShapeDtypeStruct((r, c), F32)
    return pl.pallas_call(
        body, name=name, grid=(r // tr,), in_specs=[blk] * len(args), out_specs=[blk] * 4, out_shape=[sds] * 4,
        compiler_params=_params(("parallel",)))(*args)


def _pack(arrs, rows):
    flat = jnp.concatenate([a.reshape(-1) for a in arrs])
    return jnp.pad(flat, (0, rows * LANES - flat.shape[0])).reshape(rows, LANES)


def _unpack(buf, shapes):
    flat = buf.reshape(-1)
    out, off = [], 0
    for shp in shapes:
        size = 1
        for d in shp:
            size *= d
        out.append(flat[off:off + size].reshape(shp))
        off += size
    return out


BIG = ["w_in", "w_out", "w_up", "w_down", "w_ple_gate", "w_ple_proj"]
COL_SHARDED = ["w_in", "w_up", "w_ple_proj"]
SMALL_REPL = ["g_mix", "q_gain", "k_gain", "dw_b", "conv_ln_g", "conv_ln_b", "g_ffn", "ffn_conv_b", "g_ple"]
SMALL_SHARDED = ["dw_w", "ffn_conv_w"]
WEIGHTS = ["g_mix", "w_in", "q_gain", "k_gain", "dw_w", "dw_b", "conv_ln_g", "conv_ln_b", "w_out", "g_ffn", "w_up",
           "ffn_conv_w", "ffn_conv_b", "w_down", "g_ple", "w_ple_gate", "w_ple_proj"]


def _rows_for(n_elems):
    return -(-n_elems // (8 * LANES)) * 8


def kernel(x, p, g_mix, w_in, q_gain, k_gain, dw_w, dw_b, conv_ln_g, conv_ln_b, w_out, g_ffn, w_up, ffn_conv_w, ffn_conv_b, w_down, g_ple, w_ple_gate, w_ple_proj, loss_target, m_g_mix, m_w_in, m_q_gain, m_k_gain, m_dw_w, m_dw_b, m_conv_ln_g, m_conv_ln_b, m_w_out, m_g_ffn, m_w_up, m_ffn_conv_w, m_ffn_conv_b, m_w_down, m_g_ple, m_w_ple_gate, m_w_ple_proj, v_g_mix, v_w_in, v_q_gain, v_k_gain, v_dw_w, v_dw_b, v_conv_ln_g, v_conv_ln_b, v_w_out, v_g_ffn, v_w_up, v_ffn_conv_w, v_ffn_conv_b, v_w_down, v_g_ple, v_w_ple_gate, v_w_ple_proj):
    given = dict(locals())
    strip = lambda n, a: a if n in SMALL_REPL else a[0]
    wts = {n: strip(n, given[n]) for n in WEIGHTS}
    mom = {n: strip(n, given["m_" + n]) for n in WEIGHTS}
    var = {n: strip(n, given["v_" + n]) for n in WEIGHTS}
    chip = 2 * lax.axis_index("x") + lax.axis_index("y")

    small_shard_shapes = [wts[n].shape for n in SMALL_SHARDED]
    filt_rows = _rows_for(sum(wts[n].size for n in SMALL_SHARDED))
    shards = {n: wts[n].astype(BF16) for n in BIG}
    shards["filters"] = _pack([wts[n] for n in SMALL_SHARDED], filt_rows)

    def unpack_filters(filt_all):
        per_chip = [_unpack(filt_all[k], small_shard_shapes) for k in range(N_CHIPS)]
        return {n: jnp.concatenate([per_chip[k][idx] for k in range(N_CHIPS)], axis=1)
                for idx, n in enumerate(SMALL_SHARDED)}

    sq, grad_x, big, slots, small = _local_step(x[0], p[0, 0], loss_target[0], {n: wts[n] for n in SMALL_REPL},
                                                shards, unpack_filters)
    loss = lax.psum(sq * (0.5 / x.shape[-1]), ("x", "y", "c"))

    small_names = SMALL_REPL + SMALL_SHARDED
    small_shapes = [small[n].shape for n in small_names]
    small_rows = _rows_for(sum(small[n].size for n in small_names))
    mine = [_sum_slots(slots[n], "sum_" + n) for n in BIG]
    *theirs, small_slots = _finish_exchange(mine, _pack([small[n] for n in small_names], small_rows))
    small_sum = dict(zip(small_names, _unpack(_sum_slots(small_slots, "sum_small"), small_shapes)))

    outs = {}
    for n, ga, gb in zip(BIG, mine, theirs):
        outs[n] = _adamw(wts[n], ga, gb, mom[n], var[n], "adamw_" + n)
    for n in SMALL_SHARDED:
        width = wts[n].shape[1]
        small_sum[n] = lax.dynamic_slice_in_dim(small_sum[n], chip * width, width, axis=1)
    local_shapes = [wts[n].shape for n in small_names]
    local_rows = _rows_for(sum(wts[n].size for n in small_names))
    packed = _adamw(_pack([wts[n] for n in small_names], local_rows), _pack([small_sum[n] for n in small_names], local_rows),
                    None, _pack([mom[n] for n in small_names], local_rows),
                    _pack([var[n] for n in small_names], local_rows), "adamw_small")
    unpacked = [_unpack(buf, local_shapes) for buf in packed]
    for idx, n in enumerate(small_names):
        outs[n] = [u[idx] for u in unpacked]
    result = [loss, grad_x[None]]
    for part in range(4):
        result += [outs[n][part] if n in SMALL_REPL else outs[n][part][None] for n in WEIGHTS]
    return tuple(result)
```

```python
import functools

import jax
import jax.numpy as jnp
from jax import lax
from jax.experimental import pallas as pl
from jax.experimental.pallas import tpu as pltpu

F32 = jnp.float32
BF16 = jnp.bfloat16
HIGHEST = lax.Precision.HIGHEST
MESH = pl.DeviceIdType.MESH
ANY = pl.BlockSpec(memory_space=pl.ANY)

EPS = 1e-6
HEAD_DIM = 64
N_HEADS = 8
ATTN_W = 512
CONV_W = 512
CONV_K = 31
FFN_K = 3
ATTN_SCALE = 0.125
LANES = 128
CONV_HALO = 32
FFN_HALO = 16
VMEM_LIMIT = 56 * 1024 * 1024

ADAM_LR = 0.001
ADAM_B1 = 0.9
ADAM_B2 = 0.999
ADAM_EPS = 1e-08
ADAM_WD = 0.01
ADAM_STEP = 10

N_CHIPS = 4
N_DEV = 8


def _params(sem):
    return pltpu.CompilerParams(dimension_semantics=sem, vmem_limit_bytes=VMEM_LIMIT)


def _row_tile(s):
    return min(512, s)


def _position():
    x, y, c = lax.axis_index("x"), lax.axis_index("y"), lax.axis_index("c")
    return x, y, c, [(1 - x, y), (x, 1 - y), (1 - x, 1 - y)]


def _remote(src, dst, send_sems, recv_sems, k, to):
    return pltpu.make_async_remote_copy(src_ref=src, dst_ref=dst, send_sem=send_sems.at[k], recv_sem=recv_sems.at[k],
                                        device_id=to, device_id_type=MESH)


def _chip_copies(src_refs, land_refs, send_sems, recv_sems, per_chip, landed):
    x, y, c, chips = _position()
    me = 2 * x + y
    out = []
    for a, (src, land) in enumerate(zip(src_refs, land_refs)):
        for j, (px, py) in enumerate(chips):
            peer = 2 * px + py
            out.append(_remote(src.at[peer] if per_chip else src, land.at[peer if landed else me],
                               send_sems, recv_sems, 3 * a + j, (px, py, c)))
    return out


def _local_copies(src_refs, land_refs, local_sems, per_chip):
    x, y, _, _ = _position()
    me = 2 * x + y
    return [pltpu.make_async_copy(src.at[me] if per_chip else src, land.at[me], local_sems.at[a])
            for a, (src, land) in enumerate(zip(src_refs, land_refs))]


def _exchanged_shapes(srcs, per_chip):
    return [jax.ShapeDtypeStruct(a.shape if per_chip else (N_CHIPS,) + a.shape, a.dtype) for a in srcs]


def _exchange_sems(n):
    return [pltpu.SemaphoreType.DMA((3 * n,)), pltpu.SemaphoreType.DMA((3 * n,)), pltpu.SemaphoreType.DMA((n,))]


def _call(body, *, name, grid, in_specs, out_specs, out_shape, scratch_shapes, args, semantics, carry=None):
    if carry is None:
        return pl.pallas_call(body, name=name, grid=grid, in_specs=in_specs, out_specs=out_specs, out_shape=out_shape,
                              scratch_shapes=scratch_shapes, compiler_params=_params(semantics))(*args)
    srcs, per_chip = carry
    n, n_in, n_out, n_scr = len(srcs), len(in_specs), len(out_specs), len(scratch_shapes)

    def wrapped(*refs):
        ins, xin = refs[:n_in], refs[n_in:n_in + n]
        outs, xout = refs[n_in + n:n_in + n + n_out], refs[n_in + n + n_out:n_in + 2 * n + n_out]
        scratch = refs[n_in + 2 * n + n_out:n_in + 2 * n + n_out + n_scr]
        send_sems, recv_sems, local_sems = refs[-3:]
        first = functools.reduce(jnp.logical_and, [pl.program_id(d) == 0 for d in range(len(grid))])
        last = functools.reduce(jnp.logical_and, [pl.program_id(d) == g - 1 for d, g in enumerate(grid)])

        @pl.when(first)
        def _():
            for cp in _local_copies(xin, xout, local_sems, per_chip):
                cp.start()
            for cp in _chip_copies(xin, xout, send_sems, recv_sems, per_chip, False):
                cp.start()

        body(*ins, *outs, *scratch)

        @pl.when(last)
        def _():
            for cp in _chip_copies(xin, xout, send_sems, recv_sems, per_chip, True):
                cp.wait_recv()
            for cp in _chip_copies(xin, xout, send_sems, recv_sems, per_chip, False):
                cp.wait_send()
            for cp in _local_copies(xin, xout, local_sems, per_chip):
                cp.wait()

    return pl.pallas_call(
        wrapped, name=name, grid=grid, in_specs=list(in_specs) + [ANY] * n, out_specs=list(out_specs) + [ANY] * n,
        out_shape=list(out_shape) + _exchanged_shapes(srcs, per_chip),
        scratch_shapes=list(scratch_shapes) + _exchange_sems(n),
        compiler_params=_params(("arbitrary",) * len(grid)))(*args, *srcs)


def _contract_tile(s):
    return min(1024, s)


def _rstd(x):
    return lax.rsqrt(jnp.mean(x * x, axis=-1, keepdims=True) + EPS)


def _sigmoid(x):
    return 1.0 / (1.0 + jnp.exp(-x))


def _mm(a, b, *, name, dims, grid, a_spec, b_spec, o_spec, o_tile, out_shape, res=None, res_spec=None, carry=None):
    nk = grid[2]

    def body(*refs):
        if res is None:
            a_ref, b_ref, o_ref, acc_ref = refs
            r_ref = None
        else:
            a_ref, b_ref, r_ref, o_ref, acc_ref = refs
        part = lax.dot_general(a_ref[...], b_ref[...], (dims, ((), ())), preferred_element_type=F32)

        def finish(val):
            if r_ref is not None:
                val = val + r_ref[...]
            o_ref[...] = val.astype(o_ref.dtype)

        if nk == 1:
            finish(part)
        else:
            k = pl.program_id(2)

            @pl.when(k == 0)
            def _():
                acc_ref[...] = part

            @pl.when(k > 0)
            def _():
                acc_ref[...] += part

            @pl.when(k == nk - 1)
            def _():
                finish(acc_ref[...])

    in_specs = [a_spec, b_spec]
    args = [a, b]
    if res is not None:
        in_specs.append(res_spec)
        args.append(res)
    acc_tile = o_tile if nk > 1 else (8, LANES)
    out = _call(body, name=name, grid=grid, in_specs=in_specs, out_specs=[o_spec], out_shape=[out_shape],
                scratch_shapes=[pltpu.VMEM(acc_tile, F32)], args=args,
                semantics=("parallel", "parallel", "arbitrary"), carry=carry)
    return out[0] if carry is None else out


NN = ((1,), (0,))
NT = ((1,), (1,))
TN = ((0,), (0,))


def _mm_nn_sharded(a, bg, name, out_dtype=F32, tm=None):
    s, k = a.shape
    g, _, ns = bg.shape
    tm = tm or _row_tile(s)

    def body(a_ref, b_ref, o_ref):
        av = a_ref[...]
        for gi in range(g):
            o_ref[:, gi * ns:(gi + 1) * ns] = jnp.dot(av, b_ref[gi], preferred_element_type=F32).astype(out_dtype)

    return pl.pallas_call(
        body, name=name, grid=(s // tm,),
        in_specs=[pl.BlockSpec((tm, k), lambda i: (i, 0)), pl.BlockSpec((g, k, ns), lambda i: (0, 0, 0))],
        out_specs=pl.BlockSpec((tm, g * ns), lambda i: (i, 0)),
        out_shape=jax.ShapeDtypeStruct((s, g * ns), out_dtype), compiler_params=_params(("parallel",)))(a, bg)


def _mm_nn_full(a, b, name, res=None):
    s, k = a.shape
    n = b.shape[1]
    tm = _row_tile(s)
    rs = pl.BlockSpec((tm, n), lambda i, j, kk: (i, 0))
    return _mm(a, b, name=name, dims=NN, grid=(s // tm, 1, 1),
               a_spec=pl.BlockSpec((tm, k), lambda i, j, kk: (i, 0)),
               b_spec=pl.BlockSpec((k, n), lambda i, j, kk: (0, 0)),
               o_spec=rs, o_tile=(tm, n), out_shape=jax.ShapeDtypeStruct((s, n), F32),
               res=res, res_spec=rs if res is not None else None)


def _mm_nt_full(a, b, name):
    s, n = a.shape
    k = b.shape[0]
    tm = _row_tile(s)
    return _mm(a, b, name=name, dims=NT, grid=(s // tm, 1, 1),
               a_spec=pl.BlockSpec((tm, n), lambda i, j, kk: (i, 0)),
               b_spec=pl.BlockSpec((k, n), lambda i, j, kk: (0, 0)),
               o_spec=pl.BlockSpec((tm, k), lambda i, j, kk: (i, 0)), o_tile=(tm, k),
               out_shape=jax.ShapeDtypeStruct((s, k), F32))


def _mm_nt_sharded(a, bg, name, carry=None):
    s = a.shape[0]
    g, k, ns = bg.shape
    tm = _row_tile(s)

    def body(a_ref, b_ref, o_ref):
        acc = lax.dot_general(a_ref[:, 0:ns], b_ref[0], (NT, ((), ())), preferred_element_type=F32)
        for gi in range(1, g):
            acc = acc + lax.dot_general(a_ref[:, gi * ns:(gi + 1) * ns], b_ref[gi], (NT, ((), ())),
                                        preferred_element_type=F32)
        o_ref[...] = acc

    return _call(
        body, name=name, grid=(s // tm,),
        in_specs=[pl.BlockSpec((tm, g * ns), lambda i: (i, 0)), pl.BlockSpec((g, k, ns), lambda i: (0, 0, 0))],
        out_specs=[pl.BlockSpec((tm, k), lambda i: (i, 0))], out_shape=[jax.ShapeDtypeStruct((s, k), F32)],
        scratch_shapes=[], args=(a, bg), semantics=("parallel",), carry=carry)


def _mm_tn_sharded(a, b, g, name, n_split=1):
    s, k = a.shape
    ns = b.shape[1] // g
    gs = g // n_split
    tk = _contract_tile(s)

    def body(a_ref, b_ref, o_ref):
        first = pl.program_id(1) == 0
        at = a_ref[...].T
        for gi in range(gs):
            part = jnp.dot(at, b_ref[:, gi * ns:(gi + 1) * ns], preferred_element_type=F32)

            @pl.when(first)
            def _(gi=gi, part=part):
                o_ref[gi] = part

            @pl.when(jnp.logical_not(first))
            def _(gi=gi, part=part):
                o_ref[gi] += part

    return pl.pallas_call(
        body, name=name, grid=(n_split, s // tk),
        in_specs=[pl.BlockSpec((tk, k), lambda j, kk: (kk, 0)), pl.BlockSpec((tk, gs * ns), lambda j, kk: (kk, j))],
        out_specs=pl.BlockSpec((gs, k, ns), lambda j, kk: (j, 0, 0)),
        out_shape=jax.ShapeDtypeStruct((g, k, ns), F32), compiler_params=_params(("parallel", "arbitrary")))(a, b)


def _mm_tn_full(a, b, name, tm, carry=None):
    s, m = a.shape
    n = b.shape[1]
    tk = _contract_tile(s)
    return _mm(a, b, name=name, dims=TN, grid=(m // tm, 1, s // tk),
               a_spec=pl.BlockSpec((tk, tm), lambda i, j, kk: (kk, i)),
               b_spec=pl.BlockSpec((tk, n), lambda i, j, kk: (kk, 0)),
               o_spec=pl.BlockSpec((tm, n), lambda i, j, kk: (i, 0)), o_tile=(tm, n),
               out_shape=jax.ShapeDtypeStruct((m, n), F32), carry=carry)


def _rms_fwd(x, g, name, carry=None):
    s, d = x.shape
    ts = _row_tile(s)

    def body(x_ref, g_ref, u_ref):
        xv = x_ref[...]
        u_ref[...] = (xv * _rstd(xv) * g_ref[...]).astype(BF16)

    row = pl.BlockSpec((ts, d), lambda i: (i, 0))
    out = _call(body, name=name, grid=(s // ts,), in_specs=[row, pl.BlockSpec((1, d), lambda i: (0, 0))],
                out_specs=[row], out_shape=[jax.ShapeDtypeStruct((s, d), BF16)], scratch_shapes=[], args=(x, g),
                semantics=("parallel",), carry=carry)
    return out[0] if carry is None else out


def _rms_bwd(h, du, g, dh_in, name, carry=None):
    s, d = h.shape
    ts = _row_tile(s)

    def body(h_ref, du_ref, g_ref, dhin_ref, dh_ref, dhb_ref, gg_ref):
        hv = h_ref[...]
        r = _rstd(hv)
        xhat = hv * r
        duv = du_ref[...]
        dxhat = duv * g_ref[...]
        m = jnp.mean(dxhat * xhat, axis=-1, keepdims=True)
        dh = dhin_ref[...] + r * (dxhat - xhat * m)
        dh_ref[...] = dh
        dhb_ref[...] = dh.astype(BF16)
        part = jnp.sum(duv * xhat, axis=0, keepdims=True)

        @pl.when(pl.program_id(0) == 0)
        def _():
            gg_ref[...] = part

        @pl.when(pl.program_id(0) > 0)
        def _():
            gg_ref[...] += part

    row = pl.BlockSpec((ts, d), lambda i: (i, 0))
    vec = pl.BlockSpec((1, d), lambda i: (0, 0))
    return _call(
        body, name=name, grid=(s // ts,), in_specs=[row, row, vec, row], out_specs=[row, row, vec],
        out_shape=[jax.ShapeDtypeStruct((s, d), F32), jax.ShapeDtypeStruct((s, d), BF16),
                   jax.ShapeDtypeStruct((1, d), F32)],
        scratch_shapes=[], args=(h, du, g, dh_in), semantics=("arbitrary",), carry=carry)


def _head_sum(x, bd):
    return _tri_dot(_split(x), bd)


def _qkv_prep(proj, qg, kg, bd):
    s = proj.shape[0]
    ts = _row_tile(s)

    def body(q_ref, k_ref, v_ref, qg_ref, kg_ref, bd_ref, qs_ref, kh_ref, vb_ref):
        def norm(x, gain):
            ms = _head_sum(x * x, bd_ref[...]) * (1.0 / HEAD_DIM)
            return x * lax.rsqrt(ms + EPS) * gain

        qs_ref[...] = (norm(q_ref[...], qg_ref[...]) * ATTN_SCALE).astype(BF16)
        kh_ref[...] = norm(k_ref[...], kg_ref[...]).astype(BF16)
        vb_ref[...] = v_ref[...].astype(BF16)

    col = lambda c: pl.BlockSpec((ts, ATTN_W), lambda i: (i, c))
    vec = pl.BlockSpec((1, ATTN_W), lambda i: (0, 0))
    out = pl.BlockSpec((ts, ATTN_W), lambda i: (i, 0))
    sds = jax.ShapeDtypeStruct((s, ATTN_W), BF16)
    return pl.pallas_call(
        body, name="qkv_prep", grid=(s // ts,),
        in_specs=[col(0), col(1), col(2), vec, vec, pl.BlockSpec((ATTN_W, ATTN_W), lambda i: (0, 0))],
        out_specs=[out, out, out], out_shape=[sds, sds, sds],
        compiler_params=_params(("parallel",)))(proj, proj, proj, qg, kg, bd)


def _qk_bwd(proj, dqh, dkh, dv, qg, kg, bd):
    s = proj.shape[0]
    ts = _row_tile(s)

    def body(q_ref, k_ref, dqh_ref, dkh_ref, dv_ref, qg_ref, kg_ref, bd_ref, out_ref, gq_ref, gk_ref):
        first = pl.program_id(0) == 0

        def bwd(x, dy, gain, gg_ref):
            ms = _head_sum(x * x, bd_ref[...]) * (1.0 / HEAD_DIM)
            r = lax.rsqrt(ms + EPS)
            xhat = x * r
            dxhat = dy * gain
            m = _head_sum(dxhat * xhat, bd_ref[...]) * (1.0 / HEAD_DIM)
            part = jnp.sum(dy * xhat, axis=0, keepdims=True)

            @pl.when(first)
            def _():
                gg_ref[...] = part

            @pl.when(jnp.logical_not(first))
            def _():
                gg_ref[...] += part

            return r * (dxhat - xhat * m)

        out_ref[:, 0:ATTN_W] = bwd(q_ref[...], dqh_ref[...], qg_ref[...], gq_ref).astype(BF16)
        out_ref[:, ATTN_W:2 * ATTN_W] = bwd(k_ref[...], dkh_ref[...], kg_ref[...], gk_ref).astype(BF16)
        out_ref[:, 2 * ATTN_W:3 * ATTN_W] = dv_ref[...].astype(BF16)

    col = lambda c: pl.BlockSpec((ts, ATTN_W), lambda i: (i, c))
    row = pl.BlockSpec((ts, ATTN_W), lambda i: (i, 0))
    vec = pl.BlockSpec((1, ATTN_W), lambda i: (0, 0))
    return pl.pallas_call(
        body, name="qk_bwd", grid=(s // ts,),
        in_specs=[col(0), col(1), row, row, row, vec, vec, pl.BlockSpec((ATTN_W, ATTN_W), lambda i: (0, 0))],
        out_specs=[pl.BlockSpec((ts, 3 * ATTN_W), lambda i: (i, 0)), vec, vec],
        out_shape=[jax.ShapeDtypeStruct((s, 3 * ATTN_W), BF16), jax.ShapeDtypeStruct((1, ATTN_W), F32),
                   jax.ShapeDtypeStruct((1, ATTN_W), F32)],
        compiler_params=_params(("arbitrary",)))(proj, proj, dqh, dkh, dv, qg, kg, bd)


def _split(x):
    hi = x.astype(BF16)
    return hi, (x - hi.astype(F32)).astype(BF16)


def _tri_dot(parts, tri):
    hi, lo = parts
    return jnp.dot(hi, tri, preferred_element_type=F32) + jnp.dot(lo, tri, preferred_element_type=F32)


def _log_sigmoids(z):
    neg_abs = lax.bitcast_convert_type(lax.bitcast_convert_type(z, jnp.uint32) | jnp.uint32(0x80000000), F32)
    lb = jnp.minimum(z, 0.0) - jnp.log(1.0 + jnp.exp(neg_abs))
    return lb, lb - z


DEAD_LOG_WEIGHT = -106.0


def _sweep_key_blocks(tiles, alive, i):
    @pl.when(i == 0)
    def _():
        tiles([0], [True])

    @pl.when(i > 0)
    def _():
        tiles([i, i - 1], [True, False])

    def more(state):
        kb, live = state
        return jnp.logical_and(kb >= 0, live > 0)

    def step(state):
        kb, _ = state
        tiles([kb], [False])
        return kb - 1, alive().astype(jnp.int32)

    lax.while_loop(more, step, (i - 2, alive().astype(jnp.int32)))


def _head_masks():
    lane = lax.broadcasted_iota(jnp.int32, (1, LANES), 1)
    return [lane < HEAD_DIM, lane >= HEAD_DIM]


def _attn_fwd(qs, kh, vb, tri_excl, carry=None):
    s = qs.shape[0]
    t = tri_excl.shape[0]
    nq = s // t

    def body(q_ref, k_ref, v_ref, tri_ref, o_ref, ob_ref, acc_ref, c_ref):
        i = pl.program_id(1)
        hmask = _head_masks()
        q = q_ref[...]
        qm = [jnp.where(hm, q, jnp.zeros_like(q)) for hm in hmask]
        acc_ref[...] = jnp.zeros_like(acc_ref)
        c_ref[...] = jnp.zeros_like(c_ref)
        causal = (lax.broadcasted_iota(jnp.int32, (t, t), 1) < lax.broadcasted_iota(jnp.int32, (t, t), 0))

        def tiles(kbs, masked):
            tri = tri_ref[...]
            starts = [pl.multiple_of(kb * t, t) for kb in kbs]
            kblks = [k_ref[pl.ds(k0, t), :] for k0 in starts]
            vblks = [v_ref[pl.ds(k0, t), :] for k0 in starts]
            chains = [(j, h) for j in range(len(kbs)) for h in range(2)]
            carry = [c_ref[h] for h in range(2)]
            pv = [None, None]
            lbs, loms, between = {}, {}, {}
            for step in range(len(chains) + 2):
                if step < len(chains):
                    j, h = chains[step]
                    z = lax.dot_general(qm[h], kblks[j], (NT, ((), ())), preferred_element_type=F32)
                    lbs[step], lom = _log_sigmoids(z)
                    loms[step] = jnp.where(causal, lom, 0.0) if masked[j] else lom
                if 0 <= step - 1 < len(chains):
                    between[step - 1] = _tri_dot(_split(loms[step - 1]), tri)
                if 0 <= step - 2 < len(chains):
                    n = step - 2
                    j, h = chains[n]
                    w = jnp.exp(lbs[n] + between[n] + carry[h])
                    if masked[j]:
                        w = jnp.where(causal, w, 0.0)
                    carry[h] = carry[h] + jnp.sum(loms[n], axis=-1, keepdims=True)
                    part = jnp.dot(w.astype(BF16), vblks[j], preferred_element_type=F32)
                    pv[h] = part if pv[h] is None else pv[h] + part
            for h in range(2):
                c_ref[h] = carry[h]
                acc_ref[h] += pv[h]

        _sweep_key_blocks(tiles, lambda: jnp.max(c_ref[...]) > DEAD_LOG_WEIGHT, i)
        o = jnp.where(hmask[0], acc_ref[0], acc_ref[1])
        o_ref[...] = o
        ob_ref[...] = o.astype(BF16)

    qspec = pl.BlockSpec((t, LANES), lambda hp, i: (i, hp))
    kspec = pl.BlockSpec((s, LANES), lambda hp, i: (0, hp))
    return _call(
        body, name="attn_fwd", grid=(ATTN_W // LANES, nq),
        in_specs=[qspec, kspec, kspec, pl.BlockSpec((t, t), lambda hp, i: (0, 0))],
        out_specs=[qspec, qspec],
        out_shape=[jax.ShapeDtypeStruct((s, ATTN_W), F32), jax.ShapeDtypeStruct((s, ATTN_W), BF16)],
        scratch_shapes=[pltpu.VMEM((2, t, LANES), F32), pltpu.VMEM((2, t, 1), F32)],
        args=(qs, kh, vb, tri_excl), semantics=("parallel", "arbitrary"), carry=carry)


def _attn_bwd(qs, kh, vb, o, dmix, tri_excl, tri_incl, carry=None):
    s = qs.shape[0]
    t = tri_excl.shape[0]
    nq = s // t

    def body(q_ref, k_ref, v_ref, o_ref, do_ref, te_ref, ti_ref, dq_ref, dk_ref, dv_ref, dqacc_ref, c_ref):
        i = pl.program_id(1)

        @pl.when(i == 0)
        def _():
            dk_ref[...] = jnp.zeros_like(dk_ref)
            dv_ref[...] = jnp.zeros_like(dv_ref)

        hmask = _head_masks()
        q = q_ref[...]
        do = do_ref[...]
        dob = do.astype(BF16)
        prod = dob.astype(F32) * o_ref[...]
        qm =[jnp.where(hm, q, jnp.zeros_like(q)) for hm in hmask]
        dom = [jnp.where(hm, dob, jnp.zeros_like(dob)) for hm in hmask]
        total = [jnp.sum(jnp.where(hm, prod, 0.0), axis=-1, keepdims=True) for hm in hmask]
        dqacc_ref[...] = jnp.zeros_like(dqacc_ref)
        c_ref[...] = jnp.zeros_like(c_ref)
        causal = (lax.broadcasted_iota(jnp.int32, (t, t), 1) < lax.broadcasted_iota(jnp.int32, (t, t), 0))

        def tiles(kbs, masked):
            te = te_ref[...]
            ti = ti_ref[...]
            starts = [pl.multiple_of(kb * t, t) for kb in kbs]
            kblks = [k_ref[pl.ds(k0, t), :] for k0 in starts]
            vblks = [v_ref[pl.ds(k0, t), :] for k0 in starts]
            chains = [(j, h) for j in range(len(kbs)) for h in range(2)]
            c_lom = [c_ref[2 * h] for h in range(2)]
            c_g = [c_ref[2 * h + 1] for h in range(2)]
            lbs, loms, dws, between, wbs, gs, g_after = {}, {}, {}, {}, {}, {}, {}
            dq = [None, None]
            dk = [None] * len(kbs)
            dv = [None] * len(kbs)
            add = lambda acc, part: part if acc is None else acc + part
            for step in range(len(chains) + 3):
                if step < len(chains):
                    j, h = chains[step]
                    z = lax.dot_general(qm[h], kblks[j], (NT, ((), ())), preferred_element_type=F32)
                    dws[step] = lax.dot_general(dom[h], vblks[j], (NT, ((), ())), preferred_element_type=F32)
                    lbs[step], lom = _log_sigmoids(z)
                    loms[step] = jnp.where(causal, lom, 0.0) if masked[j] else lom
                if 0 <= step - 1 < len(chains):
                    between[step - 1] = _tri_dot(_split(loms[step - 1]), te)
                if 0 <= step - 2 < len(chains):
                    n = step - 2
                    j, h = chains[n]
                    w = jnp.exp(lbs[n] + between[n] + c_lom[h])
                    if masked[j]:
                        w = jnp.where(causal, w, 0.0)
                    c_lom[h] = c_lom[h] + jnp.sum(loms[n], axis=-1, keepdims=True)
                    wbs[n] = w.astype(BF16)
                    gs[n] = dws[n] * wbs[n].astype(F32)
                    g_after[n] = _tri_dot(_split(gs[n]), ti)
                if 0 <= step - 3 < len(chains):
                    n = step - 3
                    j, h = chains[n]
                    beta = jnp.exp(lbs[n])
                    dz = gs[n] * (1.0 - beta) - beta * (total[h] - (g_after[n] + c_g[h]))
                    if masked[j]:
                        dz = jnp.where(causal, dz, 0.0)
                    c_g[h] = c_g[h] + jnp.sum(gs[n], axis=-1, keepdims=True)
                    dzb = dz.astype(BF16)
                    dq[h] = add(dq[h], jnp.dot(dzb, kblks[j], preferred_element_type=F32))
                    dk[j] = add(dk[j], lax.dot_general(dzb, qm[h], (TN, ((), ())), preferred_element_type=F32))
                    dv[j] = add(dv[j], lax.dot_general(wbs[n], dom[h], (TN, ((), ())), preferred_element_type=F32))
            for h in range(2):
                c_ref[2 * h] = c_lom[h]
                c_ref[2 * h + 1] = c_g[h]
                dqacc_ref[h] += dq[h]
            for j, k0 in enumerate(starts):
                dk_ref[pl.ds(k0, t), :] += dk[j]
                dv_ref[pl.ds(k0, t), :] += dv[j]

        _sweep_key_blocks(tiles, lambda: jnp.maximum(jnp.max(c_ref[0]), jnp.max(c_ref[2])) > DEAD_LOG_WEIGHT, i)
        dq_ref[...] = jnp.where(hmask[0], dqacc_ref[0], dqacc_ref[1]) * ATTN_SCALE

    qspec = pl.BlockSpec((t, LANES), lambda hp, i: (i, hp))
    kspec = pl.BlockSpec((s, LANES), lambda hp, i: (0, hp))
    tspec = pl.BlockSpec((t, t), lambda hp, i: (0, 0))
    sds = jax.ShapeDtypeStruct((s, ATTN_W), F32)
    return _call(
        body, name="attn_bwd", grid=(ATTN_W // LANES, nq),
        in_specs=[qspec, kspec, kspec, qspec, qspec, tspec, tspec],
        out_specs=[qspec, kspec, kspec], out_shape=[sds, sds, sds],
        scratch_shapes=[pltpu.VMEM((2, t, LANES), F32), pltpu.VMEM((4, t, 1), F32)],
        args=(qs, kh, vb, o, dmix, tri_excl, tri_incl), semantics=("parallel", "arbitrary"), carry=carry)


CONV_ROWS = 64
CONV_COLS = 256


SUBLANES = 8


def _shift_copies(src_ref, sh_ref):
    length = sh_ref.shape[1]
    for r in range(1, SUBLANES):
        sh_ref[r - 1] = src_ref[r:r + length, :]


def _shift_scratch(ts):
    return pltpu.VMEM((SUBLANES - 1, ts + CONV_HALO - SUBLANES, CONV_W), F32)


def _rows_at(src_ref, sh_ref, offset, r0, rows, cols):
    r = offset % SUBLANES
    base = offset - r + r0
    return src_ref[base:base + rows, cols] if r == 0 else sh_ref[r - 1, base:base + rows, cols]


def _taps(src_ref, sh_ref, w_ref, n_taps, first_row, rows, reverse=False):
    width = src_ref.shape[1]
    cols = min(CONV_COLS, width)
    out = []
    for r0 in range(0, rows, CONV_ROWS):
        for c0 in range(0, width, cols):
            acc = jnp.zeros((CONV_ROWS, cols), F32)
            for k in range(n_taps):
                off = (n_taps - 1 - k) if reverse else k
                acc = acc + w_ref[k:k + 1, c0:c0 + cols] * _rows_at(src_ref, sh_ref, first_row + off, r0, CONV_ROWS,
                                                                    slice(c0, c0 + cols))
            out.append(((r0, c0), acc))
    return out


def _conv_fwd(proj, dw_w, dw_b, ln_g, ln_b, carry=None):
    s = proj.shape[0]
    ts = _row_tile(s)
    hb = ts // CONV_HALO

    def body(a_ref, g_ref, ha_ref, hg_ref, w_ref, b_ref, lg_ref, lb_ref, c1_ref, c3_ref, pad_ref, sh_ref):
        i = pl.program_id(0)
        halo = ha_ref[...] * _sigmoid(hg_ref[...])
        pad_ref[0:CONV_HALO, :] = jnp.where(i > 0, halo, 0.0)
        pad_ref[CONV_HALO:, :] = a_ref[...] * _sigmoid(g_ref[...])
        _shift_copies(pad_ref, sh_ref)
        first = CONV_HALO - (CONV_K - 1)
        for (r0, c0), acc in _taps(pad_ref, sh_ref, w_ref, CONV_K, first, ts):
            c1_ref[r0:r0 + CONV_ROWS, c0:c0 + acc.shape[1]] = acc + b_ref[:, c0:c0 + acc.shape[1]]
        c1 = c1_ref[...]
        xc = c1 - jnp.mean(c1, axis=-1, keepdims=True)
        c2 = xc * _rstd(xc) * lg_ref[...] + lb_ref[...]
        c3_ref[...] = (c2 * _sigmoid(c2)).astype(BF16)

    cur = lambda c: pl.BlockSpec((ts, CONV_W), lambda i: (i, c))
    halo = lambda c: pl.BlockSpec((CONV_HALO, CONV_W), lambda i: (jnp.maximum(i * hb - 1, 0), c))
    vec = pl.BlockSpec((1, CONV_W), lambda i: (0, 0))
    row = pl.BlockSpec((ts, CONV_W), lambda i: (i, 0))
    return _call(
        body, name="conv_fwd", grid=(s // ts,),
        in_specs=[cur(3), cur(4), halo(3), halo(4), pl.BlockSpec((CONV_HALO, CONV_W), lambda i: (0, 0)), vec, vec, vec],
        out_specs=[row, row],
        out_shape=[jax.ShapeDtypeStruct((s, CONV_W), F32), jax.ShapeDtypeStruct((s, CONV_W), BF16)],
        scratch_shapes=[pltpu.VMEM((ts + CONV_HALO, CONV_W), F32), _shift_scratch(ts)],
        args=(proj, proj, proj, proj, dw_w, dw_b, ln_g, ln_b), semantics=("parallel",), carry=carry)


def _conv_bwd_ln(dmix, c1, ln_g, ln_b):
    s = c1.shape[0]
    ts = _row_tile(s)

    def body(d_ref, c1_ref, lg_ref, lb_ref, dc1_ref, glg_ref, glb_ref, gb_ref):
        c1v = c1_ref[...]
        xc = c1v - jnp.mean(c1v, axis=-1, keepdims=True)
        r = _rstd(xc)
        xhat = xc * r
        c2 = xhat * lg_ref[...] + lb_ref[...]
        sg = _sigmoid(c2)
        dc2 = d_ref[...] * (sg * (1.0 + c2 * (1.0 - sg)))
        dxhat = dc2 * lg_ref[...]
        dc1 = r * (dxhat - jnp.mean(dxhat, axis=-1, keepdims=True)
                   - xhat * jnp.mean(dxhat * xhat, axis=-1, keepdims=True))
        dc1_ref[...] = dc1
        parts = [(glg_ref, jnp.sum(dc2 * xhat, axis=0, keepdims=True)),
                 (glb_ref, jnp.sum(dc2, axis=0, keepdims=True)),
                 (gb_ref, jnp.sum(dc1, axis=0, keepdims=True))]

        @pl.when(pl.program_id(0) == 0)
        def _():
            for ref, part in parts:
                ref[...] = part

        @pl.when(pl.program_id(0) > 0)
        def _():
            for ref, part in parts:
                ref[...] += part

    row = pl.BlockSpec((ts, CONV_W), lambda i: (i, 0))
    vec = pl.BlockSpec((1, CONV_W), lambda i: (0, 0))
    vsd = jax.ShapeDtypeStruct((1, CONV_W), F32)
    return pl.pallas_call(
        body, name="conv_bwd_ln", grid=(s // ts,),
        in_specs=[pl.BlockSpec((ts, CONV_W), lambda i: (i, 1)), row, vec, vec],
        out_specs=[row, vec, vec, vec],
        out_shape=[jax.ShapeDtypeStruct((s, CONV_W), F32), vsd, vsd, vsd],
        compiler_params=_params(("arbitrary",)))(dmix, c1, ln_g, ln_b)


def _conv_bwd_taps(proj, dc1, dw_w):
    s = proj.shape[0]
    ts = _row_tile(s)
    hb = ts // CONV_HALO
    last = s // CONV_HALO - 1
    nsteps = s // ts

    def body(a_ref, g_ref, ha_ref, hg_ref, d_ref, hd_ref, w_ref, out_ref, gw_ref, pad_ref, dpad_ref, sh_ref):
        i = pl.program_id(0)
        av = a_ref[...]
        sg = _sigmoid(g_ref[...])
        halo = ha_ref[...] * _sigmoid(hg_ref[...])
        pad_ref[0:CONV_HALO, :] = jnp.where(i > 0, halo, 0.0)
        pad_ref[CONV_HALO:, :] = av * sg
        dpad_ref[0:ts, :] = d_ref[...]
        dpad_ref[ts:, :] = jnp.where(i < nsteps - 1, hd_ref[...], 0.0)

        @pl.when(i == 0)
        def _():
            gw_ref[...] = jnp.zeros_like(gw_ref)

        _shift_copies(pad_ref, sh_ref)
        first = CONV_HALO - (CONV_K - 1)
        fold = lambda v: jnp.sum(v.reshape(CONV_ROWS // SUBLANES, SUBLANES, v.shape[1]), axis=0)
        for c0 in range(0, CONV_W, CONV_COLS):
            cs = slice(c0, c0 + CONV_COLS)
            for k in range(CONV_K):
                acc = None
                for r0 in range(0, ts, CONV_ROWS):
                    part = fold(d_ref[r0:r0 + CONV_ROWS, cs] * _rows_at(pad_ref, sh_ref, first + k, r0, CONV_ROWS, cs))
                    acc = part if acc is None else acc + part
                gw_ref[k:k + 1, cs] += jnp.sum(acc, axis=0, keepdims=True)
        _shift_copies(dpad_ref, sh_ref)
        for (r0, c0), dc0 in _taps(dpad_ref, sh_ref, w_ref, CONV_K, 0, ts, reverse=True):
            cs = slice(c0, c0 + dc0.shape[1])
            a_c = a_ref[r0:r0 + CONV_ROWS, cs]
            sg_c = _sigmoid(g_ref[r0:r0 + CONV_ROWS, cs])
            out_ref[r0:r0 + CONV_ROWS, cs] = (dc0 * sg_c).astype(BF16)
            out_ref[r0:r0 + CONV_ROWS, CONV_W + c0:CONV_W + c0 + dc0.shape[1]] = (
                dc0 * a_c * sg_c * (1.0 - sg_c)).astype(BF16)

    cur = lambda c: pl.BlockSpec((ts, CONV_W), lambda i: (i, c))
    halo = lambda c: pl.BlockSpec((CONV_HALO, CONV_W), lambda i: (jnp.maximum(i * hb - 1, 0), c))
    row = pl.BlockSpec((ts, CONV_W), lambda i: (i, 0))
    nxt = pl.BlockSpec((CONV_HALO, CONV_W), lambda i: (jnp.minimum((i + 1) * hb, last), 0))
    wspec = pl.BlockSpec((CONV_HALO, CONV_W), lambda i: (0, 0))
    return pl.pallas_call(
        body, name="conv_bwd_taps", grid=(nsteps,),
        in_specs=[cur(3), cur(4), halo(3), halo(4), row, nxt, wspec],
        out_specs=[pl.BlockSpec((ts, 2 * CONV_W), lambda i: (i, 0)), wspec],
        out_shape=[jax.ShapeDtypeStruct((s, 2 * CONV_W), BF16), jax.ShapeDtypeStruct((CONV_HALO, CONV_W), F32)],
        scratch_shapes=[pltpu.VMEM((ts + CONV_HALO, CONV_W), F32), pltpu.VMEM((ts + CONV_HALO, CONV_W), F32),
                        _shift_scratch(ts)],
        compiler_params=_params(("arbitrary",)))(proj, proj, proj, proj, dc1, dc1, dw_w)


SQRT_HALF = 0.7071067811865476
INV_SQRT_2PI = 0.3989422804014327


def _gelu_parts(x):
    cdf = 0.5 * (1.0 + lax.erf(x * SQRT_HALF))
    return x * cdf, cdf + x * (INV_SQRT_2PI * jnp.exp(-0.5 * x * x))


def _ffn_tile(dff):
    return dff // 2


FFN_ROWS = 64
FFN_COLS = LANES


def _ffn_chunks(ts, tc):
    return [(r0, slice(c0, c0 + FFN_COLS)) for c0 in range(0, tc, FFN_COLS) for r0 in range(0, ts, FFN_ROWS)]


def _ffn_gate2(pad_ref, w_ref, b_ref, r0, cs):
    first = FFN_HALO - (FFN_K - 1) + r0
    g2 = b_ref[:, cs] + w_ref[0:1, cs] * pad_ref[first:first + FFN_ROWS, cs]
    for k in range(1, FFN_K):
        g2 = g2 + w_ref[k:k + 1, cs] * pad_ref[first + k:first + k + FFN_ROWS, cs]
    return g2


def _ffn_act(up, fw, fb):
    s = up.shape[0]
    dff = up.shape[1] // 2
    tc = _ffn_tile(dff)
    nj = dff // tc
    ts = _row_tile(s) // 2
    hb = ts // FFN_HALO

    def body(g_ref, v_ref, hg_ref, w_ref, b_ref, act_ref, pad_ref):
        i = pl.program_id(0)
        pad_ref[0:FFN_HALO, :] = jnp.where(i > 0, hg_ref[...].astype(F32), 0.0)
        pad_ref[FFN_HALO:, :] = g_ref[...].astype(F32)
        for r0, cs in _ffn_chunks(ts, tc):
            gelu, _ = _gelu_parts(_ffn_gate2(pad_ref, w_ref, b_ref, r0, cs))
            act_ref[r0:r0 + FFN_ROWS, cs] = (gelu * v_ref[r0:r0 + FFN_ROWS, cs].astype(F32)).astype(BF16)

    return pl.pallas_call(
        body, name="ffn_act", grid=(s // ts, nj),
        in_specs=[pl.BlockSpec((ts, tc), lambda i, j: (i, j)), pl.BlockSpec((ts, tc), lambda i, j: (i, j + nj)),
                  pl.BlockSpec((FFN_HALO, tc), lambda i, j: (jnp.maximum(i * hb - 1, 0), j)),
                  pl.BlockSpec((FFN_HALO, tc), lambda i, j: (0, j)), pl.BlockSpec((1, tc), lambda i, j: (0, j))],
        out_specs=pl.BlockSpec((ts, tc), lambda i, j: (i, j)),
        out_shape=jax.ShapeDtypeStruct((s, dff), BF16),
        scratch_shapes=[pltpu.VMEM((ts + FFN_HALO, tc), F32)],
        compiler_params=_params(("parallel", "parallel")))(up, up, up, fw, fb)


def _ffn_bwd_act(dact, up, fw, fb, carry=None):
    s = up.shape[0]
    dff = up.shape[1] // 2
    tc = _ffn_tile(dff)
    nj = dff // tc
    ts = _row_tile(s) // 2
    hb = ts // FFN_HALO

    def body(d_ref, g_ref, v_ref, hg_ref, w_ref, b_ref, dg2_ref, dval_ref, gw_ref, gb_ref, pad_ref):
        i = pl.program_id(1)
        pad_ref[0:FFN_HALO, :] = jnp.where(i > 0, hg_ref[...].astype(F32), 0.0)
        pad_ref[FFN_HALO:, :] = g_ref[...].astype(F32)

        @pl.when(i == 0)
        def _():
            gw_ref[...] = jnp.zeros_like(gw_ref)
            gb_ref[...] = jnp.zeros_like(gb_ref)

        fold = lambda v: jnp.sum(v.reshape(FFN_ROWS // SUBLANES, SUBLANES, FFN_COLS), axis=0)
        first = FFN_HALO - (FFN_K - 1)
        sums = {}
        for r0, cs in _ffn_chunks(ts, tc):
            rows = slice(r0, r0 + FFN_ROWS)
            shifted = [pad_ref[first + k + r0:first + k + r0 + FFN_ROWS, cs] for k in range(FFN_K)]
            g2 = b_ref[:, cs] + w_ref[0:1, cs] * shifted[0]
            for k in range(1, FFN_K):
                g2 = g2 + w_ref[k:k + 1, cs] * shifted[k]
            gelu, dgelu = _gelu_parts(g2)
            dactv = d_ref[rows, cs]
            dval_ref[rows, cs] = (dactv * gelu).astype(BF16)
            dg2 = dactv * v_ref[rows, cs].astype(F32) * dgelu
            dg2_ref[rows, cs] = dg2.astype(BF16)
            parts = [fold(dg2)] + [fold(dg2 * shifted[k]) for k in range(FFN_K)]
            sums = {n: part + sums[n] if r0 else part for n, part in enumerate(parts)}
            if r0 + FFN_ROWS == ts:
                gb_ref[:, cs] += jnp.sum(sums[0], axis=0, keepdims=True)
                for k in range(FFN_K):
                    gw_ref[k:k + 1, cs] += jnp.sum(sums[1 + k], axis=0, keepdims=True)

    blk = pl.BlockSpec((ts, tc), lambda j, i: (i, j))
    wspec = pl.BlockSpec((FFN_HALO, tc), lambda j, i: (0, j))
    bspec = pl.BlockSpec((1, tc), lambda j, i: (0, j))
    return _call(
        body, name="ffn_bwd_act", grid=(nj, s // ts),
        in_specs=[blk, blk, pl.BlockSpec((ts, tc), lambda j, i: (i, j + nj)),
                  pl.BlockSpec((FFN_HALO, tc), lambda j, i: (jnp.maximum(i * hb - 1, 0), j)), wspec, bspec],
        out_specs=[blk, pl.BlockSpec((ts, tc), lambda j, i: (i, j + nj)), wspec, bspec],
        out_shape=[jax.ShapeDtypeStruct((s, dff), BF16), jax.ShapeDtypeStruct((s, 2 * dff), BF16),
                   jax.ShapeDtypeStruct((FFN_HALO, dff), F32), jax.ShapeDtypeStruct((1, dff), F32)],
        scratch_shapes=[pltpu.VMEM((ts + FFN_HALO, tc), F32)],
        args=(dact, up, up, up, fw, fb), semantics=("parallel", "arbitrary"), carry=carry)


def _ffn_bwd_conv(dg2, fw, dup):
    s, dff = dg2.shape
    tc = _ffn_tile(dff)
    ts = _row_tile(s) // 2
    hb = ts // FFN_HALO
    last = s // FFN_HALO - 1
    nsteps = s // ts

    def body(d_ref, hd_ref, w_ref, dup_ref, out_ref, pad_ref):
        i = pl.program_id(0)
        pad_ref[0:ts, :] = d_ref[...].astype(F32)
        pad_ref[ts:, :] = jnp.where(i < nsteps - 1, hd_ref[...].astype(F32), 0.0)
        for r0, cs in _ffn_chunks(ts, tc):
            dg = w_ref[0:1, cs] * pad_ref[r0 + FFN_K - 1:r0 + FFN_K - 1 + FFN_ROWS, cs]
            for k in range(1, FFN_K):
                dg = dg + w_ref[k:k + 1, cs] * pad_ref[r0 + FFN_K - 1 - k:r0 + FFN_K - 1 - k + FFN_ROWS, cs]
            out_ref[r0:r0 + FFN_ROWS, cs] = dg.astype(BF16)

    blk = pl.BlockSpec((ts, tc), lambda i, j: (i, j))
    return pl.pallas_call(
        body, name="ffn_bwd_conv", grid=(nsteps, dff // tc),
        in_specs=[blk, pl.BlockSpec((FFN_HALO, tc), lambda i, j: (jnp.minimum((i + 1) * hb, last), j)),
                  pl.BlockSpec((FFN_HALO, tc), lambda i, j: (0, j)), ANY],
        out_specs=blk, out_shape=jax.ShapeDtypeStruct(dup.shape, BF16), input_output_aliases={3: 0},
        scratch_shapes=[pltpu.VMEM((ts + FFN_HALO, tc), F32)],
        compiler_params=_params(("parallel", "parallel")))(dg2, dg2, fw, dup)


def _ple_loss(h2, zg, pp, target):
    s, d = h2.shape
    ts = _row_tile(s)

    def body(h_ref, z_ref, p_ref, t_ref, dh_ref, dpp_ref, dz_ref, loss_ref):
        pg = _sigmoid(z_ref[...])
        ppv = p_ref[...]
        diff = h_ref[...] + pg * ppv - t_ref[...]
        dh = diff * (1.0 / d)
        dh_ref[...] = dh
        dpp_ref[...] = (dh * pg).astype(BF16)
        dz_ref[...] = (dh * ppv * pg * (1.0 - pg)).astype(BF16)
        part = jnp.sum(jnp.sum(diff * diff, axis=0, keepdims=True), axis=1, keepdims=True)

        @pl.when(pl.program_id(0) == 0)
        def _():
            loss_ref[...] = jnp.zeros_like(loss_ref)

        loss_ref[...] += jnp.broadcast_to(part, loss_ref.shape)

    row = pl.BlockSpec((ts, d), lambda i: (i, 0))
    return pl.pallas_call(
        body, name="ple_loss", grid=(s // ts,), in_specs=[row, row, row, row],
        out_specs=[row, row, row, pl.BlockSpec((8, LANES), lambda i: (0, 0))],
        out_shape=[jax.ShapeDtypeStruct((s, d), F32), jax.ShapeDtypeStruct((s, d), BF16),
                   jax.ShapeDtypeStruct((s, d), BF16), jax.ShapeDtypeStruct((8, LANES), F32)],
        compiler_params=_params(("arbitrary",)))(h2, zg, pp, target)


def _local_step(x, p, target, w, shards=None, unpack_filters=None):
    riding = shards is not None
    major = lambda n, g: g if n in COL_SHARDED else g.reshape(N_CHIPS, -1, g.shape[-1])
    full = lambda n, g: g if n in COL_SHARDED else g.reshape(-1, g.shape[-1])
    first = lambda res: (res[0], res[1:]) if riding else (res, [])

    def ride(names, arrays):
        return ([arrays[n] for n in names], False) if riding else None

    def leave(grads):
        return (grads, True) if riding else None

    def land(names, got):
        return {n: full(n, g) for n, g in zip(names, got)}
    s = x.shape[0]
    t = min(256, s)
    tri = jnp.tril(jnp.ones((t, t), F32))
    tri_incl = tri.astype(BF16)
    tri_excl = jnp.tril(jnp.ones((t, t), F32), -1).astype(BF16)
    bd = jnp.kron(jnp.eye(N_HEADS, dtype=F32), jnp.ones((HEAD_DIM, HEAD_DIM), F32)).astype(BF16)
    qg = jnp.tile(w["q_gain"], (1, N_HEADS))
    kg = jnp.tile(w["k_gain"], (1, N_HEADS))
    pb = p.astype(BF16)

    u1, got = first(_rms_fwd(x, w["g_mix"], "rms_mix", carry=ride(["w_in", "filters"], shards)))
    if riding:
        w = {**w, "w_in": got[0], **unpack_filters(got[1])}
    dw_w = jnp.pad(w["dw_w"], ((0, CONV_HALO - CONV_K), (0, 0)))
    fw = jnp.pad(w["ffn_conv_w"], ((0, FFN_HALO - FFN_K), (0, 0)))
    proj = _mm_nn_sharded(u1, w["w_in"], "mm_in")
    qs, kh, vb = _qkv_prep(proj, qg, kg, bd)
    with_attn, with_conv = ["w_out", "w_up", "w_ple_gate", "w_ple_proj"], ["w_down"]
    o, ob, *got = _attn_fwd(qs, kh, vb, tri_excl, carry=ride(with_attn, shards))
    w = {**w, **land(with_attn, got)}
    c1, c3, *got = _conv_fwd(proj, dw_w, w["dw_b"], w["conv_ln_g"], w["conv_ln_b"], carry=ride(with_conv, shards))
    w = {**w, **land(with_conv, got)}
    mix = jnp.concatenate([ob, c3], axis=1)
    h1 = _mm_nn_full(mix, w["w_out"], "mm_out", res=x)
    u2 = _rms_fwd(h1, w["g_ffn"], "rms_ffn")
    up = _mm_nn_sharded(u2, w["w_up"], "mm_up", out_dtype=BF16)
    act = _ffn_act(up, fw, w["ffn_conv_b"])
    h2 = _mm_nn_full(act, w["w_down"], "mm_down", res=h1)
    u3 = _rms_fwd(h2, w["g_ple"], "rms_ple")
    zg = _mm_nn_full(u3, w["w_ple_gate"], "mm_ple_gate")
    pp = _mm_nn_sharded(pb, w["w_ple_proj"], "mm_ple_proj")
    dh3, dpp, dz, sq = _ple_loss(h2, zg, pp, target)

    big = {}
    small = {}
    big["w_ple_proj"] = _mm_tn_sharded(pb, dpp, N_CHIPS, "mm_g_ple_proj")
    big["w_ple_gate"] = _mm_tn_full(u3, dz, "mm_g_ple_gate", tm=u3.shape[1])
    du3 = _mm_nt_full(dz, w["w_ple_gate"], "mm_d_ple_gate")
    dh2, dh2b, small["g_ple"] = _rms_bwd(h2, du3, w["g_ple"], dh3, "rms_ple_bwd")
    slots = {}
    with_down, with_ffn, with_attn = ["w_ple_proj", "w_ple_gate"], ["w_down"], ["w_up", "w_out"]
    leaving = leave([major(n, big.pop(n)) for n in with_down]) if riding else None
    big["w_down"], got = first(_mm_tn_full(act, dh2b, "mm_g_down", tm=act.shape[1] // 2, carry=leaving))
    slots.update(zip(with_down, got))
    dact = _mm_nt_full(dh2b, w["w_down"], "mm_d_down")
    leaving = leave([major(n, big.pop(n)) for n in with_ffn]) if riding else None
    dg2, dup, gfw, small["ffn_conv_b"], *got = _ffn_bwd_act(dact, up, fw, w["ffn_conv_b"], carry=leaving)
    slots.update(zip(with_ffn, got))
    small["ffn_conv_w"] = gfw[:FFN_K]
    dup = _ffn_bwd_conv(dg2, fw, dup)
    big["w_up"] = _mm_tn_sharded(u2, dup, N_CHIPS, "mm_g_up", n_split=2)
    du2, = _mm_nt_sharded(dup, w["w_up"], "mm_d_up")
    dh1, dh1b, small["g_ffn"] = _rms_bwd(h1, du2, w["g_ffn"], dh2, "rms_ffn_bwd")
    big["w_out"] = _mm_tn_full(mix, dh1b, "mm_g_out", tm=mix.shape[1])
    dmix = _mm_nt_full(dh1b, w["w_out"], "mm_d_out")
    dc1, small["conv_ln_g"], small["conv_ln_b"], small["dw_b"] = _conv_bwd_ln(dmix, c1, w["conv_ln_g"], w["conv_ln_b"])
    dcacg, gdw = _conv_bwd_taps(proj, dc1, dw_w)
    small["dw_w"] = gdw[:CONV_K]
    leaving = leave([major(n, big.pop(n)) for n in with_attn]) if riding else None
    dqh, dkh, dv, *got = _attn_bwd(qs, kh, vb, o, dmix, tri_excl, tri_incl, carry=leaving)
    slots.update(zip(with_attn, got))
    dqkv, gq, gk = _qk_bwd(proj, dqh, dkh, dv, qg, kg, bd)
    small["q_gain"] = gq.reshape(N_HEADS, HEAD_DIM).sum(axis=0, keepdims=True)
    small["k_gain"] = gk.reshape(N_HEADS, HEAD_DIM).sum(axis=0, keepdims=True)
    dproj = jnp.concatenate([dqkv, dcacg], axis=1)
    big["w_in"] = _mm_tn_sharded(u1, dproj, N_CHIPS, "mm_g_in")
    half = big["w_in"].shape[1] // 2
    halves = [big["w_in"][:, :half], big.pop("w_in")[:, half:]] if riding else [None, None]
    du1, *got_a = _mm_nt_sharded(dproj, w["w_in"], "mm_d_in", carry=leave([halves[0]]) if riding else None)
    grad_x, _, small["g_mix"], *got_b = _rms_bwd(x, du1, w["g_mix"], dh1, "rms_mix_bwd",
                                                 carry=leave([halves[1]]) if riding else None)
    if riding:
        slots["w_in"] = jnp.concatenate([got_a[0], got_b[0]], axis=1)
    return sq[0, 0], grad_x, big, slots, small


def _exchange(srcs, per_chip, name):
    n = len(srcs)

    def body(*refs):
        src_refs, land_refs = refs[:n], refs[n:2 * n]
        send_sems, recv_sems, local_sems = refs[2 * n:]
        local = _local_copies(src_refs, land_refs, local_sems, per_chip)
        sends = _chip_copies(src_refs, land_refs, send_sems, recv_sems, per_chip, False)
        for cp in local + sends:
            cp.start()
        for cp in _chip_copies(src_refs, land_refs, send_sems, recv_sems, per_chip, True):
            cp.wait_recv()
        for cp in sends:
            cp.wait_send()
        for cp in local:
            cp.wait()

    return pl.pallas_call(
        body, name=name, in_specs=[ANY] * n, out_specs=[ANY] * n, out_shape=_exchanged_shapes(srcs, per_chip),
        scratch_shapes=_exchange_sems(n))(*srcs)


def _finish_exchange(mine, small):
    n = len(mine)

    def body(*refs):
        gin, sin = refs[:n], refs[n]
        gout, sout = refs[n + 1:2 * n + 1], refs[2 * n + 1]
        send_sems, recv_sems, small_send, small_recv, local_sem = refs[2 * n + 2:]
        x, y, c, _ = _position()
        dev = 4 * x + 2 * y + c
        flip = lambda v, bit: 1 - v if bit else v
        others = [(flip(x, k & 4), flip(y, k & 2), flip(c, k & 1)) for k in range(1, N_DEV)]
        local = pltpu.make_async_copy(sin, sout.at[dev], local_sem)
        local.start()
        swaps = [_remote(gin[a], gout[a], send_sems, recv_sems, a, (x, y, 1 - c)) for a in range(n)]
        sends = swaps + [_remote(sin, sout.at[dev], small_send, small_recv, k, peer) for k, peer in enumerate(others)]
        for cp in sends:
            cp.start()
        for cp in swaps:
            cp.wait_recv()
        for k, (px, py, pc) in enumerate(others):
            _remote(sin, sout.at[4 * px + 2 * py + pc], small_send, small_recv, k, (px, py, pc)).wait_recv()
        for cp in sends:
            cp.wait_send()
        local.wait()

    return pl.pallas_call(
        body, name="finish_exchange", in_specs=[ANY] * (n + 1), out_specs=[ANY] * (n + 1),
        out_shape=[jax.ShapeDtypeStruct(g.shape, g.dtype) for g in mine]
        + [jax.ShapeDtypeStruct((N_DEV,) + small.shape, small.dtype)],
        scratch_shapes=[pltpu.SemaphoreType.DMA((n,)), pltpu.SemaphoreType.DMA((n,)),
                        pltpu.SemaphoreType.DMA((N_DEV - 1,)), pltpu.SemaphoreType.DMA((N_DEV - 1,)),
                        pltpu.SemaphoreType.DMA])(*mine, small)


def _elem_tile(rows):
    return 128 if rows % 128 == 0 else (64 if rows % 64 == 0 else rows)


def _sum_slots(a, name):
    g, r, c = a.shape
    tr = _elem_tile(r)

    def body(a_ref, o_ref):
        acc = a_ref[0]
        for k in range(1, g):
            acc = acc + a_ref[k]
        o_ref[...] = acc

    return pl.pallas_call(
        body, name=name, grid=(r // tr,), in_specs=[pl.BlockSpec((g, tr, c), lambda i: (0, i, 0))],
        out_specs=pl.BlockSpec((tr, c), lambda i: (i, 0)), out_shape=jax.ShapeDtypeStruct((r, c), a.dtype),
        compiler_params=_params(("parallel",)))(a)


def _adamw(wt, ga, gb, m, v, name):
    r, c = wt.shape
    tr = _elem_tile(r)
    two = gb is not None

    def body(*refs):
        if two:
            w_ref, ga_ref, gb_ref, m_ref, v_ref, g_out, d_out, m_out, v_out = refs
            g = ga_ref[...] + gb_ref[...]
        else:
            w_ref, ga_ref, m_ref, v_ref, g_out, d_out, m_out, v_out = refs
            g = ga_ref[...]
        mn = ADAM_B1 * m_ref[...] + (1.0 - ADAM_B1) * g
        vn = ADAM_B2 * v_ref[...] + (1.0 - ADAM_B2) * (g * g)
        m_hat = mn / (1.0 - ADAM_B1 ** ADAM_STEP)
        v_hat = vn / (1.0 - ADAM_B2 ** ADAM_STEP)
        g_out[...] = g
        d_out[...] = -ADAM_LR * (m_hat / (jnp.sqrt(v_hat) + ADAM_EPS) + ADAM_WD * w_ref[...])
        m_out[...] = mn
        v_out[...] = vn

    blk = pl.BlockSpec((tr, c), lambda i: (i, 0))
    args = [wt, ga] + ([gb] if two else []) + [m, v]
    sds = jax.ShapeDtypeStruct((r, c), F32)
    return pl.pallas_call(
        body, name=name, grid=(r // tr,), in_specs=[blk] * len(args), out_specs=[blk] * 4, out_shape=[sds] * 4,
        compiler_params=_params(("parallel",)))(*args)


def _pack(arrs, rows):
    flat = jnp.concatenate([a.reshape(-1) for a in arrs])
    return jnp.pad(flat, (0, rows * LANES - flat.shape[0])).reshape(rows, LANES)


def _unpack(buf, shapes):
    flat = buf.reshape(-1)
    out, off = [], 0
    for shp in shapes:
        size = 1
        for d in shp:
            size *= d
        out.append(flat[off:off + size].reshape(shp))
        off += size
    return out


BIG = ["w_in", "w_out", "w_up", "w_down", "w_ple_gate", "w_ple_proj"]
COL_SHARDED = ["w_in", "w_up", "w_ple_proj"]
SMALL_REPL = ["g_mix", "q_gain", "k_gain", "dw_b", "conv_ln_g", "conv_ln_b", "g_ffn", "ffn_conv_b", "g_ple"]
SMALL_SHARDED = ["dw_w", "ffn_conv_w"]
WEIGHTS = ["g_mix", "w_in", "q_gain", "k_gain", "dw_w", "dw_b", "conv_ln_g", "conv_ln_b", "w_out", "g_ffn", "w_up",
           "ffn_conv_w", "ffn_conv_b", "w_down", "g_ple", "w_ple_gate", "w_ple_proj"]


def _rows_for(n_elems):
    return -(-n_elems // (8 * LANES)) * 8


def kernel(x, p, g_mix, w_in, q_gain, k_gain, dw_w, dw_b, conv_ln_g, conv_ln_b, w_out, g_ffn, w_up, ffn_conv_w, ffn_conv_b, w_down, g_ple, w_ple_gate, w_ple_proj, loss_target, m_g_mix, m_w_in, m_q_gain, m_k_gain, m_dw_w, m_dw_b, m_conv_ln_g, m_conv_ln_b, m_w_out, m_g_ffn, m_w_up, m_ffn_conv_w, m_ffn_conv_b, m_w_down, m_g_ple, m_w_ple_gate, m_w_ple_proj, v_g_mix, v_w_in, v_q_gain, v_k_gain, v_dw_w, v_dw_b, v_conv_ln_g, v_conv_ln_b, v_w_out, v_g_ffn, v_w_up, v_ffn_conv_w, v_ffn_conv_b, v_w_down, v_g_ple, v_w_ple_gate, v_w_ple_proj):
    given = dict(locals())
    strip = lambda n, a: a if n in SMALL_REPL else a[0]
    wts = {n: strip(n, given[n]) for n in WEIGHTS}
    mom = {n: strip(n, given["m_" + n]) for n in WEIGHTS}
    var = {n: strip(n, given["v_" + n]) for n in WEIGHTS}
    chip = 2 * lax.axis_index("x") + lax.axis_index("y")

    small_shard_shapes = [wts[n].shape for n in SMALL_SHARDED]
    filt_rows = _rows_for(sum(wts[n].size for n in SMALL_SHARDED))
    shards = {n: wts[n].astype(BF16) for n in BIG}
    shards["filters"] = _pack([wts[n] for n in SMALL_SHARDED], filt_rows)

    def unpack_filters(filt_all):
        per_chip = [_unpack(filt_all[k], small_shard_shapes) for k in range(N_CHIPS)]
        return {n: jnp.concatenate([per_chip[k][idx] for k in range(N_CHIPS)], axis=1)
                for idx, n in enumerate(SMALL_SHARDED)}

    sq, grad_x, big, slots, small = _local_step(x[0], p[0, 0], loss_target[0], {n: wts[n] for n in SMALL_REPL},
                                                shards, unpack_filters)
    loss = lax.psum(sq * (0.5 / x.shape[-1]), ("x", "y", "c"))

    small_names = SMALL_REPL + SMALL_SHARDED
    small_shapes = [small[n].shape for n in small_names]
    small_rows = _rows_for(sum(small[n].size for n in small_names))
    mine = [_sum_slots(slots[n], "sum_" + n) for n in BIG]
    *theirs, small_slots = _finish_exchange(mine, _pack([small[n] for n in small_names], small_rows))
    small_sum = dict(zip(small_names, _unpack(_sum_slots(small_slots, "sum_small"), small_shapes)))

    outs = {}
    for n, ga, gb in zip(BIG, mine, theirs):
        outs[n] = _adamw(wts[n], ga, gb, mom[n], var[n], "adamw_" + n)
    for n in SMALL_SHARDED:
        width = wts[n].shape[1]
        small_sum[n] = lax.dynamic_slice_in_dim(small_sum[n], chip * width, width, axis=1)
    local_shapes = [wts[n].shape for n in small_names]
    local_rows = _rows_for(sum(wts[n].size for n in small_names))
    packed = _adamw(_pack([wts[n] for n in small_names], local_rows), _pack([small_sum[n] for n in small_names], local_rows),
                    None, _pack([mom[n] for n in small_names], local_rows),
                    _pack([var[n] for n in small_names], local_rows), "adamw_small")
    unpacked = [_unpack(buf, local_shapes) for buf in packed]
    for idx, n in enumerate(small_names):
        outs[n] = [u[idx] for u in unpacked]
    result = [loss, grad_x[None]]
    for part in range(4):
        result += [outs[n][part] if n in SMALL_REPL else outs[n][part][None] for n in WEIGHTS]
    return tuple(result)
```

```python
import functools

import jax
import jax.numpy as jnp
from jax import lax
from jax.experimental import pallas as pl
from jax.experimental.pallas import tpu as pltpu

F32 = jnp.float32
BF16 = jnp.bfloat16
HIGHEST = lax.Precision.HIGHEST
MESH = pl.DeviceIdType.MESH
ANY = pl.BlockSpec(memory_space=pl.ANY)

EPS = 1e-6
HEAD_DIM = 64
N_HEADS = 8
ATTN_W = 512
CONV_W = 512
CONV_K = 31
FFN_K = 3
ATTN_SCALE = 0.125
LANES = 128
CONV_HALO = 32
FFN_HALO = 16
VMEM_LIMIT = 56 * 1024 * 1024

ADAM_LR = 0.001
ADAM_B1 = 0.9
ADAM_B2 = 0.999
ADAM_EPS = 1e-08
ADAM_WD = 0.01
ADAM_STEP = 10

N_CHIPS = 4
N_DEV = 8


def _params(sem):
    return pltpu.CompilerParams(dimension_semantics=sem, vmem_limit_bytes=VMEM_LIMIT)


def _row_tile(s):
    return min(512, s)


def _position():
    x, y, c = lax.axis_index("x"), lax.axis_index("y"), lax.axis_index("c")
    return x, y, c, [(1 - x, y), (x, 1 - y), (1 - x, 1 - y)]


def _remote(src, dst, send_sems, recv_sems, k, to):
    return pltpu.make_async_remote_copy(src_ref=src, dst_ref=dst, send_sem=send_sems.at[k], recv_sem=recv_sems.at[k],
                                        device_id=to, device_id_type=MESH)


def _chip_copies(src_refs, land_refs, send_sems, recv_sems, per_chip, landed):
    x, y, c, chips = _position()
    me = 2 * x + y
    out = []
    for a, (src, land) in enumerate(zip(src_refs, land_refs)):
        for j, (px, py) in enumerate(chips):
            peer = 2 * px + py
            out.append(_remote(src.at[peer] if per_chip else src, land.at[peer if landed else me],
                               send_sems, recv_sems, 3 * a + j, (px, py, c)))
    return out


def _local_copies(src_refs, land_refs, local_sems, per_chip):
    x, y, _, _ = _position()
    me = 2 * x + y
    return [pltpu.make_async_copy(src.at[me] if per_chip else src, land.at[me], local_sems.at[a])
            for a, (src, land) in enumerate(zip(src_refs, land_refs))]


def _exchanged_shapes(srcs, per_chip):
    return [jax.ShapeDtypeStruct(a.shape if per_chip else (N_CHIPS,) + a.shape, a.dtype) for a in srcs]


def _exchange_sems(n):
    return [pltpu.SemaphoreType.DMA((3 * n,)), pltpu.SemaphoreType.DMA((3 * n,)), pltpu.SemaphoreType.DMA((n,))]


def _call(body, *, name, grid, in_specs, out_specs, out_shape, scratch_shapes, args, semantics, carry=None):
    if carry is None:
        return pl.pallas_call(body, name=name, grid=grid, in_specs=in_specs, out_specs=out_specs, out_shape=out_shape,
                              scratch_shapes=scratch_shapes, compiler_params=_params(semantics))(*args)
    srcs, per_chip = carry
    n, n_in, n_out, n_scr = len(srcs), len(in_specs), len(out_specs), len(scratch_shapes)

    def wrapped(*refs):
        ins, xin = refs[:n_in], refs[n_in:n_in + n]
        outs, xout = refs[n_in + n:n_in + n + n_out], refs[n_in + n + n_out:n_in + 2 * n + n_out]
        scratch = refs[n_in + 2 * n + n_out:n_in + 2 * n + n_out + n_scr]
        send_sems, recv_sems, local_sems = refs[-3:]
        first = functools.reduce(jnp.logical_and, [pl.program_id(d) == 0 for d in range(len(grid))])
        last = functools.reduce(jnp.logical_and, [pl.program_id(d) == g - 1 for d, g in enumerate(grid)])

        @pl.when(first)
        def _():
            for cp in _local_copies(xin, xout, local_sems, per_chip):
                cp.start()
            for cp in _chip_copies(xin, xout, send_sems, recv_sems, per_chip, False):
                cp.start()

        body(*ins, *outs, *scratch)

        @pl.when(last)
        def _():
            for cp in _chip_copies(xin, xout, send_sems, recv_sems, per_chip, True):
                cp.wait_recv()
            for cp in _chip_copies(xin, xout, send_sems, recv_sems, per_chip, False):
                cp.wait_send()
            for cp in _local_copies(xin, xout, local_sems, per_chip):
                cp.wait()

    return pl.pallas_call(
        wrapped, name=name, grid=grid, in_specs=list(in_specs) + [ANY] * n, out_specs=list(out_specs) + [ANY] * n,
        out_shape=list(out_shape) + _exchanged_shapes(srcs, per_chip),
        scratch_shapes=list(scratch_shapes) + _exchange_sems(n),
        compiler_params=_params(("arbitrary",) * len(grid)))(*args, *srcs)


def _contract_tile(s):
    return min(2048, s)


def _rstd(x):
    return lax.rsqrt(jnp.mean(x * x, axis=-1, keepdims=True) + EPS)


def _sigmoid(x):
    return 1.0 / (1.0 + jnp.exp(-x))


def _mm(a, b, *, name, dims, grid, a_spec, b_spec, o_spec, o_tile, out_shape, res=None, res_spec=None, carry=None):
    nk = grid[2]

    def body(*refs):
        if res is None:
            a_ref, b_ref, o_ref, acc_ref = refs
            r_ref = None
        else:
            a_ref, b_ref, r_ref, o_ref, acc_ref = refs
        part = lax.dot_general(a_ref[...], b_ref[...], (dims, ((), ())), preferred_element_type=F32)

        def finish(val):
            if r_ref is not None:
                val = val + r_ref[...]
            o_ref[...] = val.astype(o_ref.dtype)

        if nk == 1:
            finish(part)
        else:
            k = pl.program_id(2)

            @pl.when(k == 0)
            def _():
                acc_ref[...] = part

            @pl.when(k > 0)
            def _():
                acc_ref[...] += part

            @pl.when(k == nk - 1)
            def _():
                finish(acc_ref[...])

    in_specs = [a_spec, b_spec]
    args = [a, b]
    if res is not None:
        in_specs.append(res_spec)
        args.append(res)
    acc_tile = o_tile if nk > 1 else (8, LANES)
    out = _call(body, name=name, grid=grid, in_specs=in_specs, out_specs=[o_spec], out_shape=[out_shape],
                scratch_shapes=[pltpu.VMEM(acc_tile, F32)], args=args,
                semantics=("parallel", "parallel", "arbitrary"), carry=carry)
    return out[0] if carry is None else out


NN = ((1,), (0,))
NT = ((1,), (1,))
TN = ((0,), (0,))


def _mm_nn_sharded(a, bg, name, out_dtype=F32, tm=None):
    s, k = a.shape
    g, _, ns = bg.shape
    tm = tm or _row_tile(s)

    def body(a_ref, b_ref, o_ref):
        av = a_ref[...]
        for gi in range(g):
            o_ref[:, gi * ns:(gi + 1) * ns] = jnp.dot(av, b_ref[gi], preferred_element_type=F32).astype(out_dtype)

    return pl.pallas_call(
        body, name=name, grid=(s // tm,),
        in_specs=[pl.BlockSpec((tm, k), lambda i: (i, 0)), pl.BlockSpec((g, k, ns), lambda i: (0, 0, 0))],
        out_specs=pl.BlockSpec((tm, g * ns), lambda i: (i, 0)),
        out_shape=jax.ShapeDtypeStruct((s, g * ns), out_dtype), compiler_params=_params(("parallel",)))(a, bg)


def _mm_nn_full(a, b, name, res=None):
    s, k = a.shape
    n = b.shape[1]
    tm = _row_tile(s)
    rs = pl.BlockSpec((tm, n), lambda i, j, kk: (i, 0))
    return _mm(a, b, name=name, dims=NN, grid=(s // tm, 1, 1),
               a_spec=pl.BlockSpec((tm, k), lambda i, j, kk: (i, 0)),
               b_spec=pl.BlockSpec((k, n), lambda i, j, kk: (0, 0)),
               o_spec=rs, o_tile=(tm, n), out_shape=jax.ShapeDtypeStruct((s, n), F32),
               res=res, res_spec=rs if res is not None else None)


def _mm_nt_full(a, b, name):
    s, n = a.shape
    k = b.shape[0]
    tm = _row_tile(s)
    return _mm(a, b, name=name, dims=NT, grid=(s // tm, 1, 1),
               a_spec=pl.BlockSpec((tm, n), lambda i, j, kk: (i, 0)),
               b_spec=pl.BlockSpec((k, n), lambda i, j, kk: (0, 0)),
               o_spec=pl.BlockSpec((tm, k), lambda i, j, kk: (i, 0)), o_tile=(tm, k),
               out_shape=jax.ShapeDtypeStruct((s, k), F32))


def _mm_nt_sharded(a, bg, name, carry=None):
    s = a.shape[0]
    g, k, ns = bg.shape
    tm = _row_tile(s)

    def body(a_ref, b_ref, o_ref):
        acc = lax.dot_general(a_ref[:, 0:ns], b_ref[0], (NT, ((), ())), preferred_element_type=F32)
        for gi in range(1, g):
            acc = acc + lax.dot_general(a_ref[:, gi * ns:(gi + 1) * ns], b_ref[gi], (NT, ((), ())),
                                        preferred_element_type=F32)
        o_ref[...] = acc

    return _call(
        body, name=name, grid=(s // tm,),
        in_specs=[pl.BlockSpec((tm, g * ns), lambda i: (i, 0)), pl.BlockSpec((g, k, ns), lambda i: (0, 0, 0))],
        out_specs=[pl.BlockSpec((tm, k), lambda i: (i, 0))], out_shape=[jax.ShapeDtypeStruct((s, k), F32)],
        scratch_shapes=[], args=(a, bg), semantics=("parallel",), carry=carry)


def _mm_tn_sharded(a, b, g, name, n_split=1, tk=None):
    s, k = a.shape
    ns = b.shape[1] // g
    gs = g // n_split
    tk = tk or _contract_tile(s)

    def body(a_ref, b_ref, o_ref):
        first = pl.program_id(1) == 0
        at = a_ref[...].T
        for gi in range(gs):
            part = jnp.dot(at, b_ref[:, gi * ns:(gi + 1) * ns], preferred_element_type=F32)

            @pl.when(first)
            def _(gi=gi, part=part):
                o_ref[gi] = part

            @pl.when(jnp.logical_not(first))
            def _(gi=gi, part=part):
                o_ref[gi] += part

    return pl.pallas_call(
        body, name=name, grid=(n_split, s // tk),
        in_specs=[pl.BlockSpec((tk, k), lambda j, kk: (kk, 0)), pl.BlockSpec((tk, gs * ns), lambda j, kk: (kk, j))],
        out_specs=pl.BlockSpec((gs, k, ns), lambda j, kk: (j, 0, 0)),
        out_shape=jax.ShapeDtypeStruct((g, k, ns), F32), compiler_params=_params(("parallel", "arbitrary")))(a, b)


def _mm_tn_full(a, b, name, tm, carry=None):
    s, m = a.shape
    n = b.shape[1]
    tk = _contract_tile(s)
    return _mm(a, b, name=name, dims=TN, grid=(m // tm, 1, s // tk),
               a_spec=pl.BlockSpec((tk, tm), lambda i, j, kk: (kk, i)),
               b_spec=pl.BlockSpec((tk, n), lambda i, j, kk: (kk, 0)),
               o_spec=pl.BlockSpec((tm, n), lambda i, j, kk: (i, 0)), o_tile=(tm, n),
               out_shape=jax.ShapeDtypeStruct((m, n), F32), carry=carry)


def _rms_fwd(x, g, name, carry=None):
    s, d = x.shape
    ts = _row_tile(s)

    def body(x_ref, g_ref, u_ref):
        xv = x_ref[...]
        u_ref[...] = (xv * _rstd(xv) * g_ref[...]).astype(BF16)

    row = pl.BlockSpec((ts, d), lambda i: (i, 0))
    out = _call(body, name=name, grid=(s // ts,), in_specs=[row, pl.BlockSpec((1, d), lambda i: (0, 0))],
                out_specs=[row], out_shape=[jax.ShapeDtypeStruct((s, d), BF16)], scratch_shapes=[], args=(x, g),
                semantics=("parallel",), carry=carry)
    return out[0] if carry is None else out


def _rms_bwd(h, du, g, dh_in, name, carry=None):
    s, d = h.shape
    ts = _row_tile(s)

    def body(h_ref, du_ref, g_ref, dhin_ref, dh_ref, dhb_ref, gg_ref):
        hv = h_ref[...]
        r = _rstd(hv)
        xhat = hv * r
        duv = du_ref[...]
        dxhat = duv * g_ref[...]
        m = jnp.mean(dxhat * xhat, axis=-1, keepdims=True)
        dh = dhin_ref[...] + r * (dxhat - xhat * m)
        dh_ref[...] = dh
        dhb_ref[...] = dh.astype(BF16)
        part = jnp.sum(duv * xhat, axis=0, keepdims=True)

        @pl.when(pl.program_id(0) == 0)
        def _():
            gg_ref[...] = part

        @pl.when(pl.program_id(0) > 0)
        def _():
            gg_ref[...] += part

    row = pl.BlockSpec((ts, d), lambda i: (i, 0))
    vec = pl.BlockSpec((1, d), lambda i: (0, 0))
    return _call(
        body, name=name, grid=(s // ts,), in_specs=[row, row, vec, row], out_specs=[row, row, vec],
        out_shape=[jax.ShapeDtypeStruct((s, d), F32), jax.ShapeDtypeStruct((s, d), BF16),
                   jax.ShapeDtypeStruct((1, d), F32)],
        scratch_shapes=[], args=(h, du, g, dh_in), semantics=("arbitrary",), carry=carry)


def _head_sum(x, bd):
    return _tri_dot(_split(x), bd)


def _qkv_prep(proj, qg, kg, bd):
    s = proj.shape[0]
    ts = _row_tile(s)

    def body(q_ref, k_ref, v_ref, qg_ref, kg_ref, bd_ref, qs_ref, kh_ref, vb_ref):
        def norm(x, gain):
            ms = _head_sum(x * x, bd_ref[...]) * (1.0 / HEAD_DIM)
            return x * lax.rsqrt(ms + EPS) * gain

        qs_ref[...] = (norm(q_ref[...], qg_ref[...]) * ATTN_SCALE).astype(BF16)
        kh_ref[...] = norm(k_ref[...], kg_ref[...]).astype(BF16)
        vb_ref[...] = v_ref[...].astype(BF16)

    col = lambda c: pl.BlockSpec((ts, ATTN_W), lambda i: (i, c))
    vec = pl.BlockSpec((1, ATTN_W), lambda i: (0, 0))
    out = pl.BlockSpec((ts, ATTN_W), lambda i: (i, 0))
    sds = jax.ShapeDtypeStruct((s, ATTN_W), BF16)
    return pl.pallas_call(
        body, name="qkv_prep", grid=(s // ts,),
        in_specs=[col(0), col(1), col(2), vec, vec, pl.BlockSpec((ATTN_W, ATTN_W), lambda i: (0, 0))],
        out_specs=[out, out, out], out_shape=[sds, sds, sds],
        compiler_params=_params(("parallel",)))(proj, proj, proj, qg, kg, bd)


def _qk_bwd(proj, dqh, dkh, dv, qg, kg, bd):
    s = proj.shape[0]
    ts = _row_tile(s)

    def body(q_ref, k_ref, dqh_ref, dkh_ref, dv_ref, qg_ref, kg_ref, bd_ref, out_ref, gq_ref, gk_ref):
        first = pl.program_id(0) == 0

        def bwd(x, dy, gain, gg_ref):
            ms = _head_sum(x * x, bd_ref[...]) * (1.0 / HEAD_DIM)
            r = lax.rsqrt(ms + EPS)
            xhat = x * r
            dxhat = dy * gain
            m = _head_sum(dxhat * xhat, bd_ref[...]) * (1.0 / HEAD_DIM)
            part = jnp.sum(dy * xhat, axis=0, keepdims=True)

            @pl.when(first)
            def _():
                gg_ref[...] = part

            @pl.when(jnp.logical_not(first))
            def _():
                gg_ref[...] += part

            return r * (dxhat - xhat * m)

        out_ref[:, 0:ATTN_W] = bwd(q_ref[...], dqh_ref[...], qg_ref[...], gq_ref).astype(BF16)
        out_ref[:, ATTN_W:2 * ATTN_W] = bwd(k_ref[...], dkh_ref[...], kg_ref[...], gk_ref).astype(BF16)
        out_ref[:, 2 * ATTN_W:3 * ATTN_W] = dv_ref[...].astype(BF16)

    col = lambda c: pl.BlockSpec((ts, ATTN_W), lambda i: (i, c))
    row = pl.BlockSpec((ts, ATTN_W), lambda i: (i, 0))
    vec = pl.BlockSpec((1, ATTN_W), lambda i: (0, 0))
    return pl.pallas_call(
        body, name="qk_bwd", grid=(s // ts,),
        in_specs=[col(0), col(1), row, row, row, vec, vec, pl.BlockSpec((ATTN_W, ATTN_W), lambda i: (0, 0))],
        out_specs=[pl.BlockSpec((ts, 3 * ATTN_W), lambda i: (i, 0)), vec, vec],
        out_shape=[jax.ShapeDtypeStruct((s, 3 * ATTN_W), BF16), jax.ShapeDtypeStruct((1, ATTN_W), F32),
                   jax.ShapeDtypeStruct((1, ATTN_W), F32)],
        compiler_params=_params(("arbitrary",)))(proj, proj, dqh, dkh, dv, qg, kg, bd)


def _split(x):
    hi = x.astype(BF16)
    return hi, (x - hi.astype(F32)).astype(BF16)


def _tri_dot(parts, tri):
    hi, lo = parts
    return jnp.dot(hi, tri, preferred_element_type=F32) + jnp.dot(lo, tri, preferred_element_type=F32)


def _log_sigmoids(z):
    neg_abs = lax.bitcast_convert_type(lax.bitcast_convert_type(z, jnp.uint32) | jnp.uint32(0x80000000), F32)
    lb = jnp.minimum(z, 0.0) - jnp.log(1.0 + jnp.exp(neg_abs))
    return lb, lb - z


DEAD_LOG_WEIGHT = -106.0


def _sweep_key_blocks(tiles, alive, i):
    @pl.when(i == 0)
    def _():
        tiles([0], [True])

    @pl.when(i > 0)
    def _():
        tiles([i, i - 1], [True, False])

    def more(state):
        kb, live = state
        return jnp.logical_and(kb >= 0, live > 0)

    def step(state):
        kb, _ = state
        tiles([kb], [False])
        return kb - 1, alive().astype(jnp.int32)

    lax.while_loop(more, step, (i - 2, alive().astype(jnp.int32)))


def _head_masks():
    lane = lax.broadcasted_iota(jnp.int32, (1, LANES), 1)
    return [lane < HEAD_DIM, lane >= HEAD_DIM]


def _attn_fwd(qs, kh, vb, tri_excl, carry=None):
    s = qs.shape[0]
    t = tri_excl.shape[0]
    nq = s // t

    def body(q_ref, k_ref, v_ref, tri_ref, o_ref, ob_ref, acc_ref, c_ref):
        i = pl.program_id(1)
        hmask = _head_masks()
        q = q_ref[...]
        qm = [jnp.where(hm, q, jnp.zeros_like(q)) for hm in hmask]
        acc_ref[...] = jnp.zeros_like(acc_ref)
        c_ref[...] = jnp.zeros_like(c_ref)
        causal = (lax.broadcasted_iota(jnp.int32, (t, t), 1) < lax.broadcasted_iota(jnp.int32, (t, t), 0))

        def tiles(kbs, masked):
            tri = tri_ref[...]
            starts = [pl.multiple_of(kb * t, t) for kb in kbs]
            kblks = [k_ref[pl.ds(k0, t), :] for k0 in starts]
            vblks = [v_ref[pl.ds(k0, t), :] for k0 in starts]
            chains = [(j, h) for j in range(len(kbs)) for h in range(2)]
            carry = [c_ref[h] for h in range(2)]
            pv = [None, None]
            lbs, loms, between = {}, {}, {}
            for step in range(len(chains) + 2):
                if step < len(chains):
                    j, h = chains[step]
                    z = lax.dot_general(qm[h], kblks[j], (NT, ((), ())), preferred_element_type=F32)
                    lbs[step], lom = _log_sigmoids(z)
                    loms[step] = jnp.where(causal, lom, 0.0) if masked[j] else lom
                if 0 <= step - 1 < len(chains):
                    between[step - 1] = _tri_dot(_split(loms[step - 1]), tri)
                if 0 <= step - 2 < len(chains):
                    n = step - 2
                    j, h = chains[n]
                    w = jnp.exp(lbs[n] + between[n] + carry[h])
                    if masked[j]:
                        w = jnp.where(causal, w, 0.0)
                    carry[h] = carry[h] + jnp.sum(loms[n], axis=-1, keepdims=True)
                    part = jnp.dot(w.astype(BF16), vblks[j], preferred_element_type=F32)
                    pv[h] = part if pv[h] is None else pv[h] + part
            for h in range(2):
                c_ref[h] = carry[h]
                acc_ref[h] += pv[h]

        _sweep_key_blocks(tiles, lambda: jnp.max(c_ref[...]) > DEAD_LOG_WEIGHT, i)
        o = jnp.where(hmask[0], acc_ref[0], acc_ref[1])
        o_ref[...] = o
        ob_ref[...] = o.astype(BF16)

    qspec = pl.BlockSpec((t, LANES), lambda hp, i: (i, hp))
    kspec = pl.BlockSpec((s, LANES), lambda hp, i: (0, hp))
    return _call(
        body, name="attn_fwd", grid=(ATTN_W // LANES, nq),
        in_specs=[qspec, kspec, kspec, pl.BlockSpec((t, t), lambda hp, i: (0, 0))],
        out_specs=[qspec, qspec],
        out_shape=[jax.ShapeDtypeStruct((s, ATTN_W), F32), jax.ShapeDtypeStruct((s, ATTN_W), BF16)],
        scratch_shapes=[pltpu.VMEM((2, t, LANES), F32), pltpu.VMEM((2, t, 1), F32)],
        args=(qs, kh, vb, tri_excl), semantics=("parallel", "arbitrary"), carry=carry)


def _attn_bwd(qs, kh, vb, o, dmix, tri_excl, tri_incl, carry=None):
    s = qs.shape[0]
    t = tri_excl.shape[0]
    nq = s // t

    def body(q_ref, k_ref, v_ref, o_ref, do_ref, te_ref, ti_ref, dq_ref, dk_ref, dv_ref, dqacc_ref, c_ref):
        i = pl.program_id(1)

        @pl.when(i == 0)
        def _():
            dk_ref[...] = jnp.zeros_like(dk_ref)
            dv_ref[...] = jnp.zeros_like(dv_ref)

        hmask = _head_masks()
        q = q_ref[...]
        do = do_ref[...]
        dob = do.astype(BF16)
        prod = dob.astype(F32) * o_ref[...]
        qm =[jnp.where(hm, q, jnp.zeros_like(q)) for hm in hmask]
        dom = [jnp.where(hm, dob, jnp.zeros_like(dob)) for hm in hmask]
        total = [jnp.sum(jnp.where(hm, prod, 0.0), axis=-1, keepdims=True) for hm in hmask]
        dqacc_ref[...] = jnp.zeros_like(dqacc_ref)
        c_ref[...] = jnp.zeros_like(c_ref)
        causal = (lax.broadcasted_iota(jnp.int32, (t, t), 1) < lax.broadcasted_iota(jnp.int32, (t, t), 0))

        def tiles(kbs, masked):
            te = te_ref[...]
            ti = ti_ref[...]
            starts = [pl.multiple_of(kb * t, t) for kb in kbs]
            kblks = [k_ref[pl.ds(k0, t), :] for k0 in starts]
            vblks = [v_ref[pl.ds(k0, t), :] for k0 in starts]
            chains = [(j, h) for j in range(len(kbs)) for h in range(2)]
            c_lom = [c_ref[2 * h] for h in range(2)]
            c_g = [c_ref[2 * h + 1] for h in range(2)]
            lbs, loms, dws, between, wbs, gs, g_after = {}, {}, {}, {}, {}, {}, {}
            dq = [None, None]
            dk = [None] * len(kbs)
            dv = [None] * len(kbs)
            add = lambda acc, part: part if acc is None else acc + part
            for step in range(len(chains) + 3):
                if step < len(chains):
                    j, h = chains[step]
                    z = lax.dot_general(qm[h], kblks[j], (NT, ((), ())), preferred_element_type=F32)
                    dws[step] = lax.dot_general(dom[h], vblks[j], (NT, ((), ())), preferred_element_type=F32)
                    lbs[step], lom = _log_sigmoids(z)
                    loms[step] = jnp.where(causal, lom, 0.0) if masked[j] else lom
                if 0 <= step - 1 < len(chains):
                    between[step - 1] = _tri_dot(_split(loms[step - 1]), te)
                if 0 <= step - 2 < len(chains):
                    n = step - 2
                    j, h = chains[n]
                    w = jnp.exp(lbs[n] + between[n] + c_lom[h])
                    if masked[j]:
                        w = jnp.where(causal, w, 0.0)
                    c_lom[h] = c_lom[h] + jnp.sum(loms[n], axis=-1, keepdims=True)
                    wbs[n] = w.astype(BF16)
                    gs[n] = dws[n] * wbs[n].astype(F32)
                    g_after[n] = _tri_dot(_split(gs[n]), ti)
                if 0 <= step - 3 < len(chains):
                    n = step - 3
                    j, h = chains[n]
                    beta = jnp.exp(lbs[n])
                    dz = gs[n] * (1.0 - beta) - beta * (total[h] - (g_after[n] + c_g[h]))
                    if masked[j]:
                        dz = jnp.where(causal, dz, 0.0)
                    c_g[h] = c_g[h] + jnp.sum(gs[n], axis=-1, keepdims=True)
                    dzb = dz.astype(BF16)
                    dq[h] = add(dq[h], jnp.dot(dzb, kblks[j], preferred_element_type=F32))
                    dk[j] = add(dk[j], lax.dot_general(dzb, qm[h], (TN, ((), ())), preferred_element_type=F32))
                    dv[j] = add(dv[j], lax.dot_general(wbs[n], dom[h], (TN, ((), ())), preferred_element_type=F32))
            for h in range(2):
                c_ref[2 * h] = c_lom[h]
                c_ref[2 * h + 1] = c_g[h]
                dqacc_ref[h] += dq[h]
            for j, k0 in enumerate(starts):
                dk_ref[pl.ds(k0, t), :] += dk[j]
                dv_ref[pl.ds(k0, t), :] += dv[j]

        _sweep_key_blocks(tiles, lambda: jnp.maximum(jnp.max(c_ref[0]), jnp.max(c_ref[2])) > DEAD_LOG_WEIGHT, i)
        dq_ref[...] = jnp.where(hmask[0], dqacc_ref[0], dqacc_ref[1]) * ATTN_SCALE

    qspec = pl.BlockSpec((t, LANES), lambda hp, i: (i, hp))
    kspec = pl.BlockSpec((s, LANES), lambda hp, i: (0, hp))
    tspec = pl.BlockSpec((t, t), lambda hp, i: (0, 0))
    sds = jax.ShapeDtypeStruct((s, ATTN_W), F32)
    return _call(
        body, name="attn_bwd", grid=(ATTN_W // LANES, nq),
        in_specs=[qspec, kspec, kspec, qspec, qspec, tspec, tspec],
        out_specs=[qspec, kspec, kspec], out_shape=[sds, sds, sds],
        scratch_shapes=[pltpu.VMEM((2, t, LANES), F32), pltpu.VMEM((4, t, 1), F32)],
        args=(qs, kh, vb, o, dmix, tri_excl, tri_incl), semantics=("parallel", "arbitrary"), carry=carry)


CONV_ROWS = 64
CONV_COLS = 256


SUBLANES = 8


def _shift_copies(src_ref, sh_ref):
    length = sh_ref.shape[1]
    for r in range(1, SUBLANES):
        sh_ref[r - 1] = src_ref[r:r + length, :]


def _shift_scratch(ts):
    return pltpu.VMEM((SUBLANES - 1, ts + CONV_HALO - SUBLANES, CONV_W), F32)


def _rows_at(src_ref, sh_ref, offset, r0, rows, cols):
    r = offset % SUBLANES
    base = offset - r + r0
    return src_ref[base:base + rows, cols] if r == 0 else sh_ref[r - 1, base:base + rows, cols]


def _taps(src_ref, sh_ref, w_ref, n_taps, first_row, rows, reverse=False):
    width = src_ref.shape[1]
    cols = min(CONV_COLS, width)
    out = []
    for r0 in range(0, rows, CONV_ROWS):
        for c0 in range(0, width, cols):
            acc = jnp.zeros((CONV_ROWS, cols), F32)
            for k in range(n_taps):
                off = (n_taps - 1 - k) if reverse else k
                acc = acc + w_ref[k:k + 1, c0:c0 + cols] * _rows_at(src_ref, sh_ref, first_row + off, r0, CONV_ROWS,
                                                                    slice(c0, c0 + cols))
            out.append(((r0, c0), acc))
    return out


def _conv_fwd(proj, dw_w, dw_b, ln_g, ln_b, carry=None):
    s = proj.shape[0]
    ts = _row_tile(s)
    hb = ts // CONV_HALO

    def body(a_ref, g_ref, ha_ref, hg_ref, w_ref, b_ref, lg_ref, lb_ref, c1_ref, c3_ref, pad_ref, sh_ref):
        i = pl.program_id(0)
        halo = ha_ref[...] * _sigmoid(hg_ref[...])
        pad_ref[0:CONV_HALO, :] = jnp.where(i > 0, halo, 0.0)
        pad_ref[CONV_HALO:, :] = a_ref[...] * _sigmoid(g_ref[...])
        _shift_copies(pad_ref, sh_ref)
        first = CONV_HALO - (CONV_K - 1)
        for (r0, c0), acc in _taps(pad_ref, sh_ref, w_ref, CONV_K, first, ts):
            c1_ref[r0:r0 + CONV_ROWS, c0:c0 + acc.shape[1]] = acc + b_ref[:, c0:c0 + acc.shape[1]]
        c1 = c1_ref[...]
        xc = c1 - jnp.mean(c1, axis=-1, keepdims=True)
        c2 = xc * _rstd(xc) * lg_ref[...] + lb_ref[...]
        c3_ref[...] = (c2 * _sigmoid(c2)).astype(BF16)

    cur = lambda c: pl.BlockSpec((ts, CONV_W), lambda i: (i, c))
    halo = lambda c: pl.BlockSpec((CONV_HALO, CONV_W), lambda i: (jnp.maximum(i * hb - 1, 0), c))
    vec = pl.BlockSpec((1, CONV_W), lambda i: (0, 0))
    row = pl.BlockSpec((ts, CONV_W), lambda i: (i, 0))
    return _call(
        body, name="conv_fwd", grid=(s // ts,),
        in_specs=[cur(3), cur(4), halo(3), halo(4), pl.BlockSpec((CONV_HALO, CONV_W), lambda i: (0, 0)), vec, vec, vec],
        out_specs=[row, row],
        out_shape=[jax.ShapeDtypeStruct((s, CONV_W), F32), jax.ShapeDtypeStruct((s, CONV_W), BF16)],
        scratch_shapes=[pltpu.VMEM((ts + CONV_HALO, CONV_W), F32), _shift_scratch(ts)],
        args=(proj, proj, proj, proj, dw_w, dw_b, ln_g, ln_b), semantics=("parallel",), carry=carry)


def _conv_bwd_ln(dmix, c1, ln_g, ln_b):
    s = c1.shape[0]
    ts = _row_tile(s)

    def body(d_ref, c1_ref, lg_ref, lb_ref, dc1_ref, glg_ref, glb_ref, gb_ref):
        c1v = c1_ref[...]
        xc = c1v - jnp.mean(c1v, axis=-1, keepdims=True)
        r = _rstd(xc)
        xhat = xc * r
        c2 = xhat * lg_ref[...] + lb_ref[...]
        sg = _sigmoid(c2)
        dc2 = d_ref[...] * (sg * (1.0 + c2 * (1.0 - sg)))
        dxhat = dc2 * lg_ref[...]
        dc1 = r * (dxhat - jnp.mean(dxhat, axis=-1, keepdims=True)
                   - xhat * jnp.mean(dxhat * xhat, axis=-1, keepdims=True))
        dc1_ref[...] = dc1
        parts = [(glg_ref, jnp.sum(dc2 * xhat, axis=0, keepdims=True)),
                 (glb_ref, jnp.sum(dc2, axis=0, keepdims=True)),
                 (gb_ref, jnp.sum(dc1, axis=0, keepdims=True))]

        @pl.when(pl.program_id(0) == 0)
        def _():
            for ref, part in parts:
                ref[...] = part

        @pl.when(pl.program_id(0) > 0)
        def _():
            for ref, part in parts:
                ref[...] += part

    row = pl.BlockSpec((ts, CONV_W), lambda i: (i, 0))
    vec = pl.BlockSpec((1, CONV_W), lambda i: (0, 0))
    vsd = jax.ShapeDtypeStruct((1, CONV_W), F32)
    return pl.pallas_call(
        body, name="conv_bwd_ln", grid=(s // ts,),
        in_specs=[pl.BlockSpec((ts, CONV_W), lambda i: (i, 1)), row, vec, vec],
        out_specs=[row, vec, vec, vec],
        out_shape=[jax.ShapeDtypeStruct((s, CONV_W), F32), vsd, vsd, vsd],
        compiler_params=_params(("arbitrary",)))(dmix, c1, ln_g, ln_b)


def _conv_bwd_taps(proj, dc1, dw_w):
    s = proj.shape[0]
    ts = _row_tile(s)
    hb = ts // CONV_HALO
    last = s // CONV_HALO - 1
    nsteps = s // ts

    def body(a_ref, g_ref, ha_ref, hg_ref, d_ref, hd_ref, w_ref, out_ref, gw_ref, pad_ref, dpad_ref, sh_ref):
        i = pl.program_id(0)
        av = a_ref[...]
        sg = _sigmoid(g_ref[...])
        halo = ha_ref[...] * _sigmoid(hg_ref[...])
        pad_ref[0:CONV_HALO, :] = jnp.where(i > 0, halo, 0.0)
        pad_ref[CONV_HALO:, :] = av * sg
        dpad_ref[0:ts, :] = d_ref[...]
        dpad_ref[ts:, :] = jnp.where(i < nsteps - 1, hd_ref[...], 0.0)

        @pl.when(i == 0)
        def _():
            gw_ref[...] = jnp.zeros_like(gw_ref)

        _shift_copies(pad_ref, sh_ref)
        first = CONV_HALO - (CONV_K - 1)
        fold = lambda v: jnp.sum(v.reshape(CONV_ROWS // SUBLANES, SUBLANES, v.shape[1]), axis=0)
        for c0 in range(0, CONV_W, CONV_COLS):
            cs = slice(c0, c0 + CONV_COLS)
            for k in range(CONV_K):
                acc = None
                for r0 in range(0, ts, CONV_ROWS):
                    part = fold(d_ref[r0:r0 + CONV_ROWS, cs] * _rows_at(pad_ref, sh_ref, first + k, r0, CONV_ROWS, cs))
                    acc = part if acc is None else acc + part
                gw_ref[k:k + 1, cs] += jnp.sum(acc, axis=0, keepdims=True)
        _shift_copies(dpad_ref, sh_ref)
        for (r0, c0), dc0 in _taps(dpad_ref, sh_ref, w_ref, CONV_K, 0, ts, reverse=True):
            cs = slice(c0, c0 + dc0.shape[1])
            a_c = a_ref[r0:r0 + CONV_ROWS, cs]
            sg_c = _sigmoid(g_ref[r0:r0 + CONV_ROWS, cs])
            out_ref[r0:r0 + CONV_ROWS, cs] = (dc0 * sg_c).astype(BF16)
            out_ref[r0:r0 + CONV_ROWS, CONV_W + c0:CONV_W + c0 + dc0.shape[1]] = (
                dc0 * a_c * sg_c * (1.0 - sg_c)).astype(BF16)

    cur = lambda c: pl.BlockSpec((ts, CONV_W), lambda i: (i, c))
    halo = lambda c: pl.BlockSpec((CONV_HALO, CONV_W), lambda i: (jnp.maximum(i * hb - 1, 0), c))
    row = pl.BlockSpec((ts, CONV_W), lambda i: (i, 0))
    nxt = pl.BlockSpec((CONV_HALO, CONV_W), lambda i: (jnp.minimum((i + 1) * hb, last), 0))
    wspec = pl.BlockSpec((CONV_HALO, CONV_W), lambda i: (0, 0))
    return pl.pallas_call(
        body, name="conv_bwd_taps", grid=(nsteps,),
        in_specs=[cur(3), cur(4), halo(3), halo(4), row, nxt, wspec],
        out_specs=[pl.BlockSpec((ts, 2 * CONV_W), lambda i: (i, 0)), wspec],
        out_shape=[jax.ShapeDtypeStruct((s, 2 * CONV_W), BF16), jax.ShapeDtypeStruct((CONV_HALO, CONV_W), F32)],
        scratch_shapes=[pltpu.VMEM((ts + CONV_HALO, CONV_W), F32), pltpu.VMEM((ts + CONV_HALO, CONV_W), F32),
                        _shift_scratch(ts)],
        compiler_params=_params(("arbitrary",)))(proj, proj, proj, proj, dc1, dc1, dw_w)


SQRT_HALF = 0.7071067811865476
INV_SQRT_2PI = 0.3989422804014327


def _gelu_parts(x):
    cdf = 0.5 * (1.0 + lax.erf(x * SQRT_HALF))
    return x * cdf, cdf + x * (INV_SQRT_2PI * jnp.exp(-0.5 * x * x))


def _ffn_tile(dff):
    return dff // 2


FFN_ROWS = 64
FFN_COLS = LANES


def _ffn_chunks(ts, tc):
    return [(r0, slice(c0, c0 + FFN_COLS)) for c0 in range(0, tc, FFN_COLS) for r0 in range(0, ts, FFN_ROWS)]


def _ffn_gate2(pad_ref, w_ref, b_ref, r0, cs):
    first = FFN_HALO - (FFN_K - 1) + r0
    g2 = b_ref[:, cs] + w_ref[0:1, cs] * pad_ref[first:first + FFN_ROWS, cs]
    for k in range(1, FFN_K):
        g2 = g2 + w_ref[k:k + 1, cs] * pad_ref[first + k:first + k + FFN_ROWS, cs]
    return g2


def _ffn_act(up, fw, fb):
    s = up.shape[0]
    dff = up.shape[1] // 2
    tc = _ffn_tile(dff)
    nj = dff // tc
    ts = _row_tile(s) // 2
    hb = ts // FFN_HALO

    def body(g_ref, v_ref, hg_ref, w_ref, b_ref, act_ref, pad_ref):
        i = pl.program_id(0)
        pad_ref[0:FFN_HALO, :] = jnp.where(i > 0, hg_ref[...].astype(F32), 0.0)
        pad_ref[FFN_HALO:, :] = g_ref[...].astype(F32)
        for r0, cs in _ffn_chunks(ts, tc):
            gelu, _ = _gelu_parts(_ffn_gate2(pad_ref, w_ref, b_ref, r0, cs))
            act_ref[r0:r0 + FFN_ROWS, cs] = (gelu * v_ref[r0:r0 + FFN_ROWS, cs].astype(F32)).astype(BF16)

    return pl.pallas_call(
        body, name="ffn_act", grid=(s // ts, nj),
        in_specs=[pl.BlockSpec((ts, tc), lambda i, j: (i, j)), pl.BlockSpec((ts, tc), lambda i, j: (i, j + nj)),
                  pl.BlockSpec((FFN_HALO, tc), lambda i, j: (jnp.maximum(i * hb - 1, 0), j)),
                  pl.BlockSpec((FFN_HALO, tc), lambda i, j: (0, j)), pl.BlockSpec((1, tc), lambda i, j: (0, j))],
        out_specs=pl.BlockSpec((ts, tc), lambda i, j: (i, j)),
        out_shape=jax.ShapeDtypeStruct((s, dff), BF16),
        scratch_shapes=[pltpu.VMEM((ts + FFN_HALO, tc), F32)],
        compiler_params=_params(("parallel", "parallel")))(up, up, up, fw, fb)


def _ffn_bwd_act(dact, up, fw, fb, carry=None):
    s = up.shape[0]
    dff = up.shape[1] // 2
    tc = _ffn_tile(dff)
    nj = dff // tc
    ts = _row_tile(s) // 2
    hb = ts // FFN_HALO

    def body(d_ref, g_ref, v_ref, hg_ref, w_ref, b_ref, dg2_ref, dval_ref, gw_ref, gb_ref, pad_ref):
        i = pl.program_id(1)
        pad_ref[0:FFN_HALO, :] = jnp.where(i > 0, hg_ref[...].astype(F32), 0.0)
        pad_ref[FFN_HALO:, :] = g_ref[...].astype(F32)

        @pl.when(i == 0)
        def _():
            gw_ref[...] = jnp.zeros_like(gw_ref)
            gb_ref[...] = jnp.zeros_like(gb_ref)

        fold = lambda v: jnp.sum(v.reshape(FFN_ROWS // SUBLANES, SUBLANES, FFN_COLS), axis=0)
        first = FFN_HALO - (FFN_K - 1)
        sums = {}
        for r0, cs in _ffn_chunks(ts, tc):
            rows = slice(r0, r0 + FFN_ROWS)
            shifted = [pad_ref[first + k + r0:first + k + r0 + FFN_ROWS, cs] for k in range(FFN_K)]
            g2 = b_ref[:, cs] + w_ref[0:1, cs] * shifted[0]
            for k in range(1, FFN_K):
                g2 = g2 + w_ref[k:k + 1, cs] * shifted[k]
            gelu, dgelu = _gelu_parts(g2)
            dactv = d_ref[rows, cs]
            dval_ref[rows, cs] = (dactv * gelu).astype(BF16)
            dg2 = dactv * v_ref[rows, cs].astype(F32) * dgelu
            dg2_ref[rows, cs] = dg2.astype(BF16)
            parts = [fold(dg2)] + [fold(dg2 * shifted[k]) for k in range(FFN_K)]
            sums = {n: part + sums[n] if r0 else part for n, part in enumerate(parts)}
            if r0 + FFN_ROWS == ts:
                gb_ref[:, cs] += jnp.sum(sums[0], axis=0, keepdims=True)
                for k in range(FFN_K):
                    gw_ref[k:k + 1, cs] += jnp.sum(sums[1 + k], axis=0, keepdims=True)

    blk = pl.BlockSpec((ts, tc), lambda j, i: (i, j))
    wspec = pl.BlockSpec((FFN_HALO, tc), lambda j, i: (0, j))
    bspec = pl.BlockSpec((1, tc), lambda j, i: (0, j))
    return _call(
        body, name="ffn_bwd_act", grid=(nj, s // ts),
        in_specs=[blk, blk, pl.BlockSpec((ts, tc), lambda j, i: (i, j + nj)),
                  pl.BlockSpec((FFN_HALO, tc), lambda j, i: (jnp.maximum(i * hb - 1, 0), j)), wspec, bspec],
        out_specs=[blk, pl.BlockSpec((ts, tc), lambda j, i: (i, j + nj)), wspec, bspec],
        out_shape=[jax.ShapeDtypeStruct((s, dff), BF16), jax.ShapeDtypeStruct((s, 2 * dff), BF16),
                   jax.ShapeDtypeStruct((FFN_HALO, dff), F32), jax.ShapeDtypeStruct((1, dff), F32)],
        scratch_shapes=[pltpu.VMEM((ts + FFN_HALO, tc), F32)],
        args=(dact, up, up, up, fw, fb), semantics=("parallel", "arbitrary"), carry=carry)


def _ffn_bwd_conv(dg2, fw, dup):
    s, dff = dg2.shape
    tc = _ffn_tile(dff)
    ts = _row_tile(s) // 2
    hb = ts // FFN_HALO
    last = s // FFN_HALO - 1
    nsteps = s // ts

    def body(d_ref, hd_ref, w_ref, dup_ref, out_ref, pad_ref):
        i = pl.program_id(0)
        pad_ref[0:ts, :] = d_ref[...].astype(F32)
        pad_ref[ts:, :] = jnp.where(i < nsteps - 1, hd_ref[...].astype(F32), 0.0)
        for r0, cs in _ffn_chunks(ts, tc):
            dg = w_ref[0:1, cs] * pad_ref[r0 + FFN_K - 1:r0 + FFN_K - 1 + FFN_ROWS, cs]
            for k in range(1, FFN_K):
                dg = dg + w_ref[k:k + 1, cs] * pad_ref[r0 + FFN_K - 1 - k:r0 + FFN_K - 1 - k + FFN_ROWS, cs]
            out_ref[r0:r0 + FFN_ROWS, cs] = dg.astype(BF16)

    blk = pl.BlockSpec((ts, tc), lambda i, j: (i, j))
    return pl.pallas_call(
        body, name="ffn_bwd_conv", grid=(nsteps, dff // tc),
        in_specs=[blk, pl.BlockSpec((FFN_HALO, tc), lambda i, j: (jnp.minimum((i + 1) * hb, last), j)),
                  pl.BlockSpec((FFN_HALO, tc), lambda i, j: (0, j)), ANY],
        out_specs=blk, out_shape=jax.ShapeDtypeStruct(dup.shape, BF16), input_output_aliases={3: 0},
        scratch_shapes=[pltpu.VMEM((ts + FFN_HALO, tc), F32)],
        compiler_params=_params(("parallel", "parallel")))(dg2, dg2, fw, dup)


def _ple_loss(h2, zg, pp, target):
    s, d = h2.shape
    ts = _row_tile(s)

    def body(h_ref, z_ref, p_ref, t_ref, dh_ref, dpp_ref, dz_ref, loss_ref):
        pg = _sigmoid(z_ref[...])
        ppv = p_ref[...]
        diff = h_ref[...] + pg * ppv - t_ref[...]
        dh = diff * (1.0 / d)
        dh_ref[...] = dh
        dpp_ref[...] = (dh * pg).astype(BF16)
        dz_ref[...] = (dh * ppv * pg * (1.0 - pg)).astype(BF16)
        part = jnp.sum(jnp.sum(diff * diff, axis=0, keepdims=True), axis=1, keepdims=True)

        @pl.when(pl.program_id(0) == 0)
        def _():
            loss_ref[...] = jnp.zeros_like(loss_ref)

        loss_ref[...] += jnp.broadcast_to(part, loss_ref.shape)

    row = pl.BlockSpec((ts, d), lambda i: (i, 0))
    return pl.pallas_call(
        body, name="ple_loss", grid=(s // ts,), in_specs=[row, row, row, row],
        out_specs=[row, row, row, pl.BlockSpec((8, LANES), lambda i: (0, 0))],
        out_shape=[jax.ShapeDtypeStruct((s, d), F32), jax.ShapeDtypeStruct((s, d), BF16),
                   jax.ShapeDtypeStruct((s, d), BF16), jax.ShapeDtypeStruct((8, LANES), F32)],
        compiler_params=_params(("arbitrary",)))(h2, zg, pp, target)


def _local_step(x, p, target, w, shards=None, unpack_filters=None):
    riding = shards is not None
    major = lambda n, g: g if n in COL_SHARDED else g.reshape(N_CHIPS, -1, g.shape[-1])
    full = lambda n, g: g if n in COL_SHARDED else g.reshape(-1, g.shape[-1])
    first = lambda res: (res[0], res[1:]) if riding else (res, [])

    def ride(names, arrays):
        return ([arrays[n] for n in names], False) if riding else None

    def leave(grads):
        return (grads, True) if riding else None

    def land(names, got):
        return {n: full(n, g) for n, g in zip(names, got)}
    s = x.shape[0]
    t = min(256, s)
    tri = jnp.tril(jnp.ones((t, t), F32))
    tri_incl = tri.astype(BF16)
    tri_excl = jnp.tril(jnp.ones((t, t), F32), -1).astype(BF16)
    bd = jnp.kron(jnp.eye(N_HEADS, dtype=F32), jnp.ones((HEAD_DIM, HEAD_DIM), F32)).astype(BF16)
    qg = jnp.tile(w["q_gain"], (1, N_HEADS))
    kg = jnp.tile(w["k_gain"], (1, N_HEADS))
    pb = p.astype(BF16)

    u1, got = first(_rms_fwd(x, w["g_mix"], "rms_mix", carry=ride(["w_in", "filters"], shards)))
    if riding:
        w = {**w, "w_in": got[0], **unpack_filters(got[1])}
    dw_w = jnp.pad(w["dw_w"], ((0, CONV_HALO - CONV_K), (0, 0)))
    fw = jnp.pad(w["ffn_conv_w"], ((0, FFN_HALO - FFN_K), (0, 0)))
    proj = _mm_nn_sharded(u1, w["w_in"], "mm_in")
    qs, kh, vb = _qkv_prep(proj, qg, kg, bd)
    with_attn, with_conv = ["w_out", "w_up", "w_ple_gate", "w_ple_proj"], ["w_down"]
    o, ob, *got = _attn_fwd(qs, kh, vb, tri_excl, carry=ride(with_attn, shards))
    w = {**w, **land(with_attn, got)}
    c1, c3, *got = _conv_fwd(proj, dw_w, w["dw_b"], w["conv_ln_g"], w["conv_ln_b"], carry=ride(with_conv, shards))
    w = {**w, **land(with_conv, got)}
    mix = jnp.concatenate([ob, c3], axis=1)
    h1 = _mm_nn_full(mix, w["w_out"], "mm_out", res=x)
    u2 = _rms_fwd(h1, w["g_ffn"], "rms_ffn")
    up = _mm_nn_sharded(u2, w["w_up"], "mm_up", out_dtype=BF16)
    act = _ffn_act(up, fw, w["ffn_conv_b"])
    h2 = _mm_nn_full(act, w["w_down"], "mm_down", res=h1)
    u3 = _rms_fwd(h2, w["g_ple"], "rms_ple")
    zg = _mm_nn_full(u3, w["w_ple_gate"], "mm_ple_gate")
    pp = _mm_nn_sharded(pb, w["w_ple_proj"], "mm_ple_proj")
    dh3, dpp, dz, sq = _ple_loss(h2, zg, pp, target)

    big = {}
    small = {}
    big["w_ple_proj"] = _mm_tn_sharded(pb, dpp, N_CHIPS, "mm_g_ple_proj")
    big["w_ple_gate"] = _mm_tn_full(u3, dz, "mm_g_ple_gate", tm=u3.shape[1])
    du3 = _mm_nt_full(dz, w["w_ple_gate"], "mm_d_ple_gate")
    dh2, dh2b, small["g_ple"] = _rms_bwd(h2, du3, w["g_ple"], dh3, "rms_ple_bwd")
    slots = {}
    with_down, with_ffn, with_attn = ["w_ple_proj", "w_ple_gate"], ["w_down"], ["w_up", "w_out"]
    leaving = leave([major(n, big.pop(n)) for n in with_down]) if riding else None
    big["w_down"], got = first(_mm_tn_full(act, dh2b, "mm_g_down", tm=act.shape[1] // 2, carry=leaving))
    slots.update(zip(with_down, got))
    dact = _mm_nt_full(dh2b, w["w_down"], "mm_d_down")
    leaving = leave([major(n, big.pop(n)) for n in with_ffn]) if riding else None
    dg2, dup, gfw, small["ffn_conv_b"], *got = _ffn_bwd_act(dact, up, fw, w["ffn_conv_b"], carry=leaving)
    slots.update(zip(with_ffn, got))
    small["ffn_conv_w"] = gfw[:FFN_K]
    dup = _ffn_bwd_conv(dg2, fw, dup)
    big["w_up"] = _mm_tn_sharded(u2, dup, N_CHIPS, "mm_g_up", n_split=2, tk=_contract_tile(s) // 2)
    du2, = _mm_nt_sharded(dup, w["w_up"], "mm_d_up")
    dh1, dh1b, small["g_ffn"] = _rms_bwd(h1, du2, w["g_ffn"], dh2, "rms_ffn_bwd")
    big["w_out"] = _mm_tn_full(mix, dh1b, "mm_g_out", tm=mix.shape[1])
    dmix = _mm_nt_full(dh1b, w["w_out"], "mm_d_out")
    dc1, small["conv_ln_g"], small["conv_ln_b"], small["dw_b"] = _conv_bwd_ln(dmix, c1, w["conv_ln_g"], w["conv_ln_b"])
    dcacg, gdw = _conv_bwd_taps(proj, dc1, dw_w)
    small["dw_w"] = gdw[:CONV_K]
    leaving = leave([major(n, big.pop(n)) for n in with_attn]) if riding else None
    dqh, dkh, dv, *got = _attn_bwd(qs, kh, vb, o, dmix, tri_excl, tri_incl, carry=leaving)
    slots.update(zip(with_attn, got))
    dqkv, gq, gk = _qk_bwd(proj, dqh, dkh, dv, qg, kg, bd)
    small["q_gain"] = gq.reshape(N_HEADS, HEAD_DIM).sum(axis=0, keepdims=True)
    small["k_gain"] = gk.reshape(N_HEADS, HEAD_DIM).sum(axis=0, keepdims=True)
    dproj = jnp.concatenate([dqkv, dcacg], axis=1)
    big["w_in"] = _mm_tn_sharded(u1, dproj, N_CHIPS, "mm_g_in")
    half = big["w_in"].shape[1] // 2
    halves = [big["w_in"][:, :half], big.pop("w_in")[:, half:]] if riding else [None, None]
    du1, *got_a = _mm_nt_sharded(dproj, w["w_in"], "mm_d_in", carry=leave([halves[0]]) if riding else None)
    grad_x, _, small["g_mix"], *got_b = _rms_bwd(x, du1, w["g_mix"], dh1, "rms_mix_bwd",
                                                 carry=leave([halves[1]]) if riding else None)
    if riding:
        slots["w_in"] = jnp.concatenate([got_a[0], got_b[0]], axis=1)
    return sq[0, 0], grad_x, big, slots, small


def _exchange(srcs, per_chip, name):
    n = len(srcs)

    def body(*refs):
        src_refs, land_refs = refs[:n], refs[n:2 * n]
        send_sems, recv_sems, local_sems = refs[2 * n:]
        local = _local_copies(src_refs, land_refs, local_sems, per_chip)
        sends = _chip_copies(src_refs, land_refs, send_sems, recv_sems, per_chip, False)
        for cp in local + sends:
            cp.start()
        for cp in _chip_copies(src_refs, land_refs, send_sems, recv_sems, per_chip, True):
            cp.wait_recv()
        for cp in sends:
            cp.wait_send()
        for cp in local:
            cp.wait()

    return pl.pallas_call(
        body, name=name, in_specs=[ANY] * n, out_specs=[ANY] * n, out_shape=_exchanged_shapes(srcs, per_chip),
        scratch_shapes=_exchange_sems(n))(*srcs)


def _finish_exchange(mine, small):
    n = len(mine)

    def body(*refs):
        gin, sin = refs[:n], refs[n]
        gout, sout = refs[n + 1:2 * n + 1], refs[2 * n + 1]
        send_sems, recv_sems, small_send, small_recv, local_sem = refs[2 * n + 2:]
        x, y, c, _ = _position()
        dev = 4 * x + 2 * y + c
        flip = lambda v, bit: 1 - v if bit else v
        others = [(flip(x, k & 4), flip(y, k & 2), flip(c, k & 1)) for k in range(1, N_DEV)]
        local = pltpu.make_async_copy(sin, sout.at[dev], local_sem)
        local.start()
        swaps = [_remote(gin[a], gout[a], send_sems, recv_sems, a, (x, y, 1 - c)) for a in range(n)]
        sends = swaps + [_remote(sin, sout.at[dev], small_send, small_recv, k, peer) for k, peer in enumerate(others)]
        for cp in sends:
            cp.start()
        for cp in swaps:
            cp.wait_recv()
        for k, (px, py, pc) in enumerate(others):
            _remote(sin, sout.at[4 * px + 2 * py + pc], small_send, small_recv, k, (px, py, pc)).wait_recv()
        for cp in sends:
            cp.wait_send()
        local.wait()

    return pl.pallas_call(
        body, name="finish_exchange", in_specs=[ANY] * (n + 1), out_specs=[ANY] * (n + 1),
        out_shape=[jax.ShapeDtypeStruct(g.shape, g.dtype) for g in mine]
        + [jax.ShapeDtypeStruct((N_DEV,) + small.shape, small.dtype)],
        scratch_shapes=[pltpu.SemaphoreType.DMA((n,)), pltpu.SemaphoreType.DMA((n,)),
                        pltpu.SemaphoreType.DMA((N_DEV - 1,)), pltpu.SemaphoreType.DMA((N_DEV - 1,)),
                        pltpu.SemaphoreType.DMA])(*mine, small)


def _elem_tile(rows):
    return 128 if rows % 128 == 0 else (64 if rows % 64 == 0 else rows)


def _sum_slots(a, name):
    g, r, c = a.shape
    tr = _elem_tile(r)

    def body(a_ref, o_ref):
        acc = a_ref[0]
        for k in range(1, g):
            acc = acc + a_ref[k]
        o_ref[...] = acc

    return pl.pallas_call(
        body, name=name, grid=(r // tr,), in_specs=[pl.BlockSpec((g, tr, c), lambda i: (0, i, 0))],
        out_specs=pl.BlockSpec((tr, c), lambda i: (i, 0)), out_shape=jax.ShapeDtypeStruct((r, c), a.dtype),
        compiler_params=_params(("parallel",)))(a)


def _adamw(wt, ga, gb, m, v, name):
    r, c = wt.shape
    tr = _elem_tile(r)
    two = gb is not None

    def body(*refs):
        if two:
            w_ref, ga_ref, gb_ref, m_ref, v_ref, g_out, d_out, m_out, v_out = refs
            g = ga_ref[...] + gb_ref[...]
        else:
            w_ref, ga_ref, m_ref, v_ref, g_out, d_out, m_out, v_out = refs
            g = ga_ref[...]
        mn = ADAM_B1 * m_ref[...] + (1.0 - ADAM_B1) * g
        vn = ADAM_B2 * v_ref[...] + (1.0 - ADAM_B2) * (g * g)
        m_hat = mn / (1.0 - ADAM_B1 ** ADAM_STEP)
        v_hat = vn / (1.0 - ADAM_B2 ** ADAM_STEP)
        g_out[...] = g
        d_out[...] = -ADAM_LR * (m_hat / (jnp.sqrt(v_hat) + ADAM_EPS) + ADAM_WD * w_ref[...])
        m_out[...] = mn
        v_out[...] = vn

    blk = pl.BlockSpec((tr, c), lambda i: (i, 0))
    args = [wt, ga] + ([gb] if two else []) + [m, v]
    sds = jax.ShapeDtypeStruct((r, c), F32)
    return pl.pallas_call(
        body, name=name, grid=(r // tr,), in_specs=[blk] * len(args), out_specs=[blk] * 4, out_shape=[sds] * 4,
        compiler_params=_params(("parallel",)))(*args)


def _pack(arrs, rows):
    flat = jnp.concatenate([a.reshape(-1) for a in arrs])
    return jnp.pad(flat, (0, rows * LANES - flat.shape[0])).reshape(rows, LANES)


def _unpack(buf, shapes):
    flat = buf.reshape(-1)
    out, off = [], 0
    for shp in shapes:
        size = 1
        for d in shp:
            size *= d
        out.append(flat[off:off + size].reshape(shp))
        off += size
    return out


BIG = ["w_in", "w_out", "w_up", "w_down", "w_ple_gate", "w_ple_proj"]
COL_SHARDED = ["w_in", "w_up", "w_ple_proj"]
SMALL_REPL = ["g_mix", "q_gain", "k_gain", "dw_b", "conv_ln_g", "conv_ln_b", "g_ffn", "ffn_conv_b", "g_ple"]
SMALL_SHARDED = ["dw_w", "ffn_conv_w"]
WEIGHTS = ["g_mix", "w_in", "q_gain", "k_gain", "dw_w", "dw_b", "conv_ln_g", "conv_ln_b", "w_out", "g_ffn", "w_up",
           "ffn_conv_w", "ffn_conv_b", "w_down", "g_ple", "w_ple_gate", "w_ple_proj"]


def _rows_for(n_elems):
    return -(-n_elems // (8 * LANES)) * 8


def kernel(x, p, g_mix, w_in, q_gain, k_gain, dw_w, dw_b, conv_ln_g, conv_ln_b, w_out, g_ffn, w_up, ffn_conv_w, ffn_conv_b, w_down, g_ple, w_ple_gate, w_ple_proj, loss_target, m_g_mix, m_w_in, m_q_gain, m_k_gain, m_dw_w, m_dw_b, m_conv_ln_g, m_conv_ln_b, m_w_out, m_g_ffn, m_w_up, m_ffn_conv_w, m_ffn_conv_b, m_w_down, m_g_ple, m_w_ple_gate, m_w_ple_proj, v_g_mix, v_w_in, v_q_gain, v_k_gain, v_dw_w, v_dw_b, v_conv_ln_g, v_conv_ln_b, v_w_out, v_g_ffn, v_w_up, v_ffn_conv_w, v_ffn_conv_b, v_w_down, v_g_ple, v_w_ple_gate, v_w_ple_proj):
    given = dict(locals())
    strip = lambda n, a: a if n in SMALL_REPL else a[0]
    wts = {n: strip(n, given[n]) for n in WEIGHTS}
    mom = {n: strip(n, given["m_" + n]) for n in WEIGHTS}
    var = {n: strip(n, given["v_" + n]) for n in WEIGHTS}
    chip = 2 * lax.axis_index("x") + lax.axis_index("y")

    small_shard_shapes = [wts[n].shape for n in SMALL_SHARDED]
    filt_rows = _rows_for(sum(wts[n].size for n in SMALL_SHARDED))
    shards = {n: wts[n].astype(BF16) for n in BIG}
    shards["filters"] = _pack([wts[n] for n in SMALL_SHARDED], filt_rows)

    def unpack_filters(filt_all):
        per_chip = [_unpack(filt_all[k], small_shard_shapes) for k in range(N_CHIPS)]
        return {n: jnp.concatenate([per_chip[k][idx] for k in range(N_CHIPS)], axis=1)
                for idx, n in enumerate(SMALL_SHARDED)}

    sq, grad_x, big, slots, small = _local_step(x[0], p[0, 0], loss_target[0], {n: wts[n] for n in SMALL_REPL},
                                                shards, unpack_filters)
    loss = lax.psum(sq * (0.5 / x.shape[-1]), ("x", "y", "c"))

    small_names = SMALL_REPL + SMALL_SHARDED
    small_shapes = [small[n].shape for n in small_names]
    small_rows = _rows_for(sum(small[n].size for n in small_names))
    mine = [_sum_slots(slots[n], "sum_" + n) for n in BIG]
    *theirs, small_slots = _finish_exchange(mine, _pack([small[n] for n in small_names], small_rows))
    small_sum = dict(zip(small_names, _unpack(_sum_slots(small_slots, "sum_small"), small_shapes)))

    outs = {}
    for n, ga, gb in zip(BIG, mine, theirs):
        outs[n] = _adamw(wts[n], ga, gb, mom[n], var[n], "adamw_" + n)
    for n in SMALL_SHARDED:
        width = wts[n].shape[1]
        small_sum[n] = lax.dynamic_slice_in_dim(small_sum[n], chip * width, width, axis=1)
    local_shapes = [wts[n].shape for n in small_names]
    local_rows = _rows_for(sum(wts[n].size for n in small_names))
    packed = _adamw(_pack([wts[n] for n in small_names], local_rows), _pack([small_sum[n] for n in small_names], local_rows),
                    None, _pack([mom[n] for n in small_names], local_rows),
                    _pack([var[n] for n in small_names], local_rows), "adamw_small")
    unpacked = [_unpack(buf, local_shapes) for buf in packed]
    for idx, n in enumerate(small_names):
        outs[n] = [u[idx] for u in unpacked]
    result = [loss, grad_x[None]]
    for part in range(4):
        result += [outs[n][part] if n in SMALL_REPL else outs[n][part][None] for n in WEIGHTS]
    return tuple(result)
```

```python
import functools

import jax
import jax.numpy as jnp
from jax import lax
from jax.experimental import pallas as pl
from jax.experimental.pallas import tpu as pltpu

F32 = jnp.float32
BF16 = jnp.bfloat16
HIGHEST = lax.Precision.HIGHEST
MESH = pl.DeviceIdType.MESH
ANY = pl.BlockSpec(memory_space=pl.ANY)

EPS = 1e-6
HEAD_DIM = 64
N_HEADS = 8
ATTN_W = 512
CONV_W = 512
CONV_K = 31
FFN_K = 3
ATTN_SCALE = 0.125
LANES = 128
CONV_HALO = 32
FFN_HALO = 16
VMEM_LIMIT = 56 * 1024 * 1024

ADAM_LR = 0.001
ADAM_B1 = 0.9
ADAM_B2 = 0.999
ADAM_EPS = 1e-08
ADAM_WD = 0.01
ADAM_STEP = 10

N_CHIPS = 4
N_DEV = 8


def _params(sem):
    return pltpu.CompilerParams(dimension_semantics=sem, vmem_limit_bytes=VMEM_LIMIT)


def _row_tile(s):
    return min(512, s)


def _position():
    x, y, c = lax.axis_index("x"), lax.axis_index("y"), lax.axis_index("c")
    return x, y, c, [(1 - x, y), (x, 1 - y), (1 - x, 1 - y)]


def _remote(src, dst, send_sems, recv_sems, k, to):
    return pltpu.make_async_remote_copy(src_ref=src, dst_ref=dst, send_sem=send_sems.at[k], recv_sem=recv_sems.at[k],
                                        device_id=to, device_id_type=MESH)


def _chip_copies(src_refs, land_refs, send_sems, recv_sems, per_chip, landed):
    x, y, c, chips = _position()
    me = 2 * x + y
    out = []
    for a, (src, land) in enumerate(zip(src_refs, land_refs)):
        for j, (px, py) in enumerate(chips):
            peer = 2 * px + py
            out.append(_remote(src.at[peer] if per_chip else src, land.at[peer if landed else me],
                               send_sems, recv_sems, 3 * a + j, (px, py, c)))
    return out


def _local_copies(src_refs, land_refs, local_sems, per_chip):
    x, y, _, _ = _position()
    me = 2 * x + y
    return [pltpu.make_async_copy(src.at[me] if per_chip else src, land.at[me], local_sems.at[a])
            for a, (src, land) in enumerate(zip(src_refs, land_refs))]


def _exchanged_shapes(srcs, per_chip):
    return [jax.ShapeDtypeStruct(a.shape if per_chip else (N_CHIPS,) + a.shape, a.dtype) for a in srcs]


def _exchange_sems(n):
    return [pltpu.SemaphoreType.DMA((3 * n,)), pltpu.SemaphoreType.DMA((3 * n,)), pltpu.SemaphoreType.DMA((n,))]


def _call(body, *, name, grid, in_specs, out_specs, out_shape, scratch_shapes, args, semantics, carry=None):
    if carry is None:
        return pl.pallas_call(body, name=name, grid=grid, in_specs=in_specs, out_specs=out_specs, out_shape=out_shape,
                              scratch_shapes=scratch_shapes, compiler_params=_params(semantics))(*args)
    srcs, per_chip = carry
    n, n_in, n_out, n_scr = len(srcs), len(in_specs), len(out_specs), len(scratch_shapes)

    def wrapped(*refs):
        ins, xin = refs[:n_in], refs[n_in:n_in + n]
        outs, xout = refs[n_in + n:n_in + n + n_out], refs[n_in + n + n_out:n_in + 2 * n + n_out]
        scratch = refs[n_in + 2 * n + n_out:n_in + 2 * n + n_out + n_scr]
        send_sems, recv_sems, local_sems = refs[-3:]
        first = functools.reduce(jnp.logical_and, [pl.program_id(d) == 0 for d in range(len(grid))])
        last = functools.reduce(jnp.logical_and, [pl.program_id(d) == g - 1 for d, g in enumerate(grid)])

        @pl.when(first)
        def _():
            for cp in _local_copies(xin, xout, local_sems, per_chip):
                cp.start()
            for cp in _chip_copies(xin, xout, send_sems, recv_sems, per_chip, False):
                cp.start()

        body(*ins, *outs, *scratch)

        @pl.when(last)
        def _():
            for cp in _chip_copies(xin, xout, send_sems, recv_sems, per_chip, True):
                cp.wait_recv()
            for cp in _chip_copies(xin, xout, send_sems, recv_sems, per_chip, False):
                cp.wait_send()
            for cp in _local_copies(xin, xout, local_sems, per_chip):
                cp.wait()

    return pl.pallas_call(
        wrapped, name=name, grid=grid, in_specs=list(in_specs) + [ANY] * n, out_specs=list(out_specs) + [ANY] * n,
        out_shape=list(out_shape) + _exchanged_shapes(srcs, per_chip),
        scratch_shapes=list(scratch_shapes) + _exchange_sems(n),
        compiler_params=_params(("arbitrary",) * len(grid)))(*args, *srcs)


def _contract_tile(s):
    return min(2048, s)


def _rstd(x):
    return lax.rsqrt(jnp.mean(x * x, axis=-1, keepdims=True) + EPS)


def _sigmoid(x):
    return 1.0 / (1.0 + jnp.exp(-x))


def _mm(a, b, *, name, dims, grid, a_spec, b_spec, o_spec, o_tile, out_shape, res=None, res_spec=None, carry=None):
    nk = grid[2]

    def body(*refs):
        if res is None:
            a_ref, b_ref, o_ref, acc_ref = refs
            r_ref = None
        else:
            a_ref, b_ref, r_ref, o_ref, acc_ref = refs
        part = lax.dot_general(a_ref[...], b_ref[...], (dims, ((), ())), preferred_element_type=F32)

        def finish(val):
            if r_ref is not None:
                val = val + r_ref[...]
            o_ref[...] = val.astype(o_ref.dtype)

        if nk == 1:
            finish(part)
        else:
            k = pl.program_id(2)

            @pl.when(k == 0)
            def _():
                acc_ref[...] = part

            @pl.when(k > 0)
            def _():
                acc_ref[...] += part

            @pl.when(k == nk - 1)
            def _():
                finish(acc_ref[...])

    in_specs = [a_spec, b_spec]
    args = [a, b]
    if res is not None:
        in_specs.append(res_spec)
        args.append(res)
    acc_tile = o_tile if nk > 1 else (8, LANES)
    out = _call(body, name=name, grid=grid, in_specs=in_specs, out_specs=[o_spec], out_shape=[out_shape],
                scratch_shapes=[pltpu.VMEM(acc_tile, F32)], args=args,
                semantics=("parallel", "parallel", "arbitrary"), carry=carry)
    return out[0] if carry is None else out


NN = ((1,), (0,))
NT = ((1,), (1,))
TN = ((0,), (0,))


def _mm_nn_sharded(a, bg, name, out_dtype=F32, tm=None):
    s, k = a.shape
    g, _, ns = bg.shape
    tm = tm or _row_tile(s)

    def body(a_ref, b_ref, o_ref):
        av = a_ref[...]
        for gi in range(g):
            o_ref[:, gi * ns:(gi + 1) * ns] = jnp.dot(av, b_ref[gi], preferred_element_type=F32).astype(out_dtype)

    return pl.pallas_call(
        body, name=name, grid=(s // tm,),
        in_specs=[pl.BlockSpec((tm, k), lambda i: (i, 0)), pl.BlockSpec((g, k, ns), lambda i: (0, 0, 0))],
        out_specs=pl.BlockSpec((tm, g * ns), lambda i: (i, 0)),
        out_shape=jax.ShapeDtypeStruct((s, g * ns), out_dtype), compiler_params=_params(("parallel",)))(a, bg)


def _mm_nn_full(a, b, name, res=None):
    s, k = a.shape
    n = b.shape[1]
    tm = _row_tile(s)
    rs = pl.BlockSpec((tm, n), lambda i, j, kk: (i, 0))
    return _mm(a, b, name=name, dims=NN, grid=(s // tm, 1, 1),
               a_spec=pl.BlockSpec((tm, k), lambda i, j, kk: (i, 0)),
               b_spec=pl.BlockSpec((k, n), lambda i, j, kk: (0, 0)),
               o_spec=rs, o_tile=(tm, n), out_shape=jax.ShapeDtypeStruct((s, n), F32),
               res=res, res_spec=rs if res is not None else None)


def _mm_nt_full(a, b, name):
    s, n = a.shape
    k = b.shape[0]
    tm = _row_tile(s)
    return _mm(a, b, name=name, dims=NT, grid=(s // tm, 1, 1),
               a_spec=pl.BlockSpec((tm, n), lambda i, j, kk: (i, 0)),
               b_spec=pl.BlockSpec((k, n), lambda i, j, kk: (0, 0)),
               o_spec=pl.BlockSpec((tm, k), lambda i, j, kk: (i, 0)), o_tile=(tm, k),
               out_shape=jax.ShapeDtypeStruct((s, k), F32))


def _mm_nt_sharded(a, bg, name, carry=None):
    s = a.shape[0]
    g, k, ns = bg.shape
    tm = _row_tile(s)

    def body(a_ref, b_ref, o_ref):
        acc = lax.dot_general(a_ref[:, 0:ns], b_ref[0], (NT, ((), ())), preferred_element_type=F32)
        for gi in range(1, g):
            acc = acc + lax.dot_general(a_ref[:, gi * ns:(gi + 1) * ns], b_ref[gi], (NT, ((), ())),
                                        preferred_element_type=F32)
        o_ref[...] = acc

    return _call(
        body, name=name, grid=(s // tm,),
        in_specs=[pl.BlockSpec((tm, g * ns), lambda i: (i, 0)), pl.BlockSpec((g, k, ns), lambda i: (0, 0, 0))],
        out_specs=[pl.BlockSpec((tm, k), lambda i: (i, 0))], out_shape=[jax.ShapeDtypeStruct((s, k), F32)],
        scratch_shapes=[], args=(a, bg), semantics=("parallel",), carry=carry)


def _mm_tn_sharded(a, b, g, name, n_split=1, tk=None):
    s, k = a.shape
    ns = b.shape[1] // g
    gs = g // n_split
    tk = tk or _contract_tile(s)

    def body(a_ref, b_ref, o_ref):
        first = pl.program_id(1) == 0
        at = a_ref[...].T
        for gi in range(gs):
            part = jnp.dot(at, b_ref[:, gi * ns:(gi + 1) * ns], preferred_element_type=F32)

            @pl.when(first)
            def _(gi=gi, part=part):
                o_ref[gi] = part

            @pl.when(jnp.logical_not(first))
            def _(gi=gi, part=part):
                o_ref[gi] += part

    return pl.pallas_call(
        body, name=name, grid=(n_split, s // tk),
        in_specs=[pl.BlockSpec((tk, k), lambda j, kk: (kk, 0)), pl.BlockSpec((tk, gs * ns), lambda j, kk: (kk, j))],
        out_specs=pl.BlockSpec((gs, k, ns), lambda j, kk: (j, 0, 0)),
        out_shape=jax.ShapeDtypeStruct((g, k, ns), F32), compiler_params=_params(("parallel", "arbitrary")))(a, b)


def _mm_tn_full(a, b, name, tm, carry=None):
    s, m = a.shape
    n = b.shape[1]
    tk = _contract_tile(s)
    return _mm(a, b, name=name, dims=TN, grid=(m // tm, 1, s // tk),
               a_spec=pl.BlockSpec((tk, tm), lambda i, j, kk: (kk, i)),
               b_spec=pl.BlockSpec((tk, n), lambda i, j, kk: (kk, 0)),
               o_spec=pl.BlockSpec((tm, n), lambda i, j, kk: (i, 0)), o_tile=(tm, n),
               out_shape=jax.ShapeDtypeStruct((m, n), F32), carry=carry)


def _rms_fwd(x, g, name, carry=None):
    s, d = x.shape
    ts = _row_tile(s)

    def body(x_ref, g_ref, u_ref):
        xv = x_ref[...]
        u_ref[...] = (xv * _rstd(xv) * g_ref[...]).astype(BF16)

    row = pl.BlockSpec((ts, d), lambda i: (i, 0))
    out = _call(body, name=name, grid=(s // ts,), in_specs=[row, pl.BlockSpec((1, d), lambda i: (0, 0))],
                out_specs=[row], out_shape=[jax.ShapeDtypeStruct((s, d), BF16)], scratch_shapes=[], args=(x, g),
                semantics=("parallel",), carry=carry)
    return out[0] if carry is None else out


def _rms_bwd(h, du, g, dh_in, name, carry=None):
    s, d = h.shape
    ts = _row_tile(s)

    def body(h_ref, du_ref, g_ref, dhin_ref, dh_ref, dhb_ref, gg_ref):
        hv = h_ref[...]
        r = _rstd(hv)
        xhat = hv * r
        duv = du_ref[...]
        dxhat = duv * g_ref[...]
        m = jnp.mean(dxhat * xhat, axis=-1, keepdims=True)
        dh = dhin_ref[...] + r * (dxhat - xhat * m)
        dh_ref[...] = dh
        dhb_ref[...] = dh.astype(BF16)
        part = jnp.sum(duv * xhat, axis=0, keepdims=True)

        @pl.when(pl.program_id(0) == 0)
        def _():
            gg_ref[...] = part

        @pl.when(pl.program_id(0) > 0)
        def _():
            gg_ref[...] += part

    row = pl.BlockSpec((ts, d), lambda i: (i, 0))
    vec = pl.BlockSpec((1, d), lambda i: (0, 0))
    return _call(
        body, name=name, grid=(s // ts,), in_specs=[row, row, vec, row], out_specs=[row, row, vec],
        out_shape=[jax.ShapeDtypeStruct((s, d), F32), jax.ShapeDtypeStruct((s, d), BF16),
                   jax.ShapeDtypeStruct((1, d), F32)],
        scratch_shapes=[], args=(h, du, g, dh_in), semantics=("arbitrary",), carry=carry)


def _head_sum(x, bd):
    return _tri_dot(_split(x), bd)


def _qkv_prep(proj, qg, kg, bd):
    s = proj.shape[0]
    ts = _row_tile(s)

    def body(q_ref, k_ref, v_ref, qg_ref, kg_ref, bd_ref, qs_ref, kh_ref, vb_ref):
        def norm(x, gain):
            ms = _head_sum(x * x, bd_ref[...]) * (1.0 / HEAD_DIM)
            return x * lax.rsqrt(ms + EPS) * gain

        qs_ref[...] = (norm(q_ref[...], qg_ref[...]) * ATTN_SCALE).astype(BF16)
        kh_ref[...] = norm(k_ref[...], kg_ref[...]).astype(BF16)
        vb_ref[...] = v_ref[...].astype(BF16)

    col = lambda c: pl.BlockSpec((ts, ATTN_W), lambda i: (i, c))
    vec = pl.BlockSpec((1, ATTN_W), lambda i: (0, 0))
    out = pl.BlockSpec((ts, ATTN_W), lambda i: (i, 0))
    sds = jax.ShapeDtypeStruct((s, ATTN_W), BF16)
    return pl.pallas_call(
        body, name="qkv_prep", grid=(s // ts,),
        in_specs=[col(0), col(1), col(2), vec, vec, pl.BlockSpec((ATTN_W, ATTN_W), lambda i: (0, 0))],
        out_specs=[out, out, out], out_shape=[sds, sds, sds],
        compiler_params=_params(("parallel",)))(proj, proj, proj, qg, kg, bd)


def _qk_bwd(proj, dqh, dkh, dv, qg, kg, bd):
    s = proj.shape[0]
    ts = _row_tile(s)

    def body(q_ref, k_ref, dqh_ref, dkh_ref, dv_ref, qg_ref, kg_ref, bd_ref, out_ref, gq_ref, gk_ref):
        first = pl.program_id(0) == 0

        def bwd(x, dy, gain, gg_ref):
            ms = _head_sum(x * x, bd_ref[...]) * (1.0 / HEAD_DIM)
            r = lax.rsqrt(ms + EPS)
            xhat = x * r
            dxhat = dy * gain
            m = _head_sum(dxhat * xhat, bd_ref[...]) * (1.0 / HEAD_DIM)
            part = jnp.sum(dy * xhat, axis=0, keepdims=True)

            @pl.when(first)
            def _():
                gg_ref[...] = part

            @pl.when(jnp.logical_not(first))
            def _():
                gg_ref[...] += part

            return r * (dxhat - xhat * m)

        out_ref[:, 0:ATTN_W] = bwd(q_ref[...], dqh_ref[...], qg_ref[...], gq_ref).astype(BF16)
        out_ref[:, ATTN_W:2 * ATTN_W] = bwd(k_ref[...], dkh_ref[...], kg_ref[...], gk_ref).astype(BF16)
        out_ref[:, 2 * ATTN_W:3 * ATTN_W] = dv_ref[...].astype(BF16)

    col = lambda c: pl.BlockSpec((ts, ATTN_W), lambda i: (i, c))
    row = pl.BlockSpec((ts, ATTN_W), lambda i: (i, 0))
    vec = pl.BlockSpec((1, ATTN_W), lambda i: (0, 0))
    return pl.pallas_call(
        body, name="qk_bwd", grid=(s // ts,),
        in_specs=[col(0), col(1), row, row, row, vec, vec, pl.BlockSpec((ATTN_W, ATTN_W), lambda i: (0, 0))],
        out_specs=[pl.BlockSpec((ts, 3 * ATTN_W), lambda i: (i, 0)), vec, vec],
        out_shape=[jax.ShapeDtypeStruct((s, 3 * ATTN_W), BF16), jax.ShapeDtypeStruct((1, ATTN_W), F32),
                   jax.ShapeDtypeStruct((1, ATTN_W), F32)],
        compiler_params=_params(("arbitrary",)))(proj, proj, dqh, dkh, dv, qg, kg, bd)


def _split(x):
    hi = x.astype(BF16)
    return hi, (x - hi.astype(F32)).astype(BF16)


def _tri_dot(parts, tri):
    hi, lo = parts
    return jnp.dot(hi, tri, preferred_element_type=F32) + jnp.dot(lo, tri, preferred_element_type=F32)


def _log_sigmoids(z):
    neg_abs = lax.bitcast_convert_type(lax.bitcast_convert_type(z, jnp.uint32) | jnp.uint32(0x80000000), F32)
    lb = jnp.minimum(z, 0.0) - jnp.log(1.0 + jnp.exp(neg_abs))
    return lb, lb - z


DEAD_LOG_WEIGHT = -106.0


def _sweep_key_blocks(tiles, alive, i):
    @pl.when(i == 0)
    def _():
        tiles([0], [True])

    @pl.when(i > 0)
    def _():
        tiles([i, i - 1], [True, False])

    def more(state):
        kb, live = state
        return jnp.logical_and(kb >= 0, live > 0)

    def step(state):
        kb, _ = state
        tiles([kb], [False])
        return kb - 1, alive().astype(jnp.int32)

    lax.while_loop(more, step, (i - 2, alive().astype(jnp.int32)))


def _head_masks():
    lane = lax.broadcasted_iota(jnp.int32, (1, LANES), 1)
    return [lane < HEAD_DIM, lane >= HEAD_DIM]


def _attn_fwd(qs, kh, vb, tri_excl, carry=None):
    s = qs.shape[0]
    t = tri_excl.shape[0]
    nq = s // t

    def body(q_ref, k_ref, v_ref, tri_ref, o_ref, ob_ref, acc_ref, c_ref):
        i = pl.program_id(1)
        hmask = _head_masks()
        q = q_ref[...]
        qm = [jnp.where(hm, q, jnp.zeros_like(q)) for hm in hmask]
        acc_ref[...] = jnp.zeros_like(acc_ref)
        c_ref[...] = jnp.zeros_like(c_ref)
        causal = (lax.broadcasted_iota(jnp.int32, (t, t), 1) < lax.broadcasted_iota(jnp.int32, (t, t), 0))

        def tiles(kbs, masked):
            tri = tri_ref[...]
            starts = [pl.multiple_of(kb * t, t) for kb in kbs]
            kblks = [k_ref[pl.ds(k0, t), :] for k0 in starts]
            vblks = [v_ref[pl.ds(k0, t), :] for k0 in starts]
            chains = [(j, h) for j in range(len(kbs)) for h in range(2)]
            carry = [c_ref[h] for h in range(2)]
            pv = [None, None]
            lbs, loms, between = {}, {}, {}
            for step in range(len(chains) + 2):
                if step < len(chains):
                    j, h = chains[step]
                    z = lax.dot_general(qm[h], kblks[j], (NT, ((), ())), preferred_element_type=F32)
                    lbs[step], lom = _log_sigmoids(z)
                    loms[step] = jnp.where(causal, lom, 0.0) if masked[j] else lom
                if 0 <= step - 1 < len(chains):
                    between[step - 1] = _tri_dot(_split(loms[step - 1]), tri)
                if 0 <= step - 2 < len(chains):
                    n = step - 2
                    j, h = chains[n]
                    w = jnp.exp(lbs[n] + between[n] + carry[h])
                    if masked[j]:
                        w = jnp.where(causal, w, 0.0)
                    carry[h] = carry[h] + jnp.sum(loms[n], axis=-1, keepdims=True)
                    part = jnp.dot(w.astype(BF16), vblks[j], preferred_element_type=F32)
                    pv[h] = part if pv[h] is None else pv[h] + part
            for h in range(2):
                c_ref[h] = carry[h]
                acc_ref[h] += pv[h]

        _sweep_key_blocks(tiles, lambda: jnp.max(c_ref[...]) > DEAD_LOG_WEIGHT, i)
        o = jnp.where(hmask[0], acc_ref[0], acc_ref[1])
        o_ref[...] = o
        ob_ref[...] = o.astype(BF16)

    qspec = pl.BlockSpec((t, LANES), lambda hp, i: (i, hp))
    kspec = pl.BlockSpec((s, LANES), lambda hp, i: (0, hp))
    return _call(
        body, name="attn_fwd", grid=(ATTN_W // LANES, nq),
        in_specs=[qspec, kspec, kspec, pl.BlockSpec((t, t), lambda hp, i: (0, 0))],
        out_specs=[qspec, qspec],
        out_shape=[jax.ShapeDtypeStruct((s, ATTN_W), F32), jax.ShapeDtypeStruct((s, ATTN_W), BF16)],
        scratch_shapes=[pltpu.VMEM((2, t, LANES), F32), pltpu.VMEM((2, t, 1), F32)],
        args=(qs, kh, vb, tri_excl), semantics=("parallel", "arbitrary"), carry=carry)


def _attn_bwd(qs, kh, vb, o, dmix, tri_excl, tri_incl, carry=None):
    s = qs.shape[0]
    t = tri_excl.shape[0]
    nq = s // t

    def body(q_ref, k_ref, v_ref, o_ref, do_ref, te_ref, ti_ref, dq_ref, dk_ref, dv_ref, dqacc_ref, c_ref):
        i = pl.program_id(1)

        @pl.when(i == 0)
        def _():
            dk_ref[...] = jnp.zeros_like(dk_ref)
            dv_ref[...] = jnp.zeros_like(dv_ref)

        hmask = _head_masks()
        q = q_ref[...]
        do = do_ref[...]
        dob = do.astype(BF16)
        prod = dob.astype(F32) * o_ref[...]
        qm =[jnp.where(hm, q, jnp.zeros_like(q)) for hm in hmask]
        dom = [jnp.where(hm, dob, jnp.zeros_like(dob)) for hm in hmask]
        total = [jnp.sum(jnp.where(hm, prod, 0.0), axis=-1, keepdims=True) for hm in hmask]
        dqacc_ref[...] = jnp.zeros_like(dqacc_ref)
        c_ref[...] = jnp.zeros_like(c_ref)
        causal = (lax.broadcasted_iota(jnp.int32, (t, t), 1) < lax.broadcasted_iota(jnp.int32, (t, t), 0))

        def tiles(kbs, masked):
            te = te_ref[...]
            ti = ti_ref[...]
            starts = [pl.multiple_of(kb * t, t) for kb in kbs]
            kblks = [k_ref[pl.ds(k0, t), :] for k0 in starts]
            vblks = [v_ref[pl.ds(k0, t), :] for k0 in starts]
            chains = [(j, h) for j in range(len(kbs)) for h in range(2)]
            c_lom = [c_ref[2 * h] for h in range(2)]
            c_g = [c_ref[2 * h + 1] for h in range(2)]
            lbs, loms, dws, between, wbs, gs, g_after = {}, {}, {}, {}, {}, {}, {}
            dq = [None, None]
            dk = [None] * len(kbs)
            dv = [None] * len(kbs)
            add = lambda acc, part: part if acc is None else acc + part
            for step in range(len(chains) + 3):
                if step < len(chains):
                    j, h = chains[step]
                    z = lax.dot_general(qm[h], kblks[j], (NT, ((), ())), preferred_element_type=F32)
                    dws[step] = lax.dot_general(dom[h], vblks[j], (NT, ((), ())), preferred_element_type=F32)
                    lbs[step], lom = _log_sigmoids(z)
                    loms[step] = jnp.where(causal, lom, 0.0) if masked[j] else lom
                if 0 <= step - 1 < len(chains):
                    between[step - 1] = _tri_dot(_split(loms[step - 1]), te)
                if 0 <= step - 2 < len(chains):
                    n = step - 2
                    j, h = chains[n]
                    w = jnp.exp(lbs[n] + between[n] + c_lom[h])
                    if masked[j]:
                        w = jnp.where(causal, w, 0.0)
                    c_lom[h] = c_lom[h] + jnp.sum(loms[n], axis=-1, keepdims=True)
                    wbs[n] = w.astype(BF16)
                    gs[n] = dws[n] * wbs[n].astype(F32)
                    g_after[n] = _tri_dot(_split(gs[n]), ti)
                if 0 <= step - 3 < len(chains):
                    n = step - 3
                    j, h = chains[n]
                    beta = jnp.exp(lbs[n])
                    dz = gs[n] * (1.0 - beta) - beta * (total[h] - (g_after[n] + c_g[h]))
                    if masked[j]:
                        dz = jnp.where(causal, dz, 0.0)
                    c_g[h] = c_g[h] + jnp.sum(gs[n], axis=-1, keepdims=True)
                    dzb = dz.astype(BF16)
                    dq[h] = add(dq[h], jnp.dot(dzb, kblks[j], preferred_element_type=F32))
                    dk[j] = add(dk[j], lax.dot_general(dzb, qm[h], (TN, ((), ())), preferred_element_type=F32))
                    dv[j] = add(dv[j], lax.dot_general(wbs[n], dom[h], (TN, ((), ())), preferred_element_type=F32))
            for h in range(2):
                c_ref[2 * h] = c_lom[h]
                c_ref[2 * h + 1] = c_g[h]
                dqacc_ref[h] += dq[h]
            for j, k0 in enumerate(starts):
                dk_ref[pl.ds(k0, t), :] += dk[j]
                dv_ref[pl.ds(k0, t), :] += dv[j]

        _sweep_key_blocks(tiles, lambda: jnp.maximum(jnp.max(c_ref[0]), jnp.max(c_ref[2])) > DEAD_LOG_WEIGHT, i)
        dq_ref[...] = jnp.where(hmask[0], dqacc_ref[0], dqacc_ref[1]) * ATTN_SCALE

    qspec = pl.BlockSpec((t, LANES), lambda hp, i: (i, hp))
    kspec = pl.BlockSpec((s, LANES), lambda hp, i: (0, hp))
    tspec = pl.BlockSpec((t, t), lambda hp, i: (0, 0))
    sds = jax.ShapeDtypeStruct((s, ATTN_W), F32)
    return _call(
        body, name="attn_bwd", grid=(ATTN_W // LANES, nq),
        in_specs=[qspec, kspec, kspec, qspec, qspec, tspec, tspec],
        out_specs=[qspec, kspec, kspec], out_shape=[sds, sds, sds],
        scratch_shapes=[pltpu.VMEM((2, t, LANES), F32), pltpu.VMEM((4, t, 1), F32)],
        args=(qs, kh, vb, o, dmix, tri_excl, tri_incl), semantics=("parallel", "arbitrary"), carry=carry)


CONV_ROWS = 64
CONV_COLS = 256


SUBLANES = 8


def _shift_copies(src_ref, sh_ref):
    length = sh_ref.shape[1]
    for r in range(1, SUBLANES):
        sh_ref[r - 1] = src_ref[r:r + length, :]


def _shift_scratch(ts):
    return pltpu.VMEM((SUBLANES - 1, ts + CONV_HALO - SUBLANES, CONV_W), F32)


def _rows_at(src_ref, sh_ref, offset, r0, rows, cols):
    r = offset % SUBLANES
    base = offset - r + r0
    return src_ref[base:base + rows, cols] if r == 0 else sh_ref[r - 1, base:base + rows, cols]


def _taps(src_ref, sh_ref, w_ref, n_taps, first_row, rows, reverse=False):
    width = src_ref.shape[1]
    cols = min(CONV_COLS, width)
    out = []
    for r0 in range(0, rows, CONV_ROWS):
        for c0 in range(0, width, cols):
            acc = jnp.zeros((CONV_ROWS, cols), F32)
            for k in range(n_taps):
                off = (n_taps - 1 - k) if reverse else k
                acc = acc + w_ref[k:k + 1, c0:c0 + cols] * _rows_at(src_ref, sh_ref, first_row + off, r0, CONV_ROWS,
                                                                    slice(c0, c0 + cols))
            out.append(((r0, c0), acc))
    return out


def _conv_fwd(proj, dw_w, dw_b, ln_g, ln_b, carry=None):
    s = proj.shape[0]
    ts = _row_tile(s)
    hb = ts // CONV_HALO

    def body(a_ref, g_ref, ha_ref, hg_ref, w_ref, b_ref, lg_ref, lb_ref, c1_ref, c3_ref, pad_ref, sh_ref):
        i = pl.program_id(0)
        halo = ha_ref[...] * _sigmoid(hg_ref[...])
        pad_ref[0:CONV_HALO, :] = jnp.where(i > 0, halo, 0.0)
        pad_ref[CONV_HALO:, :] = a_ref[...] * _sigmoid(g_ref[...])
        _shift_copies(pad_ref, sh_ref)
        first = CONV_HALO - (CONV_K - 1)
        for (r0, c0), acc in _taps(pad_ref, sh_ref, w_ref, CONV_K, first, ts):
            c1_ref[r0:r0 + CONV_ROWS, c0:c0 + acc.shape[1]] = acc + b_ref[:, c0:c0 + acc.shape[1]]
        c1 = c1_ref[...]
        xc = c1 - jnp.mean(c1, axis=-1, keepdims=True)
        c2 = xc * _rstd(xc) * lg_ref[...] + lb_ref[...]
        c3_ref[...] = (c2 * _sigmoid(c2)).astype(BF16)

    cur = lambda c: pl.BlockSpec((ts, CONV_W), lambda i: (i, c))
    halo = lambda c: pl.BlockSpec((CONV_HALO, CONV_W), lambda i: (jnp.maximum(i * hb - 1, 0), c))
    vec = pl.BlockSpec((1, CONV_W), lambda i: (0, 0))
    row = pl.BlockSpec((ts, CONV_W), lambda i: (i, 0))
    return _call(
        body, name="conv_fwd", grid=(s // ts,),
        in_specs=[cur(3), cur(4), halo(3), halo(4), pl.BlockSpec((CONV_HALO, CONV_W), lambda i: (0, 0)), vec, vec, vec],
        out_specs=[row, row],
        out_shape=[jax.ShapeDtypeStruct((s, CONV_W), F32), jax.ShapeDtypeStruct((s, CONV_W), BF16)],
        scratch_shapes=[pltpu.VMEM((ts + CONV_HALO, CONV_W), F32), _shift_scratch(ts)],
        args=(proj, proj, proj, proj, dw_w, dw_b, ln_g, ln_b), semantics=("parallel",), carry=carry)


def _conv_bwd_ln(dmix, c1, ln_g, ln_b):
    s = c1.shape[0]
    ts = _row_tile(s)

    def body(d_ref, c1_ref, lg_ref, lb_ref, dc1_ref, glg_ref, glb_ref, gb_ref):
        c1v = c1_ref[...]
        xc = c1v - jnp.mean(c1v, axis=-1, keepdims=True)
        r = _rstd(xc)
        xhat = xc * r
        c2 = xhat * lg_ref[...] + lb_ref[...]
        sg = _sigmoid(c2)
        dc2 = d_ref[...] * (sg * (1.0 + c2 * (1.0 - sg)))
        dxhat = dc2 * lg_ref[...]
        dc1 = r * (dxhat - jnp.mean(dxhat, axis=-1, keepdims=True)
                   - xhat * jnp.mean(dxhat * xhat, axis=-1, keepdims=True))
        dc1_ref[...] = dc1
        parts = [(glg_ref, jnp.sum(dc2 * xhat, axis=0, keepdims=True)),
                 (glb_ref, jnp.sum(dc2, axis=0, keepdims=True)),
                 (gb_ref, jnp.sum(dc1, axis=0, keepdims=True))]

        @pl.when(pl.program_id(0) == 0)
        def _():
            for ref, part in parts:
                ref[...] = part

        @pl.when(pl.program_id(0) > 0)
        def _():
            for ref, part in parts:
                ref[...] += part

    row = pl.BlockSpec((ts, CONV_W), lambda i: (i, 0))
    vec = pl.BlockSpec((1, CONV_W), lambda i: (0, 0))
    vsd = jax.ShapeDtypeStruct((1, CONV_W), F32)
    return pl.pallas_call(
        body, name="conv_bwd_ln", grid=(s // ts,),
        in_specs=[pl.BlockSpec((ts, CONV_W), lambda i: (i, 1)), row, vec, vec],
        out_specs=[row, vec, vec, vec],
        out_shape=[jax.ShapeDtypeStruct((s, CONV_W), F32), vsd, vsd, vsd],
        compiler_params=_params(("arbitrary",)))(dmix, c1, ln_g, ln_b)


def _conv_bwd_taps(proj, dc1, dw_w):
    s = proj.shape[0]
    ts = _row_tile(s)
    hb = ts // CONV_HALO
    last = s // CONV_HALO - 1
    nsteps = s // ts

    def body(a_ref, g_ref, ha_ref, hg_ref, d_ref, hd_ref, w_ref, out_ref, gw_ref, pad_ref, dpad_ref, sh_ref):
        i = pl.program_id(0)
        av = a_ref[...]
        sg = _sigmoid(g_ref[...])
        halo = ha_ref[...] * _sigmoid(hg_ref[...])
        pad_ref[0:CONV_HALO, :] = jnp.where(i > 0, halo, 0.0)
        pad_ref[CONV_HALO:, :] = av * sg
        dpad_ref[0:ts, :] = d_ref[...]
        dpad_ref[ts:, :] = jnp.where(i < nsteps - 1, hd_ref[...], 0.0)

        @pl.when(i == 0)
        def _():
            gw_ref[...] = jnp.zeros_like(gw_ref)

        _shift_copies(pad_ref, sh_ref)
        first = CONV_HALO - (CONV_K - 1)
        fold = lambda v: jnp.sum(v.reshape(CONV_ROWS // SUBLANES, SUBLANES, v.shape[1]), axis=0)
        for c0 in range(0, CONV_W, CONV_COLS):
            cs = slice(c0, c0 + CONV_COLS)
            for k in range(CONV_K):
                acc = None
                for r0 in range(0, ts, CONV_ROWS):
                    part = fold(d_ref[r0:r0 + CONV_ROWS, cs] * _rows_at(pad_ref, sh_ref, first + k, r0, CONV_ROWS, cs))
                    acc = part if acc is None else acc + part
                gw_ref[k:k + 1, cs] += jnp.sum(acc, axis=0, keepdims=True)
        _shift_copies(dpad_ref, sh_ref)
        for (r0, c0), dc0 in _taps(dpad_ref, sh_ref, w_ref, CONV_K, 0, ts, reverse=True):
            cs = slice(c0, c0 + dc0.shape[1])
            a_c = a_ref[r0:r0 + CONV_ROWS, cs]
            sg_c = _sigmoid(g_ref[r0:r0 + CONV_ROWS, cs])
            out_ref[r0:r0 + CONV_ROWS, cs] = (dc0 * sg_c).astype(BF16)
            out_ref[r0:r0 + CONV_ROWS, CONV_W + c0:CONV_W + c0 + dc0.shape[1]] = (
                dc0 * a_c * sg_c * (1.0 - sg_c)).astype(BF16)

    cur = lambda c: pl.BlockSpec((ts, CONV_W), lambda i: (i, c))
    halo = lambda c: pl.BlockSpec((CONV_HALO, CONV_W), lambda i: (jnp.maximum(i * hb - 1, 0), c))
    row = pl.BlockSpec((ts, CONV_W), lambda i: (i, 0))
    nxt = pl.BlockSpec((CONV_HALO, CONV_W), lambda i: (jnp.minimum((i + 1) * hb, last), 0))
    wspec = pl.BlockSpec((CONV_HALO, CONV_W), lambda i: (0, 0))
    return pl.pallas_call(
        body, name="conv_bwd_taps", grid=(nsteps,),
        in_specs=[cur(3), cur(4), halo(3), halo(4), row, nxt, wspec],
        out_specs=[pl.BlockSpec((ts, 2 * CONV_W), lambda i: (i, 0)), wspec],
        out_shape=[jax.ShapeDtypeStruct((s, 2 * CONV_W), BF16), jax.ShapeDtypeStruct((CONV_HALO, CONV_W), F32)],
        scratch_shapes=[pltpu.VMEM((ts + CONV_HALO, CONV_W), F32), pltpu.VMEM((ts + CONV_HALO, CONV_W), F32),
                        _shift_scratch(ts)],
        compiler_params=_params(("arbitrary",)))(proj, proj, proj, proj, dc1, dc1, dw_w)


SQRT_HALF = 0.7071067811865476
INV_SQRT_2PI = 0.3989422804014327


def _gelu_parts(x):
    t = x * SQRT_HALF
    cdf = 0.5 + 0.5 * lax.erf(t)
    return x * cdf, cdf + x * (INV_SQRT_2PI * jnp.exp(-(t * t)))


def _ffn_tile(dff):
    return dff // 2


FFN_ROWS = 64
FFN_COLS = LANES


def _ffn_chunks(ts, tc):
    return [(r0, slice(c0, c0 + FFN_COLS)) for c0 in range(0, tc, FFN_COLS) for r0 in range(0, ts, FFN_ROWS)]


def _ffn_gate2(pad_ref, w_ref, b_ref, r0, cs):
    first = FFN_HALO - (FFN_K - 1) + r0
    g2 = b_ref[:, cs] + w_ref[0:1, cs] * pad_ref[first:first + FFN_ROWS, cs]
    for k in range(1, FFN_K):
        g2 = g2 + w_ref[k:k + 1, cs] * pad_ref[first + k:first + k + FFN_ROWS, cs]
    return g2


def _ffn_act(up, fw, fb):
    s = up.shape[0]
    dff = up.shape[1] // 2
    tc = _ffn_tile(dff)
    nj = dff // tc
    ts = _row_tile(s)
    hb = ts // FFN_HALO

    def body(g_ref, v_ref, hg_ref, w_ref, b_ref, act_ref, pad_ref):
        i = pl.program_id(0)
        pad_ref[0:FFN_HALO, :] = jnp.where(i > 0, hg_ref[...].astype(F32), 0.0)
        pad_ref[FFN_HALO:, :] = g_ref[...].astype(F32)
        for r0, cs in _ffn_chunks(ts, tc):
            gelu, _ = _gelu_parts(_ffn_gate2(pad_ref, w_ref, b_ref, r0, cs))
            act_ref[r0:r0 + FFN_ROWS, cs] = (gelu * v_ref[r0:r0 + FFN_ROWS, cs].astype(F32)).astype(BF16)

    return pl.pallas_call(
        body, name="ffn_act", grid=(s // ts, nj),
        in_specs=[pl.BlockSpec((ts, tc), lambda i, j: (i, j)), pl.BlockSpec((ts, tc), lambda i, j: (i, j + nj)),
                  pl.BlockSpec((FFN_HALO, tc), lambda i, j: (jnp.maximum(i * hb - 1, 0), j)),
                  pl.BlockSpec((FFN_HALO, tc), lambda i, j: (0, j)), pl.BlockSpec((1, tc), lambda i, j: (0, j))],
        out_specs=pl.BlockSpec((ts, tc), lambda i, j: (i, j)),
        out_shape=jax.ShapeDtypeStruct((s, dff), BF16),
        scratch_shapes=[pltpu.VMEM((ts + FFN_HALO, tc), F32)],
        compiler_params=_params(("parallel", "parallel")))(up, up, up, fw, fb)


def _ffn_bwd_act(dact, up, fw, fb, carry=None):
    s = up.shape[0]
    dff = up.shape[1] // 2
    tc = _ffn_tile(dff)
    nj = dff // tc
    ts = _row_tile(s)
    hb = ts // FFN_HALO

    def body(d_ref, g_ref, v_ref, hg_ref, w_ref, b_ref, dg2_ref, dval_ref, gw_ref, gb_ref, pad_ref):
        i = pl.program_id(1)
        pad_ref[0:FFN_HALO, :] = jnp.where(i > 0, hg_ref[...].astype(F32), 0.0)
        pad_ref[FFN_HALO:, :] = g_ref[...].astype(F32)

        @pl.when(i == 0)
        def _():
            gw_ref[...] = jnp.zeros_like(gw_ref)
            gb_ref[...] = jnp.zeros_like(gb_ref)

        fold = lambda v: jnp.sum(v.reshape(FFN_ROWS // SUBLANES, SUBLANES, FFN_COLS), axis=0)
        first = FFN_HALO - (FFN_K - 1)
        sums = {}
        for r0, cs in _ffn_chunks(ts, tc):
            rows = slice(r0, r0 + FFN_ROWS)
            shifted = [pad_ref[first + k + r0:first + k + r0 + FFN_ROWS, cs] for k in range(FFN_K)]
            g2 = b_ref[:, cs] + w_ref[0:1, cs] * shifted[0]
            for k in range(1, FFN_K):
                g2 = g2 + w_ref[k:k + 1, cs] * shifted[k]
            gelu, dgelu = _gelu_parts(g2)
            dactv = d_ref[rows, cs]
            dval_ref[rows, cs] = (dactv * gelu).astype(BF16)
            dg2 = dactv * v_ref[rows, cs].astype(F32) * dgelu
            dg2_ref[rows, cs] = dg2.astype(BF16)
            parts = [fold(dg2)] + [fold(dg2 * shifted[k]) for k in range(FFN_K)]
            sums = {n: part + sums[n] if r0 else part for n, part in enumerate(parts)}
            if r0 + FFN_ROWS == ts:
                gb_ref[:, cs] += jnp.sum(sums[0], axis=0, keepdims=True)
                for k in range(FFN_K):
                    gw_ref[k:k + 1, cs] += jnp.sum(sums[1 + k], axis=0, keepdims=True)

    blk = pl.BlockSpec((ts, tc), lambda j, i: (i, j))
    wspec = pl.BlockSpec((FFN_HALO, tc), lambda j, i: (0, j))
    bspec = pl.BlockSpec((1, tc), lambda j, i: (0, j))
    return _call(
        body, name="ffn_bwd_act", grid=(nj, s // ts),
        in_specs=[blk, blk, pl.BlockSpec((ts, tc), lambda j, i: (i, j + nj)),
                  pl.BlockSpec((FFN_HALO, tc), lambda j, i: (jnp.maximum(i * hb - 1, 0), j)), wspec, bspec],
        out_specs=[blk, pl.BlockSpec((ts, tc), lambda j, i: (i, j + nj)), wspec, bspec],
        out_shape=[jax.ShapeDtypeStruct((s, dff), BF16), jax.ShapeDtypeStruct((s, 2 * dff), BF16),
                   jax.ShapeDtypeStruct((FFN_HALO, dff), F32), jax.ShapeDtypeStruct((1, dff), F32)],
        scratch_shapes=[pltpu.VMEM((ts + FFN_HALO, tc), F32)],
        args=(dact, up, up, up, fw, fb), semantics=("parallel", "arbitrary"), carry=carry)


def _ffn_bwd_conv(dg2, fw, dup):
    s, dff = dg2.shape
    tc = _ffn_tile(dff)
    ts = _row_tile(s)
    hb = ts // FFN_HALO
    last = s // FFN_HALO - 1
    nsteps = s // ts

    def body(d_ref, hd_ref, w_ref, dup_ref, out_ref, pad_ref):
        i = pl.program_id(0)
        pad_ref[0:ts, :] = d_ref[...].astype(F32)
        pad_ref[ts:, :] = jnp.where(i < nsteps - 1, hd_ref[...].astype(F32), 0.0)
        for r0, cs in _ffn_chunks(ts, tc):
            dg = w_ref[0:1, cs] * pad_ref[r0 + FFN_K - 1:r0 + FFN_K - 1 + FFN_ROWS, cs]
            for k in range(1, FFN_K):
                dg = dg + w_ref[k:k + 1, cs] * pad_ref[r0 + FFN_K - 1 - k:r0 + FFN_K - 1 - k + FFN_ROWS, cs]
            out_ref[r0:r0 + FFN_ROWS, cs] = dg.astype(BF16)

    blk = pl.BlockSpec((ts, tc), lambda i, j: (i, j))
    return pl.pallas_call(
        body, name="ffn_bwd_conv", grid=(nsteps, dff // tc),
        in_specs=[blk, pl.BlockSpec((FFN_HALO, tc), lambda i, j: (jnp.minimum((i + 1) * hb, last), j)),
                  pl.BlockSpec((FFN_HALO, tc), lambda i, j: (0, j)), ANY],
        out_specs=blk, out_shape=jax.ShapeDtypeStruct(dup.shape, BF16), input_output_aliases={3: 0},
        scratch_shapes=[pltpu.VMEM((ts + FFN_HALO, tc), F32)],
        compiler_params=_params(("parallel", "parallel")))(dg2, dg2, fw, dup)


def _ple_loss(h2, zg, pp, target):
    s, d = h2.shape
    ts = _row_tile(s)

    def body(h_ref, z_ref, p_ref, t_ref, dh_ref, dpp_ref, dz_ref, loss_ref):
        pg = _sigmoid(z_ref[...])
        ppv = p_ref[...]
        diff = h_ref[...] + pg * ppv - t_ref[...]
        dh = diff * (1.0 / d)
        dh_ref[...] = dh
        dpp_ref[...] = (dh * pg).astype(BF16)
        dz_ref[...] = (dh * ppv * pg * (1.0 - pg)).astype(BF16)
        part = jnp.sum(jnp.sum(diff * diff, axis=0, keepdims=True), axis=1, keepdims=True)

        @pl.when(pl.program_id(0) == 0)
        def _():
            loss_ref[...] = jnp.zeros_like(loss_ref)

        loss_ref[...] += jnp.broadcast_to(part, loss_ref.shape)

    row = pl.BlockSpec((ts, d), lambda i: (i, 0))
    return pl.pallas_call(
        body, name="ple_loss", grid=(s // ts,), in_specs=[row, row, row, row],
        out_specs=[row, row, row, pl.BlockSpec((8, LANES), lambda i: (0, 0))],
        out_shape=[jax.ShapeDtypeStruct((s, d), F32), jax.ShapeDtypeStruct((s, d), BF16),
                   jax.ShapeDtypeStruct((s, d), BF16), jax.ShapeDtypeStruct((8, LANES), F32)],
        compiler_params=_params(("arbitrary",)))(h2, zg, pp, target)


def _local_step(x, p, target, w, shards=None, unpack_filters=None):
    riding = shards is not None
    major = lambda n, g: g if n in COL_SHARDED else g.reshape(N_CHIPS, -1, g.shape[-1])
    full = lambda n, g: g if n in COL_SHARDED else g.reshape(-1, g.shape[-1])
    first = lambda res: (res[0], res[1:]) if riding else (res, [])

    def ride(names, arrays):
        return ([arrays[n] for n in names], False) if riding else None

    def leave(grads):
        return (grads, True) if riding else None

    def land(names, got):
        return {n: full(n, g) for n, g in zip(names, got)}
    s = x.shape[0]
    t = min(256, s)
    tri = jnp.tril(jnp.ones((t, t), F32))
    tri_incl = tri.astype(BF16)
    tri_excl = jnp.tril(jnp.ones((t, t), F32), -1).astype(BF16)
    bd = jnp.kron(jnp.eye(N_HEADS, dtype=F32), jnp.ones((HEAD_DIM, HEAD_DIM), F32)).astype(BF16)
    qg = jnp.tile(w["q_gain"], (1, N_HEADS))
    kg = jnp.tile(w["k_gain"], (1, N_HEADS))
    pb = p.astype(BF16)

    u1, got = first(_rms_fwd(x, w["g_mix"], "rms_mix", carry=ride(["w_in", "filters"], shards)))
    if riding:
        w = {**w, "w_in": got[0], **unpack_filters(got[1])}
    dw_w = jnp.pad(w["dw_w"], ((0, CONV_HALO - CONV_K), (0, 0)))
    fw = jnp.pad(w["ffn_conv_w"], ((0, FFN_HALO - FFN_K), (0, 0)))
    proj = _mm_nn_sharded(u1, w["w_in"], "mm_in")
    qs, kh, vb = _qkv_prep(proj, qg, kg, bd)
    with_attn, with_conv = ["w_out", "w_up", "w_ple_gate", "w_ple_proj"], ["w_down"]
    o, ob, *got = _attn_fwd(qs, kh, vb, tri_excl, carry=ride(with_attn, shards))
    w = {**w, **land(with_attn, got)}
    c1, c3, *got = _conv_fwd(proj, dw_w, w["dw_b"], w["conv_ln_g"], w["conv_ln_b"], carry=ride(with_conv, shards))
    w = {**w, **land(with_conv, got)}
    mix = jnp.concatenate([ob, c3], axis=1)
    h1 = _mm_nn_full(mix, w["w_out"], "mm_out", res=x)
    u2 = _rms_fwd(h1, w["g_ffn"], "rms_ffn")
    up = _mm_nn_sharded(u2, w["w_up"], "mm_up", out_dtype=BF16)
    act = _ffn_act(up, fw, w["ffn_conv_b"])
    h2 = _mm_nn_full(act, w["w_down"], "mm_down", res=h1)
    u3 = _rms_fwd(h2, w["g_ple"], "rms_ple")
    zg = _mm_nn_full(u3, w["w_ple_gate"], "mm_ple_gate")
    pp = _mm_nn_sharded(pb, w["w_ple_proj"], "mm_ple_proj")
    dh3, dpp, dz, sq = _ple_loss(h2, zg, pp, target)

    big = {}
    small = {}
    big["w_ple_proj"] = _mm_tn_sharded(pb, dpp, N_CHIPS, "mm_g_ple_proj")
    big["w_ple_gate"] = _mm_tn_full(u3, dz, "mm_g_ple_gate", tm=u3.shape[1])
    du3 = _mm_nt_full(dz, w["w_ple_gate"], "mm_d_ple_gate")
    dh2, dh2b, small["g_ple"] = _rms_bwd(h2, du3, w["g_ple"], dh3, "rms_ple_bwd")
    slots = {}
    with_down, with_ffn, with_attn = ["w_ple_proj", "w_ple_gate"], ["w_down"], ["w_up", "w_out"]
    leaving = leave([major(n, big.pop(n)) for n in with_down]) if riding else None
    big["w_down"], got = first(_mm_tn_full(act, dh2b, "mm_g_down", tm=act.shape[1] // 2, carry=leaving))
    slots.update(zip(with_down, got))
    dact = _mm_nt_full(dh2b, w["w_down"], "mm_d_down")
    leaving = leave([major(n, big.pop(n)) for n in with_ffn]) if riding else None
    dg2, dup, gfw, small["ffn_conv_b"], *got = _ffn_bwd_act(dact, up, fw, w["ffn_conv_b"], carry=leaving)
    slots.update(zip(with_ffn, got))
    small["ffn_conv_w"] = gfw[:FFN_K]
    dup = _ffn_bwd_conv(dg2, fw, dup)
    big["w_up"] = _mm_tn_sharded(u2, dup, N_CHIPS, "mm_g_up", n_split=2, tk=_contract_tile(s) // 2)
    du2, = _mm_nt_sharded(dup, w["w_up"], "mm_d_up")
    dh1, dh1b, small["g_ffn"] = _rms_bwd(h1, du2, w["g_ffn"], dh2, "rms_ffn_bwd")
    big["w_out"] = _mm_tn_full(mix, dh1b, "mm_g_out", tm=mix.shape[1])
    dmix = _mm_nt_full(dh1b, w["w_out"], "mm_d_out")
    dc1, small["conv_ln_g"], small["conv_ln_b"], small["dw_b"] = _conv_bwd_ln(dmix, c1, w["conv_ln_g"], w["conv_ln_b"])
    dcacg, gdw = _conv_bwd_taps(proj, dc1, dw_w)
    small["dw_w"] = gdw[:CONV_K]
    leaving = leave([major(n, big.pop(n)) for n in with_attn]) if riding else None
    dqh, dkh, dv, *got = _attn_bwd(qs, kh, vb, o, dmix, tri_excl, tri_incl, carry=leaving)
    slots.update(zip(with_attn, got))
    dqkv, gq, gk = _qk_bwd(proj, dqh, dkh, dv, qg, kg, bd)
    small["q_gain"] = gq.reshape(N_HEADS, HEAD_DIM).sum(axis=0, keepdims=True)
    small["k_gain"] = gk.reshape(N_HEADS, HEAD_DIM).sum(axis=0, keepdims=True)
    dproj = jnp.concatenate([dqkv, dcacg], axis=1)
    big["w_in"] = _mm_tn_sharded(u1, dproj, N_CHIPS, "mm_g_in")
    half = big["w_in"].shape[1] // 2
    halves = [big["w_in"][:, :half], big.pop("w_in")[:, half:]] if riding else [None, None]
    du1, *got_a = _mm_nt_sharded(dproj, w["w_in"], "mm_d_in", carry=leave([halves[0]]) if riding else None)
    grad_x, _, small["g_mix"], *got_b = _rms_bwd(x, du1, w["g_mix"], dh1, "rms_mix_bwd",
                                                 carry=leave([halves[1]]) if riding else None)
    if riding:
        slots["w_in"] = jnp.concatenate([got_a[0], got_b[0]], axis=1)
    return sq[0, 0], grad_x, big, slots, small


def _exchange(srcs, per_chip, name):
    n = len(srcs)

    def body(*refs):
        src_refs, land_refs = refs[:n], refs[n:2 * n]
        send_sems, recv_sems, local_sems = refs[2 * n:]
        local = _local_copies(src_refs, land_refs, local_sems, per_chip)
        sends = _chip_copies(src_refs, land_refs, send_sems, recv_sems, per_chip, False)
        for cp in local + sends:
            cp.start()
        for cp in _chip_copies(src_refs, land_refs, send_sems, recv_sems, per_chip, True):
            cp.wait_recv()
        for cp in sends:
            cp.wait_send()
        for cp in local:
            cp.wait()

    return pl.pallas_call(
        body, name=name, in_specs=[ANY] * n, out_specs=[ANY] * n, out_shape=_exchanged_shapes(srcs, per_chip),
        scratch_shapes=_exchange_sems(n))(*srcs)


def _finish_exchange(mine, small):
    n = len(mine)

    def body(*refs):
        gin, sin = refs[:n], refs[n]
        gout, sout = refs[n + 1:2 * n + 1], refs[2 * n + 1]
        send_sems, recv_sems, small_send, small_recv, local_sem = refs[2 * n + 2:]
        x, y, c, _ = _position()
        dev = 4 * x + 2 * y + c
        flip = lambda v, bit: 1 - v if bit else v
        others = [(flip(x, k & 4), flip(y, k & 2), flip(c, k & 1)) for k in range(1, N_DEV)]
        local = pltpu.make_async_copy(sin, sout.at[dev], local_sem)
        local.start()
        swaps = [_remote(gin[a], gout[a], send_sems, recv_sems, a, (x, y, 1 - c)) for a in range(n)]
        sends = swaps + [_remote(sin, sout.at[dev], small_send, small_recv, k, peer) for k, peer in enumerate(others)]
        for cp in sends:
            cp.start()
        for cp in swaps:
            cp.wait_recv()
        for k, (px, py, pc) in enumerate(others):
            _remote(sin, sout.at[4 * px + 2 * py + pc], small_send, small_recv, k, (px, py, pc)).wait_recv()
        for cp in sends:
            cp.wait_send()
        local.wait()

    return pl.pallas_call(
        body, name="finish_exchange", in_specs=[ANY] * (n + 1), out_specs=[ANY] * (n + 1),
        out_shape=[jax.ShapeDtypeStruct(g.shape, g.dtype) for g in mine]
        + [jax.ShapeDtypeStruct((N_DEV,) + small.shape, small.dtype)],
        scratch_shapes=[pltpu.SemaphoreType.DMA((n,)), pltpu.SemaphoreType.DMA((n,)),
                        pltpu.SemaphoreType.DMA((N_DEV - 1,)), pltpu.SemaphoreType.DMA((N_DEV - 1,)),
                        pltpu.SemaphoreType.DMA])(*mine, small)


def _elem_tile(rows):
    return 128 if rows % 128 == 0 else (64 if rows % 64 == 0 else rows)


def _sum_slots(a, name):
    g, r, c = a.shape
    tr = _elem_tile(r)

    def body(a_ref, o_ref):
        acc = a_ref[0]
        for k in range(1, g):
            acc = acc + a_ref[k]
        o_ref[...] = acc

    return pl.pallas_call(
        body, name=name, grid=(r // tr,), in_specs=[pl.BlockSpec((g, tr, c), lambda i: (0, i, 0))],
        out_specs=pl.BlockSpec((tr, c), lambda i: (i, 0)), out_shape=jax.ShapeDtypeStruct((r, c), a.dtype),
        compiler_params=_params(("parallel",)))(a)


def _adamw(wt, ga, gb, m, v, name):
    r, c = wt.shape
    tr = _elem_tile(r)
    two = gb is not None

    def body(*refs):
        if two:
            w_ref, ga_ref, gb_ref, m_ref, v_ref, g_out, d_out, m_out, v_out = refs
            g = ga_ref[...] + gb_ref[...]
        else:
            w_ref, ga_ref, m_ref, v_ref, g_out, d_out, m_out, v_out = refs
            g = ga_ref[...]
        mn = ADAM_B1 * m_ref[...] + (1.0 - ADAM_B1) * g
        vn = ADAM_B2 * v_ref[...] + (1.0 - ADAM_B2) * (g * g)
        m_hat = mn / (1.0 - ADAM_B1 ** ADAM_STEP)
        v_hat = vn / (1.0 - ADAM_B2 ** ADAM_STEP)
        g_out[...] = g
        d_out[...] = -ADAM_LR * (m_hat / (jnp.sqrt(v_hat) + ADAM_EPS) + ADAM_WD * w_ref[...])
        m_out[...] = mn
        v_out[...] = vn

    blk = pl.BlockSpec((tr, c), lambda i: (i, 0))
    args = [wt, ga] + ([gb] if two else []) + [m, v]
    sds = jax.ShapeDtypeStruct((r, c), F32)
    return pl.pallas_call(
        body, name=name, grid=(r // tr,), in_specs=[blk] * len(args), out_specs=[blk] * 4, out_shape=[sds] * 4,
        compiler_params=_params(("parallel",)))(*args)


def _pack(arrs, rows):
    flat = jnp.concatenate([a.reshape(-1) for a in arrs])
    return jnp.pad(flat, (0, rows * LANES - flat.shape[0])).reshape(rows, LANES)


def _unpack(buf, shapes):
    flat = buf.reshape(-1)
    out, off = [], 0
    for shp in shapes:
        size = 1
        for d in shp:
            size *= d
        out.append(flat[off:off + size].reshape(shp))
        off += size
    return out


BIG = ["w_in", "w_out", "w_up", "w_down", "w_ple_gate", "w_ple_proj"]
COL_SHARDED = ["w_in", "w_up", "w_ple_proj"]
SMALL_REPL = ["g_mix", "q_gain", "k_gain", "dw_b", "conv_ln_g", "conv_ln_b", "g_ffn", "ffn_conv_b", "g_ple"]
SMALL_SHARDED = ["dw_w", "ffn_conv_w"]
WEIGHTS = ["g_mix", "w_in", "q_gain", "k_gain", "dw_w", "dw_b", "conv_ln_g", "conv_ln_b", "w_out", "g_ffn", "w_up",
           "ffn_conv_w", "ffn_conv_b", "w_down", "g_ple", "w_ple_gate", "w_ple_proj"]


def _rows_for(n_elems):
    return -(-n_elems // (8 * LANES)) * 8


def kernel(x, p, g_mix, w_in, q_gain, k_gain, dw_w, dw_b, conv_ln_g, conv_ln_b, w_out, g_ffn, w_up, ffn_conv_w, ffn_conv_b, w_down, g_ple, w_ple_gate, w_ple_proj, loss_target, m_g_mix, m_w_in, m_q_gain, m_k_gain, m_dw_w, m_dw_b, m_conv_ln_g, m_conv_ln_b, m_w_out, m_g_ffn, m_w_up, m_ffn_conv_w, m_ffn_conv_b, m_w_down, m_g_ple, m_w_ple_gate, m_w_ple_proj, v_g_mix, v_w_in, v_q_gain, v_k_gain, v_dw_w, v_dw_b, v_conv_ln_g, v_conv_ln_b, v_w_out, v_g_ffn, v_w_up, v_ffn_conv_w, v_ffn_conv_b, v_w_down, v_g_ple, v_w_ple_gate, v_w_ple_proj):
    given = dict(locals())
    strip = lambda n, a: a if n in SMALL_REPL else a[0]
    wts = {n: strip(n, given[n]) for n in WEIGHTS}
    mom = {n: strip(n, given["m_" + n]) for n in WEIGHTS}
    var = {n: strip(n, given["v_" + n]) for n in WEIGHTS}
    chip = 2 * lax.axis_index("x") + lax.axis_index("y")

    small_shard_shapes = [wts[n].shape for n in SMALL_SHARDED]
    filt_rows = _rows_for(sum(wts[n].size for n in SMALL_SHARDED))
    shards = {n: wts[n].astype(BF16) for n in BIG}
    shards["filters"] = _pack([wts[n] for n in SMALL_SHARDED], filt_rows)

    def unpack_filters(filt_all):
        per_chip = [_unpack(filt_all[k], small_shard_shapes) for k in range(N_CHIPS)]
        return {n: jnp.concatenate([per_chip[k][idx] for k in range(N_CHIPS)], axis=1)
                for idx, n in enumerate(SMALL_SHARDED)}

    sq, grad_x, big, slots, small = _local_step(x[0], p[0, 0], loss_target[0], {n: wts[n] for n in SMALL_REPL},
                                                shards, unpack_filters)
    loss = lax.psum(sq * (0.5 / x.shape[-1]), ("x", "y", "c"))

    small_names = SMALL_REPL + SMALL_SHARDED
    small_shapes = [small[n].shape for n in small_names]
    small_rows = _rows_for(sum(small[n].size for n in small_names))
    mine = [_sum_slots(slots[n], "sum_" + n) for n in BIG]
    *theirs, small_slots = _finish_exchange(mine, _pack([small[n] for n in small_names], small_rows))
    small_sum = dict(zip(small_names, _unpack(_sum_slots(small_slots, "sum_small"), small_shapes)))

    outs = {}
    for n, ga, gb in zip(BIG, mine, theirs):
        outs[n] = _adamw(wts[n], ga, gb, mom[n], var[n], "adamw_" + n)
    for n in SMALL_SHARDED:
        width = wts[n].shape[1]
        small_sum[n] = lax.dynamic_slice_in_dim(small_sum[n], chip * width, width, axis=1)
    local_shapes = [wts[n].shape for n in small_names]
    local_rows = _rows_for(sum(wts[n].size for n in small_names))
    packed = _adamw(_pack([wts[n] for n in small_names], local_rows), _pack([small_sum[n] for n in small_names], local_rows),
                    None, _pack([mom[n] for n in small_names], local_rows),
                    _pack([var[n] for n in small_names], local_rows), "adamw_small")
    unpacked = [_unpack(buf, local_shapes) for buf in packed]
    for idx, n in enumerate(small_names):
        outs[n] = [u[idx] for u in unpacked]
    result = [loss, grad_x[None]]
    for part in range(4):
        result += [outs[n][part] if n in SMALL_REPL else outs[n][part][None] for n in WEIGHTS]
    return tuple(result)
```

```python
import functools

import jax
import jax.numpy as jnp
from jax import lax
from jax.experimental import pallas as pl
from jax.experimental.pallas import tpu as pltpu

F32 = jnp.float32
BF16 = jnp.bfloat16
HIGHEST = lax.Precision.HIGHEST
MESH = pl.DeviceIdType.MESH
ANY = pl.BlockSpec(memory_space=pl.ANY)

EPS = 1e-6
HEAD_DIM = 64
N_HEADS = 8
ATTN_W = 512
CONV_W = 512
CONV_K = 31
FFN_K = 3
ATTN_SCALE = 0.125
LANES = 128
CONV_HALO = 32
FFN_HALO = 16
VMEM_LIMIT = 56 * 1024 * 1024

ADAM_LR = 0.001
ADAM_B1 = 0.9
ADAM_B2 = 0.999
ADAM_EPS = 1e-08
ADAM_WD = 0.01
ADAM_STEP = 10

N_CHIPS = 4
N_DEV = 8


def _params(sem):
    return pltpu.CompilerParams(dimension_semantics=sem, vmem_limit_bytes=VMEM_LIMIT)


def _row_tile(s):
    return min(512, s)


def _position():
    x, y, c = lax.axis_index("x"), lax.axis_index("y"), lax.axis_index("c")
    return x, y, c, [(1 - x, y), (x, 1 - y), (1 - x, 1 - y)]


def _remote(src, dst, send_sems, recv_sems, k, to):
    return pltpu.make_async_remote_copy(src_ref=src, dst_ref=dst, send_sem=send_sems.at[k], recv_sem=recv_sems.at[k],
                                        device_id=to, device_id_type=MESH)


def _chip_copies(src_refs, land_refs, send_sems, recv_sems, per_chip, landed):
    x, y, c, chips = _position()
    me = 2 * x + y
    out = []
    for a, (src, land) in enumerate(zip(src_refs, land_refs)):
        for j, (px, py) in enumerate(chips):
            peer = 2 * px + py
            out.append(_remote(src.at[peer] if per_chip else src, land.at[peer if landed else me],
                               send_sems, recv_sems, 3 * a + j, (px, py, c)))
    return out


def _local_copies(src_refs, land_refs, local_sems, per_chip):
    x, y, _, _ = _position()
    me = 2 * x + y
    return [pltpu.make_async_copy(src.at[me] if per_chip else src, land.at[me], local_sems.at[a])
            for a, (src, land) in enumerate(zip(src_refs, land_refs))]


def _exchanged_shapes(srcs, per_chip):
    return [jax.ShapeDtypeStruct(a.shape if per_chip else (N_CHIPS,) + a.shape, a.dtype) for a in srcs]


def _exchange_sems(n):
    return [pltpu.SemaphoreType.DMA((3 * n,)), pltpu.SemaphoreType.DMA((3 * n,)), pltpu.SemaphoreType.DMA((n,))]


def _call(body, *, name, grid, in_specs, out_specs, out_shape, scratch_shapes, args, semantics, carry=None):
    if carry is None:
        return pl.pallas_call(body, name=name, grid=grid, in_specs=in_specs, out_specs=out_specs, out_shape=out_shape,
                              scratch_shapes=scratch_shapes, compiler_params=_params(semantics))(*args)
    srcs, per_chip = carry
    n, n_in, n_out, n_scr = len(srcs), len(in_specs), len(out_specs), len(scratch_shapes)

    def wrapped(*refs):
        ins, xin = refs[:n_in], refs[n_in:n_in + n]
        outs, xout = refs[n_in + n:n_in + n + n_out], refs[n_in + n + n_out:n_in + 2 * n + n_out]
        scratch = refs[n_in + 2 * n + n_out:n_in + 2 * n + n_out + n_scr]
        send_sems, recv_sems, local_sems = refs[-3:]
        first = functools.reduce(jnp.logical_and, [pl.program_id(d) == 0 for d in range(len(grid))])
        last = functools.reduce(jnp.logical_and, [pl.program_id(d) == g - 1 for d, g in enumerate(grid)])

        @pl.when(first)
        def _():
            for cp in _local_copies(xin, xout, local_sems, per_chip):
                cp.start()
            for cp in _chip_copies(xin, xout, send_sems, recv_sems, per_chip, False):
                cp.start()

        body(*ins, *outs, *scratch)

        @pl.when(last)
        def _():
            for cp in _chip_copies(xin, xout, send_sems, recv_sems, per_chip, True):
                cp.wait_recv()
            for cp in _chip_copies(xin, xout, send_sems, recv_sems, per_chip, False):
                cp.wait_send()
            for cp in _local_copies(xin, xout, local_sems, per_chip):
                cp.wait()

    return pl.pallas_call(
        wrapped, name=name, grid=grid, in_specs=list(in_specs) + [ANY] * n, out_specs=list(out_specs) + [ANY] * n,
        out_shape=list(out_shape) + _exchanged_shapes(srcs, per_chip),
        scratch_shapes=list(scratch_shapes) + _exchange_sems(n),
        compiler_params=_params(("arbitrary",) * len(grid)))(*args, *srcs)


def _contract_tile(s):
    return min(2048, s)


def _rstd(x):
    return lax.rsqrt(jnp.mean(x * x, axis=-1, keepdims=True) + EPS)


def _sigmoid(x):
    return 1.0 / (1.0 + jnp.exp(-x))


def _mm(a, b, *, name, dims, grid, a_spec, b_spec, o_spec, o_tile, out_shape, res=None, res_spec=None, carry=None):
    nk = grid[2]

    def body(*refs):
        if res is None:
            a_ref, b_ref, o_ref, acc_ref = refs
            r_ref = None
        else:
            a_ref, b_ref, r_ref, o_ref, acc_ref = refs
        part = lax.dot_general(a_ref[...], b_ref[...], (dims, ((), ())), preferred_element_type=F32)

        def finish(val):
            if r_ref is not None:
                val = val + r_ref[...]
            o_ref[...] = val.astype(o_ref.dtype)

        if nk == 1:
            finish(part)
        else:
            k = pl.program_id(2)

            @pl.when(k == 0)
            def _():
                acc_ref[...] = part

            @pl.when(k > 0)
            def _():
                acc_ref[...] += part

            @pl.when(k == nk - 1)
            def _():
                finish(acc_ref[...])

    in_specs = [a_spec, b_spec]
    args = [a, b]
    if res is not None:
        in_specs.append(res_spec)
        args.append(res)
    acc_tile = o_tile if nk > 1 else (8, LANES)
    out = _call(body, name=name, grid=grid, in_specs=in_specs, out_specs=[o_spec], out_shape=[out_shape],
                scratch_shapes=[pltpu.VMEM(acc_tile, F32)], args=args,
                semantics=("parallel", "parallel", "arbitrary"), carry=carry)
    return out[0] if carry is None else out


NN = ((1,), (0,))
NT = ((1,), (1,))
TN = ((0,), (0,))


def _mm_nn_sharded(a, bg, name, out_dtype=F32, tm=None):
    s, k = a.shape
    g, _, ns = bg.shape
    tm = tm or _row_tile(s)

    def body(a_ref, b_ref, o_ref):
        av = a_ref[...]
        for gi in range(g):
            o_ref[:, gi * ns:(gi + 1) * ns] = jnp.dot(av, b_ref[gi], preferred_element_type=F32).astype(out_dtype)

    return pl.pallas_call(
        body, name=name, grid=(s // tm,),
        in_specs=[pl.BlockSpec((tm, k), lambda i: (i, 0)), pl.BlockSpec((g, k, ns), lambda i: (0, 0, 0))],
        out_specs=pl.BlockSpec((tm, g * ns), lambda i: (i, 0)),
        out_shape=jax.ShapeDtypeStruct((s, g * ns), out_dtype), compiler_params=_params(("parallel",)))(a, bg)


def _mm_nn_full(a, b, name, res=None):
    s, k = a.shape
    n = b.shape[1]
    tm = _row_tile(s)
    rs = pl.BlockSpec((tm, n), lambda i, j, kk: (i, 0))
    return _mm(a, b, name=name, dims=NN, grid=(s // tm, 1, 1),
               a_spec=pl.BlockSpec((tm, k), lambda i, j, kk: (i, 0)),
               b_spec=pl.BlockSpec((k, n), lambda i, j, kk: (0, 0)),
               o_spec=rs, o_tile=(tm, n), out_shape=jax.ShapeDtypeStruct((s, n), F32),
               res=res, res_spec=rs if res is not None else None)


def _mm_nt_full(a, b, name):
    s, n = a.shape
    k = b.shape[0]
    tm = _row_tile(s)
    return _mm(a, b, name=name, dims=NT, grid=(s // tm, 1, 1),
               a_spec=pl.BlockSpec((tm, n), lambda i, j, kk: (i, 0)),
               b_spec=pl.BlockSpec((k, n), lambda i, j, kk: (0, 0)),
               o_spec=pl.BlockSpec((tm, k), lambda i, j, kk: (i, 0)), o_tile=(tm, k),
               out_shape=jax.ShapeDtypeStruct((s, k), F32))


def _mm_nt_sharded(a, bg, name, carry=None):
    s = a.shape[0]
    g, k, ns = bg.shape
    tm = _row_tile(s)

    def body(a_ref, b_ref, o_ref):
        acc = lax.dot_general(a_ref[:, 0:ns], b_ref[0], (NT, ((), ())), preferred_element_type=F32)
        for gi in range(1, g):
            acc = acc + lax.dot_general(a_ref[:, gi * ns:(gi + 1) * ns], b_ref[gi], (NT, ((), ())),
                                        preferred_element_type=F32)
        o_ref[...] = acc

    return _call(
        body, name=name, grid=(s // tm,),
        in_specs=[pl.BlockSpec((tm, g * ns), lambda i: (i, 0)), pl.BlockSpec((g, k, ns), lambda i: (0, 0, 0))],
        out_specs=[pl.BlockSpec((tm, k), lambda i: (i, 0))], out_shape=[jax.ShapeDtypeStruct((s, k), F32)],
        scratch_shapes=[], args=(a, bg), semantics=("parallel",), carry=carry)


def _mm_tn_sharded(a, b, g, name, n_split=1, tk=None):
    s, k = a.shape
    ns = b.shape[1] // g
    gs = g // n_split
    tk = tk or _contract_tile(s)

    def body(a_ref, b_ref, o_ref):
        first = pl.program_id(1) == 0
        at = a_ref[...].T
        for gi in range(gs):
            part = jnp.dot(at, b_ref[:, gi * ns:(gi + 1) * ns], preferred_element_type=F32)

            @pl.when(first)
            def _(gi=gi, part=part):
                o_ref[gi] = part

            @pl.when(jnp.logical_not(first))
            def _(gi=gi, part=part):
                o_ref[gi] += part

    return pl.pallas_call(
        body, name=name, grid=(n_split, s // tk),
        in_specs=[pl.BlockSpec((tk, k), lambda j, kk: (kk, 0)), pl.BlockSpec((tk, gs * ns), lambda j, kk: (kk, j))],
        out_specs=pl.BlockSpec((gs, k, ns), lambda j, kk: (j, 0, 0)),
        out_shape=jax.ShapeDtypeStruct((g, k, ns), F32), compiler_params=_params(("parallel", "arbitrary")))(a, b)


def _mm_tn_full(a, b, name, tm, carry=None):
    s, m = a.shape
    n = b.shape[1]
    tk = _contract_tile(s)
    return _mm(a, b, name=name, dims=TN, grid=(m // tm, 1, s // tk),
               a_spec=pl.BlockSpec((tk, tm), lambda i, j, kk: (kk, i)),
               b_spec=pl.BlockSpec((tk, n), lambda i, j, kk: (kk, 0)),
               o_spec=pl.BlockSpec((tm, n), lambda i, j, kk: (i, 0)), o_tile=(tm, n),
               out_shape=jax.ShapeDtypeStruct((m, n), F32), carry=carry)


def _rms_fwd(x, g, name, carry=None):
    s, d = x.shape
    ts = _row_tile(s)

    def body(x_ref, g_ref, u_ref):
        xv = x_ref[...]
        u_ref[...] = (xv * _rstd(xv) * g_ref[...]).astype(BF16)

    row = pl.BlockSpec((ts, d), lambda i: (i, 0))
    out = _call(body, name=name, grid=(s // ts,), in_specs=[row, pl.BlockSpec((1, d), lambda i: (0, 0))],
                out_specs=[row], out_shape=[jax.ShapeDtypeStruct((s, d), BF16)], scratch_shapes=[], args=(x, g),
                semantics=("parallel",), carry=carry)
    return out[0] if carry is None else out


def _rms_bwd(h, du, g, dh_in, name, carry=None):
    s, d = h.shape
    ts = _row_tile(s)

    def body(h_ref, du_ref, g_ref, dhin_ref, dh_ref, dhb_ref, gg_ref):
        hv = h_ref[...]
        r = _rstd(hv)
        xhat = hv * r
        duv = du_ref[...]
        dxhat = duv * g_ref[...]
        m = jnp.mean(dxhat * xhat, axis=-1, keepdims=True)
        dh = dhin_ref[...] + r * (dxhat - xhat * m)
        dh_ref[...] = dh
        dhb_ref[...] = dh.astype(BF16)
        part = jnp.sum(duv * xhat, axis=0, keepdims=True)

        @pl.when(pl.program_id(0) == 0)
        def _():
            gg_ref[...] = part

        @pl.when(pl.program_id(0) > 0)
        def _():
            gg_ref[...] += part

    row = pl.BlockSpec((ts, d), lambda i: (i, 0))
    vec = pl.BlockSpec((1, d), lambda i: (0, 0))
    return _call(
        body, name=name, grid=(s // ts,), in_specs=[row, row, vec, row], out_specs=[row, row, vec],
        out_shape=[jax.ShapeDtypeStruct((s, d), F32), jax.ShapeDtypeStruct((s, d), BF16),
                   jax.ShapeDtypeStruct((1, d), F32)],
        scratch_shapes=[], args=(h, du, g, dh_in), semantics=("arbitrary",), carry=carry)


def _head_sum(x, bd):
    return _tri_dot(_split(x), bd)


def _qkv_prep(proj, qg, kg, bd):
    s = proj.shape[0]
    ts = _row_tile(s)

    def body(q_ref, k_ref, v_ref, qg_ref, kg_ref, bd_ref, qs_ref, kh_ref, vb_ref):
        def norm(x, gain):
            ms = _head_sum(x * x, bd_ref[...]) * (1.0 / HEAD_DIM)
            return x * lax.rsqrt(ms + EPS) * gain

        qs_ref[...] = (norm(q_ref[...], qg_ref[...]) * ATTN_SCALE).astype(BF16)
        kh_ref[...] = norm(k_ref[...], kg_ref[...]).astype(BF16)
        vb_ref[...] = v_ref[...].astype(BF16)

    col = lambda c: pl.BlockSpec((ts, ATTN_W), lambda i: (i, c))
    vec = pl.BlockSpec((1, ATTN_W), lambda i: (0, 0))
    out = pl.BlockSpec((ts, ATTN_W), lambda i: (i, 0))
    sds = jax.ShapeDtypeStruct((s, ATTN_W), BF16)
    return pl.pallas_call(
        body, name="qkv_prep", grid=(s // ts,),
        in_specs=[col(0), col(1), col(2), vec, vec, pl.BlockSpec((ATTN_W, ATTN_W), lambda i: (0, 0))],
        out_specs=[out, out, out], out_shape=[sds, sds, sds],
        compiler_params=_params(("parallel",)))(proj, proj, proj, qg, kg, bd)


def _qk_bwd(proj, dqh, dkh, dv, qg, kg, bd):
    s = proj.shape[0]
    ts = _row_tile(s)

    def body(q_ref, k_ref, dqh_ref, dkh_ref, dv_ref, qg_ref, kg_ref, bd_ref, out_ref, gq_ref, gk_ref):
        first = pl.program_id(0) == 0

        def bwd(x, dy, gain, gg_ref):
            ms = _head_sum(x * x, bd_ref[...]) * (1.0 / HEAD_DIM)
            r = lax.rsqrt(ms + EPS)
            xhat = x * r
            dxhat = dy * gain
            m = _head_sum(dxhat * xhat, bd_ref[...]) * (1.0 / HEAD_DIM)
            part = jnp.sum(dy * xhat, axis=0, keepdims=True)

            @pl.when(first)
            def _():
                gg_ref[...] = part

            @pl.when(jnp.logical_not(first))
            def _():
                gg_ref[...] += part

            return r * (dxhat - xhat * m)

        out_ref[:, 0:ATTN_W] = bwd(q_ref[...], dqh_ref[...], qg_ref[...], gq_ref).astype(BF16)
        out_ref[:, ATTN_W:2 * ATTN_W] = bwd(k_ref[...], dkh_ref[...], kg_ref[...], gk_ref).astype(BF16)
        out_ref[:, 2 * ATTN_W:3 * ATTN_W] = dv_ref[...].astype(BF16)

    col = lambda c: pl.BlockSpec((ts, ATTN_W), lambda i: (i, c))
    row = pl.BlockSpec((ts, ATTN_W), lambda i: (i, 0))
    vec = pl.BlockSpec((1, ATTN_W), lambda i: (0, 0))
    return pl.pallas_call(
        body, name="qk_bwd", grid=(s // ts,),
        in_specs=[col(0), col(1), row, row, row, vec, vec, pl.BlockSpec((ATTN_W, ATTN_W), lambda i: (0, 0))],
        out_specs=[pl.BlockSpec((ts, 3 * ATTN_W), lambda i: (i, 0)), vec, vec],
        out_shape=[jax.ShapeDtypeStruct((s, 3 * ATTN_W), BF16), jax.ShapeDtypeStruct((1, ATTN_W), F32),
                   jax.ShapeDtypeStruct((1, ATTN_W), F32)],
        compiler_params=_params(("arbitrary",)))(proj, proj, dqh, dkh, dv, qg, kg, bd)


def _split(x):
    hi = x.astype(BF16)
    return hi, (x - hi.astype(F32)).astype(BF16)


def _tri_dot(parts, tri):
    hi, lo = parts
    return jnp.dot(hi, tri, preferred_element_type=F32) + jnp.dot(lo, tri, preferred_element_type=F32)


def _log_sigmoids(z):
    neg_abs = lax.bitcast_convert_type(lax.bitcast_convert_type(z, jnp.uint32) | jnp.uint32(0x80000000), F32)
    lb = jnp.minimum(z, 0.0) - jnp.log(1.0 + jnp.exp(neg_abs))
    return lb, lb - z


DEAD_LOG_WEIGHT = -106.0


def _sweep_key_blocks(tiles, alive, i):
    @pl.when(i == 0)
    def _():
        tiles([0], [True])

    @pl.when(i > 0)
    def _():
        tiles([i, i - 1], [True, False])

    def more(state):
        kb, live = state
        return jnp.logical_and(kb >= 0, live > 0)

    def step(state):
        kb, _ = state
        tiles([kb], [False])
        return kb - 1, alive().astype(jnp.int32)

    lax.while_loop(more, step, (i - 2, alive().astype(jnp.int32)))


def _head_masks():
    lane = lax.broadcasted_iota(jnp.int32, (1, LANES), 1)
    return [lane < HEAD_DIM, lane >= HEAD_DIM]


def _attn_fwd(qs, kh, vb, tri_excl, carry=None):
    s = qs.shape[0]
    t = tri_excl.shape[0]
    nq = s // t

    def body(q_ref, k_ref, v_ref, tri_ref, o_ref, ob_ref, acc_ref, c_ref):
        i = pl.program_id(1)
        hmask = _head_masks()
        q = q_ref[...]
        qm = [jnp.where(hm, q, jnp.zeros_like(q)) for hm in hmask]
        acc_ref[...] = jnp.zeros_like(acc_ref)
        c_ref[...] = jnp.zeros_like(c_ref)
        causal = (lax.broadcasted_iota(jnp.int32, (t, t), 1) < lax.broadcasted_iota(jnp.int32, (t, t), 0))

        def tiles(kbs, masked):
            tri = tri_ref[...]
            starts = [pl.multiple_of(kb * t, t) for kb in kbs]
            kblks = [k_ref[pl.ds(k0, t), :] for k0 in starts]
            vblks = [v_ref[pl.ds(k0, t), :] for k0 in starts]
            chains = [(j, h) for j in range(len(kbs)) for h in range(2)]
            carry = [c_ref[h] for h in range(2)]
            pv = [None, None]
            lbs, loms, between = {}, {}, {}
            for step in range(len(chains) + 2):
                if step < len(chains):
                    j, h = chains[step]
                    z = lax.dot_general(qm[h], kblks[j], (NT, ((), ())), preferred_element_type=F32)
                    lbs[step], lom = _log_sigmoids(z)
                    loms[step] = jnp.where(causal, lom, 0.0) if masked[j] else lom
                if 0 <= step - 1 < len(chains):
                    between[step - 1] = _tri_dot(_split(loms[step - 1]), tri)
                if 0 <= step - 2 < len(chains):
                    n = step - 2
                    j, h = chains[n]
                    w = jnp.exp(lbs[n] + between[n] + carry[h])
                    if masked[j]:
                        w = jnp.where(causal, w, 0.0)
                    carry[h] = carry[h] + jnp.sum(loms[n], axis=-1, keepdims=True)
                    part = jnp.dot(w.astype(BF16), vblks[j], preferred_element_type=F32)
                    pv[h] = part if pv[h] is None else pv[h] + part
            for h in range(2):
                c_ref[h] = carry[h]
                acc_ref[h] += pv[h]

        _sweep_key_blocks(tiles, lambda: jnp.max(c_ref[...]) > DEAD_LOG_WEIGHT, i)
        o = jnp.where(hmask[0], acc_ref[0], acc_ref[1])
        o_ref[...] = o
        ob_ref[...] = o.astype(BF16)

    qspec = pl.BlockSpec((t, LANES), lambda hp, i: (i, hp))
    kspec = pl.BlockSpec((s, LANES), lambda hp, i: (0, hp))
    return _call(
        body, name="attn_fwd", grid=(ATTN_W // LANES, nq),
        in_specs=[qspec, kspec, kspec, pl.BlockSpec((t, t), lambda hp, i: (0, 0))],
        out_specs=[qspec, qspec],
        out_shape=[jax.ShapeDtypeStruct((s, ATTN_W), F32), jax.ShapeDtypeStruct((s, ATTN_W), BF16)],
        scratch_shapes=[pltpu.VMEM((2, t, LANES), F32), pltpu.VMEM((2, t, 1), F32)],
        args=(qs, kh, vb, tri_excl), semantics=("parallel", "arbitrary"), carry=carry)


def _attn_bwd(qs, kh, vb, o, dmix, tri_excl, tri_incl, carry=None):
    s = qs.shape[0]
    t = tri_excl.shape[0]
    nq = s // t

    def body(q_ref, k_ref, v_ref, o_ref, do_ref, te_ref, ti_ref, dq_ref, dk_ref, dv_ref, dqacc_ref, c_ref):
        i = pl.program_id(1)

        @pl.when(i == 0)
        def _():
            dk_ref[...] = jnp.zeros_like(dk_ref)
            dv_ref[...] = jnp.zeros_like(dv_ref)

        hmask = _head_masks()
        q = q_ref[...]
        do = do_ref[...]
        dob = do.astype(BF16)
        prod = dob.astype(F32) * o_ref[...]
        qm =[jnp.where(hm, q, jnp.zeros_like(q)) for hm in hmask]
        dom = [jnp.where(hm, dob, jnp.zeros_like(dob)) for hm in hmask]
        total = [jnp.sum(jnp.where(hm, prod, 0.0), axis=-1, keepdims=True) for hm in hmask]
        dqacc_ref[...] = jnp.zeros_like(dqacc_ref)
        c_ref[...] = jnp.zeros_like(c_ref)
        causal = (lax.broadcasted_iota(jnp.int32, (t, t), 1) < lax.broadcasted_iota(jnp.int32, (t, t), 0))

        def tiles(kbs, masked):
            te = te_ref[...]
            ti = ti_ref[...]
            starts = [pl.multiple_of(kb * t, t) for kb in kbs]
            kblks = [k_ref[pl.ds(k0, t), :] for k0 in starts]
            vblks = [v_ref[pl.ds(k0, t), :] for k0 in starts]
            chains = [(j, h) for j in range(len(kbs)) for h in range(2)]
            c_lom = [c_ref[2 * h] for h in range(2)]
            c_g = [c_ref[2 * h + 1] for h in range(2)]
            lbs, loms, dws, between, wbs, gs, g_after = {}, {}, {}, {}, {}, {}, {}
            dq = [None, None]
            dk = [None] * len(kbs)
            dv = [None] * len(kbs)
            add = lambda acc, part: part if acc is None else acc + part
            for step in range(len(chains) + 3):
                if step < len(chains):
                    j, h = chains[step]
                    z = lax.dot_general(qm[h], kblks[j], (NT, ((), ())), preferred_element_type=F32)
                    dws[step] = lax.dot_general(dom[h], vblks[j], (NT, ((), ())), preferred_element_type=F32)
                    lbs[step], lom = _log_sigmoids(z)
                    loms[step] = jnp.where(causal, lom, 0.0) if masked[j] else lom
                if 0 <= step - 1 < len(chains):
                    between[step - 1] = _tri_dot(_split(loms[step - 1]), te)
                if 0 <= step - 2 < len(chains):
                    n = step - 2
                    j, h = chains[n]
                    w = jnp.exp(lbs[n] + between[n] + c_lom[h])
                    if masked[j]:
                        w = jnp.where(causal, w, 0.0)
                    c_lom[h] = c_lom[h] + jnp.sum(loms[n], axis=-1, keepdims=True)
                    wbs[n] = w.astype(BF16)
                    gs[n] = dws[n] * wbs[n].astype(F32)
                    g_after[n] = _tri_dot(_split(gs[n]), ti)
                if 0 <= step - 3 < len(chains):
                    n = step - 3
                    j, h = chains[n]
                    beta = jnp.exp(lbs[n])
                    dz = gs[n] * (1.0 - beta) - beta * (total[h] - (g_after[n] + c_g[h]))
                    if masked[j]:
                        dz = jnp.where(causal, dz, 0.0)
                    c_g[h] = c_g[h] + jnp.sum(gs[n], axis=-1, keepdims=True)
                    dzb = dz.astype(BF16)
                    dq[h] = add(dq[h], jnp.dot(dzb, kblks[j], preferred_element_type=F32))
                    dk[j] = add(dk[j], lax.dot_general(dzb, qm[h], (TN, ((), ())), preferred_element_type=F32))
                    dv[j] = add(dv[j], lax.dot_general(wbs[n], dom[h], (TN, ((), ())), preferred_element_type=F32))
            for h in range(2):
                c_ref[2 * h] = c_lom[h]
                c_ref[2 * h + 1] = c_g[h]
                dqacc_ref[h] += dq[h]
            for j, k0 in enumerate(starts):
                dk_ref[pl.ds(k0, t), :] += dk[j]
                dv_ref[pl.ds(k0, t), :] += dv[j]

        _sweep_key_blocks(tiles, lambda: jnp.maximum(jnp.max(c_ref[0]), jnp.max(c_ref[2])) > DEAD_LOG_WEIGHT, i)
        dq_ref[...] = jnp.where(hmask[0], dqacc_ref[0], dqacc_ref[1]) * ATTN_SCALE

    qspec = pl.BlockSpec((t, LANES), lambda hp, i: (i, hp))
    kspec = pl.BlockSpec((s, LANES), lambda hp, i: (0, hp))
    tspec = pl.BlockSpec((t, t), lambda hp, i: (0, 0))
    sds = jax.ShapeDtypeStruct((s, ATTN_W), F32)
    return _call(
        body, name="attn_bwd", grid=(ATTN_W // LANES, nq),
        in_specs=[qspec, kspec, kspec, qspec, qspec, tspec, tspec],
        out_specs=[qspec, kspec, kspec], out_shape=[sds, sds, sds],
        scratch_shapes=[pltpu.VMEM((2, t, LANES), F32), pltpu.VMEM((4, t, 1), F32)],
        args=(qs, kh, vb, o, dmix, tri_excl, tri_incl), semantics=("parallel", "arbitrary"), carry=carry)


CONV_ROWS = 64
CONV_COLS = 256


SUBLANES = 8


def _shift_copies(src_ref, sh_ref):
    length = sh_ref.shape[1]
    for r in range(1, SUBLANES):
        sh_ref[r - 1] = src_ref[r:r + length, :]


def _shift_scratch(ts):
    return pltpu.VMEM((SUBLANES - 1, ts + CONV_HALO - SUBLANES, CONV_W), F32)


def _rows_at(src_ref, sh_ref, offset, r0, rows, cols):
    r = offset % SUBLANES
    base = offset - r + r0
    return src_ref[base:base + rows, cols] if r == 0 else sh_ref[r - 1, base:base + rows, cols]


def _taps(src_ref, sh_ref, w_ref, n_taps, first_row, rows, reverse=False):
    width = src_ref.shape[1]
    cols = min(CONV_COLS, width)
    out = []
    for r0 in range(0, rows, CONV_ROWS):
        for c0 in range(0, width, cols):
            acc = jnp.zeros((CONV_ROWS, cols), F32)
            for k in range(n_taps):
                off = (n_taps - 1 - k) if reverse else k
                acc = acc + w_ref[k:k + 1, c0:c0 + cols] * _rows_at(src_ref, sh_ref, first_row + off, r0, CONV_ROWS,
                                                                    slice(c0, c0 + cols))
            out.append(((r0, c0), acc))
    return out


def _conv_fwd(proj, dw_w, dw_b, ln_g, ln_b, carry=None):
    s = proj.shape[0]
    ts = _row_tile(s)
    hb = ts // CONV_HALO

    def body(a_ref, g_ref, ha_ref, hg_ref, w_ref, b_ref, lg_ref, lb_ref, c1_ref, c3_ref, pad_ref, sh_ref):
        i = pl.program_id(0)
        halo = ha_ref[...] * _sigmoid(hg_ref[...])
        pad_ref[0:CONV_HALO, :] = jnp.where(i > 0, halo, 0.0)
        pad_ref[CONV_HALO:, :] = a_ref[...] * _sigmoid(g_ref[...])
        _shift_copies(pad_ref, sh_ref)
        first = CONV_HALO - (CONV_K - 1)
        for (r0, c0), acc in _taps(pad_ref, sh_ref, w_ref, CONV_K, first, ts):
            c1_ref[r0:r0 + CONV_ROWS, c0:c0 + acc.shape[1]] = acc + b_ref[:, c0:c0 + acc.shape[1]]
        c1 = c1_ref[...]
        xc = c1 - jnp.mean(c1, axis=-1, keepdims=True)
        c2 = xc * _rstd(xc) * lg_ref[...] + lb_ref[...]
        c3_ref[...] = (c2 * _sigmoid(c2)).astype(BF16)

    cur = lambda c: pl.BlockSpec((ts, CONV_W), lambda i: (i, c))
    halo = lambda c: pl.BlockSpec((CONV_HALO, CONV_W), lambda i: (jnp.maximum(i * hb - 1, 0), c))
    vec = pl.BlockSpec((1, CONV_W), lambda i: (0, 0))
    row = pl.BlockSpec((ts, CONV_W), lambda i: (i, 0))
    return _call(
        body, name="conv_fwd", grid=(s // ts,),
        in_specs=[cur(3), cur(4), halo(3), halo(4), pl.BlockSpec((CONV_HALO, CONV_W), lambda i: (0, 0)), vec, vec, vec],
        out_specs=[row, row],
        out_shape=[jax.ShapeDtypeStruct((s, CONV_W), F32), jax.ShapeDtypeStruct((s, CONV_W), BF16)],
        scratch_shapes=[pltpu.VMEM((ts + CONV_HALO, CONV_W), F32), _shift_scratch(ts)],
        args=(proj, proj, proj, proj, dw_w, dw_b, ln_g, ln_b), semantics=("parallel",), carry=carry)


def _conv_bwd_ln(dmix, c1, ln_g, ln_b):
    s = c1.shape[0]
    ts = _row_tile(s)

    def body(d_ref, c1_ref, lg_ref, lb_ref, dc1_ref, glg_ref, glb_ref, gb_ref):
        c1v = c1_ref[...]
        xc = c1v - jnp.mean(c1v, axis=-1, keepdims=True)
        r = _rstd(xc)
        xhat = xc * r
        c2 = xhat * lg_ref[...] + lb_ref[...]
        sg = _sigmoid(c2)
        dc2 = d_ref[...] * (sg * (1.0 + c2 * (1.0 - sg)))
        dxhat = dc2 * lg_ref[...]
        dc1 = r * (dxhat - jnp.mean(dxhat, axis=-1, keepdims=True)
                   - xhat * jnp.mean(dxhat * xhat, axis=-1, keepdims=True))
        dc1_ref[...] = dc1
        parts = [(glg_ref, jnp.sum(dc2 * xhat, axis=0, keepdims=True)),
                 (glb_ref, jnp.sum(dc2, axis=0, keepdims=True)),
                 (gb_ref, jnp.sum(dc1, axis=0, keepdims=True))]

        @pl.when(pl.program_id(0) == 0)
        def _():
            for ref, part in parts:
                ref[...] = part

        @pl.when(pl.program_id(0) > 0)
        def _():
            for ref, part in parts:
                ref[...] += part

    row = pl.BlockSpec((ts, CONV_W), lambda i: (i, 0))
    vec = pl.BlockSpec((1, CONV_W), lambda i: (0, 0))
    vsd = jax.ShapeDtypeStruct((1, CONV_W), F32)
    return pl.pallas_call(
        body, name="conv_bwd_ln", grid=(s // ts,),
        in_specs=[pl.BlockSpec((ts, CONV_W), lambda i: (i, 1)), row, vec, vec],
        out_specs=[row, vec, vec, vec],
        out_shape=[jax.ShapeDtypeStruct((s, CONV_W), F32), vsd, vsd, vsd],
        compiler_params=_params(("arbitrary",)))(dmix, c1, ln_g, ln_b)


def _conv_bwd_taps(proj, dc1, dw_w):
    s = proj.shape[0]
    ts = _row_tile(s)
    hb = ts // CONV_HALO
    last = s // CONV_HALO - 1
    nsteps = s // ts

    def body(a_ref, g_ref, ha_ref, hg_ref, d_ref, hd_ref, w_ref, out_ref, gw_ref, pad_ref, dpad_ref, sh_ref):
        i = pl.program_id(0)
        av = a_ref[...]
        sg = _sigmoid(g_ref[...])
        halo = ha_ref[...] * _sigmoid(hg_ref[...])
        pad_ref[0:CONV_HALO, :] = jnp.where(i > 0, halo, 0.0)
        pad_ref[CONV_HALO:, :] = av * sg
        dpad_ref[0:ts, :] = d_ref[...]
        dpad_ref[ts:, :] = jnp.where(i < nsteps - 1, hd_ref[...], 0.0)

        @pl.when(i == 0)
        def _():
            gw_ref[...] = jnp.zeros_like(gw_ref)

        _shift_copies(pad_ref, sh_ref)
        first = CONV_HALO - (CONV_K - 1)
        fold = lambda v: jnp.sum(v.reshape(CONV_ROWS // SUBLANES, SUBLANES, v.shape[1]), axis=0)
        for c0 in range(0, CONV_W, CONV_COLS):
            cs = slice(c0, c0 + CONV_COLS)
            for k in range(CONV_K):
                acc = None
                for r0 in range(0, ts, CONV_ROWS):
                    part = fold(d_ref[r0:r0 + CONV_ROWS, cs] * _rows_at(pad_ref, sh_ref, first + k, r0, CONV_ROWS, cs))
                    acc = part if acc is None else acc + part
                gw_ref[k:k + 1, cs] += jnp.sum(acc, axis=0, keepdims=True)
        _shift_copies(dpad_ref, sh_ref)
        for (r0, c0), dc0 in _taps(dpad_ref, sh_ref, w_ref, CONV_K, 0, ts, reverse=True):
            cs = slice(c0, c0 + dc0.shape[1])
            a_c = a_ref[r0:r0 + CONV_ROWS, cs]
            sg_c = _sigmoid(g_ref[r0:r0 + CONV_ROWS, cs])
            out_ref[r0:r0 + CONV_ROWS, cs] = (dc0 * sg_c).astype(BF16)
            out_ref[r0:r0 + CONV_ROWS, CONV_W + c0:CONV_W + c0 + dc0.shape[1]] = (
                dc0 * a_c * sg_c * (1.0 - sg_c)).astype(BF16)

    cur = lambda c: pl.BlockSpec((ts, CONV_W), lambda i: (i, c))
    halo = lambda c: pl.BlockSpec((CONV_HALO, CONV_W), lambda i: (jnp.maximum(i * hb - 1, 0), c))
    row = pl.BlockSpec((ts, CONV_W), lambda i: (i, 0))
    nxt = pl.BlockSpec((CONV_HALO, CONV_W), lambda i: (jnp.minimum((i + 1) * hb, last), 0))
    wspec = pl.BlockSpec((CONV_HALO, CONV_W), lambda i: (0, 0))
    return pl.pallas_call(
        body, name="conv_bwd_taps", grid=(nsteps,),
        in_specs=[cur(3), cur(4), halo(3), halo(4), row, nxt, wspec],
        out_specs=[pl.BlockSpec((ts, 2 * CONV_W), lambda i: (i, 0)), wspec],
        out_shape=[jax.ShapeDtypeStruct((s, 2 * CONV_W), BF16), jax.ShapeDtypeStruct((CONV_HALO, CONV_W), F32)],
        scratch_shapes=[pltpu.VMEM((ts + CONV_HALO, CONV_W), F32), pltpu.VMEM((ts + CONV_HALO, CONV_W), F32),
                        _shift_scratch(ts)],
        compiler_params=_params(("arbitrary",)))(proj, proj, proj, proj, dc1, dc1, dw_w)


SQRT_HALF = 0.7071067811865476
INV_SQRT_2PI = 0.3989422804014327


def _gelu_parts(x):
    t = x * SQRT_HALF
    cdf = 0.5 + 0.5 * lax.erf(t)
    return x * cdf, cdf + x * (INV_SQRT_2PI * jnp.exp(-(t * t)))


def _ffn_tile(dff):
    return dff


FFN_ROWS = 64
FFN_COLS = LANES


def _ffn_chunks(ts, tc):
    return [(r0, slice(c0, c0 + FFN_COLS)) for c0 in range(0, tc, FFN_COLS) for r0 in range(0, ts, FFN_ROWS)]


def _ffn_gate2(pad_ref, w_ref, b_ref, r0, cs):
    first = FFN_HALO - (FFN_K - 1) + r0
    g2 = b_ref[:, cs] + w_ref[0:1, cs] * pad_ref[first:first + FFN_ROWS, cs]
    for k in range(1, FFN_K):
        g2 = g2 + w_ref[k:k + 1, cs] * pad_ref[first + k:first + k + FFN_ROWS, cs]
    return g2


def _ffn_act(up, fw, fb):
    s = up.shape[0]
    dff = up.shape[1] // 2
    tc = _ffn_tile(dff)
    nj = dff // tc
    ts = _row_tile(s)
    hb = ts // FFN_HALO

    def body(g_ref, v_ref, hg_ref, w_ref, b_ref, act_ref, pad_ref):
        i = pl.program_id(0)
        pad_ref[0:FFN_HALO, :] = jnp.where(i > 0, hg_ref[...].astype(F32), 0.0)
        pad_ref[FFN_HALO:, :] = g_ref[...].astype(F32)
        for r0, cs in _ffn_chunks(ts, tc):
            gelu, _ = _gelu_parts(_ffn_gate2(pad_ref, w_ref, b_ref, r0, cs))
            act_ref[r0:r0 + FFN_ROWS, cs] = (gelu * v_ref[r0:r0 + FFN_ROWS, cs].astype(F32)).astype(BF16)

    return pl.pallas_call(
        body, name="ffn_act", grid=(s // ts, nj),
        in_specs=[pl.BlockSpec((ts, tc), lambda i, j: (i, j)), pl.BlockSpec((ts, tc), lambda i, j: (i, j + nj)),
                  pl.BlockSpec((FFN_HALO, tc), lambda i, j: (jnp.maximum(i * hb - 1, 0), j)),
                  pl.BlockSpec((FFN_HALO, tc), lambda i, j: (0, j)), pl.BlockSpec((1, tc), lambda i, j: (0, j))],
        out_specs=pl.BlockSpec((ts, tc), lambda i, j: (i, j)),
        out_shape=jax.ShapeDtypeStruct((s, dff), BF16),
        scratch_shapes=[pltpu.VMEM((ts + FFN_HALO, tc), F32)],
        compiler_params=_params(("parallel", "parallel")))(up, up, up, fw, fb)


def _ffn_bwd_act(dact, up, fw, fb, carry=None):
    s = up.shape[0]
    dff = up.shape[1] // 2
    tc = _ffn_tile(dff)
    nj = dff // tc
    ts = _row_tile(s)
    hb = ts // FFN_HALO

    def body(d_ref, g_ref, v_ref, hg_ref, w_ref, b_ref, dg2_ref, dval_ref, gw_ref, gb_ref, pad_ref):
        i = pl.program_id(1)
        pad_ref[0:FFN_HALO, :] = jnp.where(i > 0, hg_ref[...].astype(F32), 0.0)
        pad_ref[FFN_HALO:, :] = g_ref[...].astype(F32)

        @pl.when(i == 0)
        def _():
            gw_ref[...] = jnp.zeros_like(gw_ref)
            gb_ref[...] = jnp.zeros_like(gb_ref)

        fold = lambda v: jnp.sum(v.reshape(FFN_ROWS // SUBLANES, SUBLANES, FFN_COLS), axis=0)
        first = FFN_HALO - (FFN_K - 1)
        sums = {}
        for r0, cs in _ffn_chunks(ts, tc):
            rows = slice(r0, r0 + FFN_ROWS)
            shifted = [pad_ref[first + k + r0:first + k + r0 + FFN_ROWS, cs] for k in range(FFN_K)]
            g2 = b_ref[:, cs] + w_ref[0:1, cs] * shifted[0]
            for k in range(1, FFN_K):
                g2 = g2 + w_ref[k:k + 1, cs] * shifted[k]
            gelu, dgelu = _gelu_parts(g2)
            dactv = d_ref[rows, cs]
            dval_ref[rows, cs] = (dactv * gelu).astype(BF16)
            dg2 = dactv * v_ref[rows, cs].astype(F32) * dgelu
            dg2_ref[rows, cs] = dg2.astype(BF16)
            parts = [fold(dg2)] + [fold(dg2 * shifted[k]) for k in range(FFN_K)]
            sums = {n: part + sums[n] if r0 else part for n, part in enumerate(parts)}
            if r0 + FFN_ROWS == ts:
                gb_ref[:, cs] += jnp.sum(sums[0], axis=0, keepdims=True)
                for k in range(FFN_K):
                    gw_ref[k:k + 1, cs] += jnp.sum(sums[1 + k], axis=0, keepdims=True)

    blk = pl.BlockSpec((ts, tc), lambda j, i: (i, j))
    wspec = pl.BlockSpec((FFN_HALO, tc), lambda j, i: (0, j))
    bspec = pl.BlockSpec((1, tc), lambda j, i: (0, j))
    return _call(
        body, name="ffn_bwd_act", grid=(nj, s // ts),
        in_specs=[blk, blk, pl.BlockSpec((ts, tc), lambda j, i: (i, j + nj)),
                  pl.BlockSpec((FFN_HALO, tc), lambda j, i: (jnp.maximum(i * hb - 1, 0), j)), wspec, bspec],
        out_specs=[blk, pl.BlockSpec((ts, tc), lambda j, i: (i, j + nj)), wspec, bspec],
        out_shape=[jax.ShapeDtypeStruct((s, dff), BF16), jax.ShapeDtypeStruct((s, 2 * dff), BF16),
                   jax.ShapeDtypeStruct((FFN_HALO, dff), F32), jax.ShapeDtypeStruct((1, dff), F32)],
        scratch_shapes=[pltpu.VMEM((ts + FFN_HALO, tc), F32)],
        args=(dact, up, up, up, fw, fb), semantics=("parallel", "arbitrary"), carry=carry)


def _ffn_bwd_conv(dg2, fw, dup):
    s, dff = dg2.shape
    tc = _ffn_tile(dff)
    ts = _row_tile(s)
    hb = ts // FFN_HALO
    last = s // FFN_HALO - 1
    nsteps = s // ts

    def body(d_ref, hd_ref, w_ref, dup_ref, out_ref, pad_ref):
        i = pl.program_id(0)
        pad_ref[0:ts, :] = d_ref[...].astype(F32)
        pad_ref[ts:, :] = jnp.where(i < nsteps - 1, hd_ref[...].astype(F32), 0.0)
        for r0, cs in _ffn_chunks(ts, tc):
            dg = w_ref[0:1, cs] * pad_ref[r0 + FFN_K - 1:r0 + FFN_K - 1 + FFN_ROWS, cs]
            for k in range(1, FFN_K):
                dg = dg + w_ref[k:k + 1, cs] * pad_ref[r0 + FFN_K - 1 - k:r0 + FFN_K - 1 - k + FFN_ROWS, cs]
            out_ref[r0:r0 + FFN_ROWS, cs] = dg.astype(BF16)

    blk = pl.BlockSpec((ts, tc), lambda i, j: (i, j))
    return pl.pallas_call(
        body, name="ffn_bwd_conv", grid=(nsteps, dff // tc),
        in_specs=[blk, pl.BlockSpec((FFN_HALO, tc), lambda i, j: (jnp.minimum((i + 1) * hb, last), j)),
                  pl.BlockSpec((FFN_HALO, tc), lambda i, j: (0, j)), ANY],
        out_specs=blk, out_shape=jax.ShapeDtypeStruct(dup.shape, BF16), input_output_aliases={3: 0},
        scratch_shapes=[pltpu.VMEM((ts + FFN_HALO, tc), F32)],
        compiler_params=_params(("parallel", "parallel")))(dg2, dg2, fw, dup)


def _ple_loss(h2, zg, pp, target):
    s, d = h2.shape
    ts = _row_tile(s)

    def body(h_ref, z_ref, p_ref, t_ref, dh_ref, dpp_ref, dz_ref, loss_ref):
        pg = _sigmoid(z_ref[...])
        ppv = p_ref[...]
        diff = h_ref[...] + pg * ppv - t_ref[...]
        dh = diff * (1.0 / d)
        dh_ref[...] = dh
        dpp_ref[...] = (dh * pg).astype(BF16)
        dz_ref[...] = (dh * ppv * pg * (1.0 - pg)).astype(BF16)
        part = jnp.sum(jnp.sum(diff * diff, axis=0, keepdims=True), axis=1, keepdims=True)

        @pl.when(pl.program_id(0) == 0)
        def _():
            loss_ref[...] = jnp.zeros_like(loss_ref)

        loss_ref[...] += jnp.broadcast_to(part, loss_ref.shape)

    row = pl.BlockSpec((ts, d), lambda i: (i, 0))
    return pl.pallas_call(
        body, name="ple_loss", grid=(s // ts,), in_specs=[row, row, row, row],
        out_specs=[row, row, row, pl.BlockSpec((8, LANES), lambda i: (0, 0))],
        out_shape=[jax.ShapeDtypeStruct((s, d), F32), jax.ShapeDtypeStruct((s, d), BF16),
                   jax.ShapeDtypeStruct((s, d), BF16), jax.ShapeDtypeStruct((8, LANES), F32)],
        compiler_params=_params(("arbitrary",)))(h2, zg, pp, target)


def _local_step(x, p, target, w, shards=None, unpack_filters=None):
    riding = shards is not None
    major = lambda n, g: g if n in COL_SHARDED else g.reshape(N_CHIPS, -1, g.shape[-1])
    full = lambda n, g: g if n in COL_SHARDED else g.reshape(-1, g.shape[-1])
    first = lambda res: (res[0], res[1:]) if riding else (res, [])

    def ride(names, arrays):
        return ([arrays[n] for n in names], False) if riding else None

    def leave(grads):
        return (grads, True) if riding else None

    def land(names, got):
        return {n: full(n, g) for n, g in zip(names, got)}
    s = x.shape[0]
    t = min(256, s)
    tri = jnp.tril(jnp.ones((t, t), F32))
    tri_incl = tri.astype(BF16)
    tri_excl = jnp.tril(jnp.ones((t, t), F32), -1).astype(BF16)
    bd = jnp.kron(jnp.eye(N_HEADS, dtype=F32), jnp.ones((HEAD_DIM, HEAD_DIM), F32)).astype(BF16)
    qg = jnp.tile(w["q_gain"], (1, N_HEADS))
    kg = jnp.tile(w["k_gain"], (1, N_HEADS))
    pb = p.astype(BF16)

    u1, got = first(_rms_fwd(x, w["g_mix"], "rms_mix", carry=ride(["w_in", "filters"], shards)))
    if riding:
        w = {**w, "w_in": got[0], **unpack_filters(got[1])}
    dw_w = jnp.pad(w["dw_w"], ((0, CONV_HALO - CONV_K), (0, 0)))
    fw = jnp.pad(w["ffn_conv_w"], ((0, FFN_HALO - FFN_K), (0, 0)))
    proj = _mm_nn_sharded(u1, w["w_in"], "mm_in")
    qs, kh, vb = _qkv_prep(proj, qg, kg, bd)
    with_attn, with_conv = ["w_out", "w_up", "w_ple_gate", "w_ple_proj"], ["w_down"]
    o, ob, *got = _attn_fwd(qs, kh, vb, tri_excl, carry=ride(with_attn, shards))
    w = {**w, **land(with_attn, got)}
    c1, c3, *got = _conv_fwd(proj, dw_w, w["dw_b"], w["conv_ln_g"], w["conv_ln_b"], carry=ride(with_conv, shards))
    w = {**w, **land(with_conv, got)}
    mix = jnp.concatenate([ob, c3], axis=1)
    h1 = _mm_nn_full(mix, w["w_out"], "mm_out", res=x)
    u2 = _rms_fwd(h1, w["g_ffn"], "rms_ffn")
    up = _mm_nn_sharded(u2, w["w_up"], "mm_up", out_dtype=BF16)
    act = _ffn_act(up, fw, w["ffn_conv_b"])
    h2 = _mm_nn_full(act, w["w_down"], "mm_down", res=h1)
    u3 = _rms_fwd(h2, w["g_ple"], "rms_ple")
    zg = _mm_nn_full(u3, w["w_ple_gate"], "mm_ple_gate")
    pp = _mm_nn_sharded(pb, w["w_ple_proj"], "mm_ple_proj")
    dh3, dpp, dz, sq = _ple_loss(h2, zg, pp, target)

    big = {}
    small = {}
    big["w_ple_proj"] = _mm_tn_sharded(pb, dpp, N_CHIPS, "mm_g_ple_proj")
    big["w_ple_gate"] = _mm_tn_full(u3, dz, "mm_g_ple_gate", tm=u3.shape[1])
    du3 = _mm_nt_full(dz, w["w_ple_gate"], "mm_d_ple_gate")
    dh2, dh2b, small["g_ple"] = _rms_bwd(h2, du3, w["g_ple"], dh3, "rms_ple_bwd")
    slots = {}
    with_down, with_ffn, with_attn = ["w_ple_proj", "w_ple_gate"], ["w_down"], ["w_up", "w_out"]
    leaving = leave([major(n, big.pop(n)) for n in with_down]) if riding else None
    big["w_down"], got = first(_mm_tn_full(act, dh2b, "mm_g_down", tm=act.shape[1] // 2, carry=leaving))
    slots.update(zip(with_down, got))
    dact = _mm_nt_full(dh2b, w["w_down"], "mm_d_down")
    leaving = leave([major(n, big.pop(n)) for n in with_ffn]) if riding else None
    dg2, dup, gfw, small["ffn_conv_b"], *got = _ffn_bwd_act(dact, up, fw, w["ffn_conv_b"], carry=leaving)
    slots.update(zip(with_ffn, got))
    small["ffn_conv_w"] = gfw[:FFN_K]
    dup = _ffn_bwd_conv(dg2, fw, dup)
    big["w_up"] = _mm_tn_sharded(u2, dup, N_CHIPS, "mm_g_up", n_split=2, tk=_contract_tile(s) // 2)
    du2, = _mm_nt_sharded(dup, w["w_up"], "mm_d_up")
    dh1, dh1b, small["g_ffn"] = _rms_bwd(h1, du2, w["g_ffn"], dh2, "rms_ffn_bwd")
    big["w_out"] = _mm_tn_full(mix, dh1b, "mm_g_out", tm=mix.shape[1])
    dmix = _mm_nt_full(dh1b, w["w_out"], "mm_d_out")
    dc1, small["conv_ln_g"], small["conv_ln_b"], small["dw_b"] = _conv_bwd_ln(dmix, c1, w["conv_ln_g"], w["conv_ln_b"])
    dcacg, gdw = _conv_bwd_taps(proj, dc1, dw_w)
    small["dw_w"] = gdw[:CONV_K]
    leaving = leave([major(n, big.pop(n)) for n in with_attn]) if riding else None
    dqh, dkh, dv, *got = _attn_bwd(qs, kh, vb, o, dmix, tri_excl, tri_incl, carry=leaving)
    slots.update(zip(with_attn, got))
    dqkv, gq, gk = _qk_bwd(proj, dqh, dkh, dv, qg, kg, bd)
    small["q_gain"] = gq.reshape(N_HEADS, HEAD_DIM).sum(axis=0, keepdims=True)
    small["k_gain"] = gk.reshape(N_HEADS, HEAD_DIM).sum(axis=0, keepdims=True)
    dproj = jnp.concatenate([dqkv, dcacg], axis=1)
    big["w_in"] = _mm_tn_sharded(u1, dproj, N_CHIPS, "mm_g_in")
    half = big["w_in"].shape[1] // 2
    halves = [big["w_in"][:, :half], big.pop("w_in")[:, half:]] if riding else [None, None]
    du1, *got_a = _mm_nt_sharded(dproj, w["w_in"], "mm_d_in", carry=leave([halves[0]]) if riding else None)
    grad_x, _, small["g_mix"], *got_b = _rms_bwd(x, du1, w["g_mix"], dh1, "rms_mix_bwd",
                                                 carry=leave([halves[1]]) if riding else None)
    if riding:
        slots["w_in"] = jnp.concatenate([got_a[0], got_b[0]], axis=1)
    return sq[0, 0], grad_x, big, slots, small


def _exchange(srcs, per_chip, name):
    n = len(srcs)

    def body(*refs):
        src_refs, land_refs = refs[:n], refs[n:2 * n]
        send_sems, recv_sems, local_sems = refs[2 * n:]
        local = _local_copies(src_refs, land_refs, local_sems, per_chip)
        sends = _chip_copies(src_refs, land_refs, send_sems, recv_sems, per_chip, False)
        for cp in local + sends:
            cp.start()
        for cp in _chip_copies(src_refs, land_refs, send_sems, recv_sems, per_chip, True):
            cp.wait_recv()
        for cp in sends:
            cp.wait_send()
        for cp in local:
            cp.wait()

    return pl.pallas_call(
        body, name=name, in_specs=[ANY] * n, out_specs=[ANY] * n, out_shape=_exchanged_shapes(srcs, per_chip),
        scratch_shapes=_exchange_sems(n))(*srcs)


def _finish_exchange(mine, small):
    n = len(mine)

    def body(*refs):
        gin, sin = refs[:n], refs[n]
        gout, sout = refs[n + 1:2 * n + 1], refs[2 * n + 1]
        send_sems, recv_sems, small_send, small_recv, local_sem = refs[2 * n + 2:]
        x, y, c, _ = _position()
        dev = 4 * x + 2 * y + c
        flip = lambda v, bit: 1 - v if bit else v
        others = [(flip(x, k & 4), flip(y, k & 2), flip(c, k & 1)) for k in range(1, N_DEV)]
        local = pltpu.make_async_copy(sin, sout.at[dev], local_sem)
        local.start()
        swaps = [_remote(gin[a], gout[a], send_sems, recv_sems, a, (x, y, 1 - c)) for a in range(n)]
        sends = swaps + [_remote(sin, sout.at[dev], small_send, small_recv, k, peer) for k, peer in enumerate(others)]
        for cp in sends:
            cp.start()
        for cp in swaps:
            cp.wait_recv()
        for k, (px, py, pc) in enumerate(others):
            _remote(sin, sout.at[4 * px + 2 * py + pc], small_send, small_recv, k, (px, py, pc)).wait_recv()
        for cp in sends:
            cp.wait_send()
        local.wait()

    return pl.pallas_call(
        body, name="finish_exchange", in_specs=[ANY] * (n + 1), out_specs=[ANY] * (n + 1),
        out_shape=[jax.ShapeDtypeStruct(g.shape, g.dtype) for g in mine]
        + [jax.ShapeDtypeStruct((N_DEV,) + small.shape, small.dtype)],
        scratch_shapes=[pltpu.SemaphoreType.DMA((n,)), pltpu.SemaphoreType.DMA((n,)),
                        pltpu.SemaphoreType.DMA((N_DEV - 1,)), pltpu.SemaphoreType.DMA((N_DEV - 1,)),
                        pltpu.SemaphoreType.DMA])(*mine, small)


def _elem_tile(rows):
    return 128 if rows % 128 == 0 else (64 if rows % 64 == 0 else rows)


def _sum_slots(a, name):
    g, r, c = a.shape
    tr = _elem_tile(r)

    def body(a_ref, o_ref):
        acc = a_ref[0]
        for k in range(1, g):
            acc = acc + a_ref[k]
        o_ref[...] = acc

    return pl.pallas_call(
        body, name=name, grid=(r // tr,), in_specs=[pl.BlockSpec((g, tr, c), lambda i: (0, i, 0))],
        out_specs=pl.BlockSpec((tr, c), lambda i: (i, 0)), out_shape=jax.ShapeDtypeStruct((r, c), a.dtype),
        compiler_params=_params(("parallel",)))(a)


def _adamw(wt, ga, gb, m, v, name):
    r, c = wt.shape
    tr = _elem_tile(r)
    two = gb is not None

    def body(*refs):
        if two:
            w_ref, ga_ref, gb_ref, m_ref, v_ref, g_out, d_out, m_out, v_out = refs
            g = ga_ref[...] + gb_ref[...]
        else:
            w_ref, ga_ref, m_ref, v_ref, g_out, d_out, m_out, v_out = refs
            g = ga_ref[...]
        mn = ADAM_B1 * m_ref[...] + (1.0 - ADAM_B1) * g
        vn = ADAM_B2 * v_ref[...] + (1.0 - ADAM_B2) * (g * g)
        m_hat = mn / (1.0 - ADAM_B1 ** ADAM_STEP)
        v_hat = vn / (1.0 - ADAM_B2 ** ADAM_STEP)
        g_out[...] = g
        d_out[...] = -ADAM_LR * (m_hat / (jnp.sqrt(v_hat) + ADAM_EPS) + ADAM_WD * w_ref[...])
        m_out[...] = mn
        v_out[...] = vn

    blk = pl.BlockSpec((tr, c), lambda i: (i, 0))
    args = [wt, ga] + ([gb] if two else []) + [m, v]
    sds = jax.ShapeDtypeStruct((r, c), F32)
    return pl.pallas_call(
        body, name=name, grid=(r // tr,), in_specs=[blk] * len(args), out_specs=[blk] * 4, out_shape=[sds] * 4,
        compiler_params=_params(("parallel",)))(*args)


def _pack(arrs, rows):
    flat = jnp.concatenate([a.reshape(-1) for a in arrs])
    return jnp.pad(flat, (0, rows * LANES - flat.shape[0])).reshape(rows, LANES)


def _unpack(buf, shapes):
    flat = buf.reshape(-1)
    out, off = [], 0
    for shp in shapes:
        size = 1
        for d in shp:
            size *= d
        out.append(flat[off:off + size].reshape(shp))
        off += size
    return out


BIG = ["w_in", "w_out", "w_up", "w_down", "w_ple_gate", "w_ple_proj"]
COL_SHARDED = ["w_in", "w_up", "w_ple_proj"]
SMALL_REPL = ["g_mix", "q_gain", "k_gain", "dw_b", "conv_ln_g", "conv_ln_b", "g_ffn", "ffn_conv_b", "g_ple"]
SMALL_SHARDED = ["dw_w", "ffn_conv_w"]
WEIGHTS = ["g_mix", "w_in", "q_gain", "k_gain", "dw_w", "dw_b", "conv_ln_g", "conv_ln_b", "w_out", "g_ffn", "w_up",
           "ffn_conv_w", "ffn_conv_b", "w_down", "g_ple", "w_ple_gate", "w_ple_proj"]


def _rows_for(n_elems):
    return -(-n_elems // (8 * LANES)) * 8


def kernel(x, p, g_mix, w_in, q_gain, k_gain, dw_w, dw_b, conv_ln_g, conv_ln_b, w_out, g_ffn, w_up, ffn_conv_w, ffn_conv_b, w_down, g_ple, w_ple_gate, w_ple_proj, loss_target, m_g_mix, m_w_in, m_q_gain, m_k_gain, m_dw_w, m_dw_b, m_conv_ln_g, m_conv_ln_b, m_w_out, m_g_ffn, m_w_up, m_ffn_conv_w, m_ffn_conv_b, m_w_down, m_g_ple, m_w_ple_gate, m_w_ple_proj, v_g_mix, v_w_in, v_q_gain, v_k_gain, v_dw_w, v_dw_b, v_conv_ln_g, v_conv_ln_b, v_w_out, v_g_ffn, v_w_up, v_ffn_conv_w, v_ffn_conv_b, v_w_down, v_g_ple, v_w_ple_gate, v_w_ple_proj):
    given = dict(locals())
    strip = lambda n, a: a if n in SMALL_REPL else a[0]
    wts = {n: strip(n, given[n]) for n in WEIGHTS}
    mom = {n: strip(n, given["m_" + n]) for n in WEIGHTS}
    var = {n: strip(n, given["v_" + n]) for n in WEIGHTS}
    chip = 2 * lax.axis_index("x") + lax.axis_index("y")

    small_shard_shapes = [wts[n].shape for n in SMALL_SHARDED]
    filt_rows = _rows_for(sum(wts[n].size for n in SMALL_SHARDED))
    shards = {n: wts[n].astype(BF16) for n in BIG}
    shards["filters"] = _pack([wts[n] for n in SMALL_SHARDED], filt_rows)

    def unpack_filters(filt_all):
        per_chip = [_unpack(filt_all[k], small_shard_shapes) for k in range(N_CHIPS)]
        return {n: jnp.concatenate([per_chip[k][idx] for k in range(N_CHIPS)], axis=1)
                for idx, n in enumerate(SMALL_SHARDED)}

    sq, grad_x, big, slots, small = _local_step(x[0], p[0, 0], loss_target[0], {n: wts[n] for n in SMALL_REPL},
                                                shards, unpack_filters)
    loss = lax.psum(sq * (0.5 / x.shape[-1]), ("x", "y", "c"))

    small_names = SMALL_REPL + SMALL_SHARDED
    small_shapes = [small[n].shape for n in small_names]
    small_rows = _rows_for(sum(small[n].size for n in small_names))
    mine = [_sum_slots(slots[n], "sum_" + n) for n in BIG]
    *theirs, small_slots = _finish_exchange(mine, _pack([small[n] for n in small_names], small_rows))
    small_sum = dict(zip(small_names, _unpack(_sum_slots(small_slots, "sum_small"), small_shapes)))

    outs = {}
    for n, ga, gb in zip(BIG, mine, theirs):
        outs[n] = _adamw(wts[n], ga, gb, mom[n], var[n], "adamw_" + n)
    for n in SMALL_SHARDED:
        width = wts[n].shape[1]
        small_sum[n] = lax.dynamic_slice_in_dim(small_sum[n], chip * width, width, axis=1)
    local_shapes = [wts[n].shape for n in small_names]
    local_rows = _rows_for(sum(wts[n].size for n in small_names))
    packed = _adamw(_pack([wts[n] for n in small_names], local_rows), _pack([small_sum[n] for n in small_names], local_rows),
                    None, _pack([mom[n] for n in small_names], local_rows),
                    _pack([var[n] for n in small_names], local_rows), "adamw_small")
    unpacked = [_unpack(buf, local_shapes) for buf in packed]
    for idx, n in enumerate(small_names):
        outs[n] = [u[idx] for u in unpacked]
    result = [loss, grad_x[None]]
    for part in range(4):
        result += [outs[n][part] if n in SMALL_REPL else outs[n][part][None] for n in WEIGHTS]
    return tuple(result)
```
